```python
import jax, jax.numpy as jnp
from jax import lax
import numpy as np

D_MODEL = 1024
BATCH = 16
SEQ = 2048
DEPTH = 1

HGRN_HEADS = 8
HGRN_KEY_DIM = 128
HGRN_VAL_DIM = D_MODEL // HGRN_HEADS
HGRN_FWIDTH = HGRN_HEADS * HGRN_KEY_DIM
HGRN_WIDTH = HGRN_HEADS * HGRN_VAL_DIM
CHUNK = 16
POOL_WINDOWS = (2, 4, 8, 16)
POOL_GROUPS = len(POOL_WINDOWS)
POOL_WIDTH = D_MODEL
POOL_GROUP_DIM = POOL_WIDTH // POOL_GROUPS
D_FF = -(-(8 * D_MODEL) // (3 * 256)) * 256
RMS_EPS = 1e-6
IN_SPLITS = (HGRN_FWIDTH, HGRN_FWIDTH, HGRN_FWIDTH, HGRN_WIDTH, HGRN_WIDTH, POOL_WIDTH, D_MODEL, D_MODEL)
IN_WIDTH = sum(IN_SPLITS)

kernel_name = "hgrn2_multipool_gated_hybrid_encoder"


def rmsnorm(x, g):
    xf = x.astype(jnp.float32)
    y = xf * lax.rsqrt(jnp.mean(xf * xf, axis=-1, keepdims=True) + RMS_EPS)
    return (y * g.astype(jnp.float32)).astype(x.dtype)


def chunk_gated_recurrence(q, k, v, log_f):
    B, H, L, N = q.shape
    Dv = v.shape[-1]
    n_chunks = L // CHUNK

    def to_chunks(t):
        return jnp.moveaxis(t.reshape(B, H, n_chunks, CHUNK, t.shape[-1]), 2, 0)

    qc, kc, vc, gc = to_chunks(q), to_chunks(k), to_chunks(v), to_chunks(log_f)
    mask = jnp.tril(jnp.ones((CHUNK, CHUNK), dtype=bool))

    def step(S, inp):
        qi, ki, vi, gi = inp
        b = jnp.cumsum(gi, axis=-2)
        b_last = b[..., -1:, :]
        q_dec = qi * jnp.exp(b)
        k_inv = ki * jnp.exp(-b)
        scores = jnp.einsum('bhin,bhjn->bhij', q_dec, k_inv)
        scores = jnp.where(mask, scores, 0.0)
        o = (jnp.einsum('bhij,bhjd->bhid', scores, vi)
             + jnp.einsum('bhin,bhnd->bhid', q_dec, S))
        k_end = ki * jnp.exp(b_last - b)
        S = (jnp.exp(b_last)[..., 0, :, None] * S
             + jnp.einsum('bhjn,bhjd->bhnd', k_end, vi))
        return S, o

    S0 = jnp.zeros((B, H, N, Dv), q.dtype)
    _, o = lax.scan(step, S0, (qc, kc, vc, gc))
    return jnp.moveaxis(o, 0, 2).reshape(B, H, L, Dv)


def hgrn2_bidirectional(q_raw, ff_raw, fb_raw, i_raw, og_raw, lb, norm_g):
    B, L, _ = q_raw.shape
    f32 = jnp.float32

    def heads(t, d):
        return t.reshape(B, L, HGRN_HEADS, d).transpose(0, 2, 1, 3)

    q = heads(jax.nn.silu(q_raw.astype(f32)), HGRN_KEY_DIM)
    v = heads(i_raw.astype(f32), HGRN_VAL_DIM)
    f_fwd = lb[0] + (1.0 - lb[0]) * jax.nn.sigmoid(ff_raw.astype(f32))
    f_bwd = lb[1] + (1.0 - lb[1]) * jax.nn.sigmoid(fb_raw.astype(f32))
    f = jnp.concatenate([heads(f_fwd, HGRN_KEY_DIM), heads(f_bwd, HGRN_KEY_DIM)[:, :, ::-1]], axis=1)
    qq = jnp.concatenate([q, q[:, :, ::-1]], axis=1)
    vv = jnp.concatenate([v, v[:, :, ::-1]], axis=1)
    o = chunk_gated_recurrence(qq, 1.0 - f, vv, jnp.log(f))
    o = o[:, :HGRN_HEADS] + o[:, HGRN_HEADS:, ::-1]
    o = o * lax.rsqrt(jnp.mean(o * o, axis=-1, keepdims=True) + RMS_EPS)
    o = o.transpose(0, 2, 1, 3).reshape(B, L, HGRN_WIDTH) * norm_g.astype(f32)
    return (o * jax.nn.silu(og_raw.astype(f32))).astype(q_raw.dtype)


def multiscale_pool(p, w_grp, scale):
    B, L, _ = p.shape
    f32 = jnp.float32
    pg = p.astype(f32).reshape(B, L, POOL_GROUPS, POOL_GROUP_DIM)
    cs = jnp.concatenate([jnp.zeros((B, 1, POOL_GROUPS, POOL_GROUP_DIM), f32),
                          jnp.cumsum(pg, axis=1)], axis=1)
    half = jnp.array([w // 2 for w in POOL_WINDOWS], dtype=jnp.int32)
    t = jnp.arange(L, dtype=jnp.int32)[:, None]
    lo = jnp.clip(t - half + 1, 0, L)
    hi = jnp.clip(t + half + 1, 0, L)
    gidx = jnp.arange(POOL_GROUPS, dtype=jnp.int32)[None, :]
    win_sum = cs[:, hi, gidx, :] - cs[:, lo, gidx, :]
    count = (hi - lo).astype(f32)[..., None]
    y = win_sum / count - pg
    y = jnp.einsum('blgc,gcd->blgd', y, w_grp.astype(f32))
    return (y.reshape(B, L, POOL_WIDTH) * scale.astype(f32)).astype(p.dtype)


def _fwd_setup_inputs(seed: int = 0) -> dict:
    key = jax.random.key(seed)
    ks = jax.random.split(key, 16)
    f32 = jnp.float32
    nrm = lambda k, shape, fan_in: jax.random.normal(k, shape, f32) * (fan_in ** -0.5)
    gain = lambda k, shape: 1.0 + 0.02 * jax.random.normal(k, shape, f32)
    return {
        "x": jax.random.normal(ks[0], (BATCH, SEQ, D_MODEL), f32),
        "g_mix": gain(ks[1], (DEPTH, D_MODEL)),
        "w_in": nrm(ks[2], (DEPTH, D_MODEL, IN_WIDTH), D_MODEL),
        "lb_logits": 0.1 * jax.random.normal(ks[3], (2, DEPTH + 1, HGRN_FWIDTH), f32),
        "hgrn_norm_g": gain(ks[4], (DEPTH, HGRN_WIDTH)),
        "pool_w": nrm(ks[5], (DEPTH, POOL_GROUPS, POOL_GROUP_DIM, POOL_GROUP_DIM), POOL_GROUP_DIM),
        "pool_scale": gain(ks[6], (DEPTH, POOL_WIDTH)),
        "w_branch_a": nrm(ks[7], (DEPTH, HGRN_WIDTH, D_MODEL), HGRN_WIDTH),
        "w_branch_b": nrm(ks[8], (DEPTH, POOL_WIDTH, D_MODEL), POOL_WIDTH),
        "w_out": nrm(ks[9], (DEPTH, D_MODEL, D_MODEL), D_MODEL),
        "g_ffn": gain(ks[10], (DEPTH, D_MODEL)),
        "w_ffn_in": nrm(ks[11], (DEPTH, D_MODEL, 2 * D_FF), D_MODEL),
        "w_ffn_out": nrm(ks[12], (DEPTH, D_FF, D_MODEL), D_FF),
        "g_final": gain(ks[13], (D_MODEL,)),
    }


def _fwd_reference(x, g_mix, w_in, lb_logits, hgrn_norm_g, pool_w, pool_scale, w_branch_a,
              w_branch_b, w_out, g_ffn, w_ffn_in, w_ffn_out, g_final):
    lb_all = jnp.cumsum(jax.nn.softmax(lb_logits.astype(jnp.float32), axis=1), axis=1)
    offsets = np.cumsum(IN_SPLITS)[:-1].tolist()
    h = x
    for l in range(DEPTH):
        u = rmsnorm(h, g_mix[l])
        proj = jnp.einsum('bsd,de->bse', u, w_in[l])
        q_r, ff_r, fb_r, i_r, og_r, p_r, ga_r, gb_r = jnp.split(proj, offsets, axis=-1)
        y_a = hgrn2_bidirectional(q_r, ff_r, fb_r, i_r, og_r, lb_all[:, l], hgrn_norm_g[l])
        y_b = multiscale_pool(p_r, pool_w[l], pool_scale[l])
        z_a = jnp.einsum('bse,ed->bsd', y_a, w_branch_a[l])
        z_b = jnp.einsum('bse,ed->bsd', y_b, w_branch_b[l])
        merged = jax.nn.sigmoid(ga_r) * z_a + jax.nn.sigmoid(gb_r) * z_b
        h = h + jnp.einsum('bsd,de->bse', merged, w_out[l])
        u = rmsnorm(h, g_ffn[l])
        gate, up = jnp.split(jnp.einsum('bsd,df->bsf', u, w_ffn_in[l]), 2, axis=-1)
        h = h + jnp.einsum('bsf,fd->bsd', jax.nn.silu(gate) * up, w_ffn_out[l])
    return rmsnorm(h, g_final)


import jax as _jax
import jax.numpy as _jnp

TWIN_FORMAT = 'train_step'
FWD_PARAMS = ['x', 'g_mix', 'w_in', 'lb_logits', 'hgrn_norm_g', 'pool_w', 'pool_scale', 'w_branch_a', 'w_branch_b', 'w_out', 'g_ffn', 'w_ffn_in', 'w_ffn_out', 'g_final']
TWIN_WEIGHTS = ['g_mix', 'w_in', 'lb_logits', 'hgrn_norm_g', 'pool_w', 'pool_scale', 'w_branch_a', 'w_branch_b', 'w_out', 'g_ffn', 'w_ffn_in', 'w_ffn_out', 'g_final']
TWIN_DIFF_INPUT = 'x'
TWIN_INPUTS = ['x', 'g_mix', 'w_in', 'lb_logits', 'hgrn_norm_g', 'pool_w', 'pool_scale', 'w_branch_a', 'w_branch_b', 'w_out', 'g_ffn', 'w_ffn_in', 'w_ffn_out', 'g_final', 'loss_target', 'm_g_mix', 'm_w_in', 'm_lb_logits', 'm_hgrn_norm_g', 'm_pool_w', 'm_pool_scale', 'm_w_branch_a', 'm_w_branch_b', 'm_w_out', 'm_g_ffn', 'm_w_ffn_in', 'm_w_ffn_out', 'm_g_final', 'v_g_mix', 'v_w_in', 'v_lb_logits', 'v_hgrn_norm_g', 'v_pool_w', 'v_pool_scale', 'v_w_branch_a', 'v_w_branch_b', 'v_w_out', 'v_g_ffn', 'v_w_ffn_in', 'v_w_ffn_out', 'v_g_final']
TWIN_OUTPUTS = ['loss', 'grad_x', 'grad_g_mix', 'grad_w_in', 'grad_lb_logits', 'grad_hgrn_norm_g', 'grad_pool_w', 'grad_pool_scale', 'grad_w_branch_a', 'grad_w_branch_b', 'grad_w_out', 'grad_g_ffn', 'grad_w_ffn_in', 'grad_w_ffn_out', 'grad_g_final', 'delta_g_mix', 'delta_w_in', 'delta_lb_logits', 'delta_hgrn_norm_g', 'delta_pool_w', 'delta_pool_scale', 'delta_w_branch_a', 'delta_w_branch_b', 'delta_w_out', 'delta_g_ffn', 'delta_w_ffn_in', 'delta_w_ffn_out', 'delta_g_final', 'new_m_g_mix', 'new_m_w_in', 'new_m_lb_logits', 'new_m_hgrn_norm_g', 'new_m_pool_w', 'new_m_pool_scale', 'new_m_w_branch_a', 'new_m_w_branch_b', 'new_m_w_out', 'new_m_g_ffn', 'new_m_w_ffn_in', 'new_m_w_ffn_out', 'new_m_g_final', 'new_v_g_mix', 'new_v_w_in', 'new_v_lb_logits', 'new_v_hgrn_norm_g', 'new_v_pool_w', 'new_v_pool_scale', 'new_v_w_branch_a', 'new_v_w_branch_b', 'new_v_w_out', 'new_v_g_ffn', 'new_v_w_ffn_in', 'new_v_w_ffn_out', 'new_v_g_final']
TWIN_LEAF_KINDS = {'loss': 'loss', 'grad_x': 'grad_x', 'grad_g_mix': 'grad_w', 'grad_w_in': 'grad_w', 'grad_lb_logits': 'grad_w', 'grad_hgrn_norm_g': 'grad_w', 'grad_pool_w': 'grad_w', 'grad_pool_scale': 'grad_w', 'grad_w_branch_a': 'grad_w', 'grad_w_branch_b': 'grad_w', 'grad_w_out': 'grad_w', 'grad_g_ffn': 'grad_w', 'grad_w_ffn_in': 'grad_w', 'grad_w_ffn_out': 'grad_w', 'grad_g_final': 'grad_w', 'delta_g_mix': 'delta_w', 'delta_w_in': 'delta_w', 'delta_lb_logits': 'delta_w', 'delta_hgrn_norm_g': 'delta_w', 'delta_pool_w': 'delta_w', 'delta_pool_scale': 'delta_w', 'delta_w_branch_a': 'delta_w', 'delta_w_branch_b': 'delta_w', 'delta_w_out': 'delta_w', 'delta_g_ffn': 'delta_w', 'delta_w_ffn_in': 'delta_w', 'delta_w_ffn_out': 'delta_w', 'delta_g_final': 'delta_w', 'new_m_g_mix': 'new_m', 'new_m_w_in': 'new_m', 'new_m_lb_logits': 'new_m', 'new_m_hgrn_norm_g': 'new_m', 'new_m_pool_w': 'new_m', 'new_m_pool_scale': 'new_m', 'new_m_w_branch_a': 'new_m', 'new_m_w_branch_b': 'new_m', 'new_m_w_out': 'new_m', 'new_m_g_ffn': 'new_m', 'new_m_w_ffn_in': 'new_m', 'new_m_w_ffn_out': 'new_m', 'new_m_g_final': 'new_m', 'new_v_g_mix': 'new_v', 'new_v_w_in': 'new_v', 'new_v_lb_logits': 'new_v', 'new_v_hgrn_norm_g': 'new_v', 'new_v_pool_w': 'new_v', 'new_v_pool_scale': 'new_v', 'new_v_w_branch_a': 'new_v', 'new_v_w_branch_b': 'new_v', 'new_v_w_out': 'new_v', 'new_v_g_ffn': 'new_v', 'new_v_w_ffn_in': 'new_v', 'new_v_w_ffn_out': 'new_v', 'new_v_g_final': 'new_v'}


def _forward(args):
    return _fwd_reference(*[args[k] for k in FWD_PARAMS])


def _output_shape():
    out = _jax.eval_shape(lambda: _forward(_fwd_setup_inputs(0)))
    return out.shape, out.dtype

N_MICROBATCH = 1
ADAM_LR = 0.001
ADAM_B1 = 0.9
ADAM_B2 = 0.999
ADAM_EPS = 1e-08
ADAM_WD = 0.01
ADAM_STEP = 10
PER_EXAMPLE_BATCH_AXIS = {'x': 0, 'loss_target': 0}
SHARED_INPUTS = []
_WEIGHT_DTYPES = {'g_mix': _jnp.float32, 'w_in': _jnp.float32, 'lb_logits': _jnp.float32, 'hgrn_norm_g': _jnp.float32, 'pool_w': _jnp.float32, 'pool_scale': _jnp.float32, 'w_branch_a': _jnp.float32, 'w_branch_b': _jnp.float32, 'w_out': _jnp.float32, 'g_ffn': _jnp.float32, 'w_ffn_in': _jnp.float32, 'w_ffn_out': _jnp.float32, 'g_final': _jnp.float32}
MOMENT_SCALE = {'g_mix': 1.300090e-01, 'w_in': 4.226164e-02, 'lb_logits': 2.993972e-03, 'hgrn_norm_g': 5.698684e-02, 'pool_w': 8.080571e-02, 'pool_scale': 8.175270e-02, 'w_branch_a': 5.479108e-02, 'w_branch_b': 8.058067e-02, 'w_out': 9.746827e-02, 'g_ffn': 1.197691e-01, 'w_ffn_in': 4.995658e-02, 'w_ffn_out': 8.168744e-02, 'g_final': 3.200010e+01}


def _to_microbatches(a, axis):
    t = _jnp.moveaxis(a, axis, 0)
    t = t.reshape((N_MICROBATCH, t.shape[0] // N_MICROBATCH) + t.shape[1:])
    return _jnp.moveaxis(t, 1, axis + 1)


def setup_inputs(seed: int = 0) -> dict:
    inp = _fwd_setup_inputs(seed)
    key = _jax.random.fold_in(_jax.random.key(seed), 7919)
    shape, _ = _output_shape()
    out = dict(inp)
    out["loss_target"] = _jax.random.normal(_jax.random.fold_in(key, 0), shape, _jnp.float32)
    for i, name in enumerate(TWIN_WEIGHTS):
        w = inp[name].astype(_jnp.float32)
        if MOMENT_SCALE is None:
            s = _jnp.sqrt(_jnp.mean(_jnp.square(w)) + 1e-30)
        else:
            s = MOMENT_SCALE[name]
        km, kv = _jax.random.split(_jax.random.fold_in(key, i + 1))
        out[name] = w
        out["m_" + name] = s * _jax.random.normal(km, w.shape, _jnp.float32)
        out["v_" + name] = (s * s) * _jax.random.uniform(kv, w.shape, _jnp.float32, 0.5, 1.5)
    if N_MICROBATCH > 1:
        for name, axis in PER_EXAMPLE_BATCH_AXIS.items():
            out[name] = _to_microbatches(out[name], axis)
    return {'x': out['x'], 'g_mix': out['g_mix'], 'w_in': out['w_in'], 'lb_logits': out['lb_logits'], 'hgrn_norm_g': out['hgrn_norm_g'], 'pool_w': out['pool_w'], 'pool_scale': out['pool_scale'], 'w_branch_a': out['w_branch_a'], 'w_branch_b': out['w_branch_b'], 'w_out': out['w_out'], 'g_ffn': out['g_ffn'], 'w_ffn_in': out['w_ffn_in'], 'w_ffn_out': out['w_ffn_out'], 'g_final': out['g_final'], 'loss_target': out['loss_target'], 'm_g_mix': out['m_g_mix'], 'm_w_in': out['m_w_in'], 'm_lb_logits': out['m_lb_logits'], 'm_hgrn_norm_g': out['m_hgrn_norm_g'], 'm_pool_w': out['m_pool_w'], 'm_pool_scale': out['m_pool_scale'], 'm_w_branch_a': out['m_w_branch_a'], 'm_w_branch_b': out['m_w_branch_b'], 'm_w_out': out['m_w_out'], 'm_g_ffn': out['m_g_ffn'], 'm_w_ffn_in': out['m_w_ffn_in'], 'm_w_ffn_out': out['m_w_ffn_out'], 'm_g_final': out['m_g_final'], 'v_g_mix': out['v_g_mix'], 'v_w_in': out['v_w_in'], 'v_lb_logits': out['v_lb_logits'], 'v_hgrn_norm_g': out['v_hgrn_norm_g'], 'v_pool_w': out['v_pool_w'], 'v_pool_scale': out['v_pool_scale'], 'v_w_branch_a': out['v_w_branch_a'], 'v_w_branch_b': out['v_w_branch_b'], 'v_w_out': out['v_w_out'], 'v_g_ffn': out['v_g_ffn'], 'v_w_ffn_in': out['v_w_ffn_in'], 'v_w_ffn_out': out['v_w_ffn_out'], 'v_g_final': out['v_g_final']}


def _loss(weights, diff, rest, loss_target):
    with _jax.named_scope("forward"):
        args = {**rest, TWIN_DIFF_INPUT: diff, **{k: w.astype(_WEIGHT_DTYPES[k]) for k, w in weights.items()}}
        y = _forward(args)
    with _jax.named_scope("loss_head"):
        err = _jnp.square(y.astype(_jnp.float32) - loss_target)
        return 0.5 * _jnp.sum(_jnp.mean(err, axis=-1)) if err.ndim else 0.5 * err


def _adamw(w, g, m, v):
    m = ADAM_B1 * m + (1.0 - ADAM_B1) * g
    v = ADAM_B2 * v + (1.0 - ADAM_B2) * _jnp.square(g)
    m_hat = m / (1.0 - ADAM_B1 ** ADAM_STEP)
    v_hat = v / (1.0 - ADAM_B2 ** ADAM_STEP)
    delta = -ADAM_LR * (m_hat / (_jnp.sqrt(v_hat) + ADAM_EPS) + ADAM_WD * w)
    return delta, m, v


def reference(x, g_mix, w_in, lb_logits, hgrn_norm_g, pool_w, pool_scale, w_branch_a, w_branch_b, w_out, g_ffn, w_ffn_in, w_ffn_out, g_final, loss_target, m_g_mix, m_w_in, m_lb_logits, m_hgrn_norm_g, m_pool_w, m_pool_scale, m_w_branch_a, m_w_branch_b, m_w_out, m_g_ffn, m_w_ffn_in, m_w_ffn_out, m_g_final, v_g_mix, v_w_in, v_lb_logits, v_hgrn_norm_g, v_pool_w, v_pool_scale, v_w_branch_a, v_w_branch_b, v_w_out, v_g_ffn, v_w_ffn_in, v_w_ffn_out, v_g_final):
    given = dict(x=x, g_mix=g_mix, w_in=w_in, lb_logits=lb_logits, hgrn_norm_g=hgrn_norm_g, pool_w=pool_w, pool_scale=pool_scale, w_branch_a=w_branch_a, w_branch_b=w_branch_b, w_out=w_out, g_ffn=g_ffn, w_ffn_in=w_ffn_in, w_ffn_out=w_ffn_out, g_final=g_final, loss_target=loss_target, m_g_mix=m_g_mix, m_w_in=m_w_in, m_lb_logits=m_lb_logits, m_hgrn_norm_g=m_hgrn_norm_g, m_pool_w=m_pool_w, m_pool_scale=m_pool_scale, m_w_branch_a=m_w_branch_a, m_w_branch_b=m_w_branch_b, m_w_out=m_w_out, m_g_ffn=m_g_ffn, m_w_ffn_in=m_w_ffn_in, m_w_ffn_out=m_w_ffn_out, m_g_final=m_g_final, v_g_mix=v_g_mix, v_w_in=v_w_in, v_lb_logits=v_lb_logits, v_hgrn_norm_g=v_hgrn_norm_g, v_pool_w=v_pool_w, v_pool_scale=v_pool_scale, v_w_branch_a=v_w_branch_a, v_w_branch_b=v_w_branch_b, v_w_out=v_w_out, v_g_ffn=v_g_ffn, v_w_ffn_in=v_w_ffn_in, v_w_ffn_out=v_w_ffn_out, v_g_final=v_g_final)
    weights = {n: given[n] for n in TWIN_WEIGHTS}
    shared = {n: given[n] for n in SHARED_INPUTS}
    per_example = {n: given[n] for n in ['x']}
    grad_fn = _jax.value_and_grad(_loss, argnums=(0, 1))

    def one_microbatch(ex, loss_target):
        ex = dict(ex)
        diff = ex.pop(TWIN_DIFF_INPUT)
        return grad_fn(weights, diff, {**shared, **ex}, loss_target)

    if N_MICROBATCH == 1:
        loss, (grad_w, grad_x) = one_microbatch(per_example, given["loss_target"])
    else:
        def body(carry, xs):
            loss_sum, grad_sum = carry
            l_k, (gw_k, gx_k) = one_microbatch(xs[0], xs[1])
            with _jax.named_scope("update"):
                return (loss_sum + l_k, _jax.tree.map(_jnp.add, grad_sum, gw_k)), gx_k

        init = (_jnp.zeros((), _jnp.float32), _jax.tree.map(_jnp.zeros_like, weights))
        (loss, grad_w), grad_x = _jax.lax.scan(body, init, (per_example, given["loss_target"]))
    with _jax.named_scope("update"):
        delta_w, new_m, new_v = {}, {}, {}
        for n in TWIN_WEIGHTS:
            delta_w[n], new_m[n], new_v[n] = _adamw(weights[n], grad_w[n], given["m_" + n], given["v_" + n])
    return (loss, grad_x, *[grad_w[n] for n in TWIN_WEIGHTS], *[delta_w[n] for n in TWIN_WEIGHTS],
            *[new_m[n] for n in TWIN_WEIGHTS], *[new_v[n] for n in TWIN_WEIGHTS])
```

```python
import functools

import jax
import jax.numpy as jnp
from jax import lax
from jax.experimental import pallas as pl
from jax.experimental.pallas import tpu as pltpu

F32 = jnp.float32
MX = jnp.bfloat16

D_MODEL = 1024
N_HEADS = 8
HEAD_DIM = 128
CHUNK = 16
POOL_WINDOWS = (2, 4, 8, 16)
POOL_DIM = 256
FF_BLOCK = 704
N_DEV = 8
RMS_EPS = 1e-6
ADAM_LR, ADAM_B1, ADAM_B2, ADAM_EPS, ADAM_WD, ADAM_STEP = 0.001, 0.9, 0.999, 1e-08, 0.01, 10
VMEM_LIMIT = 56 * 1024 * 1024
ANY = pl.BlockSpec(memory_space=pl.ANY)


def _params(sem=None):
    return pltpu.CompilerParams(dimension_semantics=sem, vmem_limit_bytes=VMEM_LIMIT)


def _dot(a, b):
    return lax.dot_general(a.astype(MX), b.astype(MX), (((1,), (0,)), ((), ())), preferred_element_type=F32)


def _dot_nt(a, b):
    return lax.dot_general(a.astype(MX), b.astype(MX), (((1,), (1,)), ((), ())), preferred_element_type=F32)


def _dot_tn(a, b):
    return lax.dot_general(a.astype(MX), b.astype(MX), (((0,), (0,)), ((), ())), preferred_element_type=F32)


def _sig(x):
    return 1.0 / (1.0 + jnp.exp(-x))


def _fold8(v):
    return v.reshape(v.shape[0] // 8, 8, v.shape[1]).sum(axis=0)


def _shift_rows(x, s):
    n = x.shape[0]
    row = lax.broadcasted_iota(jnp.int32, x.shape, 0)
    if s > 0:
        return jnp.where(row >= s, pltpu.roll(x, s, 0), 0.0)
    return jnp.where(row < n + s, pltpu.roll(x, n + s, 0), 0.0)


def _cumsum_rows(x, rev):
    s = 1
    while s < x.shape[0]:
        x = x + _shift_rows(x, -s if rev else s)
        s *= 2
    return x


def _rms_inproj(x2, g_mix, w_in_g):
    T = x2.shape[0]
    tm = min(512, T)

    def body(x_ref, g_ref, w_ref, proj_ref, u_ref, u_sc):
        @pl.when(pl.program_id(1) == 0)
        def _():
            x = x_ref[...]
            r = lax.rsqrt(jnp.mean(x * x, axis=-1, keepdims=True) + RMS_EPS)
            u = (x * r * g_ref[...]).astype(MX)
            u_sc[...] = u
            u_ref[...] = u
        proj_ref[...] = jnp.dot(u_sc[...], w_ref[...], preferred_element_type=F32)

    return pl.pallas_call(
        body, name="rms_inproj", grid=(T // tm, N_DEV),
        in_specs=[pl.BlockSpec((tm, D_MODEL), lambda i, k: (i, 0)),
                  pl.BlockSpec((1, D_MODEL), lambda i, k: (0, 0)),
                  pl.BlockSpec((None, D_MODEL, D_MODEL), lambda i, k: (k, 0, 0))],
        out_specs=[pl.BlockSpec((None, tm, D_MODEL), lambda i, k: (k, i, 0)),
                   pl.BlockSpec((tm, D_MODEL), lambda i, k: (i, 0))],
        out_shape=[jax.ShapeDtypeStruct((N_DEV, T, D_MODEL), F32), jax.ShapeDtypeStruct((T, D_MODEL), MX)],
        scratch_shapes=[pltpu.VMEM((tm, D_MODEL), MX)],
        compiler_params=_params(("parallel", "arbitrary")),
    )(x2, g_mix, w_in_g)


def _chunk_gates(q_r, f_r, lb_row, rev):
    sq = _sig(q_r)
    q = q_r * sq
    sg = _sig(f_r)
    f = lb_row + (1.0 - lb_row) * sg
    k = 1.0 - f
    lf = jnp.log(f)
    b = _cumsum_rows(lf, rev)
    bl = jnp.sum(lf, axis=0, keepdims=True)
    eb = jnp.exp(b)
    enb = jnp.exp(-b)
    eend = jnp.exp(bl - b)
    dec = jnp.exp(bl)
    return dict(sq=sq, q=q, sg=sg, f=f, k=k, eb=eb, enb=enb, eend=eend, dec=dec,
                P=q * eb, Kt=k * enb, Ke=k * eend)


def _chunk_mask(rev):
    ri = lax.broadcasted_iota(jnp.int32, (CHUNK, CHUNK), 0)
    ci = lax.broadcasted_iota(jnp.int32, (CHUNK, CHUNK), 1)
    return (ci >= ri) if rev else (ci <= ri)


def _lower_bounds(lbl_ref):
    return _sig(lbl_ref[0:1, :] - lbl_ref[1:2, :]), _sig(lbl_ref[2:3, :] - lbl_ref[3:4, :])


def _hgrn_fwd(proj, lbl, norm_g, B, L):
    nC = L // CHUNK

    def body(q_ref, ff_ref, fb_ref, i_ref, og_ref, lbl_ref, ng_ref, o_ref, ya_ref, of_sc, ob_sc, sf_sc, sb_sc):
        lb_f, lb_b = _lower_bounds(lbl_ref)
        sf_sc[...] = jnp.zeros_like(sf_sc)
        sb_sc[...] = jnp.zeros_like(sb_sc)

        def one(c, f_ref, lb_row, s_sc, o_sc, rev):
            rows = pl.ds(pl.multiple_of(c * CHUNK, CHUNK), CHUNK)
            g = _chunk_gates(q_ref[rows, :], f_ref[rows, :], lb_row, rev)
            v = i_ref[rows, :]
            st = s_sc[...]
            a = jnp.where(_chunk_mask(rev), _dot_nt(g["P"], g["Kt"]), 0.0)
            o_sc[rows, :] = _dot(a, v) + _dot_nt(g["P"], st)
            s_sc[...] = st * g["dec"] + _dot_tn(v, g["Ke"])

        def step(c, carry):
            one(c, ff_ref, lb_f, sf_sc, of_sc, False)
            one(nC - 1 - c, fb_ref, lb_b, sb_sc, ob_sc, True)
            return carry

        lax.fori_loop(0, nC, step, 0)
        o = of_sc[...] + ob_sc[...]
        o_ref[...] = o
        on = o * lax.rsqrt(jnp.mean(o * o, axis=-1, keepdims=True) + RMS_EPS)
        og = og_ref[...]
        ya_ref[...] = ((on * ng_ref[...]) * (og * _sig(og))).astype(MX)

    def blk(s):
        return pl.BlockSpec((None, None, L, HEAD_DIM), lambda b, h, s=s: (s, b, 0, h))

    out_blk = pl.BlockSpec((None, L, HEAD_DIM), lambda b, h: (b, 0, h))
    return pl.pallas_call(
        body, name="hgrn_fwd", grid=(B, N_HEADS),
        in_specs=[blk(0), blk(1), blk(2), blk(3), blk(4),
                  pl.BlockSpec((4, HEAD_DIM), lambda b, h: (0, h)),
                  pl.BlockSpec((1, HEAD_DIM), lambda b, h: (0, h))],
        out_specs=[out_blk, out_blk],
        out_shape=[jax.ShapeDtypeStruct((B, L, D_MODEL), F32), jax.ShapeDtypeStruct((B, L, D_MODEL), MX)],
        scratch_shapes=[pltpu.VMEM((L, HEAD_DIM), F32), pltpu.VMEM((L, HEAD_DIM), F32),
                        pltpu.VMEM((HEAD_DIM, HEAD_DIM), F32), pltpu.VMEM((HEAD_DIM, HEAD_DIM), F32)],
        compiler_params=_params(("parallel", "parallel")),
    )(proj, proj, proj, proj, proj, lbl, norm_g)


POOL_PAD = 8


def _pool_window(p, ext_sc, half, adjoint):
    L = p.shape[0]
    n = L + 2 * POOL_PAD
    ext_sc[0:POOL_PAD, :] = jnp.zeros((POOL_PAD, p.shape[1]), F32)
    ext_sc[POOL_PAD + L:n, :] = jnp.zeros((POOL_PAD, p.shape[1]), F32)
    ext_sc[POOL_PAD:POOL_PAD + L, :] = p
    x = ext_sc[...]
    s = x + pltpu.roll(x, 1 if adjoint else n - 1, 0)
    w = 1
    while w < half:
        s = pltpu.roll(s, w, 0) + pltpu.roll(s, n - w, 0)
        w *= 2
    ext_sc[...] = s
    return ext_sc[POOL_PAD:POOL_PAD + L, :]


def _pool_count(L, half):
    t = lax.broadcasted_iota(jnp.int32, (L, 1), 0)
    lo = jnp.clip(t - half + 1, 0, L)
    hi = jnp.clip(t + half + 1, 0, L)
    return (hi - lo).astype(F32)


def _pool_fwd(proj, pool_w_full, pool_scale, B, L):
    def body(p_ref, w_ref, s_ref, yb_ref, ext_sc):
        for g, win in enumerate(POOL_WINDOWS):
            cols = slice(g * POOL_DIM, (g + 1) * POOL_DIM)
            p = p_ref[:, cols]
            y = _pool_window(p, ext_sc, win // 2, False) / _pool_count(L, win // 2) - p
            yb_ref[:, cols] = (_dot(y, w_ref[g]) * s_ref[:, cols]).astype(MX)

    return pl.pallas_call(
        body, name="pool_fwd", grid=(B,),
        in_specs=[pl.BlockSpec((None, None, L, D_MODEL), lambda b: (5, b, 0, 0)),
                  pl.BlockSpec((4, POOL_DIM, POOL_DIM), lambda b: (0, 0, 0)),
                  pl.BlockSpec((1, D_MODEL), lambda b: (0, 0))],
        out_specs=pl.BlockSpec((None, L, D_MODEL), lambda b: (b, 0, 0)),
        out_shape=jax.ShapeDtypeStruct((B, L, D_MODEL), MX),
        scratch_shapes=[pltpu.VMEM((L + 2 * POOL_PAD, POOL_DIM), F32)],
        compiler_params=_params(("parallel",)),
    )(proj, pool_w_full, pool_scale)


def _merge_out(x2, proj, ya, yb, wa, wb, wo):
    T = x2.shape[0]
    tm = min(512, T)

    def body(x_ref, ga_ref, gb_ref, ya_ref, yb_ref, wa_ref, wb_ref, wo_ref, za_ref, zb_ref, mg_ref, h_ref):
        za = jnp.dot(ya_ref[...], wa_ref[...], preferred_element_type=F32)
        zb = jnp.dot(yb_ref[...], wb_ref[...], preferred_element_type=F32)
        mg = (_sig(ga_ref[...]) * za + _sig(gb_ref[...]) * zb).astype(MX)
        za_ref[...] = za
        zb_ref[...] = zb
        mg_ref[...] = mg
        h_ref[...] = x_ref[...] + jnp.dot(mg, wo_ref[...], preferred_element_type=F32)

    tile = pl.BlockSpec((tm, D_MODEL), lambda i: (i, 0))
    wspec = pl.BlockSpec((D_MODEL, D_MODEL), lambda i: (0, 0))
    return pl.pallas_call(
        body, name="merge_out", grid=(T // tm,),
        in_specs=[tile,
                  pl.BlockSpec((None, tm, D_MODEL), lambda i: (6, i, 0)),
                  pl.BlockSpec((None, tm, D_MODEL), lambda i: (7, i, 0)),
                  tile, tile, wspec, wspec, wspec],
        out_specs=[tile, tile, tile, tile],
        out_shape=[jax.ShapeDtypeStruct((T, D_MODEL), F32), jax.ShapeDtypeStruct((T, D_MODEL), F32),
                   jax.ShapeDtypeStruct((T, D_MODEL), MX), jax.ShapeDtypeStruct((T, D_MODEL), F32)],
        compiler_params=_params(("parallel",)),
    )(x2, proj, proj, ya, yb, wa, wb, wo)


def _ffn_fwd_loss(h, tgt, g_ffn, g_final, wfi_g, wfo_g):
    T = h.shape[0]
    tm = min(512, T)
    nT = T // tm

    def body(h_ref, t_ref, gf_ref, gl_ref, wg_ref, wu_ref, wo_ref,
             gate_ref, up_ref, hid_ref, u2_ref, dh2_ref, dh2b_ref, loss_ref, dgl_ref, u2_sc, acc_sc):
        i, j = pl.program_id(0), pl.program_id(1)

        @pl.when(j == 0)
        def _():
            hh = h_ref[...]
            r = lax.rsqrt(jnp.mean(hh * hh, axis=-1, keepdims=True) + RMS_EPS)
            u2 = (hh * r * gf_ref[...]).astype(MX)
            u2_sc[...] = u2
            u2_ref[...] = u2
            acc_sc[...] = jnp.zeros_like(acc_sc)

        @pl.when((i == 0) & (j == 0))
        def _():
            dgl_ref[...] = jnp.zeros_like(dgl_ref)

        gate = jnp.dot(u2_sc[...], wg_ref[...], preferred_element_type=F32)
        up = jnp.dot(u2_sc[...], wu_ref[...], preferred_element_type=F32)
        hid = ((gate * _sig(gate)) * up).astype(MX)
        gate_ref[...] = gate
        up_ref[...] = up
        hid_ref[...] = hid
        acc_sc[...] += jnp.dot(hid, wo_ref[...], preferred_element_type=F32)

        @pl.when(j == 3)
        def _():
            h2 = h_ref[...] + acc_sc[...]
            r = lax.rsqrt(jnp.mean(h2 * h2, axis=-1, keepdims=True) + RMS_EPS)
            hn = h2 * r
            gl = gl_ref[...]
            err = hn * gl - t_ref[...]
            tok = jnp.mean(err * err, axis=-1, keepdims=True)
            loss_ref[...] = jnp.full(loss_ref.shape, 0.5 * jnp.sum(tok), F32)
            dy = err * (1.0 / D_MODEL)
            dgl_ref[...] += _fold8(dy * hn)
            a = dy * gl
            dh2 = r * a - hn * (r * jnp.mean(a * hn, axis=-1, keepdims=True))
            dh2_ref[...] = dh2
            dh2b_ref[...] = dh2.astype(MX)

    tile = pl.BlockSpec((tm, D_MODEL), lambda i, j: (i, 0))
    vec = pl.BlockSpec((1, D_MODEL), lambda i, j: (0, 0))
    ftile = pl.BlockSpec((None, tm, FF_BLOCK), lambda i, j: (j, i, 0))
    return pl.pallas_call(
        body, name="ffn_fwd_loss", grid=(nT, 4),
        in_specs=[tile, tile, vec, vec,
                  pl.BlockSpec((None, D_MODEL, FF_BLOCK), lambda i, j: (j, 0, 0)),
                  pl.BlockSpec((None, D_MODEL, FF_BLOCK), lambda i, j: (j + 4, 0, 0)),
                  pl.BlockSpec((None, FF_BLOCK, D_MODEL), lambda i, j: (j, 0, 0))],
        out_specs=[ftile, ftile, ftile, tile, tile, tile,
                   pl.BlockSpec((None, 8, 128), lambda i, j: (i, 0, 0)),
                   pl.BlockSpec((8, D_MODEL), lambda i, j: (0, 0))],
        out_shape=[jax.ShapeDtypeStruct((4, T, FF_BLOCK), F32), jax.ShapeDtypeStruct((4, T, FF_BLOCK), F32),
                   jax.ShapeDtypeStruct((4, T, FF_BLOCK), MX), jax.ShapeDtypeStruct((T, D_MODEL), MX),
                   jax.ShapeDtypeStruct((T, D_MODEL), F32), jax.ShapeDtypeStruct((T, D_MODEL), MX),
                   jax.ShapeDtypeStruct((nT, 8, 128), F32), jax.ShapeDtypeStruct((8, D_MODEL), F32)],
        scratch_shapes=[pltpu.VMEM((tm, D_MODEL), MX), pltpu.VMEM((tm, D_MODEL), F32)],
        compiler_params=_params(("arbitrary", "arbitrary")),
    )(h, tgt, g_ffn, g_final, wfi_g, wfi_g, wfo_g)


def _ffn_bwd(h, dh2, dh2b, gate, up, g_ffn, wfi_g, wfo_g):
    T = h.shape[0]
    tm = min(512, T)

    def body(h_ref, dh2_ref, dh2b_ref, gate_ref, up_ref, gf_ref, wg_ref, wu_ref, wo_ref,
             dgu_ref, dh_ref, dhb_ref, dgf_ref, acc_sc):
        i, j = pl.program_id(0), pl.program_id(1)

        @pl.when(j == 0)
        def _():
            acc_sc[...] = jnp.zeros_like(acc_sc)

        @pl.when((i == 0) & (j == 0))
        def _():
            dgf_ref[...] = jnp.zeros_like(dgf_ref)

        dhid = _dot_nt(dh2b_ref[...], wo_ref[...])
        gate, up = gate_ref[...], up_ref[...]
        sg = _sig(gate)
        dgate = (dhid * up * (sg * (1.0 + gate * (1.0 - sg)))).astype(MX)
        dup = (dhid * (gate * sg)).astype(MX)
        dgu_ref[0] = dgate
        dgu_ref[1] = dup
        acc_sc[...] += _dot_nt(dgate, wg_ref[...]) + _dot_nt(dup, wu_ref[...])

        @pl.when(j == 3)
        def _():
            hh = h_ref[...]
            r = lax.rsqrt(jnp.mean(hh * hh, axis=-1, keepdims=True) + RMS_EPS)
            hn = hh * r
            du2 = acc_sc[...]
            dgf_ref[...] += _fold8(du2 * hn)
            a = du2 * gf_ref[...]
            dh = dh2_ref[...] + r * a - hn * (r * jnp.mean(a * hn, axis=-1, keepdims=True))
            dh_ref[...] = dh
            dhb_ref[...] = dh.astype(MX)

    tile = pl.BlockSpec((tm, D_MODEL), lambda i, j: (i, 0))
    ftile = pl.BlockSpec((None, tm, FF_BLOCK), lambda i, j: (j, i, 0))
    return pl.pallas_call(
        body, name="ffn_bwd", grid=(T // tm, 4),
        in_specs=[tile, tile, tile, ftile, ftile,
                  pl.BlockSpec((1, D_MODEL), lambda i, j: (0, 0)),
                  pl.BlockSpec((None, D_MODEL, FF_BLOCK), lambda i, j: (j, 0, 0)),
                  pl.BlockSpec((None, D_MODEL, FF_BLOCK), lambda i, j: (j + 4, 0, 0)),
                  pl.BlockSpec((None, FF_BLOCK, D_MODEL), lambda i, j: (j, 0, 0))],
        out_specs=[pl.BlockSpec((2, None, tm, FF_BLOCK), lambda i, j: (0, j, i, 0)),
                   tile, tile, pl.BlockSpec((8, D_MODEL), lambda i, j: (0, 0))],
        out_shape=[jax.ShapeDtypeStruct((2, 4, T, FF_BLOCK), MX),
                   jax.ShapeDtypeStruct((T, D_MODEL), F32), jax.ShapeDtypeStruct((T, D_MODEL), MX),
                   jax.ShapeDtypeStruct((8, D_MODEL), F32)],
        scratch_shapes=[pltpu.VMEM((tm, D_MODEL), F32)],
        compiler_params=_params(("arbitrary", "arbitrary")),
    )(h, dh2, dh2b, gate, up, g_ffn, wfi_g, wfi_g, wfo_g)


def _merge_bwd(dhb, proj, za, zb, wa, wb, wo):
    T = dhb.shape[0]
    tm = min(512, T)

    def body(dh_ref, ga_ref, gb_ref, za_ref, zb_ref, wa_ref, wb_ref, wo_ref,
             dza_ref, dzb_ref, dgab_ref, dya_ref, dyb_ref):
        dm = _dot_nt(dh_ref[...], wo_ref[...])
        sa, sb = _sig(ga_ref[...]), _sig(gb_ref[...])
        dza = (dm * sa).astype(MX)
        dzb = (dm * sb).astype(MX)
        dza_ref[...] = dza
        dzb_ref[...] = dzb
        dgab_ref[0] = (dm * za_ref[...] * (sa * (1.0 - sa))).astype(MX)
        dgab_ref[1] = (dm * zb_ref[...] * (sb * (1.0 - sb))).astype(MX)
        dya_ref[...] = _dot_nt(dza, wa_ref[...])
        dyb_ref[...] = _dot_nt(dzb, wb_ref[...])

    tile = pl.BlockSpec((tm, D_MODEL), lambda i: (i, 0))
    wspec = pl.BlockSpec((D_MODEL, D_MODEL), lambda i: (0, 0))
    return pl.pallas_call(
        body, name="merge_bwd", grid=(T // tm,),
        in_specs=[tile,
                  pl.BlockSpec((None, tm, D_MODEL), lambda i: (6, i, 0)),
                  pl.BlockSpec((None, tm, D_MODEL), lambda i: (7, i, 0)),
                  tile, tile, wspec, wspec, wspec],
        out_specs=[tile, tile, pl.BlockSpec((2, tm, D_MODEL), lambda i: (0, i, 0)), tile, tile],
        out_shape=[jax.ShapeDtypeStruct((T, D_MODEL), MX), jax.ShapeDtypeStruct((T, D_MODEL), MX),
                   jax.ShapeDtypeStruct((2, T, D_MODEL), MX),
                   jax.ShapeDtypeStruct((T, D_MODEL), F32), jax.ShapeDtypeStruct((T, D_MODEL), F32)],
        compiler_params=_params(("parallel",)),
    )(dhb, proj, proj, za, zb, wa, wb, wo)


def _pool_bwd(proj, dyb, pool_w_full, pool_scale, B, L):
    def body(p_ref, dy_ref, w_ref, s_ref, dp_ref, dw_ref, ds_ref, ext_sc):
        for g, win in enumerate(POOL_WINDOWS):
            cols = slice(g * POOL_DIM, (g + 1) * POOL_DIM)
            p = p_ref[:, cols]
            cnt = _pool_count(L, win // 2)
            y = _pool_window(p, ext_sc, win // 2, False) / cnt - p
            z = _dot(y, w_ref[g])
            dyb_g = dy_ref[:, cols]
            ds_ref[:, cols] = jnp.sum(dyb_g * z, axis=0, keepdims=True)
            dz = dyb_g * s_ref[:, cols]
            dw_ref[g] = _dot_tn(y, dz)
            dy = _dot_nt(dz, w_ref[g])
            dp_ref[:, cols] = (_pool_window(dy / cnt, ext_sc, win // 2, True) - dy).astype(MX)

    seq = pl.BlockSpec((None, L, D_MODEL), lambda b: (b, 0, 0))
    return pl.pallas_call(
        body, name="pool_bwd", grid=(B,),
        in_specs=[pl.BlockSpec((None, None, L, D_MODEL), lambda b: (5, b, 0, 0)), seq,
                  pl.BlockSpec((4, POOL_DIM, POOL_DIM), lambda b: (0, 0, 0)),
                  pl.BlockSpec((1, D_MODEL), lambda b: (0, 0))],
        out_specs=[seq, pl.BlockSpec((None, 4, POOL_DIM, POOL_DIM), lambda b: (b, 0, 0, 0)),
                   pl.BlockSpec((None, 1, D_MODEL), lambda b: (b, 0, 0))],
        out_shape=[jax.ShapeDtypeStruct((B, L, D_MODEL), MX),
                   jax.ShapeDtypeStruct((B, 4, POOL_DIM, POOL_DIM), F32),
                   jax.ShapeDtypeStruct((B, 1, D_MODEL), F32)],
        scratch_shapes=[pltpu.VMEM((L + 2 * POOL_PAD, POOL_DIM), F32)],
        compiler_params=_params(("parallel",)),
    )(proj, dyb, pool_w_full, pool_scale)


def _hgrn_bwd(proj, o, dya, lbl, norm_g, B, L):
    nC = L // CHUNK

    def body(q_ref, ff_ref, fb_ref, i_ref, og_ref, o_ref, dy_ref, lbl_ref, ng_ref,
             dp_ref, dlb_ref, dng_ref, do_sc, ckf_sc, ckb_sc, sf_sc, sb_sc, dlf_sc, dlbk_sc,
             dqf_sc, dvf_sc, dqb_sc, dvb_sc):
        lb_f, lb_b = _lower_bounds(lbl_ref)
        o_ = o_ref[...]
        r = lax.rsqrt(jnp.mean(o_ * o_, axis=-1, keepdims=True) + RMS_EPS)
        on = o_ * r
        og = og_ref[...]
        sog = _sig(og)
        dy = dy_ref[...]
        ng = ng_ref[...]
        dp_ref[4] = (dy * (on * ng) * (sog * (1.0 + og * (1.0 - sog)))).astype(MX)
        dn = dy * (og * sog)
        dng_ref[...] = jnp.sum(dn * on, axis=0, keepdims=True)
        don = dn * ng
        do_sc[...] = r * don - on * (r * jnp.mean(don * on, axis=-1, keepdims=True))

        sf_sc[...] = jnp.zeros_like(sf_sc)
        sb_sc[...] = jnp.zeros_like(sb_sc)

        def state_one(c, f_ref, lb_row, s_sc, ck_sc, rev):
            rows = pl.ds(pl.multiple_of(c * CHUNK, CHUNK), CHUNK)
            g = _chunk_gates(q_ref[rows, :], f_ref[rows, :], lb_row, rev)
            st = s_sc[...]
            ck_sc[c] = st.astype(MX)
            s_sc[...] = st * g["dec"] + _dot_tn(i_ref[rows, :], g["Ke"])

        def state_step(c, carry):
            state_one(c, ff_ref, lb_f, sf_sc, ckf_sc, False)
            state_one(nC - 1 - c, fb_ref, lb_b, sb_sc, ckb_sc, True)
            return carry

        lax.fori_loop(0, nC, state_step, 0)

        sf_sc[...] = jnp.zeros_like(sf_sc)
        sb_sc[...] = jnp.zeros_like(sb_sc)
        dlf_sc[...] = jnp.zeros_like(dlf_sc)
        dlbk_sc[...] = jnp.zeros_like(dlbk_sc)

        def back_one(c, f_ref, lb_row, ds_sc, ck_sc, dlb_sc, dq_sc, dv_sc, slot, rev):
            rows = pl.ds(pl.multiple_of(c * CHUNK, CHUNK), CHUNK)
            q_r, f_r = q_ref[rows, :], f_ref[rows, :]
            g = _chunk_gates(q_r, f_r, lb_row, rev)
            v = i_ref[rows, :]
            do = do_sc[rows, :]
            st = ck_sc[c].astype(F32)
            dst = ds_sc[...]
            mask = _chunk_mask(rev)
            a = jnp.where(mask, _dot_nt(g["P"], g["Kt"]), 0.0)
            da = jnp.where(mask, _dot_nt(do, v), 0.0)
            dv = _dot_tn(a, do) + _dot_nt(g["Ke"], dst)
            dke = _dot(v, dst)
            dP = _dot(da, g["Kt"]) + _dot(do, st)
            dkt = _dot_tn(da, g["P"])
            ds_sc[...] = dst * g["dec"] + _dot_tn(do, g["P"])
            dbl = g["dec"] * jnp.sum(dst * st, axis=0, keepdims=True) + jnp.sum(dke * g["Ke"], axis=0, keepdims=True)
            db = dP * g["P"] - dkt * g["Kt"] - dke * g["Ke"]
            dq = dP * g["eb"]
            dk = dkt * g["enb"] + dke * g["eend"]
            dlf = _cumsum_rows(db, not rev) + dbl
            df = dlf / g["f"] - dk
            dlb_sc[...] += jnp.sum(df * (1.0 - g["sg"]), axis=0, keepdims=True)
            dfr = df * (1.0 - lb_row) * (g["sg"] * (1.0 - g["sg"]))
            dqr = dq * (g["sq"] * (1.0 + q_r * (1.0 - g["sq"])))
            dp_ref[slot, rows, :] = dfr.astype(MX)
            dq_sc[rows, :] = dqr
            dv_sc[rows, :] = dv

        def back_step(c, carry):
            back_one(nC - 1 - c, ff_ref, lb_f, sf_sc, ckf_sc, dlf_sc, dqf_sc, dvf_sc, 1, False)
            back_one(c, fb_ref, lb_b, sb_sc, ckb_sc, dlbk_sc, dqb_sc, dvb_sc, 2, True)
            return carry

        lax.fori_loop(0, nC, back_step, 0)
        dp_ref[0] = (dqf_sc[...] + dqb_sc[...]).astype(MX)
        dp_ref[3] = (dvf_sc[...] + dvb_sc[...]).astype(MX)
        dlb_ref[0:1, :] = dlf_sc[...]
        dlb_ref[1:2, :] = dlbk_sc[...]

    def blk(s):
        return pl.BlockSpec((None, None, L, HEAD_DIM), lambda b, h, s=s: (s, b, 0, h))

    seq = pl.BlockSpec((None, L, HEAD_DIM), lambda b, h: (b, 0, h))
    return pl.pallas_call(
        body, name="hgrn_bwd", grid=(B, N_HEADS),
        in_specs=[blk(0), blk(1), blk(2), blk(3), blk(4), seq, seq,
                  pl.BlockSpec((4, HEAD_DIM), lambda b, h: (0, h)),
                  pl.BlockSpec((1, HEAD_DIM), lambda b, h: (0, h))],
        out_specs=[pl.BlockSpec((5, None, L, HEAD_DIM), lambda b, h: (0, b, 0, h)),
                   pl.BlockSpec((None, 2, HEAD_DIM), lambda b, h: (b, 0, h)),
                   pl.BlockSpec((None, 1, HEAD_DIM), lambda b, h: (b, 0, h))],
        out_shape=[jax.ShapeDtypeStruct((5, B, L, D_MODEL), MX),
                   jax.ShapeDtypeStruct((B, 2, D_MODEL), F32),
                   jax.ShapeDtypeStruct((B, 1, D_MODEL), F32)],
        scratch_shapes=[pltpu.VMEM((L, HEAD_DIM), F32),
                        pltpu.VMEM((nC, HEAD_DIM, HEAD_DIM), MX), pltpu.VMEM((nC, HEAD_DIM, HEAD_DIM), MX),
                        pltpu.VMEM((HEAD_DIM, HEAD_DIM), F32), pltpu.VMEM((HEAD_DIM, HEAD_DIM), F32),
                        pltpu.VMEM((1, HEAD_DIM), F32), pltpu.VMEM((1, HEAD_DIM), F32)]
                       + [pltpu.VMEM((L, HEAD_DIM), F32)] * 4,
        compiler_params=_params(("parallel", "parallel")),
    )(proj, proj, proj, proj, proj, o, dya, lbl, norm_g)


def _dproj_select(s, a5_ref, p_ref, g2_ref):
    return jnp.where(s < 5, a5_ref[...], jnp.where(s == 5, p_ref[...], g2_ref[...]))


def _dproj_specs(tm, tile_axis):
    def ix(args):
        return args[tile_axis], args[1 - tile_axis]
    a5 = pl.BlockSpec((None, tm, D_MODEL), lambda *a: (jnp.minimum(ix(a)[1], 4), ix(a)[0], 0))
    p = pl.BlockSpec((tm, D_MODEL), lambda *a: (ix(a)[0], 0))
    g2 = pl.BlockSpec((None, tm, D_MODEL), lambda *a: (jnp.clip(ix(a)[1] - 6, 0, 1), ix(a)[0], 0))
    return [a5, p, g2]


def _inproj_bwd(x2, dh, dproj5, dp, dgab, g_mix, w_in_g):
    T = x2.shape[0]
    tm = min(512, T)

    def body(a5_ref, p_ref, g2_ref, w_ref, x_ref, dh_ref, g_ref, dx_ref, dg_ref, acc_sc):
        i, s = pl.program_id(0), pl.program_id(1)

        @pl.when(s == 0)
        def _():
            acc_sc[...] = jnp.zeros_like(acc_sc)

        @pl.when((i == 0) & (s == 0))
        def _():
            dg_ref[...] = jnp.zeros_like(dg_ref)

        acc_sc[...] += _dot_nt(_dproj_select(s, a5_ref, p_ref, g2_ref), w_ref[...])

        @pl.when(s == N_DEV - 1)
        def _():
            x = x_ref[...]
            r = lax.rsqrt(jnp.mean(x * x, axis=-1, keepdims=True) + RMS_EPS)
            xn = x * r
            du = acc_sc[...]
            dg_ref[...] += _fold8(du * xn)
            a = du * g_ref[...]
            dx_ref[...] = dh_ref[...] + r * a - xn * (r * jnp.mean(a * xn, axis=-1, keepdims=True))

    tile = pl.BlockSpec((tm, D_MODEL), lambda i, s: (i, 0))
    return pl.pallas_call(
        body, name="inproj_bwd", grid=(T // tm, N_DEV),
        in_specs=_dproj_specs(tm, 0) + [pl.BlockSpec((None, D_MODEL, D_MODEL), lambda i, s: (s, 0, 0)),
                                        tile, tile, pl.BlockSpec((1, D_MODEL), lambda i, s: (0, 0))],
        out_specs=[tile, pl.BlockSpec((8, D_MODEL), lambda i, s: (0, 0))],
        out_shape=[jax.ShapeDtypeStruct((T, D_MODEL), F32), jax.ShapeDtypeStruct((8, D_MODEL), F32)],
        scratch_shapes=[pltpu.VMEM((tm, D_MODEL), F32)],
        compiler_params=_params(("arbitrary", "arbitrary")),
    )(dproj5, dp, dgab, w_in_g, x2, dh, g_mix)


def _wgrad_in(u, dproj5, dp, dgab):
    T = u.shape[0]
    tm = min(512, T)

    def body(a5_ref, p_ref, g2_ref, u_ref, out_ref, acc_sc):
        s, t = pl.program_id(0), pl.program_id(1)

        @pl.when(t == 0)
        def _():
            acc_sc[...] = jnp.zeros_like(acc_sc)

        acc_sc[...] += _dot_tn(u_ref[...], _dproj_select(s, a5_ref, p_ref, g2_ref))

        @pl.when(t == pl.num_programs(1) - 1)
        def _():
            out_ref[...] = acc_sc[...].astype(MX)

    return pl.pallas_call(
        body, name="wgrad_in", grid=(N_DEV, T // tm),
        in_specs=_dproj_specs(tm, 1) + [pl.BlockSpec((tm, D_MODEL), lambda s, t: (t, 0))],
        out_specs=pl.BlockSpec((None, D_MODEL, D_MODEL), lambda s, t: (s, 0, 0)),
        out_shape=jax.ShapeDtypeStruct((N_DEV, D_MODEL, D_MODEL), MX),
        scratch_shapes=[pltpu.VMEM((D_MODEL, D_MODEL), F32)],
        compiler_params=_params(("parallel", "arbitrary")),
    )(dproj5, dp, dgab, u)


def _wgrad(a, g, name):
    Ba, T, K = a.shape
    Bg, _, Nn = g.shape
    nb = max(Ba, Bg)
    tm = min(512, T)
    nt = T // tm

    def body(a_ref, g_ref, out_ref, acc_sc):
        t = pl.program_id(1)

        @pl.when(t == 0)
        def _():
            acc_sc[...] = jnp.zeros_like(acc_sc)

        acc_sc[...] += _dot_tn(a_ref[...], g_ref[...])

        @pl.when(t == nt - 1)
        def _():
            out_ref[...] = acc_sc[...].astype(MX)

    return pl.pallas_call(
        body, name=name, grid=(nb, nt),
        in_specs=[pl.BlockSpec((None, tm, K), lambda s, t: (s if Ba > 1 else 0, t, 0)),
                  pl.BlockSpec((None, tm, Nn), lambda s, t: (s if Bg > 1 else 0, t, 0))],
        out_specs=pl.BlockSpec((None, K, Nn), lambda s, t: (s, 0, 0)),
        out_shape=jax.ShapeDtypeStruct((nb, K, Nn), MX),
        scratch_shapes=[pltpu.VMEM((K, Nn), F32)],
        compiler_params=_params(("parallel", "arbitrary")),
    )(a, g)


def _mesh_pos():
    return lax.axis_index("x"), lax.axis_index("y"), lax.axis_index("c")


def _all_gather(shards):
    n = len(shards)

    def body(*refs):
        xs, outs = refs[:n], refs[n:2 * n]
        send_sems, recv_sems, local_sems = refs[2 * n:]
        x, y, c = _mesh_pos()
        me, sibling = (x, y, c), (x, y, 1 - c)
        chips = [(1 - x, y), (x, 1 - y), (1 - x, 1 - y)]

        def copy(a, k, block, to, src=None):
            slot = outs[a].at[4 * block[0] + 2 * block[1] + block[2]]
            return pltpu.make_async_remote_copy(
                src_ref=slot if src is None else src, dst_ref=slot,
                send_sem=send_sems.at[7 * a + k], recv_sem=recv_sems.at[7 * a + k],
                device_id=to, device_id_type=pl.DeviceIdType.MESH)

        mine = [pltpu.make_async_copy(xs[a], outs[a].at[4 * x + 2 * y + c], local_sems.at[a]) for a in range(n)]
        for cp in mine:
            cp.start()
        first = []
        for a in range(n):
            first.append(copy(a, 0, me, sibling, src=xs[a]))
            first += [copy(a, 1 + j, me, (*chip, c), src=xs[a]) for j, chip in enumerate(chips)]
        for cp in first:
            cp.start()
        passed = []
        for j, chip in enumerate(chips):
            for a in range(n):
                copy(a, 1 + j, (*chip, c), me).wait_recv()
                fwd = copy(a, 4 + j, (*chip, c), sibling)
                fwd.start()
                passed.append(fwd)
        for a in range(n):
            copy(a, 0, sibling, me).wait_recv()
            for j, chip in enumerate(chips):
                copy(a, 4 + j, (*chip, 1 - c), me).wait_recv()
        for cp in first + passed:
            cp.wait_send()
        for cp in mine:
            cp.wait()

    return pl.pallas_call(
        body, name="all_gather_weights",
        in_specs=[ANY] * n, out_specs=[ANY] * n,
        out_shape=[jax.ShapeDtypeStruct((N_DEV,) + s.shape, s.dtype) for s in shards],
        scratch_shapes=[pltpu.SemaphoreType.DMA((7 * n,)), pltpu.SemaphoreType.DMA((7 * n,)),
                        pltpu.SemaphoreType.DMA((n,))],
    )(*shards)


def _device_of(p):
    return (p // 4, (p // 2) % 2, p % 2)


def _scatter_slices(grads):
    n = len(grads)

    def body(*refs):
        gs, outs = refs[:n], refs[n:2 * n]
        send_sems, recv_sems, local_sems = refs[2 * n:]
        x, y, c = _mesh_pos()
        me = 4 * x + 2 * y + c

        def copy(a, j):
            to = (me + j) % N_DEV
            return pltpu.make_async_remote_copy(
                src_ref=gs[a].at[to], dst_ref=outs[a].at[me],
                send_sem=send_sems.at[7 * a + j - 1], recv_sem=recv_sems.at[7 * a + j - 1],
                device_id=_device_of(to), device_id_type=pl.DeviceIdType.MESH)

        def arrival(a, j):
            frm = (me + N_DEV - j) % N_DEV
            return pltpu.make_async_remote_copy(
                src_ref=gs[a].at[frm], dst_ref=outs[a].at[frm],
                send_sem=send_sems.at[7 * a + j - 1], recv_sem=recv_sems.at[7 * a + j - 1],
                device_id=_device_of(frm), device_id_type=pl.DeviceIdType.MESH)

        mine = [pltpu.make_async_copy(gs[a].at[me], outs[a].at[me], local_sems.at[a]) for a in range(n)]
        for cp in mine:
            cp.start()
        sends = [copy(a, j) for j in range(1, N_DEV) for a in range(n)]
        for cp in sends:
            cp.start()
        for j in range(1, N_DEV):
            for a in range(n):
                arrival(a, j).wait_recv()
        for cp in sends:
            cp.wait_send()
        for cp in mine:
            cp.wait()

    return pl.pallas_call(
        body, name="scatter_grad_slices",
        in_specs=[ANY] * n, out_specs=[ANY] * n,
        out_shape=[jax.ShapeDtypeStruct(g.shape, g.dtype) for g in grads],
        scratch_shapes=[pltpu.SemaphoreType.DMA((7 * n,)), pltpu.SemaphoreType.DMA((7 * n,)),
                        pltpu.SemaphoreType.DMA((n,))],
    )(*grads)


def _all_reduce_small(v):
    R, C = v.shape

    def body(v_ref, out_ref, slots, send_sems, recv_sems):
        x, y, c = _mesh_pos()
        me = 4 * x + 2 * y + c

        def copy(j, to):
            return pltpu.make_async_remote_copy(
                src_ref=v_ref, dst_ref=slots.at[me],
                send_sem=send_sems.at[j - 1], recv_sem=recv_sems.at[j - 1],
                device_id=_device_of(to), device_id_type=pl.DeviceIdType.MESH)

        sends = [copy(j, (me + j) % N_DEV) for j in range(1, N_DEV)]
        for cp in sends:
            cp.start()
        slots[me] = v_ref[...]
        for j in range(1, N_DEV):
            frm = (me + N_DEV - j) % N_DEV
            pltpu.make_async_remote_copy(
                src_ref=v_ref, dst_ref=slots.at[frm], send_sem=send_sems.at[j - 1], recv_sem=recv_sems.at[j - 1],
                device_id=_device_of(frm), device_id_type=pl.DeviceIdType.MESH).wait_recv()
        for cp in sends:
            cp.wait_send()
        acc = slots[0]
        for p in range(1, N_DEV):
            acc = acc + slots[p]
        out_ref[...] = acc

    return pl.pallas_call(
        body, name="all_reduce_small",
        in_specs=[pl.BlockSpec(memory_space=pltpu.VMEM)], out_specs=pl.BlockSpec(memory_space=pltpu.VMEM),
        out_shape=jax.ShapeDtypeStruct((R, C), F32),
        scratch_shapes=[pltpu.VMEM((N_DEV, R, C), F32), pltpu.SemaphoreType.DMA((7,)), pltpu.SemaphoreType.DMA((7,))],
    )(v)


def _adamw_math(w, g, m, v):
    m = ADAM_B1 * m + (1.0 - ADAM_B1) * g
    v = ADAM_B2 * v + (1.0 - ADAM_B2) * (g * g)
    m_hat = m / (1.0 - ADAM_B1 ** ADAM_STEP)
    v_hat = v / (1.0 - ADAM_B2 ** ADAM_STEP)
    delta = -ADAM_LR * (m_hat / (jnp.sqrt(v_hat) + ADAM_EPS) + ADAM_WD * w)
    return delta, m, v


def _adamw_reduce(recv, w, m, v, name):
    R, C = w.shape
    tr = R if R <= 256 else 256
    while R % tr:
        tr //= 2

    def body(r_ref, w_ref, m_ref, v_ref, g_ref, d_ref, nm_ref, nv_ref):
        g = r_ref[0].astype(F32)
        for p in range(1, N_DEV):
            g = g + r_ref[p].astype(F32)
        d, nm, nv = _adamw_math(w_ref[...], g, m_ref[...], v_ref[...])
        g_ref[...] = g
        d_ref[...] = d
        nm_ref[...] = nm
        nv_ref[...] = nv

    tile = pl.BlockSpec((tr, C), lambda i: (i, 0))
    shp = jax.ShapeDtypeStruct((R, C), F32)
    return pl.pallas_call(
        body, name=name, grid=(R // tr,),
        in_specs=[pl.BlockSpec((N_DEV, tr, C), lambda i: (0, i, 0)), tile, tile, tile],
        out_specs=[tile] * 4, out_shape=[shp] * 4,
        compiler_params=_params(("parallel",)),
    )(recv, w, m, v)


def _adamw_small(g, w, m, v):
    def body(g_ref, w_ref, m_ref, v_ref, go_ref, d_ref, nm_ref, nv_ref):
        go_ref[...] = g_ref[...]
        for d in range(2):
            p0 = _sig(w_ref[8 + 2 * d:9 + 2 * d, :] - w_ref[9 + 2 * d:10 + 2 * d, :])
            dl0 = g_ref[12 + d:13 + d, :] * p0 * (1.0 - p0)
            go_ref[8 + 2 * d:9 + 2 * d, :] = dl0
            go_ref[9 + 2 * d:10 + 2 * d, :] = -dl0
            go_ref[12 + d:13 + d, :] = jnp.zeros((1, D_MODEL), F32)
        d, nm, nv = _adamw_math(w_ref[...], go_ref[...], m_ref[...], v_ref[...])
        d_ref[...] = d
        nm_ref[...] = nm
        nv_ref[...] = nv

    shp = jax.ShapeDtypeStruct(g.shape, F32)
    vm = pl.BlockSpec(memory_space=pltpu.VMEM)
    return pl.pallas_call(body, name="adamw_small", in_specs=[vm] * 4, out_specs=[vm] * 4, out_shape=[shp] * 4)(g, w, m, v)


def _local_step(x, tgt, g_mix, lb, norm_g, pool_w_full, pool_scale, g_ffn, g_final, w_in_g, wa, wb, wo, wfi_g, wfo_g):
    B, L, _ = x.shape
    T = B * L
    x2, tgt2 = x.reshape(T, D_MODEL), tgt.reshape(T, D_MODEL)

    proj, u = _rms_inproj(x2, g_mix, w_in_g)
    proj4 = proj.reshape(N_DEV, B, L, D_MODEL)
    o, ya = _hgrn_fwd(proj4, lb, norm_g, B, L)
    yb = _pool_fwd(proj4, pool_w_full, pool_scale, B, L)
    ya2, yb2 = ya.reshape(T, D_MODEL), yb.reshape(T, D_MODEL)
    za, zb, mg, h = _merge_out(x2, proj, ya2, yb2, wa, wb, wo)
    gate, up, hid, u2, dh2, dh2b, loss_p, dg_final = _ffn_fwd_loss(h, tgt2, g_ffn, g_final, wfi_g, wfo_g)
    loss = jnp.sum(loss_p[:, 0, 0])

    dgu, dh, dhb, dg_ffn = _ffn_bwd(h, dh2, dh2b, gate, up, g_ffn, wfi_g, wfo_g)
    d_wfo = _wgrad(hid, dh2b[None], "wgrad_ffn_out")
    d_wfi = _wgrad(u2[None], dgu.reshape(N_DEV, T, FF_BLOCK), "wgrad_ffn_in")
    dza, dzb, dgab, dya, dyb = _merge_bwd(dhb, proj, za, zb, wa, wb, wo)
    d_wo = _wgrad(mg[None], dhb[None], "wgrad_out")
    d_wa = _wgrad(ya2[None], dza[None], "wgrad_branch_a")
    d_wb = _wgrad(yb2[None], dzb[None], "wgrad_branch_b")
    dp, dpw_p, dps_p = _pool_bwd(proj4, dyb.reshape(B, L, D_MODEL), pool_w_full, pool_scale, B, L)
    dproj5, dlb_p, dng_p = _hgrn_bwd(proj4, o, dya.reshape(B, L, D_MODEL), lb, norm_g, B, L)
    dproj5 = dproj5.reshape(5, T, D_MODEL)
    dp2 = dp.reshape(T, D_MODEL)
    grad_x, dg_mix = _inproj_bwd(x2, dh, dproj5, dp2, dgab, g_mix, w_in_g)
    d_win = _wgrad_in(u, dproj5, dp2, dgab)

    small = dict(g_mix=dg_mix.sum(0), hgrn_norm_g=dng_p.sum((0, 1)), pool_scale=dps_p.sum((0, 1)),
                 g_ffn=dg_ffn.sum(0), g_final=dg_final.sum(0), lb=dlb_p.sum(0))
    big = dict(w_in=d_win, w_branch_a=d_wa[0], w_branch_b=d_wb[0], w_out=d_wo[0],
               w_ffn_in=d_wfi, w_ffn_out=d_wfo, pool_w=dpw_p.sum(0))
    return loss, grad_x.reshape(B, L, D_MODEL), big, small


def kernel(x, g_mix, w_in, lb_logits, hgrn_norm_g, pool_w, pool_scale, w_branch_a, w_branch_b, w_out, g_ffn, w_ffn_in, w_ffn_out, g_final, loss_target, m_g_mix, m_w_in, m_lb_logits, m_hgrn_norm_g, m_pool_w, m_pool_scale, m_w_branch_a, m_w_branch_b, m_w_out, m_g_ffn, m_w_ffn_in, m_w_ffn_out, m_g_final, v_g_mix, v_w_in, v_lb_logits, v_hgrn_norm_g, v_pool_w, v_pool_scale, v_w_branch_a, v_w_branch_b, v_w_out, v_g_ffn, v_w_ffn_in, v_w_ffn_out, v_g_final):
    me = 4 * lax.axis_index("x") + 2 * lax.axis_index("y") + lax.axis_index("c")
    rows = D_MODEL // N_DEV

    shards = [w_in[0].astype(MX), w_branch_a[0].astype(MX), w_branch_b[0].astype(MX), w_out[0].astype(MX),
              w_ffn_in[0].astype(MX), w_ffn_out[0].astype(MX), pool_w[0].reshape(4 * 32, POOL_DIM).astype(MX),
              jnp.pad(lb_logits.reshape(4, HEAD_DIM), ((0, 4), (0, 0)))]
    w_in_g, wa, wb, wo, wfi_g, wfo_g, pw_g, lbl_g = _all_gather(shards)
    wa, wb, wo = (w_.reshape(D_MODEL, D_MODEL) for w_ in (wa, wb, wo))
    wfo_g = wfo_g.reshape(4, FF_BLOCK, D_MODEL)
    pool_w_full = pw_g.reshape(N_DEV, 4, 32, POOL_DIM).transpose(1, 0, 2, 3).reshape(4, POOL_DIM, POOL_DIM)
    lbl = lbl_g[:, :4].transpose(1, 0, 2).reshape(4, D_MODEL)

    loss, grad_x, big, small = _local_step(
        x, loss_target, g_mix, lbl, hgrn_norm_g, pool_w_full, pool_scale, g_ffn, g_final[None],
        w_in_g, wa, wb, wo, wfi_g, wfo_g)
    loss = lax.psum(loss, ("x", "y", "c"))

    packed = jnp.zeros((16, D_MODEL), F32)
    names = ["g_mix", "hgrn_norm_g", "pool_scale", "g_ffn", "g_final"]
    for i, nme in enumerate(names):
        packed = packed.at[i].set(small[nme])
    packed = packed.at[5:7].set(small["lb"])
    red = _all_reduce_small(packed)
    dlb_mine = lax.dynamic_slice_in_dim(red[5:7], me * HEAD_DIM, HEAD_DIM, axis=1)

    sw = jnp.zeros((16, D_MODEL), F32)
    sm = jnp.zeros((16, D_MODEL), F32)
    sv = jnp.ones((16, D_MODEL), F32)
    smalls = [(g_mix, m_g_mix, v_g_mix), (hgrn_norm_g, m_hgrn_norm_g, v_hgrn_norm_g),
              (pool_scale, m_pool_scale, v_pool_scale), (g_ffn, m_g_ffn, v_g_ffn),
              (g_final[None], m_g_final[None], v_g_final[None])]
    for i, (w_, m_, v_) in enumerate(smalls):
        sw, sm, sv = sw.at[i].set(w_[0]), sm.at[i].set(m_[0]), sv.at[i].set(v_[0])
    sg = red.at[5:].set(0.0)
    sg = sg.at[12:14, :HEAD_DIM].set(dlb_mine)
    sw = sw.at[8:12, :HEAD_DIM].set(lb_logits.reshape(4, HEAD_DIM))
    sm = sm.at[8:12, :HEAD_DIM].set(m_lb_logits.reshape(4, HEAD_DIM))
    sv = sv.at[8:12, :HEAD_DIM].set(v_lb_logits.reshape(4, HEAD_DIM))
    sg, sd, snm, snv = _adamw_small(sg, sw, sm, sv)

    def small_out(arr, i, like):
        return arr[i].reshape(like.shape)

    def lb_out(arr):
        return arr[8:12, :HEAD_DIM].reshape(2, 2, HEAD_DIM)

    order = ["w_in", "w_branch_a", "w_branch_b", "w_out", "w_ffn_in", "w_ffn_out", "pool_w"]
    slices = [big["w_in"],
              big["w_branch_a"].reshape(N_DEV, rows, D_MODEL), big["w_branch_b"].reshape(N_DEV, rows, D_MODEL),
              big["w_out"].reshape(N_DEV, rows, D_MODEL),
              big["w_ffn_in"], big["w_ffn_out"].reshape(N_DEV, FF_BLOCK // 2, D_MODEL),
              big["pool_w"].reshape(4, N_DEV, 32, POOL_DIM).transpose(1, 0, 2, 3).reshape(N_DEV, 128, POOL_DIM).astype(MX)]
    recv = _scatter_slices(slices)
    params = dict(w_in=(w_in, m_w_in, v_w_in), w_branch_a=(w_branch_a, m_w_branch_a, v_w_branch_a),
                  w_branch_b=(w_branch_b, m_w_branch_b, v_w_branch_b), w_out=(w_out, m_w_out, v_w_out),
                  w_ffn_in=(w_ffn_in, m_w_ffn_in, v_w_ffn_in), w_ffn_out=(w_ffn_out, m_w_ffn_out, v_w_ffn_out),
                  pool_w=(pool_w, m_pool_w, v_pool_w))
    res = {}
    for nme, r in zip(order, recv):
        w_, m_, v_ = params[nme]
        shape2 = r.shape[1:]
        outs = _adamw_reduce(r, w_.reshape(shape2), m_.reshape(shape2), v_.reshape(shape2), "adamw_" + nme)
        res[nme] = [o_.reshape(w_.shape) for o_ in outs]

    def pick(k):
        small_src = [sg, sd, snm, snv][k]
        return [small_out(small_src, 0, g_mix), res["w_in"][k], lb_out(small_src), small_out(small_src, 1, hgrn_norm_g),
                res["pool_w"][k], small_out(small_src, 2, pool_scale), res["w_branch_a"][k], res["w_branch_b"][k],
                res["w_out"][k], small_out(small_src, 3, g_ffn), res["w_ffn_in"][k], res["w_ffn_out"][k],
                small_out(small_src, 4, g_final)]

    return (loss, grad_x, *pick(0), *pick(1), *pick(2), *pick(3))
```

```python
import functools

import jax
import jax.numpy as jnp
from jax import lax
from jax.experimental import pallas as pl
from jax.experimental.pallas import tpu as pltpu

F32 = jnp.float32
MX = jnp.bfloat16

D_MODEL = 1024
N_HEADS = 8
HEAD_DIM = 128
CHUNK = 16
POOL_WINDOWS = (2, 4, 8, 16)
POOL_DIM = 256
FF_BLOCK = 704
N_DEV = 8
RMS_EPS = 1e-6
ADAM_LR, ADAM_B1, ADAM_B2, ADAM_EPS, ADAM_WD, ADAM_STEP = 0.001, 0.9, 0.999, 1e-08, 0.01, 10
VMEM_LIMIT = 56 * 1024 * 1024
ANY = pl.BlockSpec(memory_space=pl.ANY)


def _params(sem=None):
    return pltpu.CompilerParams(dimension_semantics=sem, vmem_limit_bytes=VMEM_LIMIT)


def _dot(a, b):
    return lax.dot_general(a.astype(MX), b.astype(MX), (((1,), (0,)), ((), ())), preferred_element_type=F32)


def _dot_nt(a, b):
    return lax.dot_general(a.astype(MX), b.astype(MX), (((1,), (1,)), ((), ())), preferred_element_type=F32)


def _dot_tn(a, b):
    return lax.dot_general(a.astype(MX), b.astype(MX), (((0,), (0,)), ((), ())), preferred_element_type=F32)


def _sig(x):
    return 1.0 / (1.0 + jnp.exp(-x))


def _fold8(v):
    return v.reshape(v.shape[0] // 8, 8, v.shape[1]).sum(axis=0)


def _shift_rows(x, s):
    n = x.shape[0]
    row = lax.broadcasted_iota(jnp.int32, x.shape, 0)
    if s > 0:
        return jnp.where(row >= s, pltpu.roll(x, s, 0), 0.0)
    return jnp.where(row < n + s, pltpu.roll(x, n + s, 0), 0.0)


def _cumsum_rows(x, rev):
    s = 1
    while s < x.shape[0]:
        x = x + _shift_rows(x, -s if rev else s)
        s *= 2
    return x


def _rms_inproj(x2, g_mix, w_in_g):
    T = x2.shape[0]
    tm = min(512, T)

    def body(x_ref, g_ref, w_ref, proj_ref, u_ref, u_sc):
        @pl.when(pl.program_id(1) == 0)
        def _():
            x = x_ref[...]
            r = lax.rsqrt(jnp.mean(x * x, axis=-1, keepdims=True) + RMS_EPS)
            u = (x * r * g_ref[...]).astype(MX)
            u_sc[...] = u
            u_ref[...] = u
        proj_ref[...] = jnp.dot(u_sc[...], w_ref[...], preferred_element_type=F32)

    return pl.pallas_call(
        body, name="rms_inproj", grid=(T // tm, N_DEV),
        in_specs=[pl.BlockSpec((tm, D_MODEL), lambda i, k: (i, 0)),
                  pl.BlockSpec((1, D_MODEL), lambda i, k: (0, 0)),
                  pl.BlockSpec((None, D_MODEL, D_MODEL), lambda i, k: (k, 0, 0))],
        out_specs=[pl.BlockSpec((None, tm, D_MODEL), lambda i, k: (k, i, 0)),
                   pl.BlockSpec((tm, D_MODEL), lambda i, k: (i, 0))],
        out_shape=[jax.ShapeDtypeStruct((N_DEV, T, D_MODEL), F32), jax.ShapeDtypeStruct((T, D_MODEL), MX)],
        scratch_shapes=[pltpu.VMEM((tm, D_MODEL), MX)],
        compiler_params=_params(("parallel", "arbitrary")),
    )(x2, g_mix, w_in_g)


HBLK = 128


def _seg_cumsum(x, rev):
    n = x.shape[0]
    pos = lax.broadcasted_iota(jnp.int32, x.shape, 0) & (CHUNK - 1)
    s = 1
    while s < CHUNK:
        if rev:
            x = x + jnp.where(pos < CHUNK - s, pltpu.roll(x, n - s, 0), 0.0)
        else:
            x = x + jnp.where(pos >= s, pltpu.roll(x, s, 0), 0.0)
        s *= 2
    return x


def _block_gates(q_r, f_r, lb_row, rev):
    sq = _sig(q_r)
    q = q_r * sq
    sg = _sig(f_r)
    f = lb_row + (1.0 - lb_row) * sg
    k = 1.0 - f
    lf = jnp.log(f)
    pre = _seg_cumsum(lf, False)
    suf = _seg_cumsum(lf, True)
    tot = pre + suf - lf
    b = suf if rev else pre
    eb = jnp.exp(b)
    enb = jnp.exp(-b)
    eend = jnp.exp(tot - b)
    return dict(sq=sq, sg=sg, f=f, eb=eb, enb=enb, eend=eend, dec=jnp.exp(tot),
                P=q * eb, Kt=k * enb, Ke=k * eend)


def _block_mask(rev, transposed=False):
    ri = lax.broadcasted_iota(jnp.int32, (HBLK, HBLK), 0)
    ci = lax.broadcasted_iota(jnp.int32, (HBLK, HBLK), 1)
    same = (ri // CHUNK) == (ci // CHUNK)
    return same & ((ci >= ri) if rev != transposed else (ci <= ri))


def _chunk_rows(c):
    return pl.ds(pl.multiple_of(c * CHUNK, CHUNK), CHUNK)


def _chunk_outer_products(a, b, out_sc, kb):
    a, b = a.astype(MX), b.astype(MX)
    for u in range(HBLK // CHUNK):
        r = slice(u * CHUNK, (u + 1) * CHUNK)
        out_sc[kb * (HBLK // CHUNK) + u] = _dot_tn(a[r, :], b[r, :])


BLOCK_UNROLL = 2


def _block_loop(n_blocks, fn, init):
    def group(kg, carry):
        for u in range(BLOCK_UNROLL):
            carry = fn(kg * BLOCK_UNROLL + u, carry)
        return carry

    return lax.fori_loop(0, n_blocks // BLOCK_UNROLL, group, init)


SEQ_UNROLL = 16


def _chunk_loop(n_chunks, descending, step, init):
    def group(jg, carry):
        for u in range(SEQ_UNROLL):
            j = jg * SEQ_UNROLL + u
            carry = step(n_chunks - 1 - j if descending else j, carry)
        return carry

    return lax.fori_loop(0, n_chunks // SEQ_UNROLL, group, init)


def _lower_bounds(lbl_ref):
    return _sig(lbl_ref[0:1, :] - lbl_ref[1:2, :]), _sig(lbl_ref[2:3, :] - lbl_ref[3:4, :])


def _hgrn_fwd(proj, lbl, norm_g, B, L):
    nC = L // CHUNK

    def body(q_ref, ff_ref, fb_ref, i_ref, og_ref, lbl_ref, ng_ref, o_ref, ya_ref, o_sc, p_sc, dec_sc, upd_sc):
        lb_f, lb_b = _lower_bounds(lbl_ref)
        o_sc[...] = jnp.zeros_like(o_sc)

        def run_dir(f_ref, lb_row, rev):
            def block(kb, carry):
                rows = pl.ds(pl.multiple_of(kb * HBLK, HBLK), HBLK)
                g = _block_gates(q_ref[rows, :], f_ref[rows, :], lb_row, rev)
                v = i_ref[rows, :]
                a = jnp.where(_block_mask(rev), _dot_nt(g["P"], g["Kt"]), 0.0)
                o_sc[rows, :] += _dot(a, v)
                p_sc[rows, :] = g["P"].astype(MX)
                dec_sc[rows, :] = g["dec"]
                _chunk_outer_products(v, g["Ke"], upd_sc, kb)
                return carry

            _block_loop(L // HBLK, block, 0)

            def step(c, st):
                rows = _chunk_rows(c)
                o_sc[rows, :] += _dot_nt(p_sc[rows, :], st)
                dec = dec_sc[pl.ds(pl.multiple_of(c * CHUNK, CHUNK), 1), :]
                return st * dec + upd_sc[c]

            _chunk_loop(nC, rev, step, jnp.zeros((HEAD_DIM, HEAD_DIM), F32))

        run_dir(ff_ref, lb_f, False)
        run_dir(fb_ref, lb_b, True)
        o = o_sc[...]
        o_ref[...] = o
        on = o * lax.rsqrt(jnp.mean(o * o, axis=-1, keepdims=True) + RMS_EPS)
        og = og_ref[...]
        ya_ref[...] = ((on * ng_ref[...]) * (og * _sig(og))).astype(MX)

    def blk(s):
        return pl.BlockSpec((None, None, L, HEAD_DIM), lambda b, h, s=s: (s, b, 0, h))

    out_blk = pl.BlockSpec((None, L, HEAD_DIM), lambda b, h: (b, 0, h))
    return pl.pallas_call(
        body, name="hgrn_fwd", grid=(B, N_HEADS),
        in_specs=[blk(0), blk(1), blk(2), blk(3), blk(4),
                  pl.BlockSpec((4, HEAD_DIM), lambda b, h: (0, h)),
                  pl.BlockSpec((1, HEAD_DIM), lambda b, h: (0, h))],
        out_specs=[out_blk, out_blk],
        out_shape=[jax.ShapeDtypeStruct((B, L, D_MODEL), F32), jax.ShapeDtypeStruct((B, L, D_MODEL), MX)],
        scratch_shapes=[pltpu.VMEM((L, HEAD_DIM), F32), pltpu.VMEM((L, HEAD_DIM), MX),
                        pltpu.VMEM((L, HEAD_DIM), F32), pltpu.VMEM((nC, HEAD_DIM, HEAD_DIM), F32)],
        compiler_params=_params(("parallel", "parallel")),
    )(proj, proj, proj, proj, proj, lbl, norm_g)


POOL_PAD = 8


def _pool_window(p, ext_sc, half, adjoint):
    L = p.shape[0]
    n = L + 2 * POOL_PAD
    ext_sc[0:POOL_PAD, :] = jnp.zeros((POOL_PAD, p.shape[1]), F32)
    ext_sc[POOL_PAD + L:n, :] = jnp.zeros((POOL_PAD, p.shape[1]), F32)
    ext_sc[POOL_PAD:POOL_PAD + L, :] = p
    x = ext_sc[...]
    s = x + pltpu.roll(x, 1 if adjoint else n - 1, 0)
    w = 1
    while w < half:
        s = pltpu.roll(s, w, 0) + pltpu.roll(s, n - w, 0)
        w *= 2
    ext_sc[...] = s
    return ext_sc[POOL_PAD:POOL_PAD + L, :]


def _pool_count(L, half):
    t = lax.broadcasted_iota(jnp.int32, (L, 1), 0)
    lo = jnp.clip(t - half + 1, 0, L)
    hi = jnp.clip(t + half + 1, 0, L)
    return (hi - lo).astype(F32)


def _pool_fwd(proj, pool_w_full, pool_scale, B, L):
    def body(p_ref, w_ref, s_ref, yb_ref, ext_sc):
        for g, win in enumerate(POOL_WINDOWS):
            cols = slice(g * POOL_DIM, (g + 1) * POOL_DIM)
            p = p_ref[:, cols]
            y = _pool_window(p, ext_sc, win // 2, False) / _pool_count(L, win // 2) - p
            yb_ref[:, cols] = (_dot(y, w_ref[g]) * s_ref[:, cols]).astype(MX)

    return pl.pallas_call(
        body, name="pool_fwd", grid=(B,),
        in_specs=[pl.BlockSpec((None, None, L, D_MODEL), lambda b: (5, b, 0, 0)),
                  pl.BlockSpec((4, POOL_DIM, POOL_DIM), lambda b: (0, 0, 0)),
                  pl.BlockSpec((1, D_MODEL), lambda b: (0, 0))],
        out_specs=pl.BlockSpec((None, L, D_MODEL), lambda b: (b, 0, 0)),
        out_shape=jax.ShapeDtypeStruct((B, L, D_MODEL), MX),
        scratch_shapes=[pltpu.VMEM((L + 2 * POOL_PAD, POOL_DIM), F32)],
        compiler_params=_params(("parallel",)),
    )(proj, pool_w_full, pool_scale)


def _merge_out(x2, proj, ya, yb, wa, wb, wo):
    T = x2.shape[0]
    tm = min(512, T)

    def body(x_ref, ga_ref, gb_ref, ya_ref, yb_ref, wa_ref, wb_ref, wo_ref, za_ref, zb_ref, mg_ref, h_ref):
        za = jnp.dot(ya_ref[...], wa_ref[...], preferred_element_type=F32)
        zb = jnp.dot(yb_ref[...], wb_ref[...], preferred_element_type=F32)
        mg = (_sig(ga_ref[...]) * za + _sig(gb_ref[...]) * zb).astype(MX)
        za_ref[...] = za
        zb_ref[...] = zb
        mg_ref[...] = mg
        h_ref[...] = x_ref[...] + jnp.dot(mg, wo_ref[...], preferred_element_type=F32)

    tile = pl.BlockSpec((tm, D_MODEL), lambda i: (i, 0))
    wspec = pl.BlockSpec((D_MODEL, D_MODEL), lambda i: (0, 0))
    return pl.pallas_call(
        body, name="merge_out", grid=(T // tm,),
        in_specs=[tile,
                  pl.BlockSpec((None, tm, D_MODEL), lambda i: (6, i, 0)),
                  pl.BlockSpec((None, tm, D_MODEL), lambda i: (7, i, 0)),
                  tile, tile, wspec, wspec, wspec],
        out_specs=[tile, tile, tile, tile],
        out_shape=[jax.ShapeDtypeStruct((T, D_MODEL), F32), jax.ShapeDtypeStruct((T, D_MODEL), F32),
                   jax.ShapeDtypeStruct((T, D_MODEL), MX), jax.ShapeDtypeStruct((T, D_MODEL), F32)],
        compiler_params=_params(("parallel",)),
    )(x2, proj, proj, ya, yb, wa, wb, wo)


def _ffn_fwd_loss(h, tgt, g_ffn, g_final, wfi_g, wfo_g):
    T = h.shape[0]
    tm = min(512, T)
    nT = T // tm

    def body(h_ref, t_ref, gf_ref, gl_ref, wg_ref, wu_ref, wo_ref,
             gate_ref, up_ref, hid_ref, u2_ref, dh2_ref, dh2b_ref, loss_ref, dgl_ref, u2_sc, acc_sc):
        i, j = pl.program_id(0), pl.program_id(1)

        @pl.when(j == 0)
        def _():
            hh = h_ref[...]
            r = lax.rsqrt(jnp.mean(hh * hh, axis=-1, keepdims=True) + RMS_EPS)
            u2 = (hh * r * gf_ref[...]).astype(MX)
            u2_sc[...] = u2
            u2_ref[...] = u2
            acc_sc[...] = jnp.zeros_like(acc_sc)

        @pl.when((i == 0) & (j == 0))
        def _():
            dgl_ref[...] = jnp.zeros_like(dgl_ref)

        gate = jnp.dot(u2_sc[...], wg_ref[...], preferred_element_type=F32)
        up = jnp.dot(u2_sc[...], wu_ref[...], preferred_element_type=F32)
        hid = ((gate * _sig(gate)) * up).astype(MX)
        gate_ref[...] = gate
        up_ref[...] = up
        hid_ref[...] = hid
        acc_sc[...] += jnp.dot(hid, wo_ref[...], preferred_element_type=F32)

        @pl.when(j == 3)
        def _():
            h2 = h_ref[...] + acc_sc[...]
            r = lax.rsqrt(jnp.mean(h2 * h2, axis=-1, keepdims=True) + RMS_EPS)
            hn = h2 * r
            gl = gl_ref[...]
            err = hn * gl - t_ref[...]
            tok = jnp.mean(err * err, axis=-1, keepdims=True)
            loss_ref[...] = jnp.full(loss_ref.shape, 0.5 * jnp.sum(tok), F32)
            dy = err * (1.0 / D_MODEL)
            dgl_ref[...] += _fold8(dy * hn)
            a = dy * gl
            dh2 = r * a - hn * (r * jnp.mean(a * hn, axis=-1, keepdims=True))
            dh2_ref[...] = dh2
            dh2b_ref[...] = dh2.astype(MX)

    tile = pl.BlockSpec((tm, D_MODEL), lambda i, j: (i, 0))
    vec = pl.BlockSpec((1, D_MODEL), lambda i, j: (0, 0))
    ftile = pl.BlockSpec((None, tm, FF_BLOCK), lambda i, j: (j, i, 0))
    return pl.pallas_call(
        body, name="ffn_fwd_loss", grid=(nT, 4),
        in_specs=[tile, tile, vec, vec,
                  pl.BlockSpec((None, D_MODEL, FF_BLOCK), lambda i, j: (j, 0, 0)),
                  pl.BlockSpec((None, D_MODEL, FF_BLOCK), lambda i, j: (j + 4, 0, 0)),
                  pl.BlockSpec((None, FF_BLOCK, D_MODEL), lambda i, j: (j, 0, 0))],
        out_specs=[ftile, ftile, ftile, tile, tile, tile,
                   pl.BlockSpec((None, 8, 128), lambda i, j: (i, 0, 0)),
                   pl.BlockSpec((8, D_MODEL), lambda i, j: (0, 0))],
        out_shape=[jax.ShapeDtypeStruct((4, T, FF_BLOCK), F32), jax.ShapeDtypeStruct((4, T, FF_BLOCK), F32),
                   jax.ShapeDtypeStruct((4, T, FF_BLOCK), MX), jax.ShapeDtypeStruct((T, D_MODEL), MX),
                   jax.ShapeDtypeStruct((T, D_MODEL), F32), jax.ShapeDtypeStruct((T, D_MODEL), MX),
                   jax.ShapeDtypeStruct((nT, 8, 128), F32), jax.ShapeDtypeStruct((8, D_MODEL), F32)],
        scratch_shapes=[pltpu.VMEM((tm, D_MODEL), MX), pltpu.VMEM((tm, D_MODEL), F32)],
        compiler_params=_params(("arbitrary", "arbitrary")),
    )(h, tgt, g_ffn, g_final, wfi_g, wfi_g, wfo_g)


def _ffn_bwd(h, dh2, dh2b, gate, up, g_ffn, wfi_g, wfo_g):
    T = h.shape[0]
    tm = min(512, T)

    def body(h_ref, dh2_ref, dh2b_ref, gate_ref, up_ref, gf_ref, wg_ref, wu_ref, wo_ref,
             dgu_ref, dh_ref, dhb_ref, dgf_ref, acc_sc):
        i, j = pl.program_id(0), pl.program_id(1)

        @pl.when(j == 0)
        def _():
            acc_sc[...] = jnp.zeros_like(acc_sc)

        @pl.when((i == 0) & (j == 0))
        def _():
            dgf_ref[...] = jnp.zeros_like(dgf_ref)

        dhid = _dot_nt(dh2b_ref[...], wo_ref[...])
        gate, up = gate_ref[...], up_ref[...]
        sg = _sig(gate)
        dgate = (dhid * up * (sg * (1.0 + gate * (1.0 - sg)))).astype(MX)
        dup = (dhid * (gate * sg)).astype(MX)
        dgu_ref[0] = dgate
        dgu_ref[1] = dup
        acc_sc[...] += _dot_nt(dgate, wg_ref[...]) + _dot_nt(dup, wu_ref[...])

        @pl.when(j == 3)
        def _():
            hh = h_ref[...]
            r = lax.rsqrt(jnp.mean(hh * hh, axis=-1, keepdims=True) + RMS_EPS)
            hn = hh * r
            du2 = acc_sc[...]
            dgf_ref[...] += _fold8(du2 * hn)
            a = du2 * gf_ref[...]
            dh = dh2_ref[...] + r * a - hn * (r * jnp.mean(a * hn, axis=-1, keepdims=True))
            dh_ref[...] = dh
            dhb_ref[...] = dh.astype(MX)

    tile = pl.BlockSpec((tm, D_MODEL), lambda i, j: (i, 0))
    ftile = pl.BlockSpec((None, tm, FF_BLOCK), lambda i, j: (j, i, 0))
    return pl.pallas_call(
        body, name="ffn_bwd", grid=(T // tm, 4),
        in_specs=[tile, tile, tile, ftile, ftile,
                  pl.BlockSpec((1, D_MODEL), lambda i, j: (0, 0)),
                  pl.BlockSpec((None, D_MODEL, FF_BLOCK), lambda i, j: (j, 0, 0)),
                  pl.BlockSpec((None, D_MODEL, FF_BLOCK), lambda i, j: (j + 4, 0, 0)),
                  pl.BlockSpec((None, FF_BLOCK, D_MODEL), lambda i, j: (j, 0, 0))],
        out_specs=[pl.BlockSpec((2, None, tm, FF_BLOCK), lambda i, j: (0, j, i, 0)),
                   tile, tile, pl.BlockSpec((8, D_MODEL), lambda i, j: (0, 0))],
        out_shape=[jax.ShapeDtypeStruct((2, 4, T, FF_BLOCK), MX),
                   jax.ShapeDtypeStruct((T, D_MODEL), F32), jax.ShapeDtypeStruct((T, D_MODEL), MX),
                   jax.ShapeDtypeStruct((8, D_MODEL), F32)],
        scratch_shapes=[pltpu.VMEM((tm, D_MODEL), F32)],
        compiler_params=_params(("arbitrary", "arbitrary")),
    )(h, dh2, dh2b, gate, up, g_ffn, wfi_g, wfi_g, wfo_g)


def _merge_bwd(dhb, proj, za, zb, wa, wb, wo):
    T = dhb.shape[0]
    tm = min(512, T)

    def body(dh_ref, ga_ref, gb_ref, za_ref, zb_ref, wa_ref, wb_ref, wo_ref,
             dza_ref, dzb_ref, dgab_ref, dya_ref, dyb_ref):
        dm = _dot_nt(dh_ref[...], wo_ref[...])
        sa, sb = _sig(ga_ref[...]), _sig(gb_ref[...])
        dza = (dm * sa).astype(MX)
        dzb = (dm * sb).astype(MX)
        dza_ref[...] = dza
        dzb_ref[...] = dzb
        dgab_ref[0] = (dm * za_ref[...] * (sa * (1.0 - sa))).astype(MX)
        dgab_ref[1] = (dm * zb_ref[...] * (sb * (1.0 - sb))).astype(MX)
        dya_ref[...] = _dot_nt(dza, wa_ref[...])
        dyb_ref[...] = _dot_nt(dzb, wb_ref[...])

    tile = pl.BlockSpec((tm, D_MODEL), lambda i: (i, 0))
    wspec = pl.BlockSpec((D_MODEL, D_MODEL), lambda i: (0, 0))
    return pl.pallas_call(
        body, name="merge_bwd", grid=(T // tm,),
        in_specs=[tile,
                  pl.BlockSpec((None, tm, D_MODEL), lambda i: (6, i, 0)),
                  pl.BlockSpec((None, tm, D_MODEL), lambda i: (7, i, 0)),
                  tile, tile, wspec, wspec, wspec],
        out_specs=[tile, tile, pl.BlockSpec((2, tm, D_MODEL), lambda i: (0, i, 0)), tile, tile],
        out_shape=[jax.ShapeDtypeStruct((T, D_MODEL), MX), jax.ShapeDtypeStruct((T, D_MODEL), MX),
                   jax.ShapeDtypeStruct((2, T, D_MODEL), MX),
                   jax.ShapeDtypeStruct((T, D_MODEL), F32), jax.ShapeDtypeStruct((T, D_MODEL), F32)],
        compiler_params=_params(("parallel",)),
    )(dhb, proj, proj, za, zb, wa, wb, wo)


def _pool_bwd(proj, dyb, pool_w_full, pool_scale, B, L):
    def body(p_ref, dy_ref, w_ref, s_ref, dp_ref, dw_ref, ds_ref, ext_sc):
        for g, win in enumerate(POOL_WINDOWS):
            cols = slice(g * POOL_DIM, (g + 1) * POOL_DIM)
            p = p_ref[:, cols]
            cnt = _pool_count(L, win // 2)
            y = _pool_window(p, ext_sc, win // 2, False) / cnt - p
            z = _dot(y, w_ref[g])
            dyb_g = dy_ref[:, cols]
            ds_ref[:, cols] = jnp.sum(dyb_g * z, axis=0, keepdims=True)
            dz = dyb_g * s_ref[:, cols]
            dw_ref[g] = _dot_tn(y, dz)
            dy = _dot_nt(dz, w_ref[g])
            dp_ref[:, cols] = (_pool_window(dy / cnt, ext_sc, win // 2, True) - dy).astype(MX)

    seq = pl.BlockSpec((None, L, D_MODEL), lambda b: (b, 0, 0))
    return pl.pallas_call(
        body, name="pool_bwd", grid=(B,),
        in_specs=[pl.BlockSpec((None, None, L, D_MODEL), lambda b: (5, b, 0, 0)), seq,
                  pl.BlockSpec((4, POOL_DIM, POOL_DIM), lambda b: (0, 0, 0)),
                  pl.BlockSpec((1, D_MODEL), lambda b: (0, 0))],
        out_specs=[seq, pl.BlockSpec((None, 4, POOL_DIM, POOL_DIM), lambda b: (b, 0, 0, 0)),
                   pl.BlockSpec((None, 1, D_MODEL), lambda b: (b, 0, 0))],
        out_shape=[jax.ShapeDtypeStruct((B, L, D_MODEL), MX),
                   jax.ShapeDtypeStruct((B, 4, POOL_DIM, POOL_DIM), F32),
                   jax.ShapeDtypeStruct((B, 1, D_MODEL), F32)],
        scratch_shapes=[pltpu.VMEM((L + 2 * POOL_PAD, POOL_DIM), F32)],
        compiler_params=_params(("parallel",)),
    )(proj, dyb, pool_w_full, pool_scale)


def _hgrn_bwd(proj, o, dya, lbl, norm_g, B, L):
    nC = L // CHUNK

    def body(q_ref, ff_ref, fb_ref, i_ref, og_ref, o_ref, dy_ref, lbl_ref, ng_ref,
             dp_ref, dlb_ref, dng_ref,
             do_sc, dq_sc, dv_sc, dP_sc, dKt_sc, dKe_sc, dbl_sc, dec_sc, ke_sc, ck_sc, upd_sc, dupd_sc):
        lb_f, lb_b = _lower_bounds(lbl_ref)
        o_ = o_ref[...]
        r = lax.rsqrt(jnp.mean(o_ * o_, axis=-1, keepdims=True) + RMS_EPS)
        on = o_ * r
        og = og_ref[...]
        sog = _sig(og)
        dy = dy_ref[...]
        ng = ng_ref[...]
        dp_ref[4] = (dy * (on * ng) * (sog * (1.0 + og * (1.0 - sog)))).astype(MX)
        dn = dy * (og * sog)
        dng_ref[...] = jnp.sum(dn * on, axis=0, keepdims=True)
        don = dn * ng
        do_sc[...] = r * don - on * (r * jnp.mean(don * on, axis=-1, keepdims=True))

        dq_sc[...] = jnp.zeros_like(dq_sc)
        dv_sc[...] = jnp.zeros_like(dv_sc)

        def run_dir(f_ref, lb_row, rev, slot):
            def block(kb, carry):
                rows = pl.ds(pl.multiple_of(kb * HBLK, HBLK), HBLK)
                g = _block_gates(q_ref[rows, :], f_ref[rows, :], lb_row, rev)
                v = i_ref[rows, :]
                do = do_sc[rows, :]
                m, mt = _block_mask(rev), _block_mask(rev, True)
                at = jnp.where(mt, _dot_nt(g["Kt"], g["P"]), 0.0)
                da = jnp.where(m, _dot_nt(do, v), 0.0)
                dat = jnp.where(mt, _dot_nt(v, do), 0.0)
                dv_sc[rows, :] += _dot(at, do)
                dP_sc[rows, :] = _dot(da, g["Kt"])
                dKt_sc[rows, :] = _dot(dat, g["P"])
                ke_sc[rows, :] = g["Ke"].astype(MX)
                dec_sc[rows, :] = g["dec"]
                _chunk_outer_products(v, g["Ke"], upd_sc, kb)
                _chunk_outer_products(do, g["P"], dupd_sc, kb)
                return carry

            _block_loop(L // HBLK, block, 0)

            def fstep(c, st):
                rows = _chunk_rows(c)
                ck_sc[c] = st.astype(MX)
                dP_sc[rows, :] += _dot(do_sc[rows, :], st)
                dec = dec_sc[pl.ds(pl.multiple_of(c * CHUNK, CHUNK), 1), :]
                return st * dec + upd_sc[c]

            _chunk_loop(nC, rev, fstep, jnp.zeros((HEAD_DIM, HEAD_DIM), F32))

            def bstep(c, dst):
                rows = _chunk_rows(c)
                dec = dec_sc[pl.ds(pl.multiple_of(c * CHUNK, CHUNK), 1), :]
                dKe_sc[rows, :] = _dot(i_ref[rows, :], dst)
                dv_sc[rows, :] += _dot_nt(ke_sc[rows, :], dst)
                dbl = dec * jnp.sum(dst * ck_sc[c].astype(F32), axis=0, keepdims=True)
                dbl_sc[rows, :] = jnp.broadcast_to(dbl, (CHUNK, HEAD_DIM))
                return dst * dec + dupd_sc[c]

            _chunk_loop(nC, not rev, bstep, jnp.zeros((HEAD_DIM, HEAD_DIM), F32))

            def finish(kb, acc):
                rows = pl.ds(pl.multiple_of(kb * HBLK, HBLK), HBLK)
                q_r = q_ref[rows, :]
                g = _block_gates(q_r, f_ref[rows, :], lb_row, rev)
                dP, dkt, dke = dP_sc[rows, :], dKt_sc[rows, :], dKe_sc[rows, :]
                e = dke * g["Ke"]
                dlf = (_seg_cumsum(dP * g["P"] - dkt * g["Kt"], not rev) + _seg_cumsum(e, rev) - e
                       + dbl_sc[rows, :])
                df = dlf / g["f"] - (dkt * g["enb"] + dke * g["eend"])
                dp_ref[slot, rows, :] = (df * (1.0 - lb_row) * (g["sg"] * (1.0 - g["sg"]))).astype(MX)
                dq_sc[rows, :] += (dP * g["eb"]) * (g["sq"] * (1.0 + q_r * (1.0 - g["sq"])))
                return acc + jnp.sum(df * (1.0 - g["sg"]), axis=0, keepdims=True)

            dlb_ref[slot - 1:slot, :] = _block_loop(L // HBLK, finish, jnp.zeros((1, HEAD_DIM), F32))

        run_dir(ff_ref, lb_f, False, 1)
        run_dir(fb_ref, lb_b, True, 2)
        dp_ref[0] = dq_sc[...].astype(MX)
        dp_ref[3] = dv_sc[...].astype(MX)

    def blk(s):
        return pl.BlockSpec((None, None, L, HEAD_DIM), lambda b, h, s=s: (s, b, 0, h))

    seq = pl.BlockSpec((None, L, HEAD_DIM), lambda b, h: (b, 0, h))
    return pl.pallas_call(
        body, name="hgrn_bwd", grid=(B, N_HEADS),
        in_specs=[blk(0), blk(1), blk(2), blk(3), blk(4), seq, seq,
                  pl.BlockSpec((4, HEAD_DIM), lambda b, h: (0, h)),
                  pl.BlockSpec((1, HEAD_DIM), lambda b, h: (0, h))],
        out_specs=[pl.BlockSpec((5, None, L, HEAD_DIM), lambda b, h: (0, b, 0, h)),
                   pl.BlockSpec((None, 2, HEAD_DIM), lambda b, h: (b, 0, h)),
                   pl.BlockSpec((None, 1, HEAD_DIM), lambda b, h: (b, 0, h))],
        out_shape=[jax.ShapeDtypeStruct((5, B, L, D_MODEL), MX),
                   jax.ShapeDtypeStruct((B, 2, D_MODEL), F32),
                   jax.ShapeDtypeStruct((B, 1, D_MODEL), F32)],
        scratch_shapes=[pltpu.VMEM((L, HEAD_DIM), F32)] * 8
                       + [pltpu.VMEM((L, HEAD_DIM), MX), pltpu.VMEM((nC, HEAD_DIM, HEAD_DIM), MX),
                          pltpu.VMEM((nC, HEAD_DIM, HEAD_DIM), F32), pltpu.VMEM((nC, HEAD_DIM, HEAD_DIM), F32)],
        compiler_params=_params(("parallel", "parallel")),
    )(proj, proj, proj, proj, proj, o, dya, lbl, norm_g)


def _dproj_select(s, a5_ref, p_ref, g2_ref):
    return jnp.where(s < 5, a5_ref[...], jnp.where(s == 5, p_ref[...], g2_ref[...]))


def _dproj_specs(tm, tile_axis):
    def ix(args):
        return args[tile_axis], args[1 - tile_axis]
    a5 = pl.BlockSpec((None, tm, D_MODEL), lambda *a: (jnp.minimum(ix(a)[1], 4), ix(a)[0], 0))
    p = pl.BlockSpec((tm, D_MODEL), lambda *a: (ix(a)[0], 0))
    g2 = pl.BlockSpec((None, tm, D_MODEL), lambda *a: (jnp.clip(ix(a)[1] - 6, 0, 1), ix(a)[0], 0))
    return [a5, p, g2]


def _inproj_bwd(x2, dh, dproj5, dp, dgab, g_mix, w_in_g):
    T = x2.shape[0]
    tm = min(512, T)

    def body(a5_ref, p_ref, g2_ref, w_ref, x_ref, dh_ref, g_ref, dx_ref, dg_ref, acc_sc):
        i, s = pl.program_id(0), pl.program_id(1)

        @pl.when(s == 0)
        def _():
            acc_sc[...] = jnp.zeros_like(acc_sc)

        @pl.when((i == 0) & (s == 0))
        def _():
            dg_ref[...] = jnp.zeros_like(dg_ref)

        acc_sc[...] += _dot_nt(_dproj_select(s, a5_ref, p_ref, g2_ref), w_ref[...])

        @pl.when(s == N_DEV - 1)
        def _():
            x = x_ref[...]
            r = lax.rsqrt(jnp.mean(x * x, axis=-1, keepdims=True) + RMS_EPS)
            xn = x * r
            du = acc_sc[...]
            dg_ref[...] += _fold8(du * xn)
            a = du * g_ref[...]
            dx_ref[...] = dh_ref[...] + r * a - xn * (r * jnp.mean(a * xn, axis=-1, keepdims=True))

    tile = pl.BlockSpec((tm, D_MODEL), lambda i, s: (i, 0))
    return pl.pallas_call(
        body, name="inproj_bwd", grid=(T // tm, N_DEV),
        in_specs=_dproj_specs(tm, 0) + [pl.BlockSpec((None, D_MODEL, D_MODEL), lambda i, s: (s, 0, 0)),
                                        tile, tile, pl.BlockSpec((1, D_MODEL), lambda i, s: (0, 0))],
        out_specs=[tile, pl.BlockSpec((8, D_MODEL), lambda i, s: (0, 0))],
        out_shape=[jax.ShapeDtypeStruct((T, D_MODEL), F32), jax.ShapeDtypeStruct((8, D_MODEL), F32)],
        scratch_shapes=[pltpu.VMEM((tm, D_MODEL), F32)],
        compiler_params=_params(("arbitrary", "arbitrary")),
    )(dproj5, dp, dgab, w_in_g, x2, dh, g_mix)


def _wgrad_in(u, dproj5, dp, dgab):
    T = u.shape[0]
    tm = min(512, T)

    def body(a5_ref, p_ref, g2_ref, u_ref, out_ref, acc_sc):
        s, t = pl.program_id(0), pl.program_id(1)

        @pl.when(t == 0)
        def _():
            acc_sc[...] = jnp.zeros_like(acc_sc)

        acc_sc[...] += _dot_tn(u_ref[...], _dproj_select(s, a5_ref, p_ref, g2_ref))

        @pl.when(t == pl.num_programs(1) - 1)
        def _():
            out_ref[...] = acc_sc[...].astype(MX)

    return pl.pallas_call(
        body, name="wgrad_in", grid=(N_DEV, T // tm),
        in_specs=_dproj_specs(tm, 1) + [pl.BlockSpec((tm, D_MODEL), lambda s, t: (t, 0))],
        out_specs=pl.BlockSpec((None, D_MODEL, D_MODEL), lambda s, t: (s, 0, 0)),
        out_shape=jax.ShapeDtypeStruct((N_DEV, D_MODEL, D_MODEL), MX),
        scratch_shapes=[pltpu.VMEM((D_MODEL, D_MODEL), F32)],
        compiler_params=_params(("parallel", "arbitrary")),
    )(dproj5, dp, dgab, u)


def _wgrad(a, g, name):
    Ba, T, K = a.shape
    Bg, _, Nn = g.shape
    nb = max(Ba, Bg)
    tm = min(512, T)
    nt = T // tm

    def body(a_ref, g_ref, out_ref, acc_sc):
        t = pl.program_id(1)

        @pl.when(t == 0)
        def _():
            acc_sc[...] = jnp.zeros_like(acc_sc)

        acc_sc[...] += _dot_tn(a_ref[...], g_ref[...])

        @pl.when(t == nt - 1)
        def _():
            out_ref[...] = acc_sc[...].astype(MX)

    return pl.pallas_call(
        body, name=name, grid=(nb, nt),
        in_specs=[pl.BlockSpec((None, tm, K), lambda s, t: (s if Ba > 1 else 0, t, 0)),
                  pl.BlockSpec((None, tm, Nn), lambda s, t: (s if Bg > 1 else 0, t, 0))],
        out_specs=pl.BlockSpec((None, K, Nn), lambda s, t: (s, 0, 0)),
        out_shape=jax.ShapeDtypeStruct((nb, K, Nn), MX),
        scratch_shapes=[pltpu.VMEM((K, Nn), F32)],
        compiler_params=_params(("parallel", "arbitrary")),
    )(a, g)


def _mesh_pos():
    return lax.axis_index("x"), lax.axis_index("y"), lax.axis_index("c")


def _all_gather(shards):
    n = len(shards)

    def body(*refs):
        xs, outs = refs[:n], refs[n:2 * n]
        send_sems, recv_sems, local_sems = refs[2 * n:]
        x, y, c = _mesh_pos()
        me, sibling = (x, y, c), (x, y, 1 - c)
        chips = [(1 - x, y), (x, 1 - y), (1 - x, 1 - y)]

        def copy(a, k, block, to, src=None):
            slot = outs[a].at[4 * block[0] + 2 * block[1] + block[2]]
            return pltpu.make_async_remote_copy(
                src_ref=slot if src is None else src, dst_ref=slot,
                send_sem=send_sems.at[7 * a + k], recv_sem=recv_sems.at[7 * a + k],
                device_id=to, device_id_type=pl.DeviceIdType.MESH)

        mine = [pltpu.make_async_copy(xs[a], outs[a].at[4 * x + 2 * y + c], local_sems.at[a]) for a in range(n)]
        for cp in mine:
            cp.start()
        first = []
        for a in range(n):
            first.append(copy(a, 0, me, sibling, src=xs[a]))
            first += [copy(a, 1 + j, me, (*chip, c), src=xs[a]) for j, chip in enumerate(chips)]
        for cp in first:
            cp.start()
        passed = []
        for j, chip in enumerate(chips):
            for a in range(n):
                copy(a, 1 + j, (*chip, c), me).wait_recv()
                fwd = copy(a, 4 + j, (*chip, c), sibling)
                fwd.start()
                passed.append(fwd)
        for a in range(n):
            copy(a, 0, sibling, me).wait_recv()
            for j, chip in enumerate(chips):
                copy(a, 4 + j, (*chip, 1 - c), me).wait_recv()
        for cp in first + passed:
            cp.wait_send()
        for cp in mine:
            cp.wait()

    return pl.pallas_call(
        body, name="all_gather_weights",
        in_specs=[ANY] * n, out_specs=[ANY] * n,
        out_shape=[jax.ShapeDtypeStruct((N_DEV,) + s.shape, s.dtype) for s in shards],
        scratch_shapes=[pltpu.SemaphoreType.DMA((7 * n,)), pltpu.SemaphoreType.DMA((7 * n,)),
                        pltpu.SemaphoreType.DMA((n,))],
    )(*shards)


def _device_of(p):
    return (p // 4, (p // 2) % 2, p % 2)


def _scatter_slices(grads):
    n = len(grads)

    def body(*refs):
        gs, outs = refs[:n], refs[n:2 * n]
        send_sems, recv_sems, local_sems = refs[2 * n:]
        x, y, c = _mesh_pos()
        me = 4 * x + 2 * y + c

        def copy(a, j):
            to = (me + j) % N_DEV
            return pltpu.make_async_remote_copy(
                src_ref=gs[a].at[to], dst_ref=outs[a].at[me],
                send_sem=send_sems.at[7 * a + j - 1], recv_sem=recv_sems.at[7 * a + j - 1],
                device_id=_device_of(to), device_id_type=pl.DeviceIdType.MESH)

        def arrival(a, j):
            frm = (me + N_DEV - j) % N_DEV
            return pltpu.make_async_remote_copy(
                src_ref=gs[a].at[frm], dst_ref=outs[a].at[frm],
                send_sem=send_sems.at[7 * a + j - 1], recv_sem=recv_sems.at[7 * a + j - 1],
                device_id=_device_of(frm), device_id_type=pl.DeviceIdType.MESH)

        mine = [pltpu.make_async_copy(gs[a].at[me], outs[a].at[me], local_sems.at[a]) for a in range(n)]
        for cp in mine:
            cp.start()
        sends = [copy(a, j) for j in range(1, N_DEV) for a in range(n)]
        for cp in sends:
            cp.start()
        for j in range(1, N_DEV):
            for a in range(n):
                arrival(a, j).wait_recv()
        for cp in sends:
            cp.wait_send()
        for cp in mine:
            cp.wait()

    return pl.pallas_call(
        body, name="scatter_grad_slices",
        in_specs=[ANY] * n, out_specs=[ANY] * n,
        out_shape=[jax.ShapeDtypeStruct(g.shape, g.dtype) for g in grads],
        scratch_shapes=[pltpu.SemaphoreType.DMA((7 * n,)), pltpu.SemaphoreType.DMA((7 * n,)),
                        pltpu.SemaphoreType.DMA((n,))],
    )(*grads)


def _all_reduce_small(v):
    R, C = v.shape

    def body(v_ref, out_ref, slots, send_sems, recv_sems):
        x, y, c = _mesh_pos()
        me = 4 * x + 2 * y + c

        def copy(j, to):
            return pltpu.make_async_remote_copy(
                src_ref=v_ref, dst_ref=slots.at[me],
                send_sem=send_sems.at[j - 1], recv_sem=recv_sems.at[j - 1],
                device_id=_device_of(to), device_id_type=pl.DeviceIdType.MESH)

        sends = [copy(j, (me + j) % N_DEV) for j in range(1, N_DEV)]
        for cp in sends:
            cp.start()
        slots[me] = v_ref[...]
        for j in range(1, N_DEV):
            frm = (me + N_DEV - j) % N_DEV
            pltpu.make_async_remote_copy(
                src_ref=v_ref, dst_ref=slots.at[frm], send_sem=send_sems.at[j - 1], recv_sem=recv_sems.at[j - 1],
                device_id=_device_of(frm), device_id_type=pl.DeviceIdType.MESH).wait_recv()
        for cp in sends:
            cp.wait_send()
        acc = slots[0]
        for p in range(1, N_DEV):
            acc = acc + slots[p]
        out_ref[...] = acc

    return pl.pallas_call(
        body, name="all_reduce_small",
        in_specs=[pl.BlockSpec(memory_space=pltpu.VMEM)], out_specs=pl.BlockSpec(memory_space=pltpu.VMEM),
        out_shape=jax.ShapeDtypeStruct((R, C), F32),
        scratch_shapes=[pltpu.VMEM((N_DEV, R, C), F32), pltpu.SemaphoreType.DMA((7,)), pltpu.SemaphoreType.DMA((7,))],
    )(v)


def _adamw_math(w, g, m, v):
    m = ADAM_B1 * m + (1.0 - ADAM_B1) * g
    v = ADAM_B2 * v + (1.0 - ADAM_B2) * (g * g)
    m_hat = m / (1.0 - ADAM_B1 ** ADAM_STEP)
    v_hat = v / (1.0 - ADAM_B2 ** ADAM_STEP)
    delta = -ADAM_LR * (m_hat / (jnp.sqrt(v_hat) + ADAM_EPS) + ADAM_WD * w)
    return delta, m, v


def _adamw_reduce(recv, w, m, v, name):
    R, C = w.shape
    tr = R if R <= 256 else 256
    while R % tr:
        tr //= 2

    def body(r_ref, w_ref, m_ref, v_ref, g_ref, d_ref, nm_ref, nv_ref):
        g = r_ref[0].astype(F32)
        for p in range(1, N_DEV):
            g = g + r_ref[p].astype(F32)
        d, nm, nv = _adamw_math(w_ref[...], g, m_ref[...], v_ref[...])
        g_ref[...] = g
        d_ref[...] = d
        nm_ref[...] = nm
        nv_ref[...] = nv

    tile = pl.BlockSpec((tr, C), lambda i: (i, 0))
    shp = jax.ShapeDtypeStruct((R, C), F32)
    return pl.pallas_call(
        body, name=name, grid=(R // tr,),
        in_specs=[pl.BlockSpec((N_DEV, tr, C), lambda i: (0, i, 0)), tile, tile, tile],
        out_specs=[tile] * 4, out_shape=[shp] * 4,
        compiler_params=_params(("parallel",)),
    )(recv, w, m, v)


def _adamw_small(g, w, m, v):
    def body(g_ref, w_ref, m_ref, v_ref, go_ref, d_ref, nm_ref, nv_ref):
        go_ref[...] = g_ref[...]
        for d in range(2):
            p0 = _sig(w_ref[8 + 2 * d:9 + 2 * d, :] - w_ref[9 + 2 * d:10 + 2 * d, :])
            dl0 = g_ref[12 + d:13 + d, :] * p0 * (1.0 - p0)
            go_ref[8 + 2 * d:9 + 2 * d, :] = dl0
            go_ref[9 + 2 * d:10 + 2 * d, :] = -dl0
            go_ref[12 + d:13 + d, :] = jnp.zeros((1, D_MODEL), F32)
        d, nm, nv = _adamw_math(w_ref[...], go_ref[...], m_ref[...], v_ref[...])
        d_ref[...] = d
        nm_ref[...] = nm
        nv_ref[...] = nv

    shp = jax.ShapeDtypeStruct(g.shape, F32)
    vm = pl.BlockSpec(memory_space=pltpu.VMEM)
    return pl.pallas_call(body, name="adamw_small", in_specs=[vm] * 4, out_specs=[vm] * 4, out_shape=[shp] * 4)(g, w, m, v)


def _local_step(x, tgt, g_mix, lb, norm_g, pool_w_full, pool_scale, g_ffn, g_final, w_in_g, wa, wb, wo, wfi_g, wfo_g):
    B, L, _ = x.shape
    T = B * L
    x2, tgt2 = x.reshape(T, D_MODEL), tgt.reshape(T, D_MODEL)

    proj, u = _rms_inproj(x2, g_mix, w_in_g)
    proj4 = proj.reshape(N_DEV, B, L, D_MODEL)
    o, ya = _hgrn_fwd(proj4, lb, norm_g, B, L)
    yb = _pool_fwd(proj4, pool_w_full, pool_scale, B, L)
    ya2, yb2 = ya.reshape(T, D_MODEL), yb.reshape(T, D_MODEL)
    za, zb, mg, h = _merge_out(x2, proj, ya2, yb2, wa, wb, wo)
    gate, up, hid, u2, dh2, dh2b, loss_p, dg_final = _ffn_fwd_loss(h, tgt2, g_ffn, g_final, wfi_g, wfo_g)
    loss = jnp.sum(loss_p[:, 0, 0])

    dgu, dh, dhb, dg_ffn = _ffn_bwd(h, dh2, dh2b, gate, up, g_ffn, wfi_g, wfo_g)
    d_wfo = _wgrad(hid, dh2b[None], "wgrad_ffn_out")
    d_wfi = _wgrad(u2[None], dgu.reshape(N_DEV, T, FF_BLOCK), "wgrad_ffn_in")
    dza, dzb, dgab, dya, dyb = _merge_bwd(dhb, proj, za, zb, wa, wb, wo)
    d_wo = _wgrad(mg[None], dhb[None], "wgrad_out")
    d_wa = _wgrad(ya2[None], dza[None], "wgrad_branch_a")
    d_wb = _wgrad(yb2[None], dzb[None], "wgrad_branch_b")
    dp, dpw_p, dps_p = _pool_bwd(proj4, dyb.reshape(B, L, D_MODEL), pool_w_full, pool_scale, B, L)
    dproj5, dlb_p, dng_p = _hgrn_bwd(proj4, o, dya.reshape(B, L, D_MODEL), lb, norm_g, B, L)
    dproj5 = dproj5.reshape(5, T, D_MODEL)
    dp2 = dp.reshape(T, D_MODEL)
    grad_x, dg_mix = _inproj_bwd(x2, dh, dproj5, dp2, dgab, g_mix, w_in_g)
    d_win = _wgrad_in(u, dproj5, dp2, dgab)

    small = dict(g_mix=dg_mix.sum(0), hgrn_norm_g=dng_p.sum((0, 1)), pool_scale=dps_p.sum((0, 1)),
                 g_ffn=dg_ffn.sum(0), g_final=dg_final.sum(0), lb=dlb_p.sum(0))
    big = dict(w_in=d_win, w_branch_a=d_wa[0], w_branch_b=d_wb[0], w_out=d_wo[0],
               w_ffn_in=d_wfi, w_ffn_out=d_wfo, pool_w=dpw_p.sum(0))
    return loss, grad_x.reshape(B, L, D_MODEL), big, small


def kernel(x, g_mix, w_in, lb_logits, hgrn_norm_g, pool_w, pool_scale, w_branch_a, w_branch_b, w_out, g_ffn, w_ffn_in, w_ffn_out, g_final, loss_target, m_g_mix, m_w_in, m_lb_logits, m_hgrn_norm_g, m_pool_w, m_pool_scale, m_w_branch_a, m_w_branch_b, m_w_out, m_g_ffn, m_w_ffn_in, m_w_ffn_out, m_g_final, v_g_mix, v_w_in, v_lb_logits, v_hgrn_norm_g, v_pool_w, v_pool_scale, v_w_branch_a, v_w_branch_b, v_w_out, v_g_ffn, v_w_ffn_in, v_w_ffn_out, v_g_final):
    me = 4 * lax.axis_index("x") + 2 * lax.axis_index("y") + lax.axis_index("c")
    rows = D_MODEL // N_DEV

    shards = [w_in[0].astype(MX), w_branch_a[0].astype(MX), w_branch_b[0].astype(MX), w_out[0].astype(MX),
              w_ffn_in[0].astype(MX), w_ffn_out[0].astype(MX), pool_w[0].reshape(4 * 32, POOL_DIM).astype(MX),
              jnp.pad(lb_logits.reshape(4, HEAD_DIM), ((0, 4), (0, 0)))]
    w_in_g, wa, wb, wo, wfi_g, wfo_g, pw_g, lbl_g = _all_gather(shards)
    wa, wb, wo = (w_.reshape(D_MODEL, D_MODEL) for w_ in (wa, wb, wo))
    wfo_g = wfo_g.reshape(4, FF_BLOCK, D_MODEL)
    pool_w_full = pw_g.reshape(N_DEV, 4, 32, POOL_DIM).transpose(1, 0, 2, 3).reshape(4, POOL_DIM, POOL_DIM)
    lbl = lbl_g[:, :4].transpose(1, 0, 2).reshape(4, D_MODEL)

    loss, grad_x, big, small = _local_step(
        x, loss_target, g_mix, lbl, hgrn_norm_g, pool_w_full, pool_scale, g_ffn, g_final[None],
        w_in_g, wa, wb, wo, wfi_g, wfo_g)
    loss = lax.psum(loss, ("x", "y", "c"))

    packed = jnp.zeros((16, D_MODEL), F32)
    names = ["g_mix", "hgrn_norm_g", "pool_scale", "g_ffn", "g_final"]
    for i, nme in enumerate(names):
        packed = packed.at[i].set(small[nme])
    packed = packed.at[5:7].set(small["lb"])
    red = _all_reduce_small(packed)
    dlb_mine = lax.dynamic_slice_in_dim(red[5:7], me * HEAD_DIM, HEAD_DIM, axis=1)

    sw = jnp.zeros((16, D_MODEL), F32)
    sm = jnp.zeros((16, D_MODEL), F32)
    sv = jnp.ones((16, D_MODEL), F32)
    smalls = [(g_mix, m_g_mix, v_g_mix), (hgrn_norm_g, m_hgrn_norm_g, v_hgrn_norm_g),
              (pool_scale, m_pool_scale, v_pool_scale), (g_ffn, m_g_ffn, v_g_ffn),
              (g_final[None], m_g_final[None], v_g_final[None])]
    for i, (w_, m_, v_) in enumerate(smalls):
        sw, sm, sv = sw.at[i].set(w_[0]), sm.at[i].set(m_[0]), sv.at[i].set(v_[0])
    sg = red.at[5:].set(0.0)
    sg = sg.at[12:14, :HEAD_DIM].set(dlb_mine)
    sw = sw.at[8:12, :HEAD_DIM].set(lb_logits.reshape(4, HEAD_DIM))
    sm = sm.at[8:12, :HEAD_DIM].set(m_lb_logits.reshape(4, HEAD_DIM))
    sv = sv.at[8:12, :HEAD_DIM].set(v_lb_logits.reshape(4, HEAD_DIM))
    sg, sd, snm, snv = _adamw_small(sg, sw, sm, sv)

    def small_out(arr, i, like):
        return arr[i].reshape(like.shape)

    def lb_out(arr):
        return arr[8:12, :HEAD_DIM].reshape(2, 2, HEAD_DIM)

    order = ["w_in", "w_branch_a", "w_branch_b", "w_out", "w_ffn_in", "w_ffn_out", "pool_w"]
    slices = [big["w_in"],
              big["w_branch_a"].reshape(N_DEV, rows, D_MODEL), big["w_branch_b"].reshape(N_DEV, rows, D_MODEL),
              big["w_out"].reshape(N_DEV, rows, D_MODEL),
              big["w_ffn_in"], big["w_ffn_out"].reshape(N_DEV, FF_BLOCK // 2, D_MODEL),
              big["pool_w"].reshape(4, N_DEV, 32, POOL_DIM).transpose(1, 0, 2, 3).reshape(N_DEV, 128, POOL_DIM).astype(MX)]
    recv = _scatter_slices(slices)
    params = dict(w_in=(w_in, m_w_in, v_w_in), w_branch_a=(w_branch_a, m_w_branch_a, v_w_branch_a),
                  w_branch_b=(w_branch_b, m_w_branch_b, v_w_branch_b), w_out=(w_out, m_w_out, v_w_out),
                  w_ffn_in=(w_ffn_in, m_w_ffn_in, v_w_ffn_in), w_ffn_out=(w_ffn_out, m_w_ffn_out, v_w_ffn_out),
                  pool_w=(pool_w, m_pool_w, v_pool_w))
    res = {}
    for nme, r in zip(order, recv):
        w_, m_, v_ = params[nme]
        shape2 = r.shape[1:]
        outs = _adamw_reduce(r, w_.reshape(shape2), m_.reshape(shape2), v_.reshape(shape2), "adamw_" + nme)
        res[nme] = [o_.reshape(w_.shape) for o_ in outs]

    def pick(k):
        small_src = [sg, sd, snm, snv][k]
        return [small_out(small_src, 0, g_mix), res["w_in"][k], lb_out(small_src), small_out(small_src, 1, hgrn_norm_g),
                res["pool_w"][k], small_out(small_src, 2, pool_scale), res["w_branch_a"][k], res["w_branch_b"][k],
                res["w_out"][k], small_out(small_src, 3, g_ffn), res["w_ffn_in"][k], res["w_ffn_out"][k],
                small_out(small_src, 4, g_final)]

    return (loss, grad_x, *pick(0), *pick(1), *pick(2), *pick(3))
```

```python
import functools

import jax
import jax.numpy as jnp
from jax import lax
from jax.experimental import pallas as pl
from jax.experimental.pallas import tpu as pltpu

F32 = jnp.float32
MX = jnp.bfloat16

D_MODEL = 1024
N_HEADS = 8
HEAD_DIM = 128
CHUNK = 16
POOL_WINDOWS = (2, 4, 8, 16)
POOL_DIM = 256
FF_BLOCK = 704
N_DEV = 8
RMS_EPS = 1e-6
ADAM_LR, ADAM_B1, ADAM_B2, ADAM_EPS, ADAM_WD, ADAM_STEP = 0.001, 0.9, 0.999, 1e-08, 0.01, 10
VMEM_LIMIT = 56 * 1024 * 1024
ANY = pl.BlockSpec(memory_space=pl.ANY)


def _params(sem=None):
    return pltpu.CompilerParams(dimension_semantics=sem, vmem_limit_bytes=VMEM_LIMIT)


def _dot(a, b):
    return lax.dot_general(a.astype(MX), b.astype(MX), (((1,), (0,)), ((), ())), preferred_element_type=F32)


def _dot_nt(a, b):
    return lax.dot_general(a.astype(MX), b.astype(MX), (((1,), (1,)), ((), ())), preferred_element_type=F32)


def _dot_tn(a, b):
    return lax.dot_general(a.astype(MX), b.astype(MX), (((0,), (0,)), ((), ())), preferred_element_type=F32)


def _sig(x):
    return 1.0 / (1.0 + jnp.exp(-x))


def _fold8(v):
    return v.reshape(v.shape[0] // 8, 8, v.shape[1]).sum(axis=0)


def _shift_rows(x, s):
    n = x.shape[0]
    row = lax.broadcasted_iota(jnp.int32, x.shape, 0)
    if s > 0:
        return jnp.where(row >= s, pltpu.roll(x, s, 0), 0.0)
    return jnp.where(row < n + s, pltpu.roll(x, n + s, 0), 0.0)


def _cumsum_rows(x, rev):
    s = 1
    while s < x.shape[0]:
        x = x + _shift_rows(x, -s if rev else s)
        s *= 2
    return x


def _rms_inproj(x2, g_mix, w_in_g):
    T = x2.shape[0]
    tm = min(512, T)

    def body(x_ref, g_ref, w_ref, proj_ref, u_ref, u_sc):
        @pl.when(pl.program_id(1) == 0)
        def _():
            x = x_ref[...]
            r = lax.rsqrt(jnp.mean(x * x, axis=-1, keepdims=True) + RMS_EPS)
            u = (x * r * g_ref[...]).astype(MX)
            u_sc[...] = u
            u_ref[...] = u
        proj_ref[...] = jnp.dot(u_sc[...], w_ref[...], preferred_element_type=F32)

    return pl.pallas_call(
        body, name="rms_inproj", grid=(T // tm, N_DEV),
        in_specs=[pl.BlockSpec((tm, D_MODEL), lambda i, k: (i, 0)),
                  pl.BlockSpec((1, D_MODEL), lambda i, k: (0, 0)),
                  pl.BlockSpec((None, D_MODEL, D_MODEL), lambda i, k: (k, 0, 0))],
        out_specs=[pl.BlockSpec((None, tm, D_MODEL), lambda i, k: (k, i, 0)),
                   pl.BlockSpec((tm, D_MODEL), lambda i, k: (i, 0))],
        out_shape=[jax.ShapeDtypeStruct((N_DEV, T, D_MODEL), F32), jax.ShapeDtypeStruct((T, D_MODEL), MX)],
        scratch_shapes=[pltpu.VMEM((tm, D_MODEL), MX)],
        compiler_params=_params(("parallel", "arbitrary")),
    )(x2, g_mix, w_in_g)


HBLK = 128


def _seg_cumsum(x, rev):
    n = x.shape[0]
    pos = lax.broadcasted_iota(jnp.int32, x.shape, 0) & (CHUNK - 1)
    s = 1
    while s < CHUNK:
        if rev:
            x = x + jnp.where(pos < CHUNK - s, pltpu.roll(x, n - s, 0), 0.0)
        else:
            x = x + jnp.where(pos >= s, pltpu.roll(x, s, 0), 0.0)
        s *= 2
    return x


def _block_gates(q_r, f_r, lb_row, rev):
    sq = _sig(q_r)
    q = q_r * sq
    sg = _sig(f_r)
    f = lb_row + (1.0 - lb_row) * sg
    k = 1.0 - f
    lf = jnp.log(f)
    pre = _seg_cumsum(lf, False)
    suf = _seg_cumsum(lf, True)
    tot = pre + suf - lf
    b = suf if rev else pre
    eb = jnp.exp(b)
    enb = jnp.exp(-b)
    eend = jnp.exp(tot - b)
    return dict(sq=sq, sg=sg, f=f, eb=eb, enb=enb, eend=eend, dec=jnp.exp(tot),
                P=q * eb, Kt=k * enb, Ke=k * eend)


def _block_mask(rev, transposed=False):
    ri = lax.broadcasted_iota(jnp.int32, (HBLK, HBLK), 0)
    ci = lax.broadcasted_iota(jnp.int32, (HBLK, HBLK), 1)
    same = (ri // CHUNK) == (ci // CHUNK)
    return same & ((ci >= ri) if rev != transposed else (ci <= ri))


def _chunk_rows(c):
    return pl.ds(pl.multiple_of(c * CHUNK, CHUNK), CHUNK)


def _chunk_outer_products(a, b, out_sc, kb):
    a, b = a.astype(MX), b.astype(MX)
    for u in range(HBLK // CHUNK):
        r = slice(u * CHUNK, (u + 1) * CHUNK)
        out_sc[kb * (HBLK // CHUNK) + u] = _dot_tn(a[r, :], b[r, :])


BLOCK_UNROLL = 2


def _block_loop(n_blocks, fn, init):
    def group(kg, carry):
        for u in range(BLOCK_UNROLL):
            carry = fn(kg * BLOCK_UNROLL + u, carry)
        return carry

    return lax.fori_loop(0, n_blocks // BLOCK_UNROLL, group, init)


SEQ_UNROLL = 16


def _chunk_loop(n_chunks, descending, step, init):
    def group(jg, carry):
        for u in range(SEQ_UNROLL):
            j = jg * SEQ_UNROLL + u
            carry = step(n_chunks - 1 - j if descending else j, carry)
        return carry

    return lax.fori_loop(0, n_chunks // SEQ_UNROLL, group, init)


def _lower_bounds(lbl_ref):
    return _sig(lbl_ref[0:1, :] - lbl_ref[1:2, :]), _sig(lbl_ref[2:3, :] - lbl_ref[3:4, :])


def _hgrn_fwd(proj, lbl, norm_g, B, L, shards):
    nC = L // CHUNK
    n = len(shards)

    def body(*refs):
        q_ref, ff_ref, fb_ref, i_ref, og_ref, lbl_ref, ng_ref = refs[:7]
        o_ref, ya_ref = refs[7 + n:9 + n]
        o_sc, p_sc, dec_sc, upd_sc = refs[9 + 2 * n:13 + 2 * n]
        step_id = pl.program_id(0) * N_HEADS + pl.program_id(1)
        gather = _Exchange(refs[7:7 + n], refs[9 + n:9 + 2 * n], refs[13 + 2 * n:], gather=True)

        @pl.when(step_id == 0)
        def _():
            gather.start()

        lb_f, lb_b = _lower_bounds(lbl_ref)
        o_sc[...] = jnp.zeros_like(o_sc)

        def run_dir(f_ref, lb_row, rev):
            def block(kb, carry):
                rows = pl.ds(pl.multiple_of(kb * HBLK, HBLK), HBLK)
                g = _block_gates(q_ref[rows, :], f_ref[rows, :], lb_row, rev)
                v = i_ref[rows, :]
                a = jnp.where(_block_mask(rev), _dot_nt(g["P"], g["Kt"]), 0.0)
                o_sc[rows, :] += _dot(a, v)
                p_sc[rows, :] = g["P"].astype(MX)
                dec_sc[rows, :] = g["dec"]
                _chunk_outer_products(v, g["Ke"], upd_sc, kb)
                return carry

            _block_loop(L // HBLK, block, 0)

            def step(c, st):
                rows = _chunk_rows(c)
                o_sc[rows, :] += _dot_nt(p_sc[rows, :], st)
                dec = dec_sc[pl.ds(pl.multiple_of(c * CHUNK, CHUNK), 1), :]
                return st * dec + upd_sc[c]

            _chunk_loop(nC, rev, step, jnp.zeros((HEAD_DIM, HEAD_DIM), F32))

        run_dir(ff_ref, lb_f, False)
        run_dir(fb_ref, lb_b, True)
        o = o_sc[...]
        o_ref[...] = o
        on = o * lax.rsqrt(jnp.mean(o * o, axis=-1, keepdims=True) + RMS_EPS)
        og = og_ref[...]
        ya_ref[...] = ((on * ng_ref[...]) * (og * _sig(og))).astype(MX)

        @pl.when(step_id == B * N_HEADS - 1)
        def _():
            gather.wait()

    def blk(s):
        return pl.BlockSpec((None, None, L, HEAD_DIM), lambda b, h, s=s: (s, b, 0, h))

    out_blk = pl.BlockSpec((None, L, HEAD_DIM), lambda b, h: (b, 0, h))
    outs = pl.pallas_call(
        body, name="hgrn_fwd", grid=(B, N_HEADS),
        in_specs=[blk(0), blk(1), blk(2), blk(3), blk(4),
                  pl.BlockSpec((4, HEAD_DIM), lambda b, h: (0, h)),
                  pl.BlockSpec((1, HEAD_DIM), lambda b, h: (0, h))] + [ANY] * n,
        out_specs=[out_blk, out_blk] + [ANY] * n,
        out_shape=[jax.ShapeDtypeStruct((B, L, D_MODEL), F32), jax.ShapeDtypeStruct((B, L, D_MODEL), MX)]
                  + _Exchange.out_shapes(shards, True),
        scratch_shapes=[pltpu.VMEM((L, HEAD_DIM), F32), pltpu.VMEM((L, HEAD_DIM), MX),
                        pltpu.VMEM((L, HEAD_DIM), F32), pltpu.VMEM((nC, HEAD_DIM, HEAD_DIM), F32)]
                       + _Exchange.scratch(n),
        compiler_params=_params(("arbitrary", "arbitrary")),
    )(proj, proj, proj, proj, proj, lbl, norm_g, *shards)
    return outs[0], outs[1], outs[2:]


POOL_PAD = 8


def _pool_window(p, ext_sc, half, adjoint):
    L = p.shape[0]
    n = L + 2 * POOL_PAD
    ext_sc[0:POOL_PAD, :] = jnp.zeros((POOL_PAD, p.shape[1]), F32)
    ext_sc[POOL_PAD + L:n, :] = jnp.zeros((POOL_PAD, p.shape[1]), F32)
    ext_sc[POOL_PAD:POOL_PAD + L, :] = p
    x = ext_sc[...]
    s = x + pltpu.roll(x, 1 if adjoint else n - 1, 0)
    w = 1
    while w < half:
        s = pltpu.roll(s, w, 0) + pltpu.roll(s, n - w, 0)
        w *= 2
    ext_sc[...] = s
    return ext_sc[POOL_PAD:POOL_PAD + L, :]


def _pool_count(L, half):
    t = lax.broadcasted_iota(jnp.int32, (L, 1), 0)
    lo = jnp.clip(t - half + 1, 0, L)
    hi = jnp.clip(t + half + 1, 0, L)
    return (hi - lo).astype(F32)


def _pool_fwd(proj, pool_w_full, pool_scale, B, L):
    def body(p_ref, w_ref, s_ref, yb_ref, ext_sc):
        for g, win in enumerate(POOL_WINDOWS):
            cols = slice(g * POOL_DIM, (g + 1) * POOL_DIM)
            p = p_ref[:, cols]
            y = _pool_window(p, ext_sc, win // 2, False) / _pool_count(L, win // 2) - p
            yb_ref[:, cols] = (_dot(y, w_ref[g]) * s_ref[:, cols]).astype(MX)

    return pl.pallas_call(
        body, name="pool_fwd", grid=(B,),
        in_specs=[pl.BlockSpec((None, None, L, D_MODEL), lambda b: (5, b, 0, 0)),
                  pl.BlockSpec((4, POOL_DIM, POOL_DIM), lambda b: (0, 0, 0)),
                  pl.BlockSpec((1, D_MODEL), lambda b: (0, 0))],
        out_specs=pl.BlockSpec((None, L, D_MODEL), lambda b: (b, 0, 0)),
        out_shape=jax.ShapeDtypeStruct((B, L, D_MODEL), MX),
        scratch_shapes=[pltpu.VMEM((L + 2 * POOL_PAD, POOL_DIM), F32)],
        compiler_params=_params(("parallel",)),
    )(proj, pool_w_full, pool_scale)


def _merge_out(x2, proj, ya, yb, wa, wb, wo):
    T = x2.shape[0]
    tm = min(512, T)

    def body(x_ref, ga_ref, gb_ref, ya_ref, yb_ref, wa_ref, wb_ref, wo_ref, za_ref, zb_ref, mg_ref, h_ref):
        za = jnp.dot(ya_ref[...], wa_ref[...], preferred_element_type=F32)
        zb = jnp.dot(yb_ref[...], wb_ref[...], preferred_element_type=F32)
        mg = (_sig(ga_ref[...]) * za + _sig(gb_ref[...]) * zb).astype(MX)
        za_ref[...] = za
        zb_ref[...] = zb
        mg_ref[...] = mg
        h_ref[...] = x_ref[...] + jnp.dot(mg, wo_ref[...], preferred_element_type=F32)

    tile = pl.BlockSpec((tm, D_MODEL), lambda i: (i, 0))
    wspec = pl.BlockSpec((D_MODEL, D_MODEL), lambda i: (0, 0))
    return pl.pallas_call(
        body, name="merge_out", grid=(T // tm,),
        in_specs=[tile,
                  pl.BlockSpec((None, tm, D_MODEL), lambda i: (6, i, 0)),
                  pl.BlockSpec((None, tm, D_MODEL), lambda i: (7, i, 0)),
                  tile, tile, wspec, wspec, wspec],
        out_specs=[tile, tile, tile, tile],
        out_shape=[jax.ShapeDtypeStruct((T, D_MODEL), F32), jax.ShapeDtypeStruct((T, D_MODEL), F32),
                   jax.ShapeDtypeStruct((T, D_MODEL), MX), jax.ShapeDtypeStruct((T, D_MODEL), F32)],
        compiler_params=_params(("parallel",)),
    )(x2, proj, proj, ya, yb, wa, wb, wo)


def _ffn_fwd_loss(h, tgt, g_ffn, g_final, wfi_g, wfo_g):
    T = h.shape[0]
    tm = min(512, T)
    nT = T // tm

    def body(h_ref, t_ref, gf_ref, gl_ref, wg_ref, wu_ref, wo_ref,
             gate_ref, up_ref, hid_ref, u2_ref, dh2_ref, dh2b_ref, loss_ref, dgl_ref, u2_sc, acc_sc):
        i, j = pl.program_id(0), pl.program_id(1)

        @pl.when(j == 0)
        def _():
            hh = h_ref[...]
            r = lax.rsqrt(jnp.mean(hh * hh, axis=-1, keepdims=True) + RMS_EPS)
            u2 = (hh * r * gf_ref[...]).astype(MX)
            u2_sc[...] = u2
            u2_ref[...] = u2
            acc_sc[...] = jnp.zeros_like(acc_sc)

        @pl.when((i == 0) & (j == 0))
        def _():
            dgl_ref[...] = jnp.zeros_like(dgl_ref)

        gate = jnp.dot(u2_sc[...], wg_ref[...], preferred_element_type=F32)
        up = jnp.dot(u2_sc[...], wu_ref[...], preferred_element_type=F32)
        hid = ((gate * _sig(gate)) * up).astype(MX)
        gate_ref[...] = gate
        up_ref[...] = up
        hid_ref[...] = hid
        acc_sc[...] += jnp.dot(hid, wo_ref[...], preferred_element_type=F32)

        @pl.when(j == 3)
        def _():
            h2 = h_ref[...] + acc_sc[...]
            r = lax.rsqrt(jnp.mean(h2 * h2, axis=-1, keepdims=True) + RMS_EPS)
            hn = h2 * r
            gl = gl_ref[...]
            err = hn * gl - t_ref[...]
            tok = jnp.mean(err * err, axis=-1, keepdims=True)
            loss_ref[...] = jnp.full(loss_ref.shape, 0.5 * jnp.sum(tok), F32)
            dy = err * (1.0 / D_MODEL)
            dgl_ref[...] += _fold8(dy * hn)
            a = dy * gl
            dh2 = r * a - hn * (r * jnp.mean(a * hn, axis=-1, keepdims=True))
            dh2_ref[...] = dh2
            dh2b_ref[...] = dh2.astype(MX)

    tile = pl.BlockSpec((tm, D_MODEL), lambda i, j: (i, 0))
    vec = pl.BlockSpec((1, D_MODEL), lambda i, j: (0, 0))
    ftile = pl.BlockSpec((None, tm, FF_BLOCK), lambda i, j: (j, i, 0))
    return pl.pallas_call(
        body, name="ffn_fwd_loss", grid=(nT, 4),
        in_specs=[tile, tile, vec, vec,
                  pl.BlockSpec((None, D_MODEL, FF_BLOCK), lambda i, j: (j, 0, 0)),
                  pl.BlockSpec((None, D_MODEL, FF_BLOCK), lambda i, j: (j + 4, 0, 0)),
                  pl.BlockSpec((None, FF_BLOCK, D_MODEL), lambda i, j: (j, 0, 0))],
        out_specs=[ftile, ftile, ftile, tile, tile, tile,
                   pl.BlockSpec((None, 8, 128), lambda i, j: (i, 0, 0)),
                   pl.BlockSpec((8, D_MODEL), lambda i, j: (0, 0))],
        out_shape=[jax.ShapeDtypeStruct((4, T, FF_BLOCK), F32), jax.ShapeDtypeStruct((4, T, FF_BLOCK), F32),
                   jax.ShapeDtypeStruct((4, T, FF_BLOCK), MX), jax.ShapeDtypeStruct((T, D_MODEL), MX),
                   jax.ShapeDtypeStruct((T, D_MODEL), F32), jax.ShapeDtypeStruct((T, D_MODEL), MX),
                   jax.ShapeDtypeStruct((nT, 8, 128), F32), jax.ShapeDtypeStruct((8, D_MODEL), F32)],
        scratch_shapes=[pltpu.VMEM((tm, D_MODEL), MX), pltpu.VMEM((tm, D_MODEL), F32)],
        compiler_params=_params(("arbitrary", "arbitrary")),
    )(h, tgt, g_ffn, g_final, wfi_g, wfi_g, wfo_g)


def _ffn_bwd(h, dh2, dh2b, gate, up, g_ffn, wfi_g, wfo_g):
    T = h.shape[0]
    tm = min(512, T)

    def body(h_ref, dh2_ref, dh2b_ref, gate_ref, up_ref, gf_ref, wg_ref, wu_ref, wo_ref,
             dgu_ref, dh_ref, dhb_ref, dgf_ref, acc_sc):
        i, j = pl.program_id(0), pl.program_id(1)

        @pl.when(j == 0)
        def _():
            acc_sc[...] = jnp.zeros_like(acc_sc)

        @pl.when((i == 0) & (j == 0))
        def _():
            dgf_ref[...] = jnp.zeros_like(dgf_ref)

        dhid = _dot_nt(dh2b_ref[...], wo_ref[...])
        gate, up = gate_ref[...], up_ref[...]
        sg = _sig(gate)
        dgate = (dhid * up * (sg * (1.0 + gate * (1.0 - sg)))).astype(MX)
        dup = (dhid * (gate * sg)).astype(MX)
        dgu_ref[0] = dgate
        dgu_ref[1] = dup
        acc_sc[...] += _dot_nt(dgate, wg_ref[...]) + _dot_nt(dup, wu_ref[...])

        @pl.when(j == 3)
        def _():
            hh = h_ref[...]
            r = lax.rsqrt(jnp.mean(hh * hh, axis=-1, keepdims=True) + RMS_EPS)
            hn = hh * r
            du2 = acc_sc[...]
            dgf_ref[...] += _fold8(du2 * hn)
            a = du2 * gf_ref[...]
            dh = dh2_ref[...] + r * a - hn * (r * jnp.mean(a * hn, axis=-1, keepdims=True))
            dh_ref[...] = dh
            dhb_ref[...] = dh.astype(MX)

    tile = pl.BlockSpec((tm, D_MODEL), lambda i, j: (i, 0))
    ftile = pl.BlockSpec((None, tm, FF_BLOCK), lambda i, j: (j, i, 0))
    return pl.pallas_call(
        body, name="ffn_bwd", grid=(T // tm, 4),
        in_specs=[tile, tile, tile, ftile, ftile,
                  pl.BlockSpec((1, D_MODEL), lambda i, j: (0, 0)),
                  pl.BlockSpec((None, D_MODEL, FF_BLOCK), lambda i, j: (j, 0, 0)),
                  pl.BlockSpec((None, D_MODEL, FF_BLOCK), lambda i, j: (j + 4, 0, 0)),
                  pl.BlockSpec((None, FF_BLOCK, D_MODEL), lambda i, j: (j, 0, 0))],
        out_specs=[pl.BlockSpec((2, None, tm, FF_BLOCK), lambda i, j: (0, j, i, 0)),
                   tile, tile, pl.BlockSpec((8, D_MODEL), lambda i, j: (0, 0))],
        out_shape=[jax.ShapeDtypeStruct((2, 4, T, FF_BLOCK), MX),
                   jax.ShapeDtypeStruct((T, D_MODEL), F32), jax.ShapeDtypeStruct((T, D_MODEL), MX),
                   jax.ShapeDtypeStruct((8, D_MODEL), F32)],
        scratch_shapes=[pltpu.VMEM((tm, D_MODEL), F32)],
        compiler_params=_params(("arbitrary", "arbitrary")),
    )(h, dh2, dh2b, gate, up, g_ffn, wfi_g, wfi_g, wfo_g)


def _merge_bwd(dhb, proj, za, zb, wa, wb, wo):
    T = dhb.shape[0]
    tm = min(512, T)

    def body(dh_ref, ga_ref, gb_ref, za_ref, zb_ref, wa_ref, wb_ref, wo_ref,
             dza_ref, dzb_ref, dgab_ref, dya_ref, dyb_ref):
        dm = _dot_nt(dh_ref[...], wo_ref[...])
        sa, sb = _sig(ga_ref[...]), _sig(gb_ref[...])
        dza = (dm * sa).astype(MX)
        dzb = (dm * sb).astype(MX)
        dza_ref[...] = dza
        dzb_ref[...] = dzb
        dgab_ref[0] = (dm * za_ref[...] * (sa * (1.0 - sa))).astype(MX)
        dgab_ref[1] = (dm * zb_ref[...] * (sb * (1.0 - sb))).astype(MX)
        dya_ref[...] = _dot_nt(dza, wa_ref[...])
        dyb_ref[...] = _dot_nt(dzb, wb_ref[...])

    tile = pl.BlockSpec((tm, D_MODEL), lambda i: (i, 0))
    wspec = pl.BlockSpec((D_MODEL, D_MODEL), lambda i: (0, 0))
    return pl.pallas_call(
        body, name="merge_bwd", grid=(T // tm,),
        in_specs=[tile,
                  pl.BlockSpec((None, tm, D_MODEL), lambda i: (6, i, 0)),
                  pl.BlockSpec((None, tm, D_MODEL), lambda i: (7, i, 0)),
                  tile, tile, wspec, wspec, wspec],
        out_specs=[tile, tile, pl.BlockSpec((2, tm, D_MODEL), lambda i: (0, i, 0)), tile, tile],
        out_shape=[jax.ShapeDtypeStruct((T, D_MODEL), MX), jax.ShapeDtypeStruct((T, D_MODEL), MX),
                   jax.ShapeDtypeStruct((2, T, D_MODEL), MX),
                   jax.ShapeDtypeStruct((T, D_MODEL), F32), jax.ShapeDtypeStruct((T, D_MODEL), F32)],
        compiler_params=_params(("parallel",)),
    )(dhb, proj, proj, za, zb, wa, wb, wo)


def _pool_bwd(proj, dyb, pool_w_full, pool_scale, B, L):
    def body(p_ref, dy_ref, w_ref, s_ref, dp_ref, dw_ref, ds_ref, ext_sc):
        for g, win in enumerate(POOL_WINDOWS):
            cols = slice(g * POOL_DIM, (g + 1) * POOL_DIM)
            p = p_ref[:, cols]
            cnt = _pool_count(L, win // 2)
            y = _pool_window(p, ext_sc, win // 2, False) / cnt - p
            z = _dot(y, w_ref[g])
            dyb_g = dy_ref[:, cols]
            ds_ref[:, cols] = jnp.sum(dyb_g * z, axis=0, keepdims=True)
            dz = dyb_g * s_ref[:, cols]
            dw_ref[g] = _dot_tn(y, dz)
            dy = _dot_nt(dz, w_ref[g])
            dp_ref[:, cols] = (_pool_window(dy / cnt, ext_sc, win // 2, True) - dy).astype(MX)

    seq = pl.BlockSpec((None, L, D_MODEL), lambda b: (b, 0, 0))
    return pl.pallas_call(
        body, name="pool_bwd", grid=(B,),
        in_specs=[pl.BlockSpec((None, None, L, D_MODEL), lambda b: (5, b, 0, 0)), seq,
                  pl.BlockSpec((4, POOL_DIM, POOL_DIM), lambda b: (0, 0, 0)),
                  pl.BlockSpec((1, D_MODEL), lambda b: (0, 0))],
        out_specs=[seq, pl.BlockSpec((None, 4, POOL_DIM, POOL_DIM), lambda b: (b, 0, 0, 0)),
                   pl.BlockSpec((None, 1, D_MODEL), lambda b: (b, 0, 0))],
        out_shape=[jax.ShapeDtypeStruct((B, L, D_MODEL), MX),
                   jax.ShapeDtypeStruct((B, 4, POOL_DIM, POOL_DIM), F32),
                   jax.ShapeDtypeStruct((B, 1, D_MODEL), F32)],
        scratch_shapes=[pltpu.VMEM((L + 2 * POOL_PAD, POOL_DIM), F32)],
        compiler_params=_params(("parallel",)),
    )(proj, dyb, pool_w_full, pool_scale)


def _hgrn_bwd(proj, o, dya, lbl, norm_g, B, L, grads):
    nC = L // CHUNK
    n = len(grads)

    def body(*refs):
        q_ref, ff_ref, fb_ref, i_ref, og_ref, o_ref, dy_ref, lbl_ref, ng_ref = refs[:9]
        dp_ref, dlb_ref, dng_ref = refs[9 + n:12 + n]
        (do_sc, dq_sc, dv_sc, dP_sc, dKt_sc, dKe_sc, dbl_sc, dec_sc, ke_sc, ck_sc, upd_sc,
         dupd_sc) = refs[12 + 2 * n:24 + 2 * n]
        step_id = pl.program_id(0) * N_HEADS + pl.program_id(1)
        scatter = _Exchange(refs[9:9 + n], refs[12 + n:12 + 2 * n], refs[24 + 2 * n:], gather=False)

        @pl.when(step_id == 0)
        def _():
            scatter.start()

        lb_f, lb_b = _lower_bounds(lbl_ref)
        o_ = o_ref[...]
        r = lax.rsqrt(jnp.mean(o_ * o_, axis=-1, keepdims=True) + RMS_EPS)
        on = o_ * r
        og = og_ref[...]
        sog = _sig(og)
        dy = dy_ref[...]
        ng = ng_ref[...]
        dp_ref[4] = (dy * (on * ng) * (sog * (1.0 + og * (1.0 - sog)))).astype(MX)
        dn = dy * (og * sog)
        dng_ref[...] = jnp.sum(dn * on, axis=0, keepdims=True)
        don = dn * ng
        do_sc[...] = r * don - on * (r * jnp.mean(don * on, axis=-1, keepdims=True))

        dq_sc[...] = jnp.zeros_like(dq_sc)
        dv_sc[...] = jnp.zeros_like(dv_sc)

        def run_dir(f_ref, lb_row, rev, slot):
            def block(kb, carry):
                rows = pl.ds(pl.multiple_of(kb * HBLK, HBLK), HBLK)
                g = _block_gates(q_ref[rows, :], f_ref[rows, :], lb_row, rev)
                v = i_ref[rows, :]
                do = do_sc[rows, :]
                m, mt = _block_mask(rev), _block_mask(rev, True)
                at = jnp.where(mt, _dot_nt(g["Kt"], g["P"]), 0.0)
                da = jnp.where(m, _dot_nt(do, v), 0.0)
                dat = jnp.where(mt, _dot_nt(v, do), 0.0)
                dv_sc[rows, :] += _dot(at, do)
                dP_sc[rows, :] = _dot(da, g["Kt"])
                dKt_sc[rows, :] = _dot(dat, g["P"])
                ke_sc[rows, :] = g["Ke"].astype(MX)
                dec_sc[rows, :] = g["dec"]
                _chunk_outer_products(v, g["Ke"], upd_sc, kb)
                _chunk_outer_products(do, g["P"], dupd_sc, kb)
                return carry

            _block_loop(L // HBLK, block, 0)

            def fstep(c, st):
                rows = _chunk_rows(c)
                ck_sc[c] = st.astype(MX)
                dP_sc[rows, :] += _dot(do_sc[rows, :], st)
                dec = dec_sc[pl.ds(pl.multiple_of(c * CHUNK, CHUNK), 1), :]
                return st * dec + upd_sc[c]

            _chunk_loop(nC, rev, fstep, jnp.zeros((HEAD_DIM, HEAD_DIM), F32))

            def bstep(c, dst):
                rows = _chunk_rows(c)
                dec = dec_sc[pl.ds(pl.multiple_of(c * CHUNK, CHUNK), 1), :]
                dKe_sc[rows, :] = _dot(i_ref[rows, :], dst)
                dv_sc[rows, :] += _dot_nt(ke_sc[rows, :], dst)
                dbl = dec * jnp.sum(dst * ck_sc[c].astype(F32), axis=0, keepdims=True)
                dbl_sc[rows, :] = jnp.broadcast_to(dbl, (CHUNK, HEAD_DIM))
                return dst * dec + dupd_sc[c]

            _chunk_loop(nC, not rev, bstep, jnp.zeros((HEAD_DIM, HEAD_DIM), F32))

            def finish(kb, acc):
                rows = pl.ds(pl.multiple_of(kb * HBLK, HBLK), HBLK)
                q_r = q_ref[rows, :]
                g = _block_gates(q_r, f_ref[rows, :], lb_row, rev)
                dP, dkt, dke = dP_sc[rows, :], dKt_sc[rows, :], dKe_sc[rows, :]
                e = dke * g["Ke"]
                dlf = (_seg_cumsum(dP * g["P"] - dkt * g["Kt"], not rev) + _seg_cumsum(e, rev) - e
                       + dbl_sc[rows, :])
                df = dlf / g["f"] - (dkt * g["enb"] + dke * g["eend"])
                dp_ref[slot, rows, :] = (df * (1.0 - lb_row) * (g["sg"] * (1.0 - g["sg"]))).astype(MX)
                dq_sc[rows, :] += (dP * g["eb"]) * (g["sq"] * (1.0 + q_r * (1.0 - g["sq"])))
                return acc + jnp.sum(df * (1.0 - g["sg"]), axis=0, keepdims=True)

            dlb_ref[slot - 1:slot, :] = _block_loop(L // HBLK, finish, jnp.zeros((1, HEAD_DIM), F32))

        run_dir(ff_ref, lb_f, False, 1)
        run_dir(fb_ref, lb_b, True, 2)
        dp_ref[0] = dq_sc[...].astype(MX)
        dp_ref[3] = dv_sc[...].astype(MX)

        @pl.when(step_id == B * N_HEADS - 1)
        def _():
            scatter.wait()

    def blk(s):
        return pl.BlockSpec((None, None, L, HEAD_DIM), lambda b, h, s=s: (s, b, 0, h))

    seq = pl.BlockSpec((None, L, HEAD_DIM), lambda b, h: (b, 0, h))
    outs = pl.pallas_call(
        body, name="hgrn_bwd", grid=(B, N_HEADS),
        in_specs=[blk(0), blk(1), blk(2), blk(3), blk(4), seq, seq,
                  pl.BlockSpec((4, HEAD_DIM), lambda b, h: (0, h)),
                  pl.BlockSpec((1, HEAD_DIM), lambda b, h: (0, h))] + [ANY] * n,
        out_specs=[pl.BlockSpec((5, None, L, HEAD_DIM), lambda b, h: (0, b, 0, h)),
                   pl.BlockSpec((None, 2, HEAD_DIM), lambda b, h: (b, 0, h)),
                   pl.BlockSpec((None, 1, HEAD_DIM), lambda b, h: (b, 0, h))] + [ANY] * n,
        out_shape=[jax.ShapeDtypeStruct((5, B, L, D_MODEL), MX),
                   jax.ShapeDtypeStruct((B, 2, D_MODEL), F32),
                   jax.ShapeDtypeStruct((B, 1, D_MODEL), F32)] + _Exchange.out_shapes(grads, False),
        scratch_shapes=[pltpu.VMEM((L, HEAD_DIM), F32)] * 8
                       + [pltpu.VMEM((L, HEAD_DIM), MX), pltpu.VMEM((nC, HEAD_DIM, HEAD_DIM), MX),
                          pltpu.VMEM((nC, HEAD_DIM, HEAD_DIM), F32), pltpu.VMEM((nC, HEAD_DIM, HEAD_DIM), F32)]
                       + _Exchange.scratch(n),
        compiler_params=_params(("arbitrary", "arbitrary")),
    )(proj, proj, proj, proj, proj, o, dya, lbl, norm_g, *grads)
    return outs[0], outs[1], outs[2], outs[3:]


def _dproj_select(s, a5_ref, p_ref, g2_ref):
    return jnp.where(s < 5, a5_ref[...], jnp.where(s == 5, p_ref[...], g2_ref[...]))


def _dproj_specs(tm, tile_axis):
    def ix(args):
        return args[tile_axis], args[1 - tile_axis]
    a5 = pl.BlockSpec((None, tm, D_MODEL), lambda *a: (jnp.minimum(ix(a)[1], 4), ix(a)[0], 0))
    p = pl.BlockSpec((tm, D_MODEL), lambda *a: (ix(a)[0], 0))
    g2 = pl.BlockSpec((None, tm, D_MODEL), lambda *a: (jnp.clip(ix(a)[1] - 6, 0, 1), ix(a)[0], 0))
    return [a5, p, g2]


def _inproj_bwd(x2, dh, dproj5, dp, dgab, g_mix, w_in_g):
    T = x2.shape[0]
    tm = min(512, T)

    def body(a5_ref, p_ref, g2_ref, w_ref, x_ref, dh_ref, g_ref, dx_ref, dg_ref, acc_sc):
        i, s = pl.program_id(0), pl.program_id(1)

        @pl.when(s == 0)
        def _():
            acc_sc[...] = jnp.zeros_like(acc_sc)

        @pl.when((i == 0) & (s == 0))
        def _():
            dg_ref[...] = jnp.zeros_like(dg_ref)

        acc_sc[...] += _dot_nt(_dproj_select(s, a5_ref, p_ref, g2_ref), w_ref[...])

        @pl.when(s == N_DEV - 1)
        def _():
            x = x_ref[...]
            r = lax.rsqrt(jnp.mean(x * x, axis=-1, keepdims=True) + RMS_EPS)
            xn = x * r
            du = acc_sc[...]
            dg_ref[...] += _fold8(du * xn)
            a = du * g_ref[...]
            dx_ref[...] = dh_ref[...] + r * a - xn * (r * jnp.mean(a * xn, axis=-1, keepdims=True))

    tile = pl.BlockSpec((tm, D_MODEL), lambda i, s: (i, 0))
    return pl.pallas_call(
        body, name="inproj_bwd", grid=(T // tm, N_DEV),
        in_specs=_dproj_specs(tm, 0) + [pl.BlockSpec((None, D_MODEL, D_MODEL), lambda i, s: (s, 0, 0)),
                                        tile, tile, pl.BlockSpec((1, D_MODEL), lambda i, s: (0, 0))],
        out_specs=[tile, pl.BlockSpec((8, D_MODEL), lambda i, s: (0, 0))],
        out_shape=[jax.ShapeDtypeStruct((T, D_MODEL), F32), jax.ShapeDtypeStruct((8, D_MODEL), F32)],
        scratch_shapes=[pltpu.VMEM((tm, D_MODEL), F32)],
        compiler_params=_params(("arbitrary", "arbitrary")),
    )(dproj5, dp, dgab, w_in_g, x2, dh, g_mix)


def _wgrad_in(u, dproj5, dp, dgab):
    T = u.shape[0]
    tm = min(512, T)

    def body(a5_ref, p_ref, g2_ref, u_ref, out_ref, acc_sc):
        s, t = pl.program_id(0), pl.program_id(1)

        @pl.when(t == 0)
        def _():
            acc_sc[...] = jnp.zeros_like(acc_sc)

        acc_sc[...] += _dot_tn(u_ref[...], _dproj_select(s, a5_ref, p_ref, g2_ref))

        @pl.when(t == pl.num_programs(1) - 1)
        def _():
            out_ref[...] = acc_sc[...].astype(MX)

    return pl.pallas_call(
        body, name="wgrad_in", grid=(N_DEV, T // tm),
        in_specs=_dproj_specs(tm, 1) + [pl.BlockSpec((tm, D_MODEL), lambda s, t: (t, 0))],
        out_specs=pl.BlockSpec((None, D_MODEL, D_MODEL), lambda s, t: (s, 0, 0)),
        out_shape=jax.ShapeDtypeStruct((N_DEV, D_MODEL, D_MODEL), MX),
        scratch_shapes=[pltpu.VMEM((D_MODEL, D_MODEL), F32)],
        compiler_params=_params(("parallel", "arbitrary")),
    )(dproj5, dp, dgab, u)


def _wgrad(a, g, name):
    Ba, T, K = a.shape
    Bg, _, Nn = g.shape
    nb = max(Ba, Bg)
    tm = min(512, T)
    nt = T // tm

    def body(a_ref, g_ref, out_ref, acc_sc):
        t = pl.program_id(1)

        @pl.when(t == 0)
        def _():
            acc_sc[...] = jnp.zeros_like(acc_sc)

        acc_sc[...] += _dot_tn(a_ref[...], g_ref[...])

        @pl.when(t == nt - 1)
        def _():
            out_ref[...] = acc_sc[...].astype(MX)

    return pl.pallas_call(
        body, name=name, grid=(nb, nt),
        in_specs=[pl.BlockSpec((None, tm, K), lambda s, t: (s if Ba > 1 else 0, t, 0)),
                  pl.BlockSpec((None, tm, Nn), lambda s, t: (s if Bg > 1 else 0, t, 0))],
        out_specs=pl.BlockSpec((None, K, Nn), lambda s, t: (s, 0, 0)),
        out_shape=jax.ShapeDtypeStruct((nb, K, Nn), MX),
        scratch_shapes=[pltpu.VMEM((K, Nn), F32)],
        compiler_params=_params(("parallel", "arbitrary")),
    )(a, g)


def _mesh_pos():
    return lax.axis_index("x"), lax.axis_index("y"), lax.axis_index("c")


def _all_gather(shards):
    n = len(shards)

    def body(*refs):
        xs, outs = refs[:n], refs[n:2 * n]
        send_sems, recv_sems, local_sems = refs[2 * n:]
        x, y, c = _mesh_pos()
        me, sibling = (x, y, c), (x, y, 1 - c)
        chips = [(1 - x, y), (x, 1 - y), (1 - x, 1 - y)]

        def copy(a, k, block, to, src=None):
            slot = outs[a].at[4 * block[0] + 2 * block[1] + block[2]]
            return pltpu.make_async_remote_copy(
                src_ref=slot if src is None else src, dst_ref=slot,
                send_sem=send_sems.at[7 * a + k], recv_sem=recv_sems.at[7 * a + k],
                device_id=to, device_id_type=pl.DeviceIdType.MESH)

        mine = [pltpu.make_async_copy(xs[a], outs[a].at[4 * x + 2 * y + c], local_sems.at[a]) for a in range(n)]
        for cp in mine:
            cp.start()
        first = []
        for a in range(n):
            first.append(copy(a, 0, me, sibling, src=xs[a]))
            first += [copy(a, 1 + j, me, (*chip, c), src=xs[a]) for j, chip in enumerate(chips)]
        for cp in first:
            cp.start()
        passed = []
        for j, chip in enumerate(chips):
            for a in range(n):
                copy(a, 1 + j, (*chip, c), me).wait_recv()
                fwd = copy(a, 4 + j, (*chip, c), sibling)
                fwd.start()
                passed.append(fwd)
        for a in range(n):
            copy(a, 0, sibling, me).wait_recv()
            for j, chip in enumerate(chips):
                copy(a, 4 + j, (*chip, 1 - c), me).wait_recv()
        for cp in first + passed:
            cp.wait_send()
        for cp in mine:
            cp.wait()

    return pl.pallas_call(
        body, name="all_gather_weights",
        in_specs=[ANY] * n, out_specs=[ANY] * n,
        out_shape=[jax.ShapeDtypeStruct((N_DEV,) + s.shape, s.dtype) for s in shards],
        scratch_shapes=[pltpu.SemaphoreType.DMA((7 * n,)), pltpu.SemaphoreType.DMA((7 * n,)),
                        pltpu.SemaphoreType.DMA((n,))],
    )(*shards)


def _device_of(p):
    return (p // 4, (p // 2) % 2, p % 2)


class _Exchange:
    def __init__(self, srcs, outs, sems, gather):
        send_sems, recv_sems, local_sems = sems
        x, y, c = _mesh_pos()
        me = 4 * x + 2 * y + c
        self.sends, self.arrivals, self.mine = [], [], []
        for a, (src, out) in enumerate(zip(srcs, outs)):
            self.mine.append(pltpu.make_async_copy(src if gather else src.at[me], out.at[me], local_sems.at[a]))
            for j in range(1, N_DEV):
                to, frm = (me + j) % N_DEV, (me + N_DEV - j) % N_DEV
                pair = dict(send_sem=send_sems.at[7 * a + j - 1], recv_sem=recv_sems.at[7 * a + j - 1],
                            device_id_type=pl.DeviceIdType.MESH)
                self.sends.append(pltpu.make_async_remote_copy(
                    src_ref=src if gather else src.at[to], dst_ref=out.at[me], device_id=_device_of(to), **pair))
                self.arrivals.append(pltpu.make_async_remote_copy(
                    src_ref=src if gather else src.at[frm], dst_ref=out.at[frm], device_id=_device_of(frm), **pair))

    def start(self):
        for cp in self.mine + self.sends:
            cp.start()

    def wait(self):
        for cp in self.arrivals:
            cp.wait_recv()
        for cp in self.sends:
            cp.wait_send()
        for cp in self.mine:
            cp.wait()

    @staticmethod
    def scratch(n):
        return [pltpu.SemaphoreType.DMA((7 * n,)), pltpu.SemaphoreType.DMA((7 * n,)), pltpu.SemaphoreType.DMA((n,))]

    @staticmethod
    def out_shapes(arrays, gather):
        return [jax.ShapeDtypeStruct(((N_DEV,) + a.shape) if gather else a.shape, a.dtype) for a in arrays]


def _scatter_slices(grads):
    n = len(grads)

    def body(*refs):
        ex = _Exchange(refs[:n], refs[n:2 * n], refs[2 * n:], gather=False)
        ex.start()
        ex.wait()

    return pl.pallas_call(
        body, name="scatter_grad_slices",
        in_specs=[ANY] * n, out_specs=[ANY] * n,
        out_shape=_Exchange.out_shapes(grads, False), scratch_shapes=_Exchange.scratch(n),
    )(*grads)


def _all_reduce_small(v):
    R, C = v.shape

    def body(v_ref, out_ref, slots, send_sems, recv_sems):
        x, y, c = _mesh_pos()
        me = 4 * x + 2 * y + c

        def copy(j, to):
            return pltpu.make_async_remote_copy(
                src_ref=v_ref, dst_ref=slots.at[me],
                send_sem=send_sems.at[j - 1], recv_sem=recv_sems.at[j - 1],
                device_id=_device_of(to), device_id_type=pl.DeviceIdType.MESH)

        sends = [copy(j, (me + j) % N_DEV) for j in range(1, N_DEV)]
        for cp in sends:
            cp.start()
        slots[me] = v_ref[...]
        for j in range(1, N_DEV):
            frm = (me + N_DEV - j) % N_DEV
            pltpu.make_async_remote_copy(
                src_ref=v_ref, dst_ref=slots.at[frm], send_sem=send_sems.at[j - 1], recv_sem=recv_sems.at[j - 1],
                device_id=_device_of(frm), device_id_type=pl.DeviceIdType.MESH).wait_recv()
        for cp in sends:
            cp.wait_send()
        acc = slots[0]
        for p in range(1, N_DEV):
            acc = acc + slots[p]
        out_ref[...] = acc

    return pl.pallas_call(
        body, name="all_reduce_small",
        in_specs=[pl.BlockSpec(memory_space=pltpu.VMEM)], out_specs=pl.BlockSpec(memory_space=pltpu.VMEM),
        out_shape=jax.ShapeDtypeStruct((R, C), F32),
        scratch_shapes=[pltpu.VMEM((N_DEV, R, C), F32), pltpu.SemaphoreType.DMA((7,)), pltpu.SemaphoreType.DMA((7,))],
    )(v)


def _adamw_math(w, g, m, v):
    m = ADAM_B1 * m + (1.0 - ADAM_B1) * g
    v = ADAM_B2 * v + (1.0 - ADAM_B2) * (g * g)
    m_hat = m / (1.0 - ADAM_B1 ** ADAM_STEP)
    v_hat = v / (1.0 - ADAM_B2 ** ADAM_STEP)
    delta = -ADAM_LR * (m_hat / (jnp.sqrt(v_hat) + ADAM_EPS) + ADAM_WD * w)
    return delta, m, v


def _adamw_reduce(recv, w, m, v, name):
    R, C = w.shape
    tr = R if R <= 256 else 256
    while R % tr:
        tr //= 2

    def body(r_ref, w_ref, m_ref, v_ref, g_ref, d_ref, nm_ref, nv_ref):
        g = r_ref[0].astype(F32)
        for p in range(1, N_DEV):
            g = g + r_ref[p].astype(F32)
        d, nm, nv = _adamw_math(w_ref[...], g, m_ref[...], v_ref[...])
        g_ref[...] = g
        d_ref[...] = d
        nm_ref[...] = nm
        nv_ref[...] = nv

    tile = pl.BlockSpec((tr, C), lambda i: (i, 0))
    shp = jax.ShapeDtypeStruct((R, C), F32)
    return pl.pallas_call(
        body, name=name, grid=(R // tr,),
        in_specs=[pl.BlockSpec((N_DEV, tr, C), lambda i: (0, i, 0)), tile, tile, tile],
        out_specs=[tile] * 4, out_shape=[shp] * 4,
        compiler_params=_params(("parallel",)),
    )(recv, w, m, v)


def _adamw_small(g, w, m, v):
    def body(g_ref, w_ref, m_ref, v_ref, go_ref, d_ref, nm_ref, nv_ref):
        go_ref[...] = g_ref[...]
        for d in range(2):
            p0 = _sig(w_ref[8 + 2 * d:9 + 2 * d, :] - w_ref[9 + 2 * d:10 + 2 * d, :])
            dl0 = g_ref[12 + d:13 + d, :] * p0 * (1.0 - p0)
            go_ref[8 + 2 * d:9 + 2 * d, :] = dl0
            go_ref[9 + 2 * d:10 + 2 * d, :] = -dl0
            go_ref[12 + d:13 + d, :] = jnp.zeros((1, D_MODEL), F32)
        d, nm, nv = _adamw_math(w_ref[...], go_ref[...], m_ref[...], v_ref[...])
        d_ref[...] = d
        nm_ref[...] = nm
        nv_ref[...] = nv

    shp = jax.ShapeDtypeStruct(g.shape, F32)
    vm = pl.BlockSpec(memory_space=pltpu.VMEM)
    return pl.pallas_call(body, name="adamw_small", in_specs=[vm] * 4, out_specs=[vm] * 4, out_shape=[shp] * 4)(g, w, m, v)


def _local_step(x, tgt, g_mix, lb, norm_g, pool_scale, g_ffn, g_final, w_in_g, late_shards):
    B, L, _ = x.shape
    T = B * L
    rows = D_MODEL // N_DEV
    x2, tgt2 = x.reshape(T, D_MODEL), tgt.reshape(T, D_MODEL)

    proj, u = _rms_inproj(x2, g_mix, w_in_g)
    proj4 = proj.reshape(N_DEV, B, L, D_MODEL)
    o, ya, (wa, wb, wo, wfi_g, wfo_g, pw_g) = _hgrn_fwd(proj4, lb, norm_g, B, L, late_shards)
    wa, wb, wo = (w_.reshape(D_MODEL, D_MODEL) for w_ in (wa, wb, wo))
    wfo_g = wfo_g.reshape(4, FF_BLOCK, D_MODEL)
    pool_w_full = pw_g.reshape(N_DEV, 4, 32, POOL_DIM).transpose(1, 0, 2, 3).reshape(4, POOL_DIM, POOL_DIM)
    yb = _pool_fwd(proj4, pool_w_full, pool_scale, B, L)
    ya2, yb2 = ya.reshape(T, D_MODEL), yb.reshape(T, D_MODEL)
    za, zb, mg, h = _merge_out(x2, proj, ya2, yb2, wa, wb, wo)
    gate, up, hid, u2, dh2, dh2b, loss_p, dg_final = _ffn_fwd_loss(h, tgt2, g_ffn, g_final, wfi_g, wfo_g)
    loss = jnp.sum(loss_p[:, 0, 0])

    dgu, dh, dhb, dg_ffn = _ffn_bwd(h, dh2, dh2b, gate, up, g_ffn, wfi_g, wfo_g)
    d_wfo = _wgrad(hid, dh2b[None], "wgrad_ffn_out")
    d_wfi = _wgrad(u2[None], dgu.reshape(N_DEV, T, FF_BLOCK), "wgrad_ffn_in")
    dza, dzb, dgab, dya, dyb = _merge_bwd(dhb, proj, za, zb, wa, wb, wo)
    d_wo = _wgrad(mg[None], dhb[None], "wgrad_out")
    d_wa = _wgrad(ya2[None], dza[None], "wgrad_branch_a")
    d_wb = _wgrad(yb2[None], dzb[None], "wgrad_branch_b")
    dp, dpw_p, dps_p = _pool_bwd(proj4, dyb.reshape(B, L, D_MODEL), pool_w_full, pool_scale, B, L)
    d_pw = dpw_p.sum(0).reshape(4, N_DEV, 32, POOL_DIM).transpose(1, 0, 2, 3).reshape(N_DEV, 128, POOL_DIM)
    slices = [d_wa.reshape(N_DEV, rows, D_MODEL), d_wb.reshape(N_DEV, rows, D_MODEL),
              d_wo.reshape(N_DEV, rows, D_MODEL), d_wfi, d_wfo.reshape(N_DEV, FF_BLOCK // 2, D_MODEL),
              d_pw.astype(MX)]
    dproj5, dlb_p, dng_p, recv = _hgrn_bwd(proj4, o, dya.reshape(B, L, D_MODEL), lb, norm_g, B, L, slices)
    dproj5 = dproj5.reshape(5, T, D_MODEL)
    dp2 = dp.reshape(T, D_MODEL)
    d_win = _wgrad_in(u, dproj5, dp2, dgab)
    grad_x, dg_mix = _inproj_bwd(x2, dh, dproj5, dp2, dgab, g_mix, w_in_g)

    small = dict(g_mix=dg_mix.sum(0), hgrn_norm_g=dng_p.sum((0, 1)), pool_scale=dps_p.sum((0, 1)),
                 g_ffn=dg_ffn.sum(0), g_final=dg_final.sum(0), lb=dlb_p.sum(0))
    return loss, grad_x.reshape(B, L, D_MODEL), d_win, recv, small


def kernel(x, g_mix, w_in, lb_logits, hgrn_norm_g, pool_w, pool_scale, w_branch_a, w_branch_b, w_out, g_ffn, w_ffn_in, w_ffn_out, g_final, loss_target, m_g_mix, m_w_in, m_lb_logits, m_hgrn_norm_g, m_pool_w, m_pool_scale, m_w_branch_a, m_w_branch_b, m_w_out, m_g_ffn, m_w_ffn_in, m_w_ffn_out, m_g_final, v_g_mix, v_w_in, v_lb_logits, v_hgrn_norm_g, v_pool_w, v_pool_scale, v_w_branch_a, v_w_branch_b, v_w_out, v_g_ffn, v_w_ffn_in, v_w_ffn_out, v_g_final):
    me = 4 * lax.axis_index("x") + 2 * lax.axis_index("y") + lax.axis_index("c")

    w_in_g, lbl_g = _all_gather([w_in[0].astype(MX), jnp.pad(lb_logits.reshape(4, HEAD_DIM), ((0, 4), (0, 0)))])
    lbl = lbl_g[:, :4].transpose(1, 0, 2).reshape(4, D_MODEL)
    late_shards = [w_branch_a[0].astype(MX), w_branch_b[0].astype(MX), w_out[0].astype(MX),
                   w_ffn_in[0].astype(MX), w_ffn_out[0].astype(MX), pool_w[0].reshape(4 * 32, POOL_DIM).astype(MX)]

    loss, grad_x, d_win, recv_late, small = _local_step(
        x, loss_target, g_mix, lbl, hgrn_norm_g, pool_scale, g_ffn, g_final[None], w_in_g, late_shards)
    loss = lax.psum(loss, ("x", "y", "c"))

    packed = jnp.zeros((16, D_MODEL), F32)
    names = ["g_mix", "hgrn_norm_g", "pool_scale", "g_ffn", "g_final"]
    for i, nme in enumerate(names):
        packed = packed.at[i].set(small[nme])
    packed = packed.at[5:7].set(small["lb"])
    red = _all_reduce_small(packed)
    dlb_mine = lax.dynamic_slice_in_dim(red[5:7], me * HEAD_DIM, HEAD_DIM, axis=1)

    sw = jnp.zeros((16, D_MODEL), F32)
    sm = jnp.zeros((16, D_MODEL), F32)
    sv = jnp.ones((16, D_MODEL), F32)
    smalls = [(g_mix, m_g_mix, v_g_mix), (hgrn_norm_g, m_hgrn_norm_g, v_hgrn_norm_g),
              (pool_scale, m_pool_scale, v_pool_scale), (g_ffn, m_g_ffn, v_g_ffn),
              (g_final[None], m_g_final[None], v_g_final[None])]
    for i, (w_, m_, v_) in enumerate(smalls):
        sw, sm, sv = sw.at[i].set(w_[0]), sm.at[i].set(m_[0]), sv.at[i].set(v_[0])
    sg = red.at[5:].set(0.0)
    sg = sg.at[12:14, :HEAD_DIM].set(dlb_mine)
    sw = sw.at[8:12, :HEAD_DIM].set(lb_logits.reshape(4, HEAD_DIM))
    sm = sm.at[8:12, :HEAD_DIM].set(m_lb_logits.reshape(4, HEAD_DIM))
    sv = sv.at[8:12, :HEAD_DIM].set(v_lb_logits.reshape(4, HEAD_DIM))
    sg, sd, snm, snv = _adamw_small(sg, sw, sm, sv)

    def small_out(arr, i, like):
        return arr[i].reshape(like.shape)

    def lb_out(arr):
        return arr[8:12, :HEAD_DIM].reshape(2, 2, HEAD_DIM)

    order = ["w_in", "w_branch_a", "w_branch_b", "w_out", "w_ffn_in", "w_ffn_out", "pool_w"]
    recv = list(_scatter_slices([d_win])) + list(recv_late)
    params = dict(w_in=(w_in, m_w_in, v_w_in), w_branch_a=(w_branch_a, m_w_branch_a, v_w_branch_a),
                  w_branch_b=(w_branch_b, m_w_branch_b, v_w_branch_b), w_out=(w_out, m_w_out, v_w_out),
                  w_ffn_in=(w_ffn_in, m_w_ffn_in, v_w_ffn_in), w_ffn_out=(w_ffn_out, m_w_ffn_out, v_w_ffn_out),
                  pool_w=(pool_w, m_pool_w, v_pool_w))
    res = {}
    for nme, r in zip(order, recv):
        w_, m_, v_ = params[nme]
        shape2 = r.shape[1:]
        outs = _adamw_reduce(r, w_.reshape(shape2), m_.reshape(shape2), v_.reshape(shape2), "adamw_" + nme)
        res[nme] = [o_.reshape(w_.shape) for o_ in outs]

    def pick(k):
        small_src = [sg, sd, snm, snv][k]
        return [small_out(small_src, 0, g_mix), res["w_in"][k], lb_out(small_src), small_out(small_src, 1, hgrn_norm_g),
                res["pool_w"][k], small_out(small_src, 2, pool_scale), res["w_branch_a"][k], res["w_branch_b"][k],
                res["w_out"][k], small_out(small_src, 3, g_ffn), res["w_ffn_in"][k], res["w_ffn_out"][k],
                small_out(small_src, 4, g_final)]

    return (loss, grad_x, *pick(0), *pick(1), *pick(2), *pick(3))
```

```python
import functools

import jax
import jax.numpy as jnp
from jax import lax
from jax.experimental import pallas as pl
from jax.experimental.pallas import tpu as pltpu

F32 = jnp.float32
MX = jnp.bfloat16

D_MODEL = 1024
N_HEADS = 8
HEAD_DIM = 128
CHUNK = 16
POOL_WINDOWS = (2, 4, 8, 16)
POOL_DIM = 256
FF_BLOCK = 704
N_DEV = 8
RMS_EPS = 1e-6
ADAM_LR, ADAM_B1, ADAM_B2, ADAM_EPS, ADAM_WD, ADAM_STEP = 0.001, 0.9, 0.999, 1e-08, 0.01, 10
VMEM_LIMIT = 56 * 1024 * 1024
BIG_TOKEN_TILE = 1024
ANY = pl.BlockSpec(memory_space=pl.ANY)


def _params(sem=None):
    return pltpu.CompilerParams(dimension_semantics=sem, vmem_limit_bytes=VMEM_LIMIT)


def _dot(a, b):
    return lax.dot_general(a.astype(MX), b.astype(MX), (((1,), (0,)), ((), ())), preferred_element_type=F32)


def _dot_nt(a, b):
    return lax.dot_general(a.astype(MX), b.astype(MX), (((1,), (1,)), ((), ())), preferred_element_type=F32)


def _dot_tn(a, b):
    return lax.dot_general(a.astype(MX), b.astype(MX), (((0,), (0,)), ((), ())), preferred_element_type=F32)


def _sig(x):
    return 1.0 / (1.0 + jnp.exp(-x))


def _fold8(v):
    return v.reshape(v.shape[0] // 8, 8, v.shape[1]).sum(axis=0)


def _shift_rows(x, s):
    n = x.shape[0]
    row = lax.broadcasted_iota(jnp.int32, x.shape, 0)
    if s > 0:
        return jnp.where(row >= s, pltpu.roll(x, s, 0), 0.0)
    return jnp.where(row < n + s, pltpu.roll(x, n + s, 0), 0.0)


def _cumsum_rows(x, rev):
    s = 1
    while s < x.shape[0]:
        x = x + _shift_rows(x, -s if rev else s)
        s *= 2
    return x


def _rms_inproj(x2, g_mix, w_in_g):
    T = x2.shape[0]
    tm = min(BIG_TOKEN_TILE, T)

    def body(x_ref, g_ref, w_ref, proj_ref, u_ref, u_sc):
        @pl.when(pl.program_id(1) == 0)
        def _():
            x = x_ref[...]
            r = lax.rsqrt(jnp.mean(x * x, axis=-1, keepdims=True) + RMS_EPS)
            u = (x * r * g_ref[...]).astype(MX)
            u_sc[...] = u
            u_ref[...] = u
        proj_ref[...] = jnp.dot(u_sc[...], w_ref[...], preferred_element_type=F32)

    return pl.pallas_call(
        body, name="rms_inproj", grid=(T // tm, N_DEV),
        in_specs=[pl.BlockSpec((tm, D_MODEL), lambda i, k: (i, 0)),
                  pl.BlockSpec((1, D_MODEL), lambda i, k: (0, 0)),
                  pl.BlockSpec((None, D_MODEL, D_MODEL), lambda i, k: (k, 0, 0))],
        out_specs=[pl.BlockSpec((None, tm, D_MODEL), lambda i, k: (k, i, 0)),
                   pl.BlockSpec((tm, D_MODEL), lambda i, k: (i, 0))],
        out_shape=[jax.ShapeDtypeStruct((N_DEV, T, D_MODEL), F32), jax.ShapeDtypeStruct((T, D_MODEL), MX)],
        scratch_shapes=[pltpu.VMEM((tm, D_MODEL), MX)],
        compiler_params=_params(("parallel", "arbitrary")),
    )(x2, g_mix, w_in_g)


HBLK = 128


def _seg_cumsum(x, rev):
    n = x.shape[0]
    pos = lax.broadcasted_iota(jnp.int32, x.shape, 0) & (CHUNK - 1)
    s = 1
    while s < CHUNK:
        if rev:
            x = x + jnp.where(pos < CHUNK - s, pltpu.roll(x, n - s, 0), 0.0)
        else:
            x = x + jnp.where(pos >= s, pltpu.roll(x, s, 0), 0.0)
        s *= 2
    return x


def _block_gates(q_r, f_r, lb_row, rev):
    sq = _sig(q_r)
    q = q_r * sq
    sg = _sig(f_r)
    f = lb_row + (1.0 - lb_row) * sg
    k = 1.0 - f
    lf = jnp.log(f)
    pre = _seg_cumsum(lf, False)
    suf = _seg_cumsum(lf, True)
    tot = pre + suf - lf
    b = suf if rev else pre
    eb = jnp.exp(b)
    enb = jnp.exp(-b)
    eend = jnp.exp(tot - b)
    return dict(sq=sq, sg=sg, f=f, eb=eb, enb=enb, eend=eend, dec=jnp.exp(tot),
                P=q * eb, Kt=k * enb, Ke=k * eend)


def _block_mask(rev, transposed=False):
    ri = lax.broadcasted_iota(jnp.int32, (HBLK, HBLK), 0)
    ci = lax.broadcasted_iota(jnp.int32, (HBLK, HBLK), 1)
    same = (ri // CHUNK) == (ci // CHUNK)
    return same & ((ci >= ri) if rev != transposed else (ci <= ri))


def _chunk_rows(c):
    return pl.ds(pl.multiple_of(c * CHUNK, CHUNK), CHUNK)


def _chunk_outer_products(a, b, out_sc, kb):
    a, b = a.astype(MX), b.astype(MX)
    for u in range(HBLK // CHUNK):
        r = slice(u * CHUNK, (u + 1) * CHUNK)
        out_sc[kb * (HBLK // CHUNK) + u] = _dot_tn(a[r, :], b[r, :])


BLOCK_UNROLL = 2


def _block_loop(n_blocks, fn, init):
    def group(kg, carry):
        for u in range(BLOCK_UNROLL):
            carry = fn(kg * BLOCK_UNROLL + u, carry)
        return carry

    return lax.fori_loop(0, n_blocks // BLOCK_UNROLL, group, init)


SEQ_UNROLL = 16


def _chunk_loop(n_chunks, descending, step, init):
    def group(jg, carry):
        for u in range(SEQ_UNROLL):
            j = jg * SEQ_UNROLL + u
            carry = step(n_chunks - 1 - j if descending else j, carry)
        return carry

    return lax.fori_loop(0, n_chunks // SEQ_UNROLL, group, init)


def _lower_bounds(lbl_ref):
    return _sig(lbl_ref[0:1, :] - lbl_ref[1:2, :]), _sig(lbl_ref[2:3, :] - lbl_ref[3:4, :])


def _hgrn_fwd(proj, lbl, norm_g, B, L, shards):
    nC = L // CHUNK
    n = len(shards)

    def body(*refs):
        q_ref, ff_ref, fb_ref, i_ref, og_ref, lbl_ref, ng_ref = refs[:7]
        o_ref, ya_ref = refs[7 + n:9 + n]
        o_sc, p_sc, dec_sc, upd_sc = refs[9 + 2 * n:13 + 2 * n]
        step_id = pl.program_id(0) * N_HEADS + pl.program_id(1)
        gather = _Exchange(refs[7:7 + n], refs[9 + n:9 + 2 * n], refs[13 + 2 * n:], gather=True)

        @pl.when(step_id == 0)
        def _():
            gather.start()

        lb_f, lb_b = _lower_bounds(lbl_ref)
        o_sc[...] = jnp.zeros_like(o_sc)

        def run_dir(f_ref, lb_row, rev):
            def block(kb, carry):
                rows = pl.ds(pl.multiple_of(kb * HBLK, HBLK), HBLK)
                g = _block_gates(q_ref[rows, :], f_ref[rows, :], lb_row, rev)
                v = i_ref[rows, :]
                a = jnp.where(_block_mask(rev), _dot_nt(g["P"], g["Kt"]), 0.0)
                o_sc[rows, :] += _dot(a, v)
                p_sc[rows, :] = g["P"].astype(MX)
                dec_sc[rows, :] = g["dec"]
                _chunk_outer_products(v, g["Ke"], upd_sc, kb)
                return carry

            _block_loop(L // HBLK, block, 0)

            def step(c, st):
                rows = _chunk_rows(c)
                o_sc[rows, :] += _dot_nt(p_sc[rows, :], st)
                dec = dec_sc[pl.ds(pl.multiple_of(c * CHUNK, CHUNK), 1), :]
                return st * dec + upd_sc[c]

            _chunk_loop(nC, rev, step, jnp.zeros((HEAD_DIM, HEAD_DIM), F32))

        run_dir(ff_ref, lb_f, False)
        run_dir(fb_ref, lb_b, True)
        o = o_sc[...]
        o_ref[...] = o
        on = o * lax.rsqrt(jnp.mean(o * o, axis=-1, keepdims=True) + RMS_EPS)
        og = og_ref[...]
        ya_ref[...] = ((on * ng_ref[...]) * (og * _sig(og))).astype(MX)

        @pl.when(step_id == B * N_HEADS - 1)
        def _():
            gather.wait()

    def blk(s):
        return pl.BlockSpec((None, None, L, HEAD_DIM), lambda b, h, s=s: (s, b, 0, h))

    out_blk = pl.BlockSpec((None, L, HEAD_DIM), lambda b, h: (b, 0, h))
    outs = pl.pallas_call(
        body, name="hgrn_fwd", grid=(B, N_HEADS),
        in_specs=[blk(0), blk(1), blk(2), blk(3), blk(4),
                  pl.BlockSpec((4, HEAD_DIM), lambda b, h: (0, h)),
                  pl.BlockSpec((1, HEAD_DIM), lambda b, h: (0, h))] + [ANY] * n,
        out_specs=[out_blk, out_blk] + [ANY] * n,
        out_shape=[jax.ShapeDtypeStruct((B, L, D_MODEL), F32), jax.ShapeDtypeStruct((B, L, D_MODEL), MX)]
                  + _Exchange.out_shapes(shards, True),
        scratch_shapes=[pltpu.VMEM((L, HEAD_DIM), F32), pltpu.VMEM((L, HEAD_DIM), MX),
                        pltpu.VMEM((L, HEAD_DIM), F32), pltpu.VMEM((nC, HEAD_DIM, HEAD_DIM), F32)]
                       + _Exchange.scratch(n),
        compiler_params=_params(("arbitrary", "arbitrary")),
    )(proj, proj, proj, proj, proj, lbl, norm_g, *shards)
    return outs[0], outs[1], outs[2:]


POOL_PAD = 8


def _pool_window(p, ext_sc, half, adjoint):
    L = p.shape[0]
    n = L + 2 * POOL_PAD
    ext_sc[0:POOL_PAD, :] = jnp.zeros((POOL_PAD, p.shape[1]), F32)
    ext_sc[POOL_PAD + L:n, :] = jnp.zeros((POOL_PAD, p.shape[1]), F32)
    ext_sc[POOL_PAD:POOL_PAD + L, :] = p
    x = ext_sc[...]
    s = x + pltpu.roll(x, 1 if adjoint else n - 1, 0)
    w = 1
    while w < half:
        s = pltpu.roll(s, w, 0) + pltpu.roll(s, n - w, 0)
        w *= 2
    ext_sc[...] = s
    return ext_sc[POOL_PAD:POOL_PAD + L, :]


def _pool_count(L, half):
    t = lax.broadcasted_iota(jnp.int32, (L, 1), 0)
    lo = jnp.clip(t - half + 1, 0, L)
    hi = jnp.clip(t + half + 1, 0, L)
    return (hi - lo).astype(F32)


def _pool_fwd(proj, pool_w_full, pool_scale, B, L):
    def body(p_ref, w_ref, s_ref, yb_ref, ext_sc):
        for g, win in enumerate(POOL_WINDOWS):
            cols = slice(g * POOL_DIM, (g + 1) * POOL_DIM)
            p = p_ref[:, cols]
            y = _pool_window(p, ext_sc, win // 2, False) / _pool_count(L, win // 2) - p
            yb_ref[:, cols] = (_dot(y, w_ref[g]) * s_ref[:, cols]).astype(MX)

    return pl.pallas_call(
        body, name="pool_fwd", grid=(B,),
        in_specs=[pl.BlockSpec((None, None, L, D_MODEL), lambda b: (5, b, 0, 0)),
                  pl.BlockSpec((4, POOL_DIM, POOL_DIM), lambda b: (0, 0, 0)),
                  pl.BlockSpec((1, D_MODEL), lambda b: (0, 0))],
        out_specs=pl.BlockSpec((None, L, D_MODEL), lambda b: (b, 0, 0)),
        out_shape=jax.ShapeDtypeStruct((B, L, D_MODEL), MX),
        scratch_shapes=[pltpu.VMEM((L + 2 * POOL_PAD, POOL_DIM), F32)],
        compiler_params=_params(("parallel",)),
    )(proj, pool_w_full, pool_scale)


def _merge_out(x2, proj, ya, yb, wa, wb, wo):
    T = x2.shape[0]
    tm = min(512, T)

    def body(x_ref, ga_ref, gb_ref, ya_ref, yb_ref, wa_ref, wb_ref, wo_ref, za_ref, zb_ref, mg_ref, h_ref):
        za = jnp.dot(ya_ref[...], wa_ref[...], preferred_element_type=F32)
        zb = jnp.dot(yb_ref[...], wb_ref[...], preferred_element_type=F32)
        mg = (_sig(ga_ref[...]) * za + _sig(gb_ref[...]) * zb).astype(MX)
        za_ref[...] = za
        zb_ref[...] = zb
        mg_ref[...] = mg
        h_ref[...] = x_ref[...] + jnp.dot(mg, wo_ref[...], preferred_element_type=F32)

    tile = pl.BlockSpec((tm, D_MODEL), lambda i: (i, 0))
    wspec = pl.BlockSpec((D_MODEL, D_MODEL), lambda i: (0, 0))
    return pl.pallas_call(
        body, name="merge_out", grid=(T // tm,),
        in_specs=[tile,
                  pl.BlockSpec((None, tm, D_MODEL), lambda i: (6, i, 0)),
                  pl.BlockSpec((None, tm, D_MODEL), lambda i: (7, i, 0)),
                  tile, tile, wspec, wspec, wspec],
        out_specs=[tile, tile, tile, tile],
        out_shape=[jax.ShapeDtypeStruct((T, D_MODEL), F32), jax.ShapeDtypeStruct((T, D_MODEL), F32),
                   jax.ShapeDtypeStruct((T, D_MODEL), MX), jax.ShapeDtypeStruct((T, D_MODEL), F32)],
        compiler_params=_params(("parallel",)),
    )(x2, proj, proj, ya, yb, wa, wb, wo)


def _ffn_fwd_loss(h, tgt, g_ffn, g_final, wfi_g, wfo_g):
    T = h.shape[0]
    tm = min(512, T)
    nT = T // tm

    def body(h_ref, t_ref, gf_ref, gl_ref, wg_ref, wu_ref, wo_ref,
             gate_ref, up_ref, hid_ref, u2_ref, dh2_ref, dh2b_ref, loss_ref, dgl_ref, u2_sc, acc_sc):
        i, j = pl.program_id(0), pl.program_id(1)

        @pl.when(j == 0)
        def _():
            hh = h_ref[...]
            r = lax.rsqrt(jnp.mean(hh * hh, axis=-1, keepdims=True) + RMS_EPS)
            u2 = (hh * r * gf_ref[...]).astype(MX)
            u2_sc[...] = u2
            u2_ref[...] = u2
            acc_sc[...] = jnp.zeros_like(acc_sc)

        @pl.when((i == 0) & (j == 0))
        def _():
            dgl_ref[...] = jnp.zeros_like(dgl_ref)

        gate = jnp.dot(u2_sc[...], wg_ref[...], preferred_element_type=F32)
        up = jnp.dot(u2_sc[...], wu_ref[...], preferred_element_type=F32)
        hid = ((gate * _sig(gate)) * up).astype(MX)
        gate_ref[...] = gate
        up_ref[...] = up
        hid_ref[...] = hid
        acc_sc[...] += jnp.dot(hid, wo_ref[...], preferred_element_type=F32)

        @pl.when(j == 3)
        def _():
            h2 = h_ref[...] + acc_sc[...]
            r = lax.rsqrt(jnp.mean(h2 * h2, axis=-1, keepdims=True) + RMS_EPS)
            hn = h2 * r
            gl = gl_ref[...]
            err = hn * gl - t_ref[...]
            tok = jnp.mean(err * err, axis=-1, keepdims=True)
            loss_ref[...] = jnp.full(loss_ref.shape, 0.5 * jnp.sum(tok), F32)
            dy = err * (1.0 / D_MODEL)
            dgl_ref[...] += _fold8(dy * hn)
            a = dy * gl
            dh2 = r * a - hn * (r * jnp.mean(a * hn, axis=-1, keepdims=True))
            dh2_ref[...] = dh2
            dh2b_ref[...] = dh2.astype(MX)

    tile = pl.BlockSpec((tm, D_MODEL), lambda i, j: (i, 0))
    vec = pl.BlockSpec((1, D_MODEL), lambda i, j: (0, 0))
    ftile = pl.BlockSpec((None, tm, FF_BLOCK), lambda i, j: (j, i, 0))
    return pl.pallas_call(
        body, name="ffn_fwd_loss", grid=(nT, 4),
        in_specs=[tile, tile, vec, vec,
                  pl.BlockSpec((None, D_MODEL, FF_BLOCK), lambda i, j: (j, 0, 0)),
                  pl.BlockSpec((None, D_MODEL, FF_BLOCK), lambda i, j: (j + 4, 0, 0)),
                  pl.BlockSpec((None, FF_BLOCK, D_MODEL), lambda i, j: (j, 0, 0))],
        out_specs=[ftile, ftile, ftile, tile, tile, tile,
                   pl.BlockSpec((None, 8, 128), lambda i, j: (i, 0, 0)),
                   pl.BlockSpec((8, D_MODEL), lambda i, j: (0, 0))],
        out_shape=[jax.ShapeDtypeStruct((4, T, FF_BLOCK), F32), jax.ShapeDtypeStruct((4, T, FF_BLOCK), F32),
                   jax.ShapeDtypeStruct((4, T, FF_BLOCK), MX), jax.ShapeDtypeStruct((T, D_MODEL), MX),
                   jax.ShapeDtypeStruct((T, D_MODEL), F32), jax.ShapeDtypeStruct((T, D_MODEL), MX),
                   jax.ShapeDtypeStruct((nT, 8, 128), F32), jax.ShapeDtypeStruct((8, D_MODEL), F32)],
        scratch_shapes=[pltpu.VMEM((tm, D_MODEL), MX), pltpu.VMEM((tm, D_MODEL), F32)],
        compiler_params=_params(("arbitrary", "arbitrary")),
    )(h, tgt, g_ffn, g_final, wfi_g, wfi_g, wfo_g)


def _ffn_bwd(h, dh2, dh2b, gate, up, g_ffn, wfi_g, wfo_g):
    T = h.shape[0]
    tm = min(512, T)

    def body(h_ref, dh2_ref, dh2b_ref, gate_ref, up_ref, gf_ref, wg_ref, wu_ref, wo_ref,
             dgu_ref, dh_ref, dhb_ref, dgf_ref, acc_sc):
        i, j = pl.program_id(0), pl.program_id(1)

        @pl.when(j == 0)
        def _():
            acc_sc[...] = jnp.zeros_like(acc_sc)

        @pl.when((i == 0) & (j == 0))
        def _():
            dgf_ref[...] = jnp.zeros_like(dgf_ref)

        dhid = _dot_nt(dh2b_ref[...], wo_ref[...])
        gate, up = gate_ref[...], up_ref[...]
        sg = _sig(gate)
        dgate = (dhid * up * (sg * (1.0 + gate * (1.0 - sg)))).astype(MX)
        dup = (dhid * (gate * sg)).astype(MX)
        dgu_ref[0] = dgate
        dgu_ref[1] = dup
        acc_sc[...] += _dot_nt(dgate, wg_ref[...]) + _dot_nt(dup, wu_ref[...])

        @pl.when(j == 3)
        def _():
            hh = h_ref[...]
            r = lax.rsqrt(jnp.mean(hh * hh, axis=-1, keepdims=True) + RMS_EPS)
            hn = hh * r
            du2 = acc_sc[...]
            dgf_ref[...] += _fold8(du2 * hn)
            a = du2 * gf_ref[...]
            dh = dh2_ref[...] + r * a - hn * (r * jnp.mean(a * hn, axis=-1, keepdims=True))
            dh_ref[...] = dh
            dhb_ref[...] = dh.astype(MX)

    tile = pl.BlockSpec((tm, D_MODEL), lambda i, j: (i, 0))
    ftile = pl.BlockSpec((None, tm, FF_BLOCK), lambda i, j: (j, i, 0))
    return pl.pallas_call(
        body, name="ffn_bwd", grid=(T // tm, 4),
        in_specs=[tile, tile, tile, ftile, ftile,
                  pl.BlockSpec((1, D_MODEL), lambda i, j: (0, 0)),
                  pl.BlockSpec((None, D_MODEL, FF_BLOCK), lambda i, j: (j, 0, 0)),
                  pl.BlockSpec((None, D_MODEL, FF_BLOCK), lambda i, j: (j + 4, 0, 0)),
                  pl.BlockSpec((None, FF_BLOCK, D_MODEL), lambda i, j: (j, 0, 0))],
        out_specs=[pl.BlockSpec((2, None, tm, FF_BLOCK), lambda i, j: (0, j, i, 0)),
                   tile, tile, pl.BlockSpec((8, D_MODEL), lambda i, j: (0, 0))],
        out_shape=[jax.ShapeDtypeStruct((2, 4, T, FF_BLOCK), MX),
                   jax.ShapeDtypeStruct((T, D_MODEL), F32), jax.ShapeDtypeStruct((T, D_MODEL), MX),
                   jax.ShapeDtypeStruct((8, D_MODEL), F32)],
        scratch_shapes=[pltpu.VMEM((tm, D_MODEL), F32)],
        compiler_params=_params(("arbitrary", "arbitrary")),
    )(h, dh2, dh2b, gate, up, g_ffn, wfi_g, wfi_g, wfo_g)


def _merge_bwd(dhb, proj, za, zb, wa, wb, wo):
    T = dhb.shape[0]
    tm = min(512, T)

    def body(dh_ref, ga_ref, gb_ref, za_ref, zb_ref, wa_ref, wb_ref, wo_ref,
             dza_ref, dzb_ref, dgab_ref, dya_ref, dyb_ref):
        dm = _dot_nt(dh_ref[...], wo_ref[...])
        sa, sb = _sig(ga_ref[...]), _sig(gb_ref[...])
        dza = (dm * sa).astype(MX)
        dzb = (dm * sb).astype(MX)
        dza_ref[...] = dza
        dzb_ref[...] = dzb
        dgab_ref[0] = (dm * za_ref[...] * (sa * (1.0 - sa))).astype(MX)
        dgab_ref[1] = (dm * zb_ref[...] * (sb * (1.0 - sb))).astype(MX)
        dya_ref[...] = _dot_nt(dza, wa_ref[...])
        dyb_ref[...] = _dot_nt(dzb, wb_ref[...])

    tile = pl.BlockSpec((tm, D_MODEL), lambda i: (i, 0))
    wspec = pl.BlockSpec((D_MODEL, D_MODEL), lambda i: (0, 0))
    return pl.pallas_call(
        body, name="merge_bwd", grid=(T // tm,),
        in_specs=[tile,
                  pl.BlockSpec((None, tm, D_MODEL), lambda i: (6, i, 0)),
                  pl.BlockSpec((None, tm, D_MODEL), lambda i: (7, i, 0)),
                  tile, tile, wspec, wspec, wspec],
        out_specs=[tile, tile, pl.BlockSpec((2, tm, D_MODEL), lambda i: (0, i, 0)), tile, tile],
        out_shape=[jax.ShapeDtypeStruct((T, D_MODEL), MX), jax.ShapeDtypeStruct((T, D_MODEL), MX),
                   jax.ShapeDtypeStruct((2, T, D_MODEL), MX),
                   jax.ShapeDtypeStruct((T, D_MODEL), F32), jax.ShapeDtypeStruct((T, D_MODEL), F32)],
        compiler_params=_params(("parallel",)),
    )(dhb, proj, proj, za, zb, wa, wb, wo)


def _pool_bwd(proj, dyb, pool_w_full, pool_scale, B, L):
    def body(p_ref, dy_ref, w_ref, s_ref, dp_ref, dw_ref, ds_ref, ext_sc):
        for g, win in enumerate(POOL_WINDOWS):
            cols = slice(g * POOL_DIM, (g + 1) * POOL_DIM)
            p = p_ref[:, cols]
            cnt = _pool_count(L, win // 2)
            y = _pool_window(p, ext_sc, win // 2, False) / cnt - p
            z = _dot(y, w_ref[g])
            dyb_g = dy_ref[:, cols]
            ds_ref[:, cols] = jnp.sum(dyb_g * z, axis=0, keepdims=True)
            dz = dyb_g * s_ref[:, cols]
            dw_ref[g] = _dot_tn(y, dz)
            dy = _dot_nt(dz, w_ref[g])
            dp_ref[:, cols] = (_pool_window(dy / cnt, ext_sc, win // 2, True) - dy).astype(MX)

    seq = pl.BlockSpec((None, L, D_MODEL), lambda b: (b, 0, 0))
    return pl.pallas_call(
        body, name="pool_bwd", grid=(B,),
        in_specs=[pl.BlockSpec((None, None, L, D_MODEL), lambda b: (5, b, 0, 0)), seq,
                  pl.BlockSpec((4, POOL_DIM, POOL_DIM), lambda b: (0, 0, 0)),
                  pl.BlockSpec((1, D_MODEL), lambda b: (0, 0))],
        out_specs=[seq, pl.BlockSpec((None, 4, POOL_DIM, POOL_DIM), lambda b: (b, 0, 0, 0)),
                   pl.BlockSpec((None, 1, D_MODEL), lambda b: (b, 0, 0))],
        out_shape=[jax.ShapeDtypeStruct((B, L, D_MODEL), MX),
                   jax.ShapeDtypeStruct((B, 4, POOL_DIM, POOL_DIM), F32),
                   jax.ShapeDtypeStruct((B, 1, D_MODEL), F32)],
        scratch_shapes=[pltpu.VMEM((L + 2 * POOL_PAD, POOL_DIM), F32)],
        compiler_params=_params(("parallel",)),
    )(proj, dyb, pool_w_full, pool_scale)


def _hgrn_bwd(proj, o, dya, lbl, norm_g, B, L, grads):
    nC = L // CHUNK
    n = len(grads)

    def body(*refs):
        q_ref, ff_ref, fb_ref, i_ref, og_ref, o_ref, dy_ref, lbl_ref, ng_ref = refs[:9]
        dp_ref, dlb_ref, dng_ref = refs[9 + n:12 + n]
        (do_sc, dq_sc, dv_sc, dP_sc, dKt_sc, dKe_sc, dbl_sc, dec_sc, ke_sc, ck_sc, upd_sc,
         dupd_sc) = refs[12 + 2 * n:24 + 2 * n]
        step_id = pl.program_id(0) * N_HEADS + pl.program_id(1)
        scatter = _Exchange(refs[9:9 + n], refs[12 + n:12 + 2 * n], refs[24 + 2 * n:], gather=False)

        @pl.when(step_id == 0)
        def _():
            scatter.start()

        lb_f, lb_b = _lower_bounds(lbl_ref)
        o_ = o_ref[...]
        r = lax.rsqrt(jnp.mean(o_ * o_, axis=-1, keepdims=True) + RMS_EPS)
        on = o_ * r
        og = og_ref[...]
        sog = _sig(og)
        dy = dy_ref[...]
        ng = ng_ref[...]
        dp_ref[4] = (dy * (on * ng) * (sog * (1.0 + og * (1.0 - sog)))).astype(MX)
        dn = dy * (og * sog)
        dng_ref[...] = jnp.sum(dn * on, axis=0, keepdims=True)
        don = dn * ng
        do_sc[...] = r * don - on * (r * jnp.mean(don * on, axis=-1, keepdims=True))

        dq_sc[...] = jnp.zeros_like(dq_sc)
        dv_sc[...] = jnp.zeros_like(dv_sc)

        def run_dir(f_ref, lb_row, rev, slot):
            def block(kb, carry):
                rows = pl.ds(pl.multiple_of(kb * HBLK, HBLK), HBLK)
                g = _block_gates(q_ref[rows, :], f_ref[rows, :], lb_row, rev)
                v = i_ref[rows, :]
                do = do_sc[rows, :]
                m, mt = _block_mask(rev), _block_mask(rev, True)
                at = jnp.where(mt, _dot_nt(g["Kt"], g["P"]), 0.0)
                da = jnp.where(m, _dot_nt(do, v), 0.0)
                dat = jnp.where(mt, _dot_nt(v, do), 0.0)
                dv_sc[rows, :] += _dot(at, do)
                dP_sc[rows, :] = _dot(da, g["Kt"])
                dKt_sc[rows, :] = _dot(dat, g["P"])
                ke_sc[rows, :] = g["Ke"].astype(MX)
                dec_sc[rows, :] = g["dec"]
                _chunk_outer_products(v, g["Ke"], upd_sc, kb)
                _chunk_outer_products(do, g["P"], dupd_sc, kb)
                return carry

            _block_loop(L // HBLK, block, 0)

            def fstep(c, st):
                rows = _chunk_rows(c)
                ck_sc[c] = st.astype(MX)
                dP_sc[rows, :] += _dot(do_sc[rows, :], st)
                dec = dec_sc[pl.ds(pl.multiple_of(c * CHUNK, CHUNK), 1), :]
                return st * dec + upd_sc[c]

            _chunk_loop(nC, rev, fstep, jnp.zeros((HEAD_DIM, HEAD_DIM), F32))

            def bstep(c, dst):
                rows = _chunk_rows(c)
                dec = dec_sc[pl.ds(pl.multiple_of(c * CHUNK, CHUNK), 1), :]
                dKe_sc[rows, :] = _dot(i_ref[rows, :], dst)
                dv_sc[rows, :] += _dot_nt(ke_sc[rows, :], dst)
                dbl = dec * jnp.sum(dst * ck_sc[c].astype(F32), axis=0, keepdims=True)
                dbl_sc[rows, :] = jnp.broadcast_to(dbl, (CHUNK, HEAD_DIM))
                return dst * dec + dupd_sc[c]

            _chunk_loop(nC, not rev, bstep, jnp.zeros((HEAD_DIM, HEAD_DIM), F32))

            def finish(kb, acc):
                rows = pl.ds(pl.multiple_of(kb * HBLK, HBLK), HBLK)
                q_r = q_ref[rows, :]
                g = _block_gates(q_r, f_ref[rows, :], lb_row, rev)
                dP, dkt, dke = dP_sc[rows, :], dKt_sc[rows, :], dKe_sc[rows, :]
                e = dke * g["Ke"]
                dlf = (_seg_cumsum(dP * g["P"] - dkt * g["Kt"], not rev) + _seg_cumsum(e, rev) - e
                       + dbl_sc[rows, :])
                df = dlf / g["f"] - (dkt * g["enb"] + dke * g["eend"])
                dp_ref[slot, rows, :] = (df * (1.0 - lb_row) * (g["sg"] * (1.0 - g["sg"]))).astype(MX)
                dq_sc[rows, :] += (dP * g["eb"]) * (g["sq"] * (1.0 + q_r * (1.0 - g["sq"])))
                return acc + jnp.sum(df * (1.0 - g["sg"]), axis=0, keepdims=True)

            dlb_ref[slot - 1:slot, :] = _block_loop(L // HBLK, finish, jnp.zeros((1, HEAD_DIM), F32))

        run_dir(ff_ref, lb_f, False, 1)
        run_dir(fb_ref, lb_b, True, 2)
        dp_ref[0] = dq_sc[...].astype(MX)
        dp_ref[3] = dv_sc[...].astype(MX)

        @pl.when(step_id == B * N_HEADS - 1)
        def _():
            scatter.wait()

    def blk(s):
        return pl.BlockSpec((None, None, L, HEAD_DIM), lambda b, h, s=s: (s, b, 0, h))

    seq = pl.BlockSpec((None, L, HEAD_DIM), lambda b, h: (b, 0, h))
    outs = pl.pallas_call(
        body, name="hgrn_bwd", grid=(B, N_HEADS),
        in_specs=[blk(0), blk(1), blk(2), blk(3), blk(4), seq, seq,
                  pl.BlockSpec((4, HEAD_DIM), lambda b, h: (0, h)),
                  pl.BlockSpec((1, HEAD_DIM), lambda b, h: (0, h))] + [ANY] * n,
        out_specs=[pl.BlockSpec((5, None, L, HEAD_DIM), lambda b, h: (0, b, 0, h)),
                   pl.BlockSpec((None, 2, HEAD_DIM), lambda b, h: (b, 0, h)),
                   pl.BlockSpec((None, 1, HEAD_DIM), lambda b, h: (b, 0, h))] + [ANY] * n,
        out_shape=[jax.ShapeDtypeStruct((5, B, L, D_MODEL), MX),
                   jax.ShapeDtypeStruct((B, 2, D_MODEL), F32),
                   jax.ShapeDtypeStruct((B, 1, D_MODEL), F32)] + _Exchange.out_shapes(grads, False),
        scratch_shapes=[pltpu.VMEM((L, HEAD_DIM), F32)] * 8
                       + [pltpu.VMEM((L, HEAD_DIM), MX), pltpu.VMEM((nC, HEAD_DIM, HEAD_DIM), MX),
                          pltpu.VMEM((nC, HEAD_DIM, HEAD_DIM), F32), pltpu.VMEM((nC, HEAD_DIM, HEAD_DIM), F32)]
                       + _Exchange.scratch(n),
        compiler_params=_params(("arbitrary", "arbitrary")),
    )(proj, proj, proj, proj, proj, o, dya, lbl, norm_g, *grads)
    return outs[0], outs[1], outs[2], outs[3:]


def _dproj_select(s, a5_ref, p_ref, g2_ref):
    return jnp.where(s < 5, a5_ref[...], jnp.where(s == 5, p_ref[...], g2_ref[...]))


def _dproj_specs(tm, tile_axis):
    def ix(args):
        return args[tile_axis], args[1 - tile_axis]
    a5 = pl.BlockSpec((None, tm, D_MODEL), lambda *a: (jnp.minimum(ix(a)[1], 4), ix(a)[0], 0))
    p = pl.BlockSpec((tm, D_MODEL), lambda *a: (ix(a)[0], 0))
    g2 = pl.BlockSpec((None, tm, D_MODEL), lambda *a: (jnp.clip(ix(a)[1] - 6, 0, 1), ix(a)[0], 0))
    return [a5, p, g2]


def _inproj_bwd(x2, dh, dproj5, dp, dgab, g_mix, w_in_g, d_win):
    T = x2.shape[0]
    tm = min(512, T)
    nT = T // tm

    def body(a5_ref, p_ref, g2_ref, w_ref, x_ref, dh_ref, g_ref, dwin_ref, dx_ref, dg_ref, recv_ref, acc_sc, *sems):
        i, s = pl.program_id(0), pl.program_id(1)
        scatter = _Exchange([dwin_ref], [recv_ref], sems, gather=False)

        @pl.when((i == 0) & (s == 0))
        def _():
            scatter.start()

        @pl.when(s == 0)
        def _():
            acc_sc[...] = jnp.zeros_like(acc_sc)

        @pl.when((i == 0) & (s == 0))
        def _():
            dg_ref[...] = jnp.zeros_like(dg_ref)

        acc_sc[...] += _dot_nt(_dproj_select(s, a5_ref, p_ref, g2_ref), w_ref[...])

        @pl.when(s == N_DEV - 1)
        def _():
            x = x_ref[...]
            r = lax.rsqrt(jnp.mean(x * x, axis=-1, keepdims=True) + RMS_EPS)
            xn = x * r
            du = acc_sc[...]
            dg_ref[...] += _fold8(du * xn)
            a = du * g_ref[...]
            dx_ref[...] = dh_ref[...] + r * a - xn * (r * jnp.mean(a * xn, axis=-1, keepdims=True))

        @pl.when((i == nT - 1) & (s == N_DEV - 1))
        def _():
            scatter.wait()

    tile = pl.BlockSpec((tm, D_MODEL), lambda i, s: (i, 0))
    return pl.pallas_call(
        body, name="inproj_bwd", grid=(nT, N_DEV),
        in_specs=_dproj_specs(tm, 0) + [pl.BlockSpec((None, D_MODEL, D_MODEL), lambda i, s: (s, 0, 0)),
                                        tile, tile, pl.BlockSpec((1, D_MODEL), lambda i, s: (0, 0)), ANY],
        out_specs=[tile, pl.BlockSpec((8, D_MODEL), lambda i, s: (0, 0)), ANY],
        out_shape=[jax.ShapeDtypeStruct((T, D_MODEL), F32), jax.ShapeDtypeStruct((8, D_MODEL), F32)]
                  + _Exchange.out_shapes([d_win], False),
        scratch_shapes=[pltpu.VMEM((tm, D_MODEL), F32)] + _Exchange.scratch(1),
        compiler_params=_params(("arbitrary", "arbitrary")),
    )(dproj5, dp, dgab, w_in_g, x2, dh, g_mix, d_win)


def _wgrad_in(u, dproj5, dp, dgab):
    T = u.shape[0]
    tm = min(BIG_TOKEN_TILE, T)

    def body(a5_ref, p_ref, g2_ref, u_ref, out_ref, acc_sc):
        s, t = pl.program_id(0), pl.program_id(1)

        @pl.when(t == 0)
        def _():
            acc_sc[...] = jnp.zeros_like(acc_sc)

        acc_sc[...] += _dot_tn(u_ref[...], _dproj_select(s, a5_ref, p_ref, g2_ref))

        @pl.when(t == pl.num_programs(1) - 1)
        def _():
            out_ref[...] = acc_sc[...].astype(MX)

    return pl.pallas_call(
        body, name="wgrad_in", grid=(N_DEV, T // tm),
        in_specs=_dproj_specs(tm, 1) + [pl.BlockSpec((tm, D_MODEL), lambda s, t: (t, 0))],
        out_specs=pl.BlockSpec((None, D_MODEL, D_MODEL), lambda s, t: (s, 0, 0)),
        out_shape=jax.ShapeDtypeStruct((N_DEV, D_MODEL, D_MODEL), MX),
        scratch_shapes=[pltpu.VMEM((D_MODEL, D_MODEL), F32)],
        compiler_params=_params(("parallel", "arbitrary")),
    )(dproj5, dp, dgab, u)


def _wgrad(a, g, name):
    Ba, T, K = a.shape
    Bg, _, Nn = g.shape
    nb = max(Ba, Bg)
    tm = min(BIG_TOKEN_TILE, T)
    nt = T // tm

    def body(a_ref, g_ref, out_ref, acc_sc):
        t = pl.program_id(1)

        @pl.when(t == 0)
        def _():
            acc_sc[...] = jnp.zeros_like(acc_sc)

        acc_sc[...] += _dot_tn(a_ref[...], g_ref[...])

        @pl.when(t == nt - 1)
        def _():
            out_ref[...] = acc_sc[...].astype(MX)

    return pl.pallas_call(
        body, name=name, grid=(nb, nt),
        in_specs=[pl.BlockSpec((None, tm, K), lambda s, t: (s if Ba > 1 else 0, t, 0)),
                  pl.BlockSpec((None, tm, Nn), lambda s, t: (s if Bg > 1 else 0, t, 0))],
        out_specs=pl.BlockSpec((None, K, Nn), lambda s, t: (s, 0, 0)),
        out_shape=jax.ShapeDtypeStruct((nb, K, Nn), MX),
        scratch_shapes=[pltpu.VMEM((K, Nn), F32)],
        compiler_params=_params(("parallel", "arbitrary")),
    )(a, g)


def _mesh_pos():
    return lax.axis_index("x"), lax.axis_index("y"), lax.axis_index("c")


def _all_gather(shards):
    n = len(shards)

    def body(*refs):
        xs, outs = refs[:n], refs[n:2 * n]
        send_sems, recv_sems, local_sems = refs[2 * n:]
        x, y, c = _mesh_pos()
        me, sibling = (x, y, c), (x, y, 1 - c)
        chips = [(1 - x, y), (x, 1 - y), (1 - x, 1 - y)]

        def copy(a, k, block, to, src=None):
            slot = outs[a].at[4 * block[0] + 2 * block[1] + block[2]]
            return pltpu.make_async_remote_copy(
                src_ref=slot if src is None else src, dst_ref=slot,
                send_sem=send_sems.at[7 * a + k], recv_sem=recv_sems.at[7 * a + k],
                device_id=to, device_id_type=pl.DeviceIdType.MESH)

        mine = [pltpu.make_async_copy(xs[a], outs[a].at[4 * x + 2 * y + c], local_sems.at[a]) for a in range(n)]
        for cp in mine:
            cp.start()
        first = []
        for a in range(n):
            first.append(copy(a, 0, me, sibling, src=xs[a]))
            first += [copy(a, 1 + j, me, (*chip, c), src=xs[a]) for j, chip in enumerate(chips)]
        for cp in first:
            cp.start()
        passed = []
        for j, chip in enumerate(chips):
            for a in range(n):
                copy(a, 1 + j, (*chip, c), me).wait_recv()
                fwd = copy(a, 4 + j, (*chip, c), sibling)
                fwd.start()
                passed.append(fwd)
        for a in range(n):
            copy(a, 0, sibling, me).wait_recv()
            for j, chip in enumerate(chips):
                copy(a, 4 + j, (*chip, 1 - c), me).wait_recv()
        for cp in first + passed:
            cp.wait_send()
        for cp in mine:
            cp.wait()

    return pl.pallas_call(
        body, name="all_gather_weights",
        in_specs=[ANY] * n, out_specs=[ANY] * n,
        out_shape=[jax.ShapeDtypeStruct((N_DEV,) + s.shape, s.dtype) for s in shards],
        scratch_shapes=[pltpu.SemaphoreType.DMA((7 * n,)), pltpu.SemaphoreType.DMA((7 * n,)),
                        pltpu.SemaphoreType.DMA((n,))],
    )(*shards)


def _device_of(p):
    return (p // 4, (p // 2) % 2, p % 2)


class _Exchange:
    def __init__(self, srcs, outs, sems, gather):
        send_sems, recv_sems, local_sems = sems
        x, y, c = _mesh_pos()
        me = 4 * x + 2 * y + c
        self.sends, self.arrivals, self.mine = [], [], []
        for a, (src, out) in enumerate(zip(srcs, outs)):
            self.mine.append(pltpu.make_async_copy(src if gather else src.at[me], out.at[me], local_sems.at[a]))
            for j in range(1, N_DEV):
                to, frm = (me + j) % N_DEV, (me + N_DEV - j) % N_DEV
                pair = dict(send_sem=send_sems.at[7 * a + j - 1], recv_sem=recv_sems.at[7 * a + j - 1],
                            device_id_type=pl.DeviceIdType.MESH)
                self.sends.append(pltpu.make_async_remote_copy(
                    src_ref=src if gather else src.at[to], dst_ref=out.at[me], device_id=_device_of(to), **pair))
                self.arrivals.append(pltpu.make_async_remote_copy(
                    src_ref=src if gather else src.at[frm], dst_ref=out.at[frm], device_id=_device_of(frm), **pair))

    def start(self):
        for cp in self.mine + self.sends:
            cp.start()

    def wait(self):
        for cp in self.arrivals:
            cp.wait_recv()
        for cp in self.sends:
            cp.wait_send()
        for cp in self.mine:
            cp.wait()

    @staticmethod
    def scratch(n):
        return [pltpu.SemaphoreType.DMA((7 * n,)), pltpu.SemaphoreType.DMA((7 * n,)), pltpu.SemaphoreType.DMA((n,))]

    @staticmethod
    def out_shapes(arrays, gather):
        return [jax.ShapeDtypeStruct(((N_DEV,) + a.shape) if gather else a.shape, a.dtype) for a in arrays]


def _all_reduce_small(v):
    R, C = v.shape

    def body(v_ref, out_ref, slots, send_sems, recv_sems):
        x, y, c = _mesh_pos()
        me = 4 * x + 2 * y + c

        def copy(j, to):
            return pltpu.make_async_remote_copy(
                src_ref=v_ref, dst_ref=slots.at[me],
                send_sem=send_sems.at[j - 1], recv_sem=recv_sems.at[j - 1],
                device_id=_device_of(to), device_id_type=pl.DeviceIdType.MESH)

        sends = [copy(j, (me + j) % N_DEV) for j in range(1, N_DEV)]
        for cp in sends:
            cp.start()
        slots[me] = v_ref[...]
        for j in range(1, N_DEV):
            frm = (me + N_DEV - j) % N_DEV
            pltpu.make_async_remote_copy(
                src_ref=v_ref, dst_ref=slots.at[frm], send_sem=send_sems.at[j - 1], recv_sem=recv_sems.at[j - 1],
                device_id=_device_of(frm), device_id_type=pl.DeviceIdType.MESH).wait_recv()
        for cp in sends:
            cp.wait_send()
        acc = slots[0]
        for p in range(1, N_DEV):
            acc = acc + slots[p]
        out_ref[...] = acc

    return pl.pallas_call(
        body, name="all_reduce_small",
        in_specs=[pl.BlockSpec(memory_space=pltpu.VMEM)], out_specs=pl.BlockSpec(memory_space=pltpu.VMEM),
        out_shape=jax.ShapeDtypeStruct((R, C), F32),
        scratch_shapes=[pltpu.VMEM((N_DEV, R, C), F32), pltpu.SemaphoreType.DMA((7,)), pltpu.SemaphoreType.DMA((7,))],
    )(v)


def _adamw_math(w, g, m, v):
    m = ADAM_B1 * m + (1.0 - ADAM_B1) * g
    v = ADAM_B2 * v + (1.0 - ADAM_B2) * (g * g)
    m_hat = m / (1.0 - ADAM_B1 ** ADAM_STEP)
    v_hat = v / (1.0 - ADAM_B2 ** ADAM_STEP)
    delta = -ADAM_LR * (m_hat / (jnp.sqrt(v_hat) + ADAM_EPS) + ADAM_WD * w)
    return delta, m, v


def _adamw_reduce(recv, w, m, v, name):
    R, C = w.shape
    tr = R if R <= 256 else 256
    while R % tr:
        tr //= 2

    def body(r_ref, w_ref, m_ref, v_ref, g_ref, d_ref, nm_ref, nv_ref):
        g = r_ref[0].astype(F32)
        for p in range(1, N_DEV):
            g = g + r_ref[p].astype(F32)
        d, nm, nv = _adamw_math(w_ref[...], g, m_ref[...], v_ref[...])
        g_ref[...] = g
        d_ref[...] = d
        nm_ref[...] = nm
        nv_ref[...] = nv

    tile = pl.BlockSpec((tr, C), lambda i: (i, 0))
    shp = jax.ShapeDtypeStruct((R, C), F32)
    return pl.pallas_call(
        body, name=name, grid=(R // tr,),
        in_specs=[pl.BlockSpec((N_DEV, tr, C), lambda i: (0, i, 0)), tile, tile, tile],
        out_specs=[tile] * 4, out_shape=[shp] * 4,
        compiler_params=_params(("parallel",)),
    )(recv, w, m, v)


def _adamw_small(g, w, m, v):
    def body(g_ref, w_ref, m_ref, v_ref, go_ref, d_ref, nm_ref, nv_ref):
        go_ref[...] = g_ref[...]
        for d in range(2):
            p0 = _sig(w_ref[8 + 2 * d:9 + 2 * d, :] - w_ref[9 + 2 * d:10 + 2 * d, :])
            dl0 = g_ref[12 + d:13 + d, :] * p0 * (1.0 - p0)
            go_ref[8 + 2 * d:9 + 2 * d, :] = dl0
            go_ref[9 + 2 * d:10 + 2 * d, :] = -dl0
            go_ref[12 + d:13 + d, :] = jnp.zeros((1, D_MODEL), F32)
        d, nm, nv = _adamw_math(w_ref[...], go_ref[...], m_ref[...], v_ref[...])
        d_ref[...] = d
        nm_ref[...] = nm
        nv_ref[...] = nv

    shp = jax.ShapeDtypeStruct(g.shape, F32)
    vm = pl.BlockSpec(memory_space=pltpu.VMEM)
    return pl.pallas_call(body, name="adamw_small", in_specs=[vm] * 4, out_specs=[vm] * 4, out_shape=[shp] * 4)(g, w, m, v)


def _local_step(x, tgt, g_mix, lb, norm_g, pool_scale, g_ffn, g_final, w_in_g, late_shards):
    B, L, _ = x.shape
    T = B * L
    rows = D_MODEL // N_DEV
    x2, tgt2 = x.reshape(T, D_MODEL), tgt.reshape(T, D_MODEL)

    proj, u = _rms_inproj(x2, g_mix, w_in_g)
    proj4 = proj.reshape(N_DEV, B, L, D_MODEL)
    o, ya, (wa, wb, wo, wfi_g, wfo_g, pw_g) = _hgrn_fwd(proj4, lb, norm_g, B, L, late_shards)
    wa, wb, wo = (w_.reshape(D_MODEL, D_MODEL) for w_ in (wa, wb, wo))
    wfo_g = wfo_g.reshape(4, FF_BLOCK, D_MODEL)
    pool_w_full = pw_g.reshape(N_DEV, 4, 32, POOL_DIM).transpose(1, 0, 2, 3).reshape(4, POOL_DIM, POOL_DIM)
    yb = _pool_fwd(proj4, pool_w_full, pool_scale, B, L)
    ya2, yb2 = ya.reshape(T, D_MODEL), yb.reshape(T, D_MODEL)
    za, zb, mg, h = _merge_out(x2, proj, ya2, yb2, wa, wb, wo)
    gate, up, hid, u2, dh2, dh2b, loss_p, dg_final = _ffn_fwd_loss(h, tgt2, g_ffn, g_final, wfi_g, wfo_g)
    loss = jnp.sum(loss_p[:, 0, 0])

    dgu, dh, dhb, dg_ffn = _ffn_bwd(h, dh2, dh2b, gate, up, g_ffn, wfi_g, wfo_g)
    d_wfo = _wgrad(hid, dh2b[None], "wgrad_ffn_out")
    d_wfi = _wgrad(u2[None], dgu.reshape(N_DEV, T, FF_BLOCK), "wgrad_ffn_in")
    dza, dzb, dgab, dya, dyb = _merge_bwd(dhb, proj, za, zb, wa, wb, wo)
    d_wo = _wgrad(mg[None], dhb[None], "wgrad_out")
    d_wa = _wgrad(ya2[None], dza[None], "wgrad_branch_a")
    d_wb = _wgrad(yb2[None], dzb[None], "wgrad_branch_b")
    dp, dpw_p, dps_p = _pool_bwd(proj4, dyb.reshape(B, L, D_MODEL), pool_w_full, pool_scale, B, L)
    d_pw = dpw_p.sum(0).reshape(4, N_DEV, 32, POOL_DIM).transpose(1, 0, 2, 3).reshape(N_DEV, 128, POOL_DIM)
    slices = [d_wa.reshape(N_DEV, rows, D_MODEL), d_wb.reshape(N_DEV, rows, D_MODEL),
              d_wo.reshape(N_DEV, rows, D_MODEL), d_wfi, d_wfo.reshape(N_DEV, FF_BLOCK // 2, D_MODEL),
              d_pw.astype(MX)]
    dproj5, dlb_p, dng_p, recv = _hgrn_bwd(proj4, o, dya.reshape(B, L, D_MODEL), lb, norm_g, B, L, slices)
    dproj5 = dproj5.reshape(5, T, D_MODEL)
    dp2 = dp.reshape(T, D_MODEL)
    d_win = _wgrad_in(u, dproj5, dp2, dgab)
    grad_x, dg_mix, recv_win = _inproj_bwd(x2, dh, dproj5, dp2, dgab, g_mix, w_in_g, d_win)

    small = dict(g_mix=dg_mix.sum(0), hgrn_norm_g=dng_p.sum((0, 1)), pool_scale=dps_p.sum((0, 1)),
                 g_ffn=dg_ffn.sum(0), g_final=dg_final.sum(0), lb=dlb_p.sum(0))
    return loss, grad_x.reshape(B, L, D_MODEL), [recv_win] + list(recv), small


def kernel(x, g_mix, w_in, lb_logits, hgrn_norm_g, pool_w, pool_scale, w_branch_a, w_branch_b, w_out, g_ffn, w_ffn_in, w_ffn_out, g_final, loss_target, m_g_mix, m_w_in, m_lb_logits, m_hgrn_norm_g, m_pool_w, m_pool_scale, m_w_branch_a, m_w_branch_b, m_w_out, m_g_ffn, m_w_ffn_in, m_w_ffn_out, m_g_final, v_g_mix, v_w_in, v_lb_logits, v_hgrn_norm_g, v_pool_w, v_pool_scale, v_w_branch_a, v_w_branch_b, v_w_out, v_g_ffn, v_w_ffn_in, v_w_ffn_out, v_g_final):
    me = 4 * lax.axis_index("x") + 2 * lax.axis_index("y") + lax.axis_index("c")

    w_in_g, lbl_g = _all_gather([w_in[0].astype(MX), jnp.pad(lb_logits.reshape(4, HEAD_DIM), ((0, 4), (0, 0)))])
    lbl = lbl_g[:, :4].transpose(1, 0, 2).reshape(4, D_MODEL)
    late_shards = [w_branch_a[0].astype(MX), w_branch_b[0].astype(MX), w_out[0].astype(MX),
                   w_ffn_in[0].astype(MX), w_ffn_out[0].astype(MX), pool_w[0].reshape(4 * 32, POOL_DIM).astype(MX)]

    loss, grad_x, recv, small = _local_step(
        x, loss_target, g_mix, lbl, hgrn_norm_g, pool_scale, g_ffn, g_final[None], w_in_g, late_shards)
    loss = lax.psum(loss, ("x", "y", "c"))

    packed = jnp.zeros((16, D_MODEL), F32)
    names = ["g_mix", "hgrn_norm_g", "pool_scale", "g_ffn", "g_final"]
    for i, nme in enumerate(names):
        packed = packed.at[i].set(small[nme])
    packed = packed.at[5:7].set(small["lb"])
    red = _all_reduce_small(packed)
    dlb_mine = lax.dynamic_slice_in_dim(red[5:7], me * HEAD_DIM, HEAD_DIM, axis=1)

    sw = jnp.zeros((16, D_MODEL), F32)
    sm = jnp.zeros((16, D_MODEL), F32)
    sv = jnp.ones((16, D_MODEL), F32)
    smalls = [(g_mix, m_g_mix, v_g_mix), (hgrn_norm_g, m_hgrn_norm_g, v_hgrn_norm_g),
              (pool_scale, m_pool_scale, v_pool_scale), (g_ffn, m_g_ffn, v_g_ffn),
              (g_final[None], m_g_final[None], v_g_final[None])]
    for i, (w_, m_, v_) in enumerate(smalls):
        sw, sm, sv = sw.at[i].set(w_[0]), sm.at[i].set(m_[0]), sv.at[i].set(v_[0])
    sg = red.at[5:].set(0.0)
    sg = sg.at[12:14, :HEAD_DIM].set(dlb_mine)
    sw = sw.at[8:12, :HEAD_DIM].set(lb_logits.reshape(4, HEAD_DIM))
    sm = sm.at[8:12, :HEAD_DIM].set(m_lb_logits.reshape(4, HEAD_DIM))
    sv = sv.at[8:12, :HEAD_DIM].set(v_lb_logits.reshape(4, HEAD_DIM))
    sg, sd, snm, snv = _adamw_small(sg, sw, sm, sv)

    def small_out(arr, i, like):
        return arr[i].reshape(like.shape)

    def lb_out(arr):
        return arr[8:12, :HEAD_DIM].reshape(2, 2, HEAD_DIM)

    order = ["w_in", "w_branch_a", "w_branch_b", "w_out", "w_ffn_in", "w_ffn_out", "pool_w"]
    params = dict(w_in=(w_in, m_w_in, v_w_in), w_branch_a=(w_branch_a, m_w_branch_a, v_w_branch_a),
                  w_branch_b=(w_branch_b, m_w_branch_b, v_w_branch_b), w_out=(w_out, m_w_out, v_w_out),
                  w_ffn_in=(w_ffn_in, m_w_ffn_in, v_w_ffn_in), w_ffn_out=(w_ffn_out, m_w_ffn_out, v_w_ffn_out),
                  pool_w=(pool_w, m_pool_w, v_pool_w))
    res = {}
    for nme, r in zip(order, recv):
        w_, m_, v_ = params[nme]
        shape2 = r.shape[1:]
        outs = _adamw_reduce(r, w_.reshape(shape2), m_.reshape(shape2), v_.reshape(shape2), "adamw_" + nme)
        res[nme] = [o_.reshape(w_.shape) for o_ in outs]

    def pick(k):
        small_src = [sg, sd, snm, snv][k]
        return [small_out(small_src, 0, g_mix), res["w_in"][k], lb_out(small_src), small_out(small_src, 1, hgrn_norm_g),
                res["pool_w"][k], small_out(small_src, 2, pool_scale), res["w_branch_a"][k], res["w_branch_b"][k],
                res["w_out"][k], small_out(small_src, 3, g_ffn), res["w_ffn_in"][k], res["w_ffn_out"][k],
                small_out(small_src, 4, g_final)]

    return (loss, grad_x, *pick(0), *pick(1), *pick(2), *pick(3))
```

```python
import functools

import jax
import jax.numpy as jnp
from jax import lax
from jax.experimental import pallas as pl
from jax.experimental.pallas import tpu as pltpu

F32 = jnp.float32
MX = jnp.bfloat16

D_MODEL = 1024
N_HEADS = 8
HEAD_DIM = 128
CHUNK = 16
POOL_WINDOWS = (2, 4, 8, 16)
POOL_DIM = 256
FF_BLOCK = 704
N_DEV = 8
RMS_EPS = 1e-6
ADAM_LR, ADAM_B1, ADAM_B2, ADAM_EPS, ADAM_WD, ADAM_STEP = 0.001, 0.9, 0.999, 1e-08, 0.01, 10
VMEM_LIMIT = 56 * 1024 * 1024
BIG_TOKEN_TILE = 1024
ANY = pl.BlockSpec(memory_space=pl.ANY)


def _params(sem=None):
    return pltpu.CompilerParams(dimension_semantics=sem, vmem_limit_bytes=VMEM_LIMIT)


def _dot(a, b):
    return lax.dot_general(a.astype(MX), b.astype(MX), (((1,), (0,)), ((), ())), preferred_element_type=F32)


def _dot_nt(a, b):
    return lax.dot_general(a.astype(MX), b.astype(MX), (((1,), (1,)), ((), ())), preferred_element_type=F32)


def _dot_tn(a, b):
    return lax.dot_general(a.astype(MX), b.astype(MX), (((0,), (0,)), ((), ())), preferred_element_type=F32)


def _sig(x):
    return 1.0 / (1.0 + jnp.exp(-x))


def _fold8(v):
    return v.reshape(v.shape[0] // 8, 8, v.shape[1]).sum(axis=0)


def _shift_rows(x, s):
    n = x.shape[0]
    row = lax.broadcasted_iota(jnp.int32, x.shape, 0)
    if s > 0:
        return jnp.where(row >= s, pltpu.roll(x, s, 0), 0.0)
    return jnp.where(row < n + s, pltpu.roll(x, n + s, 0), 0.0)


def _cumsum_rows(x, rev):
    s = 1
    while s < x.shape[0]:
        x = x + _shift_rows(x, -s if rev else s)
        s *= 2
    return x


def _rms_inproj(x2, g_mix, w_in_g):
    T = x2.shape[0]
    tm = min(BIG_TOKEN_TILE, T)

    def body(x_ref, g_ref, w_ref, proj_ref, u_ref, u_sc):
        @pl.when(pl.program_id(1) == 0)
        def _():
            x = x_ref[...]
            r = lax.rsqrt(jnp.mean(x * x, axis=-1, keepdims=True) + RMS_EPS)
            u = (x * r * g_ref[...]).astype(MX)
            u_sc[...] = u
            u_ref[...] = u
        proj_ref[...] = jnp.dot(u_sc[...], w_ref[...], preferred_element_type=F32)

    return pl.pallas_call(
        body, name="rms_inproj", grid=(T // tm, N_DEV),
        in_specs=[pl.BlockSpec((tm, D_MODEL), lambda i, k: (i, 0)),
                  pl.BlockSpec((1, D_MODEL), lambda i, k: (0, 0)),
                  pl.BlockSpec((None, D_MODEL, D_MODEL), lambda i, k: (k, 0, 0))],
        out_specs=[pl.BlockSpec((None, tm, D_MODEL), lambda i, k: (k, i, 0)),
                   pl.BlockSpec((tm, D_MODEL), lambda i, k: (i, 0))],
        out_shape=[jax.ShapeDtypeStruct((N_DEV, T, D_MODEL), F32), jax.ShapeDtypeStruct((T, D_MODEL), MX)],
        scratch_shapes=[pltpu.VMEM((tm, D_MODEL), MX)],
        compiler_params=_params(("parallel", "arbitrary")),
    )(x2, g_mix, w_in_g)


HBLK = 128


def _seg_cumsum(x, rev):
    n = x.shape[0]
    pos = lax.broadcasted_iota(jnp.int32, x.shape, 0) & (CHUNK - 1)
    s = 1
    while s < CHUNK:
        if rev:
            x = x + jnp.where(pos < CHUNK - s, pltpu.roll(x, n - s, 0), 0.0)
        else:
            x = x + jnp.where(pos >= s, pltpu.roll(x, s, 0), 0.0)
        s *= 2
    return x


def _block_gates(q_r, f_r, lb_row, rev):
    sq = _sig(q_r)
    q = q_r * sq
    sg = _sig(f_r)
    f = lb_row + (1.0 - lb_row) * sg
    k = 1.0 - f
    lf = jnp.log(f)
    pre = _seg_cumsum(lf, False)
    suf = _seg_cumsum(lf, True)
    tot = pre + suf - lf
    b = suf if rev else pre
    eb = jnp.exp(b)
    enb = jnp.exp(-b)
    eend = jnp.exp(tot - b)
    return dict(sq=sq, sg=sg, f=f, eb=eb, enb=enb, eend=eend, dec=jnp.exp(tot),
                P=q * eb, Kt=k * enb, Ke=k * eend)


def _block_mask(rev, transposed=False):
    ri = lax.broadcasted_iota(jnp.int32, (HBLK, HBLK), 0)
    ci = lax.broadcasted_iota(jnp.int32, (HBLK, HBLK), 1)
    same = (ri // CHUNK) == (ci // CHUNK)
    return same & ((ci >= ri) if rev != transposed else (ci <= ri))


def _chunk_rows(c):
    return pl.ds(pl.multiple_of(c * CHUNK, CHUNK), CHUNK)


def _chunk_outer_products(a, b, out_sc, kb):
    a, b = a.astype(MX), b.astype(MX)
    for u in range(HBLK // CHUNK):
        r = slice(u * CHUNK, (u + 1) * CHUNK)
        out_sc[kb * (HBLK // CHUNK) + u] = _dot_tn(a[r, :], b[r, :])


UNIT_BLOCKS = 2
UNIT_CHUNKS = UNIT_BLOCKS * (HBLK // CHUNK)


def _unit_blocks(u, fn, carry):
    for b in range(UNIT_BLOCKS):
        carry = fn(u * UNIT_BLOCKS + b, carry)
    return carry


def _unit_chunks(u, descending, fn, carry):
    for j in range(UNIT_CHUNKS):
        carry = fn(u * UNIT_CHUNKS + (UNIT_CHUNKS - 1 - j if descending else j), carry)
    return carry


def _pipelined(n_units, descending, first, c1, second, c2):
    def unit(t):
        return n_units - 1 - t if descending else t

    c1 = first(unit(0), c1)

    def both(t, cs):
        return first(unit(t + 1), cs[0]), second(unit(t), cs[1])

    c1, c2 = lax.fori_loop(0, n_units - 1, both, (c1, c2))
    return c1, second(unit(n_units - 1), c2)


def _lower_bounds(lbl_ref):
    return _sig(lbl_ref[0:1, :] - lbl_ref[1:2, :]), _sig(lbl_ref[2:3, :] - lbl_ref[3:4, :])


def _hgrn_fwd(proj, lbl, norm_g, B, L, shards):
    nC = L // CHUNK
    n = len(shards)

    def body(*refs):
        q_ref, ff_ref, fb_ref, i_ref, og_ref, lbl_ref, ng_ref = refs[:7]
        o_ref, ya_ref = refs[7 + n:9 + n]
        o_sc, p_sc, dec_sc, upd_sc = refs[9 + 2 * n:13 + 2 * n]
        step_id = pl.program_id(0) * N_HEADS + pl.program_id(1)
        gather = _Exchange(refs[7:7 + n], refs[9 + n:9 + 2 * n], refs[13 + 2 * n:], gather=True)

        @pl.when(step_id == 0)
        def _():
            gather.start()

        lb_f, lb_b = _lower_bounds(lbl_ref)
        o_sc[...] = jnp.zeros_like(o_sc)

        def run_dir(f_ref, lb_row, rev):
            def block(kb, carry):
                rows = pl.ds(pl.multiple_of(kb * HBLK, HBLK), HBLK)
                g = _block_gates(q_ref[rows, :], f_ref[rows, :], lb_row, rev)
                v = i_ref[rows, :]
                a = jnp.where(_block_mask(rev), _dot_nt(g["P"], g["Kt"]), 0.0)
                o_sc[rows, :] += _dot(a, v)
                p_sc[rows, :] = g["P"].astype(MX)
                dec_sc[rows, :] = g["dec"]
                _chunk_outer_products(v, g["Ke"], upd_sc, kb)
                return carry

            def step(c, st):
                rows = _chunk_rows(c)
                o_sc[rows, :] += _dot_nt(p_sc[rows, :], st)
                dec = dec_sc[pl.ds(pl.multiple_of(c * CHUNK, CHUNK), 1), :]
                return st * dec + upd_sc[c]

            _pipelined(nC // UNIT_CHUNKS, rev,
                       lambda u, c: _unit_blocks(u, block, c), 0,
                       lambda u, st: _unit_chunks(u, rev, step, st), jnp.zeros((HEAD_DIM, HEAD_DIM), F32))

        run_dir(ff_ref, lb_f, False)
        run_dir(fb_ref, lb_b, True)
        o = o_sc[...]
        o_ref[...] = o
        on = o * lax.rsqrt(jnp.mean(o * o, axis=-1, keepdims=True) + RMS_EPS)
        og = og_ref[...]
        ya_ref[...] = ((on * ng_ref[...]) * (og * _sig(og))).astype(MX)

        @pl.when(step_id == B * N_HEADS - 1)
        def _():
            gather.wait()

    def blk(s):
        return pl.BlockSpec((None, None, L, HEAD_DIM), lambda b, h, s=s: (s, b, 0, h))

    out_blk = pl.BlockSpec((None, L, HEAD_DIM), lambda b, h: (b, 0, h))
    outs = pl.pallas_call(
        body, name="hgrn_fwd", grid=(B, N_HEADS),
        in_specs=[blk(0), blk(1), blk(2), blk(3), blk(4),
                  pl.BlockSpec((4, HEAD_DIM), lambda b, h: (0, h)),
                  pl.BlockSpec((1, HEAD_DIM), lambda b, h: (0, h))] + [ANY] * n,
        out_specs=[out_blk, out_blk] + [ANY] * n,
        out_shape=[jax.ShapeDtypeStruct((B, L, D_MODEL), F32), jax.ShapeDtypeStruct((B, L, D_MODEL), MX)]
                  + _Exchange.out_shapes(shards, True),
        scratch_shapes=[pltpu.VMEM((L, HEAD_DIM), F32), pltpu.VMEM((L, HEAD_DIM), MX),
                        pltpu.VMEM((L, HEAD_DIM), F32), pltpu.VMEM((nC, HEAD_DIM, HEAD_DIM), F32)]
                       + _Exchange.scratch(n),
        compiler_params=_params(("arbitrary", "arbitrary")),
    )(proj, proj, proj, proj, proj, lbl, norm_g, *shards)
    return outs[0], outs[1], outs[2:]


POOL_PAD = 8


def _pool_window(p, ext_sc, half, adjoint):
    L = p.shape[0]
    n = L + 2 * POOL_PAD
    ext_sc[0:POOL_PAD, :] = jnp.zeros((POOL_PAD, p.shape[1]), F32)
    ext_sc[POOL_PAD + L:n, :] = jnp.zeros((POOL_PAD, p.shape[1]), F32)
    ext_sc[POOL_PAD:POOL_PAD + L, :] = p
    x = ext_sc[...]
    s = x + pltpu.roll(x, 1 if adjoint else n - 1, 0)
    w = 1
    while w < half:
        s = pltpu.roll(s, w, 0) + pltpu.roll(s, n - w, 0)
        w *= 2
    ext_sc[...] = s
    return ext_sc[POOL_PAD:POOL_PAD + L, :]


def _pool_count(L, half):
    t = lax.broadcasted_iota(jnp.int32, (L, 1), 0)
    lo = jnp.clip(t - half + 1, 0, L)
    hi = jnp.clip(t + half + 1, 0, L)
    return (hi - lo).astype(F32)


def _pool_fwd(proj, pool_w_full, pool_scale, B, L):
    def body(p_ref, w_ref, s_ref, yb_ref, ext_sc):
        for g, win in enumerate(POOL_WINDOWS):
            cols = slice(g * POOL_DIM, (g + 1) * POOL_DIM)
            p = p_ref[:, cols]
            y = _pool_window(p, ext_sc, win // 2, False) / _pool_count(L, win // 2) - p
            yb_ref[:, cols] = (_dot(y, w_ref[g]) * s_ref[:, cols]).astype(MX)

    return pl.pallas_call(
        body, name="pool_fwd", grid=(B,),
        in_specs=[pl.BlockSpec((None, None, L, D_MODEL), lambda b: (5, b, 0, 0)),
                  pl.BlockSpec((4, POOL_DIM, POOL_DIM), lambda b: (0, 0, 0)),
                  pl.BlockSpec((1, D_MODEL), lambda b: (0, 0))],
        out_specs=pl.BlockSpec((None, L, D_MODEL), lambda b: (b, 0, 0)),
        out_shape=jax.ShapeDtypeStruct((B, L, D_MODEL), MX),
        scratch_shapes=[pltpu.VMEM((L + 2 * POOL_PAD, POOL_DIM), F32)],
        compiler_params=_params(("parallel",)),
    )(proj, pool_w_full, pool_scale)


def _merge_out(x2, proj, ya, yb, wa, wb, wo):
    T = x2.shape[0]
    tm = min(512, T)

    def body(x_ref, ga_ref, gb_ref, ya_ref, yb_ref, wa_ref, wb_ref, wo_ref, za_ref, zb_ref, mg_ref, h_ref):
        za = jnp.dot(ya_ref[...], wa_ref[...], preferred_element_type=F32)
        zb = jnp.dot(yb_ref[...], wb_ref[...], preferred_element_type=F32)
        mg = (_sig(ga_ref[...]) * za + _sig(gb_ref[...]) * zb).astype(MX)
        za_ref[...] = za
        zb_ref[...] = zb
        mg_ref[...] = mg
        h_ref[...] = x_ref[...] + jnp.dot(mg, wo_ref[...], preferred_element_type=F32)

    tile = pl.BlockSpec((tm, D_MODEL), lambda i: (i, 0))
    wspec = pl.BlockSpec((D_MODEL, D_MODEL), lambda i: (0, 0))
    return pl.pallas_call(
        body, name="merge_out", grid=(T // tm,),
        in_specs=[tile,
                  pl.BlockSpec((None, tm, D_MODEL), lambda i: (6, i, 0)),
                  pl.BlockSpec((None, tm, D_MODEL), lambda i: (7, i, 0)),
                  tile, tile, wspec, wspec, wspec],
        out_specs=[tile, tile, tile, tile],
        out_shape=[jax.ShapeDtypeStruct((T, D_MODEL), F32), jax.ShapeDtypeStruct((T, D_MODEL), F32),
                   jax.ShapeDtypeStruct((T, D_MODEL), MX), jax.ShapeDtypeStruct((T, D_MODEL), F32)],
        compiler_params=_params(("parallel",)),
    )(x2, proj, proj, ya, yb, wa, wb, wo)


def _ffn_fwd_loss(h, tgt, g_ffn, g_final, wfi_g, wfo_g):
    T = h.shape[0]
    tm = min(512, T)
    nT = T // tm

    def body(h_ref, t_ref, gf_ref, gl_ref, wg_ref, wu_ref, wo_ref,
             gate_ref, up_ref, hid_ref, u2_ref, dh2_ref, dh2b_ref, loss_ref, dgl_ref, u2_sc, acc_sc):
        i, j = pl.program_id(0), pl.program_id(1)

        @pl.when(j == 0)
        def _():
            hh = h_ref[...]
            r = lax.rsqrt(jnp.mean(hh * hh, axis=-1, keepdims=True) + RMS_EPS)
            u2 = (hh * r * gf_ref[...]).astype(MX)
            u2_sc[...] = u2
            u2_ref[...] = u2
            acc_sc[...] = jnp.zeros_like(acc_sc)

        @pl.when((i == 0) & (j == 0))
        def _():
            dgl_ref[...] = jnp.zeros_like(dgl_ref)

        gate = jnp.dot(u2_sc[...], wg_ref[...], preferred_element_type=F32)
        up = jnp.dot(u2_sc[...], wu_ref[...], preferred_element_type=F32)
        hid = ((gate * _sig(gate)) * up).astype(MX)
        gate_ref[...] = gate
        up_ref[...] = up
        hid_ref[...] = hid
        acc_sc[...] += jnp.dot(hid, wo_ref[...], preferred_element_type=F32)

        @pl.when(j == 3)
        def _():
            h2 = h_ref[...] + acc_sc[...]
            r = lax.rsqrt(jnp.mean(h2 * h2, axis=-1, keepdims=True) + RMS_EPS)
            hn = h2 * r
            gl = gl_ref[...]
            err = hn * gl - t_ref[...]
            tok = jnp.mean(err * err, axis=-1, keepdims=True)
            loss_ref[...] = jnp.full(loss_ref.shape, 0.5 * jnp.sum(tok), F32)
            dy = err * (1.0 / D_MODEL)
            dgl_ref[...] += _fold8(dy * hn)
            a = dy * gl
            dh2 = r * a - hn * (r * jnp.mean(a * hn, axis=-1, keepdims=True))
            dh2_ref[...] = dh2
            dh2b_ref[...] = dh2.astype(MX)

    tile = pl.BlockSpec((tm, D_MODEL), lambda i, j: (i, 0))
    vec = pl.BlockSpec((1, D_MODEL), lambda i, j: (0, 0))
    ftile = pl.BlockSpec((None, tm, FF_BLOCK), lambda i, j: (j, i, 0))
    return pl.pallas_call(
        body, name="ffn_fwd_loss", grid=(nT, 4),
        in_specs=[tile, tile, vec, vec,
                  pl.BlockSpec((None, D_MODEL, FF_BLOCK), lambda i, j: (j, 0, 0)),
                  pl.BlockSpec((None, D_MODEL, FF_BLOCK), lambda i, j: (j + 4, 0, 0)),
                  pl.BlockSpec((None, FF_BLOCK, D_MODEL), lambda i, j: (j, 0, 0))],
        out_specs=[ftile, ftile, ftile, tile, tile, tile,
                   pl.BlockSpec((None, 8, 128), lambda i, j: (i, 0, 0)),
                   pl.BlockSpec((8, D_MODEL), lambda i, j: (0, 0))],
        out_shape=[jax.ShapeDtypeStruct((4, T, FF_BLOCK), F32), jax.ShapeDtypeStruct((4, T, FF_BLOCK), F32),
                   jax.ShapeDtypeStruct((4, T, FF_BLOCK), MX), jax.ShapeDtypeStruct((T, D_MODEL), MX),
                   jax.ShapeDtypeStruct((T, D_MODEL), F32), jax.ShapeDtypeStruct((T, D_MODEL), MX),
                   jax.ShapeDtypeStruct((nT, 8, 128), F32), jax.ShapeDtypeStruct((8, D_MODEL), F32)],
        scratch_shapes=[pltpu.VMEM((tm, D_MODEL), MX), pltpu.VMEM((tm, D_MODEL), F32)],
        compiler_params=_params(("arbitrary", "arbitrary")),
    )(h, tgt, g_ffn, g_final, wfi_g, wfi_g, wfo_g)


def _ffn_bwd(h, dh2, dh2b, gate, up, g_ffn, wfi_g, wfo_g):
    T = h.shape[0]
    tm = min(512, T)

    def body(h_ref, dh2_ref, dh2b_ref, gate_ref, up_ref, gf_ref, wg_ref, wu_ref, wo_ref,
             dgu_ref, dh_ref, dhb_ref, dgf_ref, acc_sc):
        i, j = pl.program_id(0), pl.program_id(1)

        @pl.when(j == 0)
        def _():
            acc_sc[...] = jnp.zeros_like(acc_sc)

        @pl.when((i == 0) & (j == 0))
        def _():
            dgf_ref[...] = jnp.zeros_like(dgf_ref)

        dhid = _dot_nt(dh2b_ref[...], wo_ref[...])
        gate, up = gate_ref[...], up_ref[...]
        sg = _sig(gate)
        dgate = (dhid * up * (sg * (1.0 + gate * (1.0 - sg)))).astype(MX)
        dup = (dhid * (gate * sg)).astype(MX)
        dgu_ref[0] = dgate
        dgu_ref[1] = dup
        acc_sc[...] += _dot_nt(dgate, wg_ref[...]) + _dot_nt(dup, wu_ref[...])

        @pl.when(j == 3)
        def _():
            hh = h_ref[...]
            r = lax.rsqrt(jnp.mean(hh * hh, axis=-1, keepdims=True) + RMS_EPS)
            hn = hh * r
            du2 = acc_sc[...]
            dgf_ref[...] += _fold8(du2 * hn)
            a = du2 * gf_ref[...]
            dh = dh2_ref[...] + r * a - hn * (r * jnp.mean(a * hn, axis=-1, keepdims=True))
            dh_ref[...] = dh
            dhb_ref[...] = dh.astype(MX)

    tile = pl.BlockSpec((tm, D_MODEL), lambda i, j: (i, 0))
    ftile = pl.BlockSpec((None, tm, FF_BLOCK), lambda i, j: (j, i, 0))
    return pl.pallas_call(
        body, name="ffn_bwd", grid=(T // tm, 4),
        in_specs=[tile, tile, tile, ftile, ftile,
                  pl.BlockSpec((1, D_MODEL), lambda i, j: (0, 0)),
                  pl.BlockSpec((None, D_MODEL, FF_BLOCK), lambda i, j: (j, 0, 0)),
                  pl.BlockSpec((None, D_MODEL, FF_BLOCK), lambda i, j: (j + 4, 0, 0)),
                  pl.BlockSpec((None, FF_BLOCK, D_MODEL), lambda i, j: (j, 0, 0))],
        out_specs=[pl.BlockSpec((2, None, tm, FF_BLOCK), lambda i, j: (0, j, i, 0)),
                   tile, tile, pl.BlockSpec((8, D_MODEL), lambda i, j: (0, 0))],
        out_shape=[jax.ShapeDtypeStruct((2, 4, T, FF_BLOCK), MX),
                   jax.ShapeDtypeStruct((T, D_MODEL), F32), jax.ShapeDtypeStruct((T, D_MODEL), MX),
                   jax.ShapeDtypeStruct((8, D_MODEL), F32)],
        scratch_shapes=[pltpu.VMEM((tm, D_MODEL), F32)],
        compiler_params=_params(("arbitrary", "arbitrary")),
    )(h, dh2, dh2b, gate, up, g_ffn, wfi_g, wfi_g, wfo_g)


def _merge_bwd(dhb, proj, za, zb, wa, wb, wo):
    T = dhb.shape[0]
    tm = min(512, T)

    def body(dh_ref, ga_ref, gb_ref, za_ref, zb_ref, wa_ref, wb_ref, wo_ref,
             dza_ref, dzb_ref, dgab_ref, dya_ref, dyb_ref):
        dm = _dot_nt(dh_ref[...], wo_ref[...])
        sa, sb = _sig(ga_ref[...]), _sig(gb_ref[...])
        dza = (dm * sa).astype(MX)
        dzb = (dm * sb).astype(MX)
        dza_ref[...] = dza
        dzb_ref[...] = dzb
        dgab_ref[0] = (dm * za_ref[...] * (sa * (1.0 - sa))).astype(MX)
        dgab_ref[1] = (dm * zb_ref[...] * (sb * (1.0 - sb))).astype(MX)
        dya_ref[...] = _dot_nt(dza, wa_ref[...])
        dyb_ref[...] = _dot_nt(dzb, wb_ref[...])

    tile = pl.BlockSpec((tm, D_MODEL), lambda i: (i, 0))
    wspec = pl.BlockSpec((D_MODEL, D_MODEL), lambda i: (0, 0))
    return pl.pallas_call(
        body, name="merge_bwd", grid=(T // tm,),
        in_specs=[tile,
                  pl.BlockSpec((None, tm, D_MODEL), lambda i: (6, i, 0)),
                  pl.BlockSpec((None, tm, D_MODEL), lambda i: (7, i, 0)),
                  tile, tile, wspec, wspec, wspec],
        out_specs=[tile, tile, pl.BlockSpec((2, tm, D_MODEL), lambda i: (0, i, 0)), tile, tile],
        out_shape=[jax.ShapeDtypeStruct((T, D_MODEL), MX), jax.ShapeDtypeStruct((T, D_MODEL), MX),
                   jax.ShapeDtypeStruct((2, T, D_MODEL), MX),
                   jax.ShapeDtypeStruct((T, D_MODEL), F32), jax.ShapeDtypeStruct((T, D_MODEL), F32)],
        compiler_params=_params(("parallel",)),
    )(dhb, proj, proj, za, zb, wa, wb, wo)


def _pool_bwd(proj, dyb, pool_w_full, pool_scale, B, L):
    def body(p_ref, dy_ref, w_ref, s_ref, dp_ref, dw_ref, ds_ref, ext_sc):
        for g, win in enumerate(POOL_WINDOWS):
            cols = slice(g * POOL_DIM, (g + 1) * POOL_DIM)
            p = p_ref[:, cols]
            cnt = _pool_count(L, win // 2)
            y = _pool_window(p, ext_sc, win // 2, False) / cnt - p
            z = _dot(y, w_ref[g])
            dyb_g = dy_ref[:, cols]
            ds_ref[:, cols] = jnp.sum(dyb_g * z, axis=0, keepdims=True)
            dz = dyb_g * s_ref[:, cols]
            dw_ref[g] = _dot_tn(y, dz)
            dy = _dot_nt(dz, w_ref[g])
            dp_ref[:, cols] = (_pool_window(dy / cnt, ext_sc, win // 2, True) - dy).astype(MX)

    seq = pl.BlockSpec((None, L, D_MODEL), lambda b: (b, 0, 0))
    return pl.pallas_call(
        body, name="pool_bwd", grid=(B,),
        in_specs=[pl.BlockSpec((None, None, L, D_MODEL), lambda b: (5, b, 0, 0)), seq,
                  pl.BlockSpec((4, POOL_DIM, POOL_DIM), lambda b: (0, 0, 0)),
                  pl.BlockSpec((1, D_MODEL), lambda b: (0, 0))],
        out_specs=[seq, pl.BlockSpec((None, 4, POOL_DIM, POOL_DIM), lambda b: (b, 0, 0, 0)),
                   pl.BlockSpec((None, 1, D_MODEL), lambda b: (b, 0, 0))],
        out_shape=[jax.ShapeDtypeStruct((B, L, D_MODEL), MX),
                   jax.ShapeDtypeStruct((B, 4, POOL_DIM, POOL_DIM), F32),
                   jax.ShapeDtypeStruct((B, 1, D_MODEL), F32)],
        scratch_shapes=[pltpu.VMEM((L + 2 * POOL_PAD, POOL_DIM), F32)],
        compiler_params=_params(("parallel",)),
    )(proj, dyb, pool_w_full, pool_scale)


def _hgrn_bwd(proj, o, dya, lbl, norm_g, B, L, grads, dests):
    nC = L // CHUNK
    n = len(grads)

    def body(*refs):
        q_ref, ff_ref, fb_ref, i_ref, og_ref, o_ref, dy_ref, lbl_ref, ng_ref = refs[:9]
        dp_ref, dlb_ref, dng_ref = refs[9 + n:12 + n]
        (do_sc, dq_sc, dv_sc, dP_sc, dKt_sc, dKe_sc, dbl_sc, dec_sc, ke_sc, ck_sc, upd_sc,
         dupd_sc) = refs[12 + 2 * n:24 + 2 * n]
        step_id = pl.program_id(0) * N_HEADS + pl.program_id(1)
        scatter = _Exchange(refs[9:9 + n], refs[12 + n:12 + 2 * n], refs[24 + 2 * n:], gather=False, dests=dests)

        @pl.when(step_id == 0)
        def _():
            scatter.start()

        lb_f, lb_b = _lower_bounds(lbl_ref)
        o_ = o_ref[...]
        r = lax.rsqrt(jnp.mean(o_ * o_, axis=-1, keepdims=True) + RMS_EPS)
        on = o_ * r
        og = og_ref[...]
        sog = _sig(og)
        dy = dy_ref[...]
        ng = ng_ref[...]
        dp_ref[4] = (dy * (on * ng) * (sog * (1.0 + og * (1.0 - sog)))).astype(MX)
        dn = dy * (og * sog)
        dng_ref[...] = jnp.sum(dn * on, axis=0, keepdims=True)
        don = dn * ng
        do_sc[...] = r * don - on * (r * jnp.mean(don * on, axis=-1, keepdims=True))

        dq_sc[...] = jnp.zeros_like(dq_sc)
        dv_sc[...] = jnp.zeros_like(dv_sc)

        def run_dir(f_ref, lb_row, rev, slot):
            def block(kb, carry):
                rows = pl.ds(pl.multiple_of(kb * HBLK, HBLK), HBLK)
                g = _block_gates(q_ref[rows, :], f_ref[rows, :], lb_row, rev)
                v = i_ref[rows, :]
                do = do_sc[rows, :]
                m, mt = _block_mask(rev), _block_mask(rev, True)
                at = jnp.where(mt, _dot_nt(g["Kt"], g["P"]), 0.0)
                da = jnp.where(m, _dot_nt(do, v), 0.0)
                dat = jnp.where(mt, _dot_nt(v, do), 0.0)
                dv_sc[rows, :] += _dot(at, do)
                dP_sc[rows, :] = _dot(da, g["Kt"])
                dKt_sc[rows, :] = _dot(dat, g["P"])
                ke_sc[rows, :] = g["Ke"].astype(MX)
                dec_sc[rows, :] = g["dec"]
                _chunk_outer_products(v, g["Ke"], upd_sc, kb)
                _chunk_outer_products(do, g["P"], dupd_sc, kb)
                return carry

            def fstep(c, st):
                rows = _chunk_rows(c)
                ck_sc[c] = st.astype(MX)
                dP_sc[rows, :] += _dot(do_sc[rows, :], st)
                dec = dec_sc[pl.ds(pl.multiple_of(c * CHUNK, CHUNK), 1), :]
                return st * dec + upd_sc[c]

            n_units = nC // UNIT_CHUNKS
            zero_state = jnp.zeros((HEAD_DIM, HEAD_DIM), F32)
            _pipelined(n_units, rev,
                       lambda u, c: _unit_blocks(u, block, c), 0,
                       lambda u, st: _unit_chunks(u, rev, fstep, st), zero_state)

            def bstep(c, dst):
                rows = _chunk_rows(c)
                dec = dec_sc[pl.ds(pl.multiple_of(c * CHUNK, CHUNK), 1), :]
                dKe_sc[rows, :] = _dot(i_ref[rows, :], dst)
                dv_sc[rows, :] += _dot_nt(ke_sc[rows, :], dst)
                dbl = dec * jnp.sum(dst * ck_sc[c].astype(F32), axis=0, keepdims=True)
                dbl_sc[rows, :] = jnp.broadcast_to(dbl, (CHUNK, HEAD_DIM))
                return dst * dec + dupd_sc[c]

            def finish(kb, acc):
                rows = pl.ds(pl.multiple_of(kb * HBLK, HBLK), HBLK)
                q_r = q_ref[rows, :]
                g = _block_gates(q_r, f_ref[rows, :], lb_row, rev)
                dP, dkt, dke = dP_sc[rows, :], dKt_sc[rows, :], dKe_sc[rows, :]
                e = dke * g["Ke"]
                dlf = (_seg_cumsum(dP * g["P"] - dkt * g["Kt"], not rev) + _seg_cumsum(e, rev) - e
                       + dbl_sc[rows, :])
                df = dlf / g["f"] - (dkt * g["enb"] + dke * g["eend"])
                dp_ref[slot, rows, :] = (df * (1.0 - lb_row) * (g["sg"] * (1.0 - g["sg"]))).astype(MX)
                dq_sc[rows, :] += (dP * g["eb"]) * (g["sq"] * (1.0 + q_r * (1.0 - g["sq"])))
                return acc + jnp.sum(df * (1.0 - g["sg"]), axis=0, keepdims=True)

            _, dlb = _pipelined(n_units, not rev,
                                lambda u, dst: _unit_chunks(u, not rev, bstep, dst), zero_state,
                                lambda u, acc: _unit_blocks(u, finish, acc), jnp.zeros((1, HEAD_DIM), F32))
            dlb_ref[slot - 1:slot, :] = dlb

        run_dir(ff_ref, lb_f, False, 1)
        run_dir(fb_ref, lb_b, True, 2)
        dp_ref[0] = dq_sc[...].astype(MX)
        dp_ref[3] = dv_sc[...].astype(MX)

        @pl.when(step_id == B * N_HEADS - 1)
        def _():
            scatter.wait()

    def blk(s):
        return pl.BlockSpec((None, None, L, HEAD_DIM), lambda b, h, s=s: (s, b, 0, h))

    seq = pl.BlockSpec((None, L, HEAD_DIM), lambda b, h: (b, 0, h))
    outs = pl.pallas_call(
        body, name="hgrn_bwd", grid=(B, N_HEADS),
        in_specs=[blk(0), blk(1), blk(2), blk(3), blk(4), seq, seq,
                  pl.BlockSpec((4, HEAD_DIM), lambda b, h: (0, h)),
                  pl.BlockSpec((1, HEAD_DIM), lambda b, h: (0, h))] + [ANY] * n,
        out_specs=[pl.BlockSpec((5, None, L, HEAD_DIM), lambda b, h: (0, b, 0, h)),
                   pl.BlockSpec((None, 2, HEAD_DIM), lambda b, h: (b, 0, h)),
                   pl.BlockSpec((None, 1, HEAD_DIM), lambda b, h: (b, 0, h))] + [ANY] * n,
        out_shape=[jax.ShapeDtypeStruct((5, B, L, D_MODEL), MX),
                   jax.ShapeDtypeStruct((B, 2, D_MODEL), F32),
                   jax.ShapeDtypeStruct((B, 1, D_MODEL), F32)] + _Exchange.out_shapes(grads, False),
        scratch_shapes=[pltpu.VMEM((L, HEAD_DIM), F32)] * 8
                       + [pltpu.VMEM((L, HEAD_DIM), MX), pltpu.VMEM((nC, HEAD_DIM, HEAD_DIM), MX),
                          pltpu.VMEM((nC, HEAD_DIM, HEAD_DIM), F32), pltpu.VMEM((nC, HEAD_DIM, HEAD_DIM), F32)]
                       + _Exchange.scratch(n),
        compiler_params=_params(("arbitrary", "arbitrary")),
    )(proj, proj, proj, proj, proj, o, dya, lbl, norm_g, *grads)
    return outs[0], outs[1], outs[2], outs[3:]


def _dproj_select(s, a5_ref, p_ref, g2_ref):
    return jnp.where(s < 5, a5_ref[...], jnp.where(s == 5, p_ref[...], g2_ref[...]))


def _dproj_specs(tm, tile_axis):
    def ix(args):
        return args[tile_axis], args[1 - tile_axis]
    a5 = pl.BlockSpec((None, tm, D_MODEL), lambda *a: (jnp.minimum(ix(a)[1], 4), ix(a)[0], 0))
    p = pl.BlockSpec((tm, D_MODEL), lambda *a: (ix(a)[0], 0))
    g2 = pl.BlockSpec((None, tm, D_MODEL), lambda *a: (jnp.clip(ix(a)[1] - 6, 0, 1), ix(a)[0], 0))
    return [a5, p, g2]


def _inproj_bwd(x2, dh, dproj5, dp, dgab, g_mix, w_in_g, d_win, dests):
    T = x2.shape[0]
    tm = min(512, T)
    nT = T // tm

    def body(a5_ref, p_ref, g2_ref, w_ref, x_ref, dh_ref, g_ref, dwin_ref, dx_ref, dg_ref, recv_ref, acc_sc, *sems):
        i, s = pl.program_id(0), pl.program_id(1)
        scatter = _Exchange([dwin_ref], [recv_ref], sems, gather=False, dests=[dests])

        @pl.when((i == 0) & (s == 0))
        def _():
            scatter.start()

        @pl.when(s == 0)
        def _():
            acc_sc[...] = jnp.zeros_like(acc_sc)

        @pl.when((i == 0) & (s == 0))
        def _():
            dg_ref[...] = jnp.zeros_like(dg_ref)

        acc_sc[...] += _dot_nt(_dproj_select(s, a5_ref, p_ref, g2_ref), w_ref[...])

        @pl.when(s == N_DEV - 1)
        def _():
            x = x_ref[...]
            r = lax.rsqrt(jnp.mean(x * x, axis=-1, keepdims=True) + RMS_EPS)
            xn = x * r
            du = acc_sc[...]
            dg_ref[...] += _fold8(du * xn)
            a = du * g_ref[...]
            dx_ref[...] = dh_ref[...] + r * a - xn * (r * jnp.mean(a * xn, axis=-1, keepdims=True))

        @pl.when((i == nT - 1) & (s == N_DEV - 1))
        def _():
            scatter.wait()

    tile = pl.BlockSpec((tm, D_MODEL), lambda i, s: (i, 0))
    return pl.pallas_call(
        body, name="inproj_bwd", grid=(nT, N_DEV),
        in_specs=_dproj_specs(tm, 0) + [pl.BlockSpec((None, D_MODEL, D_MODEL), lambda i, s: (s, 0, 0)),
                                        tile, tile, pl.BlockSpec((1, D_MODEL), lambda i, s: (0, 0)), ANY],
        out_specs=[tile, pl.BlockSpec((8, D_MODEL), lambda i, s: (0, 0)), ANY],
        out_shape=[jax.ShapeDtypeStruct((T, D_MODEL), F32), jax.ShapeDtypeStruct((8, D_MODEL), F32)]
                  + _Exchange.out_shapes([d_win], False),
        scratch_shapes=[pltpu.VMEM((tm, D_MODEL), F32)] + _Exchange.scratch(1),
        compiler_params=_params(("arbitrary", "arbitrary")),
    )(dproj5, dp, dgab, w_in_g, x2, dh, g_mix, d_win)


def _wgrad(a, g, name):
    Ba, T, K = a.shape
    Bg, _, Nn = g.shape
    nb = max(Ba, Bg)
    tm = min(BIG_TOKEN_TILE, T)
    nt = T // tm

    def body(a_ref, g_ref, out_ref, acc_sc):
        t = pl.program_id(1)

        @pl.when(t == 0)
        def _():
            acc_sc[...] = jnp.zeros_like(acc_sc)

        acc_sc[...] += _dot_tn(a_ref[...], g_ref[...])

        @pl.when(t == nt - 1)
        def _():
            out_ref[...] = acc_sc[...].astype(MX)

    return pl.pallas_call(
        body, name=name, grid=(nb, nt),
        in_specs=[pl.BlockSpec((None, tm, K), lambda s, t: (s if Ba > 1 else 0, t, 0)),
                  pl.BlockSpec((None, tm, Nn), lambda s, t: (s if Bg > 1 else 0, t, 0))],
        out_specs=pl.BlockSpec((None, K, Nn), lambda s, t: (s, 0, 0)),
        out_shape=jax.ShapeDtypeStruct((nb, K, Nn), MX),
        scratch_shapes=[pltpu.VMEM((K, Nn), F32)],
        compiler_params=_params(("parallel", "arbitrary")),
    )(a, g)


def _mesh_pos():
    return lax.axis_index("x"), lax.axis_index("y"), lax.axis_index("c")


def _all_gather(shards):
    n = len(shards)

    def body(*refs):
        xs, outs = refs[:n], refs[n:2 * n]
        send_sems, recv_sems, local_sems = refs[2 * n:]
        x, y, c = _mesh_pos()
        me, sibling = (x, y, c), (x, y, 1 - c)
        chips = [(1 - x, y), (x, 1 - y), (1 - x, 1 - y)]

        def copy(a, k, block, to, src=None):
            slot = outs[a].at[4 * block[0] + 2 * block[1] + block[2]]
            return pltpu.make_async_remote_copy(
                src_ref=slot if src is None else src, dst_ref=slot,
                send_sem=send_sems.at[7 * a + k], recv_sem=recv_sems.at[7 * a + k],
                device_id=to, device_id_type=pl.DeviceIdType.MESH)

        mine = [pltpu.make_async_copy(xs[a], outs[a].at[4 * x + 2 * y + c], local_sems.at[a]) for a in range(n)]
        for cp in mine:
            cp.start()
        first = []
        for a in range(n):
            first.append(copy(a, 0, me, sibling, src=xs[a]))
            first += [copy(a, 1 + j, me, (*chip, c), src=xs[a]) for j, chip in enumerate(chips)]
        for cp in first:
            cp.start()
        passed = []
        for j, chip in enumerate(chips):
            for a in range(n):
                copy(a, 1 + j, (*chip, c), me).wait_recv()
                fwd = copy(a, 4 + j, (*chip, c), sibling)
                fwd.start()
                passed.append(fwd)
        for a in range(n):
            copy(a, 0, sibling, me).wait_recv()
            for j, chip in enumerate(chips):
                copy(a, 4 + j, (*chip, 1 - c), me).wait_recv()
        for cp in first + passed:
            cp.wait_send()
        for cp in mine:
            cp.wait()

    return pl.pallas_call(
        body, name="all_gather_weights",
        in_specs=[ANY] * n, out_specs=[ANY] * n,
        out_shape=[jax.ShapeDtypeStruct((N_DEV,) + s.shape, s.dtype) for s in shards],
        scratch_shapes=[pltpu.SemaphoreType.DMA((7 * n,)), pltpu.SemaphoreType.DMA((7 * n,)),
                        pltpu.SemaphoreType.DMA((n,))],
    )(*shards)


def _device_of(p):
    return (p // 4, (p // 2) % 2, p % 2)


class _Exchange:
    def __init__(self, srcs, outs, sems, gather, dests=None):
        send_sems, recv_sems, local_sems = sems
        x, y, c = _mesh_pos()
        me = 4 * x + 2 * y + c
        self.sends, self.arrivals, self.mine = [], [], []
        for a, (src, out) in enumerate(zip(srcs, outs)):
            lo, hi = dests[a] if dests else (0, N_DEV)

            def piece(p, src=src, lo=lo, hi=hi):
                return src if gather else src.at[jnp.clip(p - lo, 0, hi - lo - 1)]

            def served(p, lo=lo, hi=hi):
                return None if (lo, hi) == (0, N_DEV) else (p >= lo) & (p < hi)

            self.mine.append((pltpu.make_async_copy(piece(me), out.at[me], local_sems.at[a]), served(me)))
            for j in range(1, N_DEV):
                to, frm = (me + j) % N_DEV, (me + N_DEV - j) % N_DEV
                pair = dict(send_sem=send_sems.at[7 * a + j - 1], recv_sem=recv_sems.at[7 * a + j - 1],
                            device_id_type=pl.DeviceIdType.MESH)
                self.sends.append((pltpu.make_async_remote_copy(
                    src_ref=piece(to), dst_ref=out.at[me], device_id=_device_of(to), **pair), served(to)))
                self.arrivals.append((pltpu.make_async_remote_copy(
                    src_ref=piece(frm), dst_ref=out.at[frm], device_id=_device_of(frm), **pair), served(me)))

    @staticmethod
    def _each(copies, act):
        for cp, takes_part in copies:
            if takes_part is None:
                act(cp)
            else:
                pl.when(takes_part)(functools.partial(act, cp))

    def start(self):
        self._each(self.mine + self.sends, lambda cp: cp.start())

    def wait(self):
        self._each(self.arrivals, lambda cp: cp.wait_recv())
        self._each(self.sends, lambda cp: cp.wait_send())
        self._each(self.mine, lambda cp: cp.wait())

    @staticmethod
    def scratch(n):
        return [pltpu.SemaphoreType.DMA((7 * n,)), pltpu.SemaphoreType.DMA((7 * n,)), pltpu.SemaphoreType.DMA((n,))]

    @staticmethod
    def out_shapes(arrays, gather):
        return [jax.ShapeDtypeStruct((N_DEV,) + (a.shape if gather else a.shape[1:]), a.dtype) for a in arrays]


def _all_reduce_small(v):
    R, C = v.shape

    def body(v_ref, out_ref, slots, send_sems, recv_sems):
        x, y, c = _mesh_pos()
        me = 4 * x + 2 * y + c

        def copy(j, to):
            return pltpu.make_async_remote_copy(
                src_ref=v_ref, dst_ref=slots.at[me],
                send_sem=send_sems.at[j - 1], recv_sem=recv_sems.at[j - 1],
                device_id=_device_of(to), device_id_type=pl.DeviceIdType.MESH)

        sends = [copy(j, (me + j) % N_DEV) for j in range(1, N_DEV)]
        for cp in sends:
            cp.start()
        slots[me] = v_ref[...]
        for j in range(1, N_DEV):
            frm = (me + N_DEV - j) % N_DEV
            pltpu.make_async_remote_copy(
                src_ref=v_ref, dst_ref=slots.at[frm], send_sem=send_sems.at[j - 1], recv_sem=recv_sems.at[j - 1],
                device_id=_device_of(frm), device_id_type=pl.DeviceIdType.MESH).wait_recv()
        for cp in sends:
            cp.wait_send()
        acc = slots[0]
        for p in range(1, N_DEV):
            acc = acc + slots[p]
        out_ref[...] = acc

    return pl.pallas_call(
        body, name="all_reduce_small",
        in_specs=[pl.BlockSpec(memory_space=pltpu.VMEM)], out_specs=pl.BlockSpec(memory_space=pltpu.VMEM),
        out_shape=jax.ShapeDtypeStruct((R, C), F32),
        scratch_shapes=[pltpu.VMEM((N_DEV, R, C), F32), pltpu.SemaphoreType.DMA((7,)), pltpu.SemaphoreType.DMA((7,))],
    )(v)


def _adamw_math(w, g, m, v):
    m = ADAM_B1 * m + (1.0 - ADAM_B1) * g
    v = ADAM_B2 * v + (1.0 - ADAM_B2) * (g * g)
    m_hat = m / (1.0 - ADAM_B1 ** ADAM_STEP)
    v_hat = v / (1.0 - ADAM_B2 ** ADAM_STEP)
    delta = -ADAM_LR * (m_hat / (jnp.sqrt(v_hat) + ADAM_EPS) + ADAM_WD * w)
    return delta, m, v


def _adamw_reduce(recvs, dests, w, m, v, name):
    R, C = w.shape
    tr = R if R <= 256 else 256
    while R % tr:
        tr //= 2
    n = len(recvs)

    def body(*refs):
        w_ref, m_ref, v_ref, g_ref, d_ref, nm_ref, nv_ref = refs[n:]

        def update(r_ref):
            g = r_ref[0].astype(F32)
            for p in range(1, N_DEV):
                g = g + r_ref[p].astype(F32)
            d, nm, nv = _adamw_math(w_ref[...], g, m_ref[...], v_ref[...])
            g_ref[...] = g
            d_ref[...] = d
            nm_ref[...] = nm
            nv_ref[...] = nv

        if n == 1:
            update(refs[0])
        else:
            x, y, c = _mesh_pos()
            me = 4 * x + 2 * y + c
            for r_ref, (lo, hi) in zip(refs[:n], dests):
                pl.when((me >= lo) & (me < hi))(functools.partial(update, r_ref))

    tile = pl.BlockSpec((tr, C), lambda i: (i, 0))
    shp = jax.ShapeDtypeStruct((R, C), F32)
    return pl.pallas_call(
        body, name=name, grid=(R // tr,),
        in_specs=[pl.BlockSpec((N_DEV, tr, C), lambda i: (0, i, 0))] * n + [tile, tile, tile],
        out_specs=[tile] * 4, out_shape=[shp] * 4,
        compiler_params=_params(("parallel",)),
    )(*recvs, w, m, v)


def _adamw_small(g, w, m, v):
    def body(g_ref, w_ref, m_ref, v_ref, go_ref, d_ref, nm_ref, nv_ref):
        go_ref[...] = g_ref[...]
        for d in range(2):
            p0 = _sig(w_ref[8 + 2 * d:9 + 2 * d, :] - w_ref[9 + 2 * d:10 + 2 * d, :])
            dl0 = g_ref[12 + d:13 + d, :] * p0 * (1.0 - p0)
            go_ref[8 + 2 * d:9 + 2 * d, :] = dl0
            go_ref[9 + 2 * d:10 + 2 * d, :] = -dl0
            go_ref[12 + d:13 + d, :] = jnp.zeros((1, D_MODEL), F32)
        d, nm, nv = _adamw_math(w_ref[...], go_ref[...], m_ref[...], v_ref[...])
        d_ref[...] = d
        nm_ref[...] = nm
        nv_ref[...] = nv

    shp = jax.ShapeDtypeStruct(g.shape, F32)
    vm = pl.BlockSpec(memory_space=pltpu.VMEM)
    return pl.pallas_call(body, name="adamw_small", in_specs=[vm] * 4, out_specs=[vm] * 4, out_shape=[shp] * 4)(g, w, m, v)


def _local_step(x, tgt, g_mix, lb, norm_g, pool_scale, g_ffn, g_final, w_in_g, late_shards):
    B, L, _ = x.shape
    T = B * L
    rows = D_MODEL // N_DEV
    x2, tgt2 = x.reshape(T, D_MODEL), tgt.reshape(T, D_MODEL)

    proj, u = _rms_inproj(x2, g_mix, w_in_g)
    proj4 = proj.reshape(N_DEV, B, L, D_MODEL)
    o, ya, (wa, wb, wo, wfi_g, wfo_g, pw_g) = _hgrn_fwd(proj4, lb, norm_g, B, L, late_shards)
    wa, wb, wo = (w_.reshape(D_MODEL, D_MODEL) for w_ in (wa, wb, wo))
    wfo_g = wfo_g.reshape(4, FF_BLOCK, D_MODEL)
    pool_w_full = pw_g.reshape(N_DEV, 4, 32, POOL_DIM).transpose(1, 0, 2, 3).reshape(4, POOL_DIM, POOL_DIM)
    yb = _pool_fwd(proj4, pool_w_full, pool_scale, B, L)
    ya2, yb2 = ya.reshape(T, D_MODEL), yb.reshape(T, D_MODEL)
    za, zb, mg, h = _merge_out(x2, proj, ya2, yb2, wa, wb, wo)
    gate, up, hid, u2, dh2, dh2b, loss_p, dg_final = _ffn_fwd_loss(h, tgt2, g_ffn, g_final, wfi_g, wfo_g)
    loss = jnp.sum(loss_p[:, 0, 0])

    dgu, dh, dhb, dg_ffn = _ffn_bwd(h, dh2, dh2b, gate, up, g_ffn, wfi_g, wfo_g)
    d_wfo = _wgrad(hid, dh2b[None], "wgrad_ffn_out")
    d_wfi = _wgrad(u2[None], dgu.reshape(N_DEV, T, FF_BLOCK), "wgrad_ffn_in")
    dza, dzb, dgab, dya, dyb = _merge_bwd(dhb, proj, za, zb, wa, wb, wo)
    d_wo = _wgrad(mg[None], dhb[None], "wgrad_out")
    d_wa = _wgrad(ya2[None], dza[None], "wgrad_branch_a")
    d_wb = _wgrad(yb2[None], dzb[None], "wgrad_branch_b")
    dp, dpw_p, dps_p = _pool_bwd(proj4, dyb.reshape(B, L, D_MODEL), pool_w_full, pool_scale, B, L)
    d_pw = dpw_p.sum(0).reshape(4, N_DEV, 32, POOL_DIM).transpose(1, 0, 2, 3).reshape(N_DEV, 128, POOL_DIM)
    dp2 = dp.reshape(T, D_MODEL)
    w_in_dests = [(5, 6), (6, 8), (0, 5)]
    d_win_pool = _wgrad(u[None], dp2[None], "wgrad_in_pool")
    d_win_gates = _wgrad(u[None], dgab, "wgrad_in_gates")
    slices = [d_wa.reshape(N_DEV, rows, D_MODEL), d_wb.reshape(N_DEV, rows, D_MODEL),
              d_wo.reshape(N_DEV, rows, D_MODEL), d_wfi, d_wfo.reshape(N_DEV, FF_BLOCK // 2, D_MODEL),
              d_pw.astype(MX), d_win_pool, d_win_gates]
    dests = [(0, N_DEV)] * 6 + w_in_dests[:2]
    dproj5, dlb_p, dng_p, recv = _hgrn_bwd(proj4, o, dya.reshape(B, L, D_MODEL), lb, norm_g, B, L, slices, dests)
    dproj5 = dproj5.reshape(5, T, D_MODEL)
    d_win_rec = _wgrad(u[None], dproj5, "wgrad_in_recurrence")
    grad_x, dg_mix, recv_win_rec = _inproj_bwd(x2, dh, dproj5, dp2, dgab, g_mix, w_in_g, d_win_rec, w_in_dests[2])

    small = dict(g_mix=dg_mix.sum(0), hgrn_norm_g=dng_p.sum((0, 1)), pool_scale=dps_p.sum((0, 1)),
                 g_ffn=dg_ffn.sum(0), g_final=dg_final.sum(0), lb=dlb_p.sum(0))
    recv_w_in = [recv[6], recv[7], recv_win_rec]
    return loss, grad_x.reshape(B, L, D_MODEL), (recv_w_in, w_in_dests), list(recv[:6]), small


def kernel(x, g_mix, w_in, lb_logits, hgrn_norm_g, pool_w, pool_scale, w_branch_a, w_branch_b, w_out, g_ffn, w_ffn_in, w_ffn_out, g_final, loss_target, m_g_mix, m_w_in, m_lb_logits, m_hgrn_norm_g, m_pool_w, m_pool_scale, m_w_branch_a, m_w_branch_b, m_w_out, m_g_ffn, m_w_ffn_in, m_w_ffn_out, m_g_final, v_g_mix, v_w_in, v_lb_logits, v_hgrn_norm_g, v_pool_w, v_pool_scale, v_w_branch_a, v_w_branch_b, v_w_out, v_g_ffn, v_w_ffn_in, v_w_ffn_out, v_g_final):
    me = 4 * lax.axis_index("x") + 2 * lax.axis_index("y") + lax.axis_index("c")

    w_in_g, lbl_g = _all_gather([w_in[0].astype(MX), jnp.pad(lb_logits.reshape(4, HEAD_DIM), ((0, 4), (0, 0)))])
    lbl = lbl_g[:, :4].transpose(1, 0, 2).reshape(4, D_MODEL)
    late_shards = [w_branch_a[0].astype(MX), w_branch_b[0].astype(MX), w_out[0].astype(MX),
                   w_ffn_in[0].astype(MX), w_ffn_out[0].astype(MX), pool_w[0].reshape(4 * 32, POOL_DIM).astype(MX)]

    loss, grad_x, (recv_w_in, w_in_dests), recv_late, small = _local_step(
        x, loss_target, g_mix, lbl, hgrn_norm_g, pool_scale, g_ffn, g_final[None], w_in_g, late_shards)
    loss = lax.psum(loss, ("x", "y", "c"))

    packed = jnp.zeros((16, D_MODEL), F32)
    names = ["g_mix", "hgrn_norm_g", "pool_scale", "g_ffn", "g_final"]
    for i, nme in enumerate(names):
        packed = packed.at[i].set(small[nme])
    packed = packed.at[5:7].set(small["lb"])
    red = _all_reduce_small(packed)
    dlb_mine = lax.dynamic_slice_in_dim(red[5:7], me * HEAD_DIM, HEAD_DIM, axis=1)

    sw = jnp.zeros((16, D_MODEL), F32)
    sm = jnp.zeros((16, D_MODEL), F32)
    sv = jnp.ones((16, D_MODEL), F32)
    smalls = [(g_mix, m_g_mix, v_g_mix), (hgrn_norm_g, m_hgrn_norm_g, v_hgrn_norm_g),
              (pool_scale, m_pool_scale, v_pool_scale), (g_ffn, m_g_ffn, v_g_ffn),
              (g_final[None], m_g_final[None], v_g_final[None])]
    for i, (w_, m_, v_) in enumerate(smalls):
        sw, sm, sv = sw.at[i].set(w_[0]), sm.at[i].set(m_[0]), sv.at[i].set(v_[0])
    sg = red.at[5:].set(0.0)
    sg = sg.at[12:14, :HEAD_DIM].set(dlb_mine)
    sw = sw.at[8:12, :HEAD_DIM].set(lb_logits.reshape(4, HEAD_DIM))
    sm = sm.at[8:12, :HEAD_DIM].set(m_lb_logits.reshape(4, HEAD_DIM))
    sv = sv.at[8:12, :HEAD_DIM].set(v_lb_logits.reshape(4, HEAD_DIM))
    sg, sd, snm, snv = _adamw_small(sg, sw, sm, sv)

    def small_out(arr, i, like):
        return arr[i].reshape(like.shape)

    def lb_out(arr):
        return arr[8:12, :HEAD_DIM].reshape(2, 2, HEAD_DIM)

    order = ["w_in", "w_branch_a", "w_branch_b", "w_out", "w_ffn_in", "w_ffn_out", "pool_w"]
    params = dict(w_in=(w_in, m_w_in, v_w_in), w_branch_a=(w_branch_a, m_w_branch_a, v_w_branch_a),
                  w_branch_b=(w_branch_b, m_w_branch_b, v_w_branch_b), w_out=(w_out, m_w_out, v_w_out),
                  w_ffn_in=(w_ffn_in, m_w_ffn_in, v_w_ffn_in), w_ffn_out=(w_ffn_out, m_w_ffn_out, v_w_ffn_out),
                  pool_w=(pool_w, m_pool_w, v_pool_w))
    res = {}
    for nme, r, dests in zip(order, [recv_w_in] + [[r] for r in recv_late], [w_in_dests] + [None] * 6):
        w_, m_, v_ = params[nme]
        shape2 = r[0].shape[1:]
        outs = _adamw_reduce(r, dests, w_.reshape(shape2), m_.reshape(shape2), v_.reshape(shape2), "adamw_" + nme)
        res[nme] = [o_.reshape(w_.shape) for o_ in outs]

    def pick(k):
        small_src = [sg, sd, snm, snv][k]
        return [small_out(small_src, 0, g_mix), res["w_in"][k], lb_out(small_src), small_out(small_src, 1, hgrn_norm_g),
                res["pool_w"][k], small_out(small_src, 2, pool_scale), res["w_branch_a"][k], res["w_branch_b"][k],
                res["w_out"][k], small_out(small_src, 3, g_ffn), res["w_ffn_in"][k], res["w_ffn_out"][k],
                small_out(small_src, 4, g_final)]

    return (loss, grad_x, *pick(0), *pick(1), *pick(2), *pick(3))
```

```python
import functools

import jax
import jax.numpy as jnp
from jax import lax
from jax.experimental import pallas as pl
from jax.experimental.pallas import tpu as pltpu

F32 = jnp.float32
MX = jnp.bfloat16

D_MODEL = 1024
N_HEADS = 8
HEAD_DIM = 128
CHUNK = 16
POOL_WINDOWS = (2, 4, 8, 16)
POOL_DIM = 256
FF_BLOCK = 704
N_DEV = 8
RMS_EPS = 1e-6
ADAM_LR, ADAM_B1, ADAM_B2, ADAM_EPS, ADAM_WD, ADAM_STEP = 0.001, 0.9, 0.999, 1e-08, 0.01, 10
VMEM_LIMIT = 56 * 1024 * 1024
BIG_TOKEN_TILE = 1024
ANY = pl.BlockSpec(memory_space=pl.ANY)


def _params(sem=None):
    return pltpu.CompilerParams(dimension_semantics=sem, vmem_limit_bytes=VMEM_LIMIT)


def _dot(a, b):
    return lax.dot_general(a.astype(MX), b.astype(MX), (((1,), (0,)), ((), ())), preferred_element_type=F32)


def _dot_nt(a, b):
    return lax.dot_general(a.astype(MX), b.astype(MX), (((1,), (1,)), ((), ())), preferred_element_type=F32)


def _dot_tn(a, b):
    return lax.dot_general(a.astype(MX), b.astype(MX), (((0,), (0,)), ((), ())), preferred_element_type=F32)


def _sig(x):
    return 1.0 / (1.0 + jnp.exp(-x))


def _fold8(v):
    return v.reshape(v.shape[0] // 8, 8, v.shape[1]).sum(axis=0)


def _shift_rows(x, s):
    n = x.shape[0]
    row = lax.broadcasted_iota(jnp.int32, x.shape, 0)
    if s > 0:
        return jnp.where(row >= s, pltpu.roll(x, s, 0), 0.0)
    return jnp.where(row < n + s, pltpu.roll(x, n + s, 0), 0.0)


def _cumsum_rows(x, rev):
    s = 1
    while s < x.shape[0]:
        x = x + _shift_rows(x, -s if rev else s)
        s *= 2
    return x


def _gather_order():
    x, y, c = _mesh_pos()
    chips = [(1 - x, y), (x, 1 - y), (1 - x, 1 - y)]
    return [(x, y, c), (x, y, 1 - c)] + [(*chip, c) for chip in chips] + [(*chip, 1 - c) for chip in chips]


def _rms_inproj(x2, g_mix, w_shard, lb_shard):
    T = x2.shape[0]
    tm = min(BIG_TOKEN_TILE, T)
    nT = T // tm
    block_of_step = jnp.stack([4 * px + 2 * py + pc for px, py, pc in _gather_order()]).astype(jnp.int32)

    def body(order_ref, x_ref, g_ref, w_ref, lb_ref, proj_ref, u_ref, wg_ref, lbg_ref,
             u_sc, w_sc, load_sem, send_sems, recv_sems, local_sem, *lb_sems):
        k, i = pl.program_id(0), pl.program_id(1)
        order = _gather_order()
        me, sibling = order[0], order[1]

        def slot(dev):
            return wg_ref.at[4 * dev[0] + 2 * dev[1] + dev[2]]

        def copy(n, block, to, src=None):
            return pltpu.make_async_remote_copy(
                src_ref=slot(block) if src is None else src, dst_ref=slot(block),
                send_sem=send_sems.at[n], recv_sem=recv_sems.at[n], device_id=to, device_id_type=pl.DeviceIdType.MESH)

        mine = pltpu.make_async_copy(w_ref, slot(me), local_sem)
        first = [copy(0, me, sibling, src=w_ref)] + [copy(1 + j, me, order[2 + j], src=w_ref) for j in range(3)]
        passed = [copy(4 + j, order[2 + j], sibling) for j in range(3)]
        lb_gather = _Exchange([lb_ref], [lbg_ref], lb_sems, gather=True)

        @pl.when((k == 0) & (i == 0))
        def _():
            for cp in [mine] + first:
                cp.start()
            lb_gather.start()

        for s, block in enumerate(order):
            @pl.when((k == s) & (i == 0))
            def _(s=s, block=block):
                if s == 0:
                    mine.wait()
                else:
                    copy(s - 1, block, me).wait_recv()
                if 2 <= s <= 4:
                    passed[s - 2].start()
                load = pltpu.make_async_copy(slot(block), w_sc, load_sem)
                load.start()
                load.wait()

        rows = pl.ds(pl.multiple_of(i * tm, tm), tm)

        @pl.when(k == 0)
        def _():
            x = x_ref[...]
            r = lax.rsqrt(jnp.mean(x * x, axis=-1, keepdims=True) + RMS_EPS)
            u = (x * r * g_ref[...]).astype(MX)
            u_sc[rows, :] = u
            u_ref[...] = u

        proj_ref[...] = jnp.dot(u_sc[rows, :], w_sc[...], preferred_element_type=F32)

        @pl.when((k == N_DEV - 1) & (i == nT - 1))
        def _():
            for cp in first + passed:
                cp.wait_send()
            lb_gather.wait()

    def tile_once(k, i, order_ref):
        return (jnp.where(k == 0, i, nT - 1), 0)

    grid_spec = pltpu.PrefetchScalarGridSpec(
        num_scalar_prefetch=1, grid=(N_DEV, nT),
        in_specs=[pl.BlockSpec((tm, D_MODEL), tile_once),
                  pl.BlockSpec((1, D_MODEL), lambda k, i, order_ref: (0, 0)), ANY, ANY],
        out_specs=[pl.BlockSpec((None, tm, D_MODEL), lambda k, i, order_ref: (order_ref[k], i, 0)),
                   pl.BlockSpec((tm, D_MODEL), tile_once), ANY, ANY],
        scratch_shapes=[pltpu.VMEM((T, D_MODEL), MX), pltpu.VMEM((D_MODEL, D_MODEL), MX),
                        pltpu.SemaphoreType.DMA(()), pltpu.SemaphoreType.DMA((7,)), pltpu.SemaphoreType.DMA((7,)),
                        pltpu.SemaphoreType.DMA(())] + _Exchange.scratch(1))
    return pl.pallas_call(
        body, name="rms_inproj", grid_spec=grid_spec,
        out_shape=[jax.ShapeDtypeStruct((N_DEV, T, D_MODEL), F32), jax.ShapeDtypeStruct((T, D_MODEL), MX),
                   jax.ShapeDtypeStruct((N_DEV, D_MODEL, D_MODEL), MX),
                   jax.ShapeDtypeStruct((N_DEV,) + lb_shard.shape, lb_shard.dtype)],
        compiler_params=_params(("arbitrary", "arbitrary")),
    )(block_of_step, x2, g_mix, w_shard, lb_shard)


HBLK = 128


def _seg_cumsum(x, rev):
    n = x.shape[0]
    pos = lax.broadcasted_iota(jnp.int32, x.shape, 0) & (CHUNK - 1)
    s = 1
    while s < CHUNK:
        if rev:
            x = x + jnp.where(pos < CHUNK - s, pltpu.roll(x, n - s, 0), 0.0)
        else:
            x = x + jnp.where(pos >= s, pltpu.roll(x, s, 0), 0.0)
        s *= 2
    return x


def _block_gates(q_r, f_r, lb_row, rev):
    sq = _sig(q_r)
    q = q_r * sq
    sg = _sig(f_r)
    f = lb_row + (1.0 - lb_row) * sg
    k = 1.0 - f
    lf = jnp.log(f)
    pre = _seg_cumsum(lf, False)
    suf = _seg_cumsum(lf, True)
    tot = pre + suf - lf
    b = suf if rev else pre
    eb = jnp.exp(b)
    enb = jnp.exp(-b)
    eend = jnp.exp(tot - b)
    return dict(sq=sq, sg=sg, f=f, eb=eb, enb=enb, eend=eend, dec=jnp.exp(tot),
                P=q * eb, Kt=k * enb, Ke=k * eend)


def _block_mask(rev, transposed=False):
    ri = lax.broadcasted_iota(jnp.int32, (HBLK, HBLK), 0)
    ci = lax.broadcasted_iota(jnp.int32, (HBLK, HBLK), 1)
    same = (ri // CHUNK) == (ci // CHUNK)
    return same & ((ci >= ri) if rev != transposed else (ci <= ri))


def _chunk_rows(c):
    return pl.ds(pl.multiple_of(c * CHUNK, CHUNK), CHUNK)


def _chunk_outer_products(a, b, out_sc, kb):
    a, b = a.astype(MX), b.astype(MX)
    for u in range(HBLK // CHUNK):
        r = slice(u * CHUNK, (u + 1) * CHUNK)
        out_sc[kb * (HBLK // CHUNK) + u] = _dot_tn(a[r, :], b[r, :])


UNIT_BLOCKS = 2
UNIT_CHUNKS = UNIT_BLOCKS * (HBLK // CHUNK)


def _unit_blocks(u, fn, carry):
    for b in range(UNIT_BLOCKS):
        carry = fn(u * UNIT_BLOCKS + b, carry)
    return carry


def _unit_chunks(u, descending, fn, carry):
    for j in range(UNIT_CHUNKS):
        carry = fn(u * UNIT_CHUNKS + (UNIT_CHUNKS - 1 - j if descending else j), carry)
    return carry


def _pipelined(n_units, descending, first, c1, second, c2):
    def unit(t):
        return n_units - 1 - t if descending else t

    c1 = first(unit(0), c1)

    def both(t, cs):
        return first(unit(t + 1), cs[0]), second(unit(t), cs[1])

    c1, c2 = lax.fori_loop(0, n_units - 1, both, (c1, c2))
    return c1, second(unit(n_units - 1), c2)


def _lower_bounds(lbl_ref):
    return _sig(lbl_ref[0:1, :] - lbl_ref[1:2, :]), _sig(lbl_ref[2:3, :] - lbl_ref[3:4, :])


def _hgrn_fwd(proj, lbl, norm_g, B, L, shards):
    nC = L // CHUNK
    n = len(shards)

    def body(*refs):
        q_ref, ff_ref, fb_ref, i_ref, og_ref, lbl_ref, ng_ref = refs[:7]
        o_ref, ya_ref = refs[7 + n:9 + n]
        o_sc, p_sc, dec_sc, upd_sc = refs[9 + 2 * n:13 + 2 * n]
        step_id = pl.program_id(0) * N_HEADS + pl.program_id(1)
        gather = _Exchange(refs[7:7 + n], refs[9 + n:9 + 2 * n], refs[13 + 2 * n:], gather=True)

        @pl.when(step_id == 0)
        def _():
            gather.start()

        lb_f, lb_b = _lower_bounds(lbl_ref)
        o_sc[...] = jnp.zeros_like(o_sc)

        def run_dir(f_ref, lb_row, rev):
            def block(kb, carry):
                rows = pl.ds(pl.multiple_of(kb * HBLK, HBLK), HBLK)
                g = _block_gates(q_ref[rows, :], f_ref[rows, :], lb_row, rev)
                v = i_ref[rows, :]
                a = jnp.where(_block_mask(rev), _dot_nt(g["P"], g["Kt"]), 0.0)
                o_sc[rows, :] += _dot(a, v)
                p_sc[rows, :] = g["P"].astype(MX)
                dec_sc[rows, :] = g["dec"]
                _chunk_outer_products(v, g["Ke"], upd_sc, kb)
                return carry

            def step(c, st):
                rows = _chunk_rows(c)
                o_sc[rows, :] += _dot_nt(p_sc[rows, :], st)
                dec = dec_sc[pl.ds(pl.multiple_of(c * CHUNK, CHUNK), 1), :]
                return st * dec + upd_sc[c]

            _pipelined(nC // UNIT_CHUNKS, rev,
                       lambda u, c: _unit_blocks(u, block, c), 0,
                       lambda u, st: _unit_chunks(u, rev, step, st), jnp.zeros((HEAD_DIM, HEAD_DIM), F32))

        run_dir(ff_ref, lb_f, False)
        run_dir(fb_ref, lb_b, True)
        o = o_sc[...]
        o_ref[...] = o
        on = o * lax.rsqrt(jnp.mean(o * o, axis=-1, keepdims=True) + RMS_EPS)
        og = og_ref[...]
        ya_ref[...] = ((on * ng_ref[...]) * (og * _sig(og))).astype(MX)

        @pl.when(step_id == B * N_HEADS - 1)
        def _():
            gather.wait()

    def blk(s):
        return pl.BlockSpec((None, None, L, HEAD_DIM), lambda b, h, s=s: (s, b, 0, h))

    out_blk = pl.BlockSpec((None, L, HEAD_DIM), lambda b, h: (b, 0, h))
    outs = pl.pallas_call(
        body, name="hgrn_fwd", grid=(B, N_HEADS),
        in_specs=[blk(0), blk(1), blk(2), blk(3), blk(4),
                  pl.BlockSpec((4, HEAD_DIM), lambda b, h: (0, h)),
                  pl.BlockSpec((1, HEAD_DIM), lambda b, h: (0, h))] + [ANY] * n,
        out_specs=[out_blk, out_blk] + [ANY] * n,
        out_shape=[jax.ShapeDtypeStruct((B, L, D_MODEL), F32), jax.ShapeDtypeStruct((B, L, D_MODEL), MX)]
                  + _Exchange.out_shapes(shards, True),
        scratch_shapes=[pltpu.VMEM((L, HEAD_DIM), F32), pltpu.VMEM((L, HEAD_DIM), MX),
                        pltpu.VMEM((L, HEAD_DIM), F32), pltpu.VMEM((nC, HEAD_DIM, HEAD_DIM), F32)]
                       + _Exchange.scratch(n),
        compiler_params=_params(("arbitrary", "arbitrary")),
    )(proj, proj, proj, proj, proj, lbl, norm_g, *shards)
    return outs[0], outs[1], outs[2:]


POOL_PAD = 8


def _pool_window(p, ext_sc, half, adjoint):
    L = p.shape[0]
    n = L + 2 * POOL_PAD
    ext_sc[0:POOL_PAD, :] = jnp.zeros((POOL_PAD, p.shape[1]), F32)
    ext_sc[POOL_PAD + L:n, :] = jnp.zeros((POOL_PAD, p.shape[1]), F32)
    ext_sc[POOL_PAD:POOL_PAD + L, :] = p
    x = ext_sc[...]
    s = x + pltpu.roll(x, 1 if adjoint else n - 1, 0)
    w = 1
    while w < half:
        s = pltpu.roll(s, w, 0) + pltpu.roll(s, n - w, 0)
        w *= 2
    ext_sc[...] = s
    return ext_sc[POOL_PAD:POOL_PAD + L, :]


def _pool_count(L, half):
    t = lax.broadcasted_iota(jnp.int32, (L, 1), 0)
    lo = jnp.clip(t - half + 1, 0, L)
    hi = jnp.clip(t + half + 1, 0, L)
    return (hi - lo).astype(F32)


def _pool_fwd(proj, pool_w_full, pool_scale, B, L):
    def body(p_ref, w_ref, s_ref, yb_ref, ext_sc):
        for g, win in enumerate(POOL_WINDOWS):
            cols = slice(g * POOL_DIM, (g + 1) * POOL_DIM)
            p = p_ref[:, cols]
            y = _pool_window(p, ext_sc, win // 2, False) / _pool_count(L, win // 2) - p
            yb_ref[:, cols] = (_dot(y, w_ref[g]) * s_ref[:, cols]).astype(MX)

    return pl.pallas_call(
        body, name="pool_fwd", grid=(B,),
        in_specs=[pl.BlockSpec((None, None, L, D_MODEL), lambda b: (5, b, 0, 0)),
                  pl.BlockSpec((4, POOL_DIM, POOL_DIM), lambda b: (0, 0, 0)),
                  pl.BlockSpec((1, D_MODEL), lambda b: (0, 0))],
        out_specs=pl.BlockSpec((None, L, D_MODEL), lambda b: (b, 0, 0)),
        out_shape=jax.ShapeDtypeStruct((B, L, D_MODEL), MX),
        scratch_shapes=[pltpu.VMEM((L + 2 * POOL_PAD, POOL_DIM), F32)],
        compiler_params=_params(("parallel",)),
    )(proj, pool_w_full, pool_scale)


def _merge_out(x2, proj, ya, yb, wa, wb, wo):
    T = x2.shape[0]
    tm = min(512, T)

    def body(x_ref, ga_ref, gb_ref, ya_ref, yb_ref, wa_ref, wb_ref, wo_ref, za_ref, zb_ref, mg_ref, h_ref):
        za = jnp.dot(ya_ref[...], wa_ref[...], preferred_element_type=F32)
        zb = jnp.dot(yb_ref[...], wb_ref[...], preferred_element_type=F32)
        mg = (_sig(ga_ref[...]) * za + _sig(gb_ref[...]) * zb).astype(MX)
        za_ref[...] = za
        zb_ref[...] = zb
        mg_ref[...] = mg
        h_ref[...] = x_ref[...] + jnp.dot(mg, wo_ref[...], preferred_element_type=F32)

    tile = pl.BlockSpec((tm, D_MODEL), lambda i: (i, 0))
    wspec = pl.BlockSpec((D_MODEL, D_MODEL), lambda i: (0, 0))
    return pl.pallas_call(
        body, name="merge_out", grid=(T // tm,),
        in_specs=[tile,
                  pl.BlockSpec((None, tm, D_MODEL), lambda i: (6, i, 0)),
                  pl.BlockSpec((None, tm, D_MODEL), lambda i: (7, i, 0)),
                  tile, tile, wspec, wspec, wspec],
        out_specs=[tile, tile, tile, tile],
        out_shape=[jax.ShapeDtypeStruct((T, D_MODEL), F32), jax.ShapeDtypeStruct((T, D_MODEL), F32),
                   jax.ShapeDtypeStruct((T, D_MODEL), MX), jax.ShapeDtypeStruct((T, D_MODEL), F32)],
        compiler_params=_params(("parallel",)),
    )(x2, proj, proj, ya, yb, wa, wb, wo)


def _ffn_fwd_loss(h, tgt, g_ffn, g_final, wfi_g, wfo_g):
    T = h.shape[0]
    tm = min(512, T)
    nT = T // tm

    def body(h_ref, t_ref, gf_ref, gl_ref, wg_ref, wu_ref, wo_ref,
             gate_ref, up_ref, hid_ref, u2_ref, dh2_ref, dh2b_ref, loss_ref, dgl_ref, u2_sc, acc_sc):
        i, j = pl.program_id(0), pl.program_id(1)

        @pl.when(j == 0)
        def _():
            hh = h_ref[...]
            r = lax.rsqrt(jnp.mean(hh * hh, axis=-1, keepdims=True) + RMS_EPS)
            u2 = (hh * r * gf_ref[...]).astype(MX)
            u2_sc[...] = u2
            u2_ref[...] = u2
            acc_sc[...] = jnp.zeros_like(acc_sc)

        @pl.when((i == 0) & (j == 0))
        def _():
            dgl_ref[...] = jnp.zeros_like(dgl_ref)

        gate = jnp.dot(u2_sc[...], wg_ref[...], preferred_element_type=F32)
        up = jnp.dot(u2_sc[...], wu_ref[...], preferred_element_type=F32)
        hid = ((gate * _sig(gate)) * up).astype(MX)
        gate_ref[...] = gate
        up_ref[...] = up
        hid_ref[...] = hid
        acc_sc[...] += jnp.dot(hid, wo_ref[...], preferred_element_type=F32)

        @pl.when(j == 3)
        def _():
            h2 = h_ref[...] + acc_sc[...]
            r = lax.rsqrt(jnp.mean(h2 * h2, axis=-1, keepdims=True) + RMS_EPS)
            hn = h2 * r
            gl = gl_ref[...]
            err = hn * gl - t_ref[...]
            tok = jnp.mean(err * err, axis=-1, keepdims=True)
            loss_ref[...] = jnp.full(loss_ref.shape, 0.5 * jnp.sum(tok), F32)
            dy = err * (1.0 / D_MODEL)
            dgl_ref[...] += _fold8(dy * hn)
            a = dy * gl
            dh2 = r * a - hn * (r * jnp.mean(a * hn, axis=-1, keepdims=True))
            dh2_ref[...] = dh2
            dh2b_ref[...] = dh2.astype(MX)

    tile = pl.BlockSpec((tm, D_MODEL), lambda i, j: (i, 0))
    vec = pl.BlockSpec((1, D_MODEL), lambda i, j: (0, 0))
    ftile = pl.BlockSpec((None, tm, FF_BLOCK), lambda i, j: (j, i, 0))
    return pl.pallas_call(
        body, name="ffn_fwd_loss", grid=(nT, 4),
        in_specs=[tile, tile, vec, vec,
                  pl.BlockSpec((None, D_MODEL, FF_BLOCK), lambda i, j: (j, 0, 0)),
                  pl.BlockSpec((None, D_MODEL, FF_BLOCK), lambda i, j: (j + 4, 0, 0)),
                  pl.BlockSpec((None, FF_BLOCK, D_MODEL), lambda i, j: (j, 0, 0))],
        out_specs=[ftile, ftile, ftile, tile, tile, tile,
                   pl.BlockSpec((None, 8, 128), lambda i, j: (i, 0, 0)),
                   pl.BlockSpec((8, D_MODEL), lambda i, j: (0, 0))],
        out_shape=[jax.ShapeDtypeStruct((4, T, FF_BLOCK), F32), jax.ShapeDtypeStruct((4, T, FF_BLOCK), F32),
                   jax.ShapeDtypeStruct((4, T, FF_BLOCK), MX), jax.ShapeDtypeStruct((T, D_MODEL), MX),
                   jax.ShapeDtypeStruct((T, D_MODEL), F32), jax.ShapeDtypeStruct((T, D_MODEL), MX),
                   jax.ShapeDtypeStruct((nT, 8, 128), F32), jax.ShapeDtypeStruct((8, D_MODEL), F32)],
        scratch_shapes=[pltpu.VMEM((tm, D_MODEL), MX), pltpu.VMEM((tm, D_MODEL), F32)],
        compiler_params=_params(("arbitrary", "arbitrary")),
    )(h, tgt, g_ffn, g_final, wfi_g, wfi_g, wfo_g)


def _ffn_bwd(h, dh2, dh2b, gate, up, g_ffn, wfi_g, wfo_g):
    T = h.shape[0]
    tm = min(512, T)

    def body(h_ref, dh2_ref, dh2b_ref, gate_ref, up_ref, gf_ref, wg_ref, wu_ref, wo_ref,
             dgu_ref, dh_ref, dhb_ref, dgf_ref, acc_sc):
        i, j = pl.program_id(0), pl.program_id(1)

        @pl.when(j == 0)
        def _():
            acc_sc[...] = jnp.zeros_like(acc_sc)

        @pl.when((i == 0) & (j == 0))
        def _():
            dgf_ref[...] = jnp.zeros_like(dgf_ref)

        dhid = _dot_nt(dh2b_ref[...], wo_ref[...])
        gate, up = gate_ref[...], up_ref[...]
        sg = _sig(gate)
        dgate = (dhid * up * (sg * (1.0 + gate * (1.0 - sg)))).astype(MX)
        dup = (dhid * (gate * sg)).astype(MX)
        dgu_ref[0] = dgate
        dgu_ref[1] = dup
        acc_sc[...] += _dot_nt(dgate, wg_ref[...]) + _dot_nt(dup, wu_ref[...])

        @pl.when(j == 3)
        def _():
            hh = h_ref[...]
            r = lax.rsqrt(jnp.mean(hh * hh, axis=-1, keepdims=True) + RMS_EPS)
            hn = hh * r
            du2 = acc_sc[...]
            dgf_ref[...] += _fold8(du2 * hn)
            a = du2 * gf_ref[...]
            dh = dh2_ref[...] + r * a - hn * (r * jnp.mean(a * hn, axis=-1, keepdims=True))
            dh_ref[...] = dh
            dhb_ref[...] = dh.astype(MX)

    tile = pl.BlockSpec((tm, D_MODEL), lambda i, j: (i, 0))
    ftile = pl.BlockSpec((None, tm, FF_BLOCK), lambda i, j: (j, i, 0))
    return pl.pallas_call(
        body, name="ffn_bwd", grid=(T // tm, 4),
        in_specs=[tile, tile, tile, ftile, ftile,
                  pl.BlockSpec((1, D_MODEL), lambda i, j: (0, 0)),
                  pl.BlockSpec((None, D_MODEL, FF_BLOCK), lambda i, j: (j, 0, 0)),
                  pl.BlockSpec((None, D_MODEL, FF_BLOCK), lambda i, j: (j + 4, 0, 0)),
                  pl.BlockSpec((None, FF_BLOCK, D_MODEL), lambda i, j: (j, 0, 0))],
        out_specs=[pl.BlockSpec((2, None, tm, FF_BLOCK), lambda i, j: (0, j, i, 0)),
                   tile, tile, pl.BlockSpec((8, D_MODEL), lambda i, j: (0, 0))],
        out_shape=[jax.ShapeDtypeStruct((2, 4, T, FF_BLOCK), MX),
                   jax.ShapeDtypeStruct((T, D_MODEL), F32), jax.ShapeDtypeStruct((T, D_MODEL), MX),
                   jax.ShapeDtypeStruct((8, D_MODEL), F32)],
        scratch_shapes=[pltpu.VMEM((tm, D_MODEL), F32)],
        compiler_params=_params(("arbitrary", "arbitrary")),
    )(h, dh2, dh2b, gate, up, g_ffn, wfi_g, wfi_g, wfo_g)


def _merge_bwd(dhb, proj, za, zb, wa, wb, wo):
    T = dhb.shape[0]
    tm = min(512, T)

    def body(dh_ref, ga_ref, gb_ref, za_ref, zb_ref, wa_ref, wb_ref, wo_ref,
             dza_ref, dzb_ref, dgab_ref, dya_ref, dyb_ref):
        dm = _dot_nt(dh_ref[...], wo_ref[...])
        sa, sb = _sig(ga_ref[...]), _sig(gb_ref[...])
        dza = (dm * sa).astype(MX)
        dzb = (dm * sb).astype(MX)
        dza_ref[...] = dza
        dzb_ref[...] = dzb
        dgab_ref[0] = (dm * za_ref[...] * (sa * (1.0 - sa))).astype(MX)
        dgab_ref[1] = (dm * zb_ref[...] * (sb * (1.0 - sb))).astype(MX)
        dya_ref[...] = _dot_nt(dza, wa_ref[...])
        dyb_ref[...] = _dot_nt(dzb, wb_ref[...])

    tile = pl.BlockSpec((tm, D_MODEL), lambda i: (i, 0))
    wspec = pl.BlockSpec((D_MODEL, D_MODEL), lambda i: (0, 0))
    return pl.pallas_call(
        body, name="merge_bwd", grid=(T // tm,),
        in_specs=[tile,
                  pl.BlockSpec((None, tm, D_MODEL), lambda i: (6, i, 0)),
                  pl.BlockSpec((None, tm, D_MODEL), lambda i: (7, i, 0)),
                  tile, tile, wspec, wspec, wspec],
        out_specs=[tile, tile, pl.BlockSpec((2, tm, D_MODEL), lambda i: (0, i, 0)), tile, tile],
        out_shape=[jax.ShapeDtypeStruct((T, D_MODEL), MX), jax.ShapeDtypeStruct((T, D_MODEL), MX),
                   jax.ShapeDtypeStruct((2, T, D_MODEL), MX),
                   jax.ShapeDtypeStruct((T, D_MODEL), F32), jax.ShapeDtypeStruct((T, D_MODEL), F32)],
        compiler_params=_params(("parallel",)),
    )(dhb, proj, proj, za, zb, wa, wb, wo)


def _pool_bwd(proj, dyb, pool_w_full, pool_scale, B, L):
    def body(p_ref, dy_ref, w_ref, s_ref, dp_ref, dw_ref, ds_ref, ext_sc):
        for g, win in enumerate(POOL_WINDOWS):
            cols = slice(g * POOL_DIM, (g + 1) * POOL_DIM)
            p = p_ref[:, cols]
            cnt = _pool_count(L, win // 2)
            y = _pool_window(p, ext_sc, win // 2, False) / cnt - p
            z = _dot(y, w_ref[g])
            dyb_g = dy_ref[:, cols]
            ds_ref[:, cols] = jnp.sum(dyb_g * z, axis=0, keepdims=True)
            dz = dyb_g * s_ref[:, cols]
            dw_ref[g] = _dot_tn(y, dz)
            dy = _dot_nt(dz, w_ref[g])
            dp_ref[:, cols] = (_pool_window(dy / cnt, ext_sc, win // 2, True) - dy).astype(MX)

    seq = pl.BlockSpec((None, L, D_MODEL), lambda b: (b, 0, 0))
    return pl.pallas_call(
        body, name="pool_bwd", grid=(B,),
        in_specs=[pl.BlockSpec((None, None, L, D_MODEL), lambda b: (5, b, 0, 0)), seq,
                  pl.BlockSpec((4, POOL_DIM, POOL_DIM), lambda b: (0, 0, 0)),
                  pl.BlockSpec((1, D_MODEL), lambda b: (0, 0))],
        out_specs=[seq, pl.BlockSpec((None, 4, POOL_DIM, POOL_DIM), lambda b: (b, 0, 0, 0)),
                   pl.BlockSpec((None, 1, D_MODEL), lambda b: (b, 0, 0))],
        out_shape=[jax.ShapeDtypeStruct((B, L, D_MODEL), MX),
                   jax.ShapeDtypeStruct((B, 4, POOL_DIM, POOL_DIM), F32),
                   jax.ShapeDtypeStruct((B, 1, D_MODEL), F32)],
        scratch_shapes=[pltpu.VMEM((L + 2 * POOL_PAD, POOL_DIM), F32)],
        compiler_params=_params(("parallel",)),
    )(proj, dyb, pool_w_full, pool_scale)


def _hgrn_bwd(proj, o, dya, lbl, norm_g, B, L, grads, dests):
    nC = L // CHUNK
    n = len(grads)

    def body(*refs):
        q_ref, ff_ref, fb_ref, i_ref, og_ref, o_ref, dy_ref, lbl_ref, ng_ref = refs[:9]
        dp_ref, dlb_ref, dng_ref = refs[9 + n:12 + n]
        (do_sc, dq_sc, dv_sc, dP_sc, dKt_sc, dKe_sc, dbl_sc, dec_sc, ke_sc, ck_sc, upd_sc,
         dupd_sc) = refs[12 + 2 * n:24 + 2 * n]
        step_id = pl.program_id(0) * N_HEADS + pl.program_id(1)
        scatter = _Exchange(refs[9:9 + n], refs[12 + n:12 + 2 * n], refs[24 + 2 * n:], gather=False, dests=dests)

        @pl.when(step_id == 0)
        def _():
            scatter.start()

        lb_f, lb_b = _lower_bounds(lbl_ref)
        o_ = o_ref[...]
        r = lax.rsqrt(jnp.mean(o_ * o_, axis=-1, keepdims=True) + RMS_EPS)
        on = o_ * r
        og = og_ref[...]
        sog = _sig(og)
        dy = dy_ref[...]
        ng = ng_ref[...]
        dp_ref[4] = (dy * (on * ng) * (sog * (1.0 + og * (1.0 - sog)))).astype(MX)
        dn = dy * (og * sog)
        dng_ref[...] = jnp.sum(dn * on, axis=0, keepdims=True)
        don = dn * ng
        do_sc[...] = r * don - on * (r * jnp.mean(don * on, axis=-1, keepdims=True))

        dq_sc[...] = jnp.zeros_like(dq_sc)
        dv_sc[...] = jnp.zeros_like(dv_sc)

        def run_dir(f_ref, lb_row, rev, slot):
            def block(kb, carry):
                rows = pl.ds(pl.multiple_of(kb * HBLK, HBLK), HBLK)
                g = _block_gates(q_ref[rows, :], f_ref[rows, :], lb_row, rev)
                v = i_ref[rows, :]
                do = do_sc[rows, :]
                m, mt = _block_mask(rev), _block_mask(rev, True)
                at = jnp.where(mt, _dot_nt(g["Kt"], g["P"]), 0.0)
                da = jnp.where(m, _dot_nt(do, v), 0.0)
                dat = jnp.where(mt, _dot_nt(v, do), 0.0)
                dv_sc[rows, :] += _dot(at, do)
                dP_sc[rows, :] = _dot(da, g["Kt"])
                dKt_sc[rows, :] = _dot(dat, g["P"])
                ke_sc[rows, :] = g["Ke"].astype(MX)
                dec_sc[rows, :] = g["dec"]
                _chunk_outer_products(v, g["Ke"], upd_sc, kb)
                _chunk_outer_products(do, g["P"], dupd_sc, kb)
                return carry

            def fstep(c, st):
                rows = _chunk_rows(c)
                ck_sc[c] = st.astype(MX)
                dP_sc[rows, :] += _dot(do_sc[rows, :], st)
                dec = dec_sc[pl.ds(pl.multiple_of(c * CHUNK, CHUNK), 1), :]
                return st * dec + upd_sc[c]

            n_units = nC // UNIT_CHUNKS
            zero_state = jnp.zeros((HEAD_DIM, HEAD_DIM), F32)
            _pipelined(n_units, rev,
                       lambda u, c: _unit_blocks(u, block, c), 0,
                       lambda u, st: _unit_chunks(u, rev, fstep, st), zero_state)

            def bstep(c, dst):
                rows = _chunk_rows(c)
                dec = dec_sc[pl.ds(pl.multiple_of(c * CHUNK, CHUNK), 1), :]
                dKe_sc[rows, :] = _dot(i_ref[rows, :], dst)
                dv_sc[rows, :] += _dot_nt(ke_sc[rows, :], dst)
                dbl = dec * jnp.sum(dst * ck_sc[c].astype(F32), axis=0, keepdims=True)
                dbl_sc[rows, :] = jnp.broadcast_to(dbl, (CHUNK, HEAD_DIM))
                return dst * dec + dupd_sc[c]

            def finish(kb, acc):
                rows = pl.ds(pl.multiple_of(kb * HBLK, HBLK), HBLK)
                q_r = q_ref[rows, :]
                g = _block_gates(q_r, f_ref[rows, :], lb_row, rev)
                dP, dkt, dke = dP_sc[rows, :], dKt_sc[rows, :], dKe_sc[rows, :]
                e = dke * g["Ke"]
                dlf = (_seg_cumsum(dP * g["P"] - dkt * g["Kt"], not rev) + _seg_cumsum(e, rev) - e
                       + dbl_sc[rows, :])
                df = dlf / g["f"] - (dkt * g["enb"] + dke * g["eend"])
                dp_ref[slot, rows, :] = (df * (1.0 - lb_row) * (g["sg"] * (1.0 - g["sg"]))).astype(MX)
                dq_sc[rows, :] += (dP * g["eb"]) * (g["sq"] * (1.0 + q_r * (1.0 - g["sq"])))
                return acc + jnp.sum(df * (1.0 - g["sg"]), axis=0, keepdims=True)

            _, dlb = _pipelined(n_units, not rev,
                                lambda u, dst: _unit_chunks(u, not rev, bstep, dst), zero_state,
                                lambda u, acc: _unit_blocks(u, finish, acc), jnp.zeros((1, HEAD_DIM), F32))
            dlb_ref[slot - 1:slot, :] = dlb

        run_dir(ff_ref, lb_f, False, 1)
        run_dir(fb_ref, lb_b, True, 2)
        dp_ref[0] = dq_sc[...].astype(MX)
        dp_ref[3] = dv_sc[...].astype(MX)

        @pl.when(step_id == B * N_HEADS - 1)
        def _():
            scatter.wait()

    def blk(s):
        return pl.BlockSpec((None, None, L, HEAD_DIM), lambda b, h, s=s: (s, b, 0, h))

    seq = pl.BlockSpec((None, L, HEAD_DIM), lambda b, h: (b, 0, h))
    outs = pl.pallas_call(
        body, name="hgrn_bwd", grid=(B, N_HEADS),
        in_specs=[blk(0), blk(1), blk(2), blk(3), blk(4), seq, seq,
                  pl.BlockSpec((4, HEAD_DIM), lambda b, h: (0, h)),
                  pl.BlockSpec((1, HEAD_DIM), lambda b, h: (0, h))] + [ANY] * n,
        out_specs=[pl.BlockSpec((5, None, L, HEAD_DIM), lambda b, h: (0, b, 0, h)),
                   pl.BlockSpec((None, 2, HEAD_DIM), lambda b, h: (b, 0, h)),
                   pl.BlockSpec((None, 1, HEAD_DIM), lambda b, h: (b, 0, h))] + [ANY] * n,
        out_shape=[jax.ShapeDtypeStruct((5, B, L, D_MODEL), MX),
                   jax.ShapeDtypeStruct((B, 2, D_MODEL), F32),
                   jax.ShapeDtypeStruct((B, 1, D_MODEL), F32)] + _Exchange.out_shapes(grads, False),
        scratch_shapes=[pltpu.VMEM((L, HEAD_DIM), F32)] * 8
                       + [pltpu.VMEM((L, HEAD_DIM), MX), pltpu.VMEM((nC, HEAD_DIM, HEAD_DIM), MX),
                          pltpu.VMEM((nC, HEAD_DIM, HEAD_DIM), F32), pltpu.VMEM((nC, HEAD_DIM, HEAD_DIM), F32)]
                       + _Exchange.scratch(n),
        compiler_params=_params(("arbitrary", "arbitrary")),
    )(proj, proj, proj, proj, proj, o, dya, lbl, norm_g, *grads)
    return outs[0], outs[1], outs[2], outs[3:]


def _dproj_select(s, a5_ref, p_ref, g2_ref):
    return jnp.where(s < 5, a5_ref[...], jnp.where(s == 5, p_ref[...], g2_ref[...]))


def _dproj_specs(tm, tile_axis):
    def ix(args):
        return args[tile_axis], args[1 - tile_axis]
    a5 = pl.BlockSpec((None, tm, D_MODEL), lambda *a: (jnp.minimum(ix(a)[1], 4), ix(a)[0], 0))
    p = pl.BlockSpec((tm, D_MODEL), lambda *a: (ix(a)[0], 0))
    g2 = pl.BlockSpec((None, tm, D_MODEL), lambda *a: (jnp.clip(ix(a)[1] - 6, 0, 1), ix(a)[0], 0))
    return [a5, p, g2]


def _inproj_bwd(x2, dh, dproj5, dp, dgab, g_mix, w_in_g, d_win, dests):
    T = x2.shape[0]
    tm = min(512, T)
    nT = T // tm

    def body(a5_ref, p_ref, g2_ref, w_ref, x_ref, dh_ref, g_ref, dwin_ref, dx_ref, dg_ref, recv_ref, acc_sc, *sems):
        i, s = pl.program_id(0), pl.program_id(1)
        scatter = _Exchange([dwin_ref], [recv_ref], sems, gather=False, dests=[dests])

        @pl.when((i == 0) & (s == 0))
        def _():
            scatter.start()

        @pl.when(s == 0)
        def _():
            acc_sc[...] = jnp.zeros_like(acc_sc)

        @pl.when((i == 0) & (s == 0))
        def _():
            dg_ref[...] = jnp.zeros_like(dg_ref)

        acc_sc[...] += _dot_nt(_dproj_select(s, a5_ref, p_ref, g2_ref), w_ref[...])

        @pl.when(s == N_DEV - 1)
        def _():
            x = x_ref[...]
            r = lax.rsqrt(jnp.mean(x * x, axis=-1, keepdims=True) + RMS_EPS)
            xn = x * r
            du = acc_sc[...]
            dg_ref[...] += _fold8(du * xn)
            a = du * g_ref[...]
            dx_ref[...] = dh_ref[...] + r * a - xn * (r * jnp.mean(a * xn, axis=-1, keepdims=True))

        @pl.when((i == nT - 1) & (s == N_DEV - 1))
        def _():
            scatter.wait()

    tile = pl.BlockSpec((tm, D_MODEL), lambda i, s: (i, 0))
    return pl.pallas_call(
        body, name="inproj_bwd", grid=(nT, N_DEV),
        in_specs=_dproj_specs(tm, 0) + [pl.BlockSpec((None, D_MODEL, D_MODEL), lambda i, s: (s, 0, 0)),
                                        tile, tile, pl.BlockSpec((1, D_MODEL), lambda i, s: (0, 0)), ANY],
        out_specs=[tile, pl.BlockSpec((8, D_MODEL), lambda i, s: (0, 0)), ANY],
        out_shape=[jax.ShapeDtypeStruct((T, D_MODEL), F32), jax.ShapeDtypeStruct((8, D_MODEL), F32)]
                  + _Exchange.out_shapes([d_win], False),
        scratch_shapes=[pltpu.VMEM((tm, D_MODEL), F32)] + _Exchange.scratch(1),
        compiler_params=_params(("arbitrary", "arbitrary")),
    )(dproj5, dp, dgab, w_in_g, x2, dh, g_mix, d_win)


def _wgrad(a, g, name):
    Ba, T, K = a.shape
    Bg, _, Nn = g.shape
    nb = max(Ba, Bg)
    tm = min(BIG_TOKEN_TILE, T)
    nt = T // tm

    def body(a_ref, g_ref, out_ref, acc_sc):
        t = pl.program_id(1)

        @pl.when(t == 0)
        def _():
            acc_sc[...] = jnp.zeros_like(acc_sc)

        acc_sc[...] += _dot_tn(a_ref[...], g_ref[...])

        @pl.when(t == nt - 1)
        def _():
            out_ref[...] = acc_sc[...].astype(MX)

    return pl.pallas_call(
        body, name=name, grid=(nb, nt),
        in_specs=[pl.BlockSpec((None, tm, K), lambda s, t: (s if Ba > 1 else 0, t, 0)),
                  pl.BlockSpec((None, tm, Nn), lambda s, t: (s if Bg > 1 else 0, t, 0))],
        out_specs=pl.BlockSpec((None, K, Nn), lambda s, t: (s, 0, 0)),
        out_shape=jax.ShapeDtypeStruct((nb, K, Nn), MX),
        scratch_shapes=[pltpu.VMEM((K, Nn), F32)],
        compiler_params=_params(("parallel", "arbitrary")),
    )(a, g)


def _mesh_pos():
    return lax.axis_index("x"), lax.axis_index("y"), lax.axis_index("c")


def _device_of(p):
    return (p // 4, (p // 2) % 2, p % 2)


class _Exchange:
    def __init__(self, srcs, outs, sems, gather, dests=None):
        send_sems, recv_sems, local_sems = sems
        x, y, c = _mesh_pos()
        me = 4 * x + 2 * y + c
        self.sends, self.arrivals, self.mine = [], [], []
        for a, (src, out) in enumerate(zip(srcs, outs)):
            lo, hi = dests[a] if dests else (0, N_DEV)

            def piece(p, src=src, lo=lo, hi=hi):
                return src if gather else src.at[jnp.clip(p - lo, 0, hi - lo - 1)]

            def served(p, lo=lo, hi=hi):
                return None if (lo, hi) == (0, N_DEV) else (p >= lo) & (p < hi)

            self.mine.append((pltpu.make_async_copy(piece(me), out.at[me], local_sems.at[a]), served(me)))
            for j in range(1, N_DEV):
                to, frm = (me + j) % N_DEV, (me + N_DEV - j) % N_DEV
                pair = dict(send_sem=send_sems.at[7 * a + j - 1], recv_sem=recv_sems.at[7 * a + j - 1],
                            device_id_type=pl.DeviceIdType.MESH)
                self.sends.append((pltpu.make_async_remote_copy(
                    src_ref=piece(to), dst_ref=out.at[me], device_id=_device_of(to), **pair), served(to)))
                self.arrivals.append((pltpu.make_async_remote_copy(
                    src_ref=piece(frm), dst_ref=out.at[frm], device_id=_device_of(frm), **pair), served(me)))

    @staticmethod
    def _each(copies, act):
        for cp, takes_part in copies:
            if takes_part is None:
                act(cp)
            else:
                pl.when(takes_part)(functools.partial(act, cp))

    def start(self):
        self._each(self.mine + self.sends, lambda cp: cp.start())

    def wait(self):
        self._each(self.arrivals, lambda cp: cp.wait_recv())
        self._each(self.sends, lambda cp: cp.wait_send())
        self._each(self.mine, lambda cp: cp.wait())

    @staticmethod
    def scratch(n):
        return [pltpu.SemaphoreType.DMA((7 * n,)), pltpu.SemaphoreType.DMA((7 * n,)), pltpu.SemaphoreType.DMA((n,))]

    @staticmethod
    def out_shapes(arrays, gather):
        return [jax.ShapeDtypeStruct((N_DEV,) + (a.shape if gather else a.shape[1:]), a.dtype) for a in arrays]


def _all_reduce_small(v):
    R, C = v.shape

    def body(v_ref, out_ref, slots, send_sems, recv_sems):
        x, y, c = _mesh_pos()
        me = 4 * x + 2 * y + c

        def copy(j, to):
            return pltpu.make_async_remote_copy(
                src_ref=v_ref, dst_ref=slots.at[me],
                send_sem=send_sems.at[j - 1], recv_sem=recv_sems.at[j - 1],
                device_id=_device_of(to), device_id_type=pl.DeviceIdType.MESH)

        sends = [copy(j, (me + j) % N_DEV) for j in range(1, N_DEV)]
        for cp in sends:
            cp.start()
        slots[me] = v_ref[...]
        for j in range(1, N_DEV):
            frm = (me + N_DEV - j) % N_DEV
            pltpu.make_async_remote_copy(
                src_ref=v_ref, dst_ref=slots.at[frm], send_sem=send_sems.at[j - 1], recv_sem=recv_sems.at[j - 1],
                device_id=_device_of(frm), device_id_type=pl.DeviceIdType.MESH).wait_recv()
        for cp in sends:
            cp.wait_send()
        acc = slots[0]
        for p in range(1, N_DEV):
            acc = acc + slots[p]
        out_ref[...] = acc

    return pl.pallas_call(
        body, name="all_reduce_small",
        in_specs=[pl.BlockSpec(memory_space=pltpu.VMEM)], out_specs=pl.BlockSpec(memory_space=pltpu.VMEM),
        out_shape=jax.ShapeDtypeStruct((R, C), F32),
        scratch_shapes=[pltpu.VMEM((N_DEV, R, C), F32), pltpu.SemaphoreType.DMA((7,)), pltpu.SemaphoreType.DMA((7,))],
    )(v)


def _adamw_math(w, g, m, v):
    m = ADAM_B1 * m + (1.0 - ADAM_B1) * g
    v = ADAM_B2 * v + (1.0 - ADAM_B2) * (g * g)
    m_hat = m / (1.0 - ADAM_B1 ** ADAM_STEP)
    v_hat = v / (1.0 - ADAM_B2 ** ADAM_STEP)
    delta = -ADAM_LR * (m_hat / (jnp.sqrt(v_hat) + ADAM_EPS) + ADAM_WD * w)
    return delta, m, v


def _adamw_reduce(recvs, dests, w, m, v, name):
    R, C = w.shape
    tr = R if R <= 256 else 256
    while R % tr:
        tr //= 2
    n = len(recvs)

    def body(*refs):
        w_ref, m_ref, v_ref, g_ref, d_ref, nm_ref, nv_ref = refs[n:]

        def update(r_ref):
            g = r_ref[0].astype(F32)
            for p in range(1, N_DEV):
                g = g + r_ref[p].astype(F32)
            d, nm, nv = _adamw_math(w_ref[...], g, m_ref[...], v_ref[...])
            g_ref[...] = g
            d_ref[...] = d
            nm_ref[...] = nm
            nv_ref[...] = nv

        if n == 1:
            update(refs[0])
        else:
            x, y, c = _mesh_pos()
            me = 4 * x + 2 * y + c
            for r_ref, (lo, hi) in zip(refs[:n], dests):
                pl.when((me >= lo) & (me < hi))(functools.partial(update, r_ref))

    tile = pl.BlockSpec((tr, C), lambda i: (i, 0))
    shp = jax.ShapeDtypeStruct((R, C), F32)
    return pl.pallas_call(
        body, name=name, grid=(R // tr,),
        in_specs=[pl.BlockSpec((N_DEV, tr, C), lambda i: (0, i, 0))] * n + [tile, tile, tile],
        out_specs=[tile] * 4, out_shape=[shp] * 4,
        compiler_params=_params(("parallel",)),
    )(*recvs, w, m, v)


def _adamw_small(g, w, m, v):
    def body(g_ref, w_ref, m_ref, v_ref, go_ref, d_ref, nm_ref, nv_ref):
        go_ref[...] = g_ref[...]
        for d in range(2):
            p0 = _sig(w_ref[8 + 2 * d:9 + 2 * d, :] - w_ref[9 + 2 * d:10 + 2 * d, :])
            dl0 = g_ref[12 + d:13 + d, :] * p0 * (1.0 - p0)
            go_ref[8 + 2 * d:9 + 2 * d, :] = dl0
            go_ref[9 + 2 * d:10 + 2 * d, :] = -dl0
            go_ref[12 + d:13 + d, :] = jnp.zeros((1, D_MODEL), F32)
        d, nm, nv = _adamw_math(w_ref[...], go_ref[...], m_ref[...], v_ref[...])
        d_ref[...] = d
        nm_ref[...] = nm
        nv_ref[...] = nv

    shp = jax.ShapeDtypeStruct(g.shape, F32)
    vm = pl.BlockSpec(memory_space=pltpu.VMEM)
    return pl.pallas_call(body, name="adamw_small", in_specs=[vm] * 4, out_specs=[vm] * 4, out_shape=[shp] * 4)(g, w, m, v)


def _local_step(x, tgt, g_mix, lb_shard, norm_g, pool_scale, g_ffn, g_final, w_in_shard, late_shards):
    B, L, _ = x.shape
    T = B * L
    rows = D_MODEL // N_DEV
    x2, tgt2 = x.reshape(T, D_MODEL), tgt.reshape(T, D_MODEL)

    proj, u, w_in_g, lb_g = _rms_inproj(x2, g_mix, w_in_shard, lb_shard)
    lb = lb_g[:, :4].transpose(1, 0, 2).reshape(4, D_MODEL)
    proj4 = proj.reshape(N_DEV, B, L, D_MODEL)
    o, ya, (wa, wb, wo, wfi_g, wfo_g, pw_g) = _hgrn_fwd(proj4, lb, norm_g, B, L, late_shards)
    wa, wb, wo = (w_.reshape(D_MODEL, D_MODEL) for w_ in (wa, wb, wo))
    wfo_g = wfo_g.reshape(4, FF_BLOCK, D_MODEL)
    pool_w_full = pw_g.reshape(N_DEV, 4, 32, POOL_DIM).transpose(1, 0, 2, 3).reshape(4, POOL_DIM, POOL_DIM)
    yb = _pool_fwd(proj4, pool_w_full, pool_scale, B, L)
    ya2, yb2 = ya.reshape(T, D_MODEL), yb.reshape(T, D_MODEL)
    za, zb, mg, h = _merge_out(x2, proj, ya2, yb2, wa, wb, wo)
    gate, up, hid, u2, dh2, dh2b, loss_p, dg_final = _ffn_fwd_loss(h, tgt2, g_ffn, g_final, wfi_g, wfo_g)
    loss = jnp.sum(loss_p[:, 0, 0])

    dgu, dh, dhb, dg_ffn = _ffn_bwd(h, dh2, dh2b, gate, up, g_ffn, wfi_g, wfo_g)
    d_wfo = _wgrad(hid, dh2b[None], "wgrad_ffn_out")
    d_wfi = _wgrad(u2[None], dgu.reshape(N_DEV, T, FF_BLOCK), "wgrad_ffn_in")
    dza, dzb, dgab, dya, dyb = _merge_bwd(dhb, proj, za, zb, wa, wb, wo)
    d_wo = _wgrad(mg[None], dhb[None], "wgrad_out")
    d_wa = _wgrad(ya2[None], dza[None], "wgrad_branch_a")
    d_wb = _wgrad(yb2[None], dzb[None], "wgrad_branch_b")
    dp, dpw_p, dps_p = _pool_bwd(proj4, dyb.reshape(B, L, D_MODEL), pool_w_full, pool_scale, B, L)
    d_pw = dpw_p.sum(0).reshape(4, N_DEV, 32, POOL_DIM).transpose(1, 0, 2, 3).reshape(N_DEV, 128, POOL_DIM)
    dp2 = dp.reshape(T, D_MODEL)
    w_in_dests = [(5, 6), (6, 8), (0, 5)]
    d_win_pool = _wgrad(u[None], dp2[None], "wgrad_in_pool")
    d_win_gates = _wgrad(u[None], dgab, "wgrad_in_gates")
    slices = [d_wa.reshape(N_DEV, rows, D_MODEL), d_wb.reshape(N_DEV, rows, D_MODEL),
              d_wo.reshape(N_DEV, rows, D_MODEL), d_wfi, d_wfo.reshape(N_DEV, FF_BLOCK // 2, D_MODEL),
              d_pw.astype(MX), d_win_pool, d_win_gates]
    dests = [(0, N_DEV)] * 6 + w_in_dests[:2]
    dproj5, dlb_p, dng_p, recv = _hgrn_bwd(proj4, o, dya.reshape(B, L, D_MODEL), lb, norm_g, B, L, slices, dests)
    dproj5 = dproj5.reshape(5, T, D_MODEL)
    d_win_rec = _wgrad(u[None], dproj5, "wgrad_in_recurrence")
    grad_x, dg_mix, recv_win_rec = _inproj_bwd(x2, dh, dproj5, dp2, dgab, g_mix, w_in_g, d_win_rec, w_in_dests[2])

    small = dict(g_mix=dg_mix.sum(0), hgrn_norm_g=dng_p.sum((0, 1)), pool_scale=dps_p.sum((0, 1)),
                 g_ffn=dg_ffn.sum(0), g_final=dg_final.sum(0), lb=dlb_p.sum(0))
    recv_w_in = [recv[6], recv[7], recv_win_rec]
    return loss, grad_x.reshape(B, L, D_MODEL), (recv_w_in, w_in_dests), list(recv[:6]), small


def kernel(x, g_mix, w_in, lb_logits, hgrn_norm_g, pool_w, pool_scale, w_branch_a, w_branch_b, w_out, g_ffn, w_ffn_in, w_ffn_out, g_final, loss_target, m_g_mix, m_w_in, m_lb_logits, m_hgrn_norm_g, m_pool_w, m_pool_scale, m_w_branch_a, m_w_branch_b, m_w_out, m_g_ffn, m_w_ffn_in, m_w_ffn_out, m_g_final, v_g_mix, v_w_in, v_lb_logits, v_hgrn_norm_g, v_pool_w, v_pool_scale, v_w_branch_a, v_w_branch_b, v_w_out, v_g_ffn, v_w_ffn_in, v_w_ffn_out, v_g_final):
    me = 4 * lax.axis_index("x") + 2 * lax.axis_index("y") + lax.axis_index("c")

    late_shards = [w_branch_a[0].astype(MX), w_branch_b[0].astype(MX), w_out[0].astype(MX),
                   w_ffn_in[0].astype(MX), w_ffn_out[0].astype(MX), pool_w[0].reshape(4 * 32, POOL_DIM).astype(MX)]
    lb_shard = jnp.pad(lb_logits.reshape(4, HEAD_DIM), ((0, 4), (0, 0)))

    loss, grad_x, (recv_w_in, w_in_dests), recv_late, small = _local_step(
        x, loss_target, g_mix, lb_shard, hgrn_norm_g, pool_scale, g_ffn, g_final[None], w_in[0].astype(MX),
        late_shards)
    loss = lax.psum(loss, ("x", "y", "c"))

    packed = jnp.zeros((16, D_MODEL), F32)
    names = ["g_mix", "hgrn_norm_g", "pool_scale", "g_ffn", "g_final"]
    for i, nme in enumerate(names):
        packed = packed.at[i].set(small[nme])
    packed = packed.at[5:7].set(small["lb"])
    red = _all_reduce_small(packed)
    dlb_mine = lax.dynamic_slice_in_dim(red[5:7], me * HEAD_DIM, HEAD_DIM, axis=1)

    sw = jnp.zeros((16, D_MODEL), F32)
    sm = jnp.zeros((16, D_MODEL), F32)
    sv = jnp.ones((16, D_MODEL), F32)
    smalls = [(g_mix, m_g_mix, v_g_mix), (hgrn_norm_g, m_hgrn_norm_g, v_hgrn_norm_g),
              (pool_scale, m_pool_scale, v_pool_scale), (g_ffn, m_g_ffn, v_g_ffn),
              (g_final[None], m_g_final[None], v_g_final[None])]
    for i, (w_, m_, v_) in enumerate(smalls):
        sw, sm, sv = sw.at[i].set(w_[0]), sm.at[i].set(m_[0]), sv.at[i].set(v_[0])
    sg = red.at[5:].set(0.0)
    sg = sg.at[12:14, :HEAD_DIM].set(dlb_mine)
    sw = sw.at[8:12, :HEAD_DIM].set(lb_logits.reshape(4, HEAD_DIM))
    sm = sm.at[8:12, :HEAD_DIM].set(m_lb_logits.reshape(4, HEAD_DIM))
    sv = sv.at[8:12, :HEAD_DIM].set(v_lb_logits.reshape(4, HEAD_DIM))
    sg, sd, snm, snv = _adamw_small(sg, sw, sm, sv)

    def small_out(arr, i, like):
        return arr[i].reshape(like.shape)

    def lb_out(arr):
        return arr[8:12, :HEAD_DIM].reshape(2, 2, HEAD_DIM)

    order = ["w_in", "w_branch_a", "w_branch_b", "w_out", "w_ffn_in", "w_ffn_out", "pool_w"]
    params = dict(w_in=(w_in, m_w_in, v_w_in), w_branch_a=(w_branch_a, m_w_branch_a, v_w_branch_a),
                  w_branch_b=(w_branch_b, m_w_branch_b, v_w_branch_b), w_out=(w_out, m_w_out, v_w_out),
                  w_ffn_in=(w_ffn_in, m_w_ffn_in, v_w_ffn_in), w_ffn_out=(w_ffn_out, m_w_ffn_out, v_w_ffn_out),
                  pool_w=(pool_w, m_pool_w, v_pool_w))
    res = {}
    for nme, r, dests in zip(order, [recv_w_in] + [[r] for r in recv_late], [w_in_dests] + [None] * 6):
        w_, m_, v_ = params[nme]
        shape2 = r[0].shape[1:]
        outs = _adamw_reduce(r, dests, w_.reshape(shape2), m_.reshape(shape2), v_.reshape(shape2), "adamw_" + nme)
        res[nme] = [o_.reshape(w_.shape) for o_ in outs]

    def pick(k):
        small_src = [sg, sd, snm, snv][k]
        return [small_out(small_src, 0, g_mix), res["w_in"][k], lb_out(small_src), small_out(small_src, 1, hgrn_norm_g),
                res["pool_w"][k], small_out(small_src, 2, pool_scale), res["w_branch_a"][k], res["w_branch_b"][k],
                res["w_out"][k], small_out(small_src, 3, g_ffn), res["w_ffn_in"][k], res["w_ffn_out"][k],
                small_out(small_src, 4, g_final)]

    return (loss, grad_x, *pick(0), *pick(1), *pick(2), *pick(3))
```

```python
import functools

import jax
import jax.numpy as jnp
from jax import lax
from jax.experimental import pallas as pl
from jax.experimental.pallas import tpu as pltpu

F32 = jnp.float32
MX = jnp.bfloat16

D_MODEL = 1024
N_HEADS = 8
HEAD_DIM = 128
CHUNK = 16
POOL_WINDOWS = (2, 4, 8, 16)
POOL_DIM = 256
FF_BLOCK = 704
N_DEV = 8
RMS_EPS = 1e-6
ADAM_LR, ADAM_B1, ADAM_B2, ADAM_EPS, ADAM_WD, ADAM_STEP = 0.001, 0.9, 0.999, 1e-08, 0.01, 10
VMEM_LIMIT = 56 * 1024 * 1024
BIG_TOKEN_TILE = 1024
ANY = pl.BlockSpec(memory_space=pl.ANY)


def _params(sem=None):
    return pltpu.CompilerParams(dimension_semantics=sem, vmem_limit_bytes=VMEM_LIMIT)


def _dot(a, b):
    return lax.dot_general(a.astype(MX), b.astype(MX), (((1,), (0,)), ((), ())), preferred_element_type=F32)


def _dot_nt(a, b):
    return lax.dot_general(a.astype(MX), b.astype(MX), (((1,), (1,)), ((), ())), preferred_element_type=F32)


def _dot_tn(a, b):
    return lax.dot_general(a.astype(MX), b.astype(MX), (((0,), (0,)), ((), ())), preferred_element_type=F32)


def _sig(x):
    return 1.0 / (1.0 + jnp.exp(-x))


def _fold8(v):
    return v.reshape(v.shape[0] // 8, 8, v.shape[1]).sum(axis=0)


def _shift_rows(x, s):
    n = x.shape[0]
    row = lax.broadcasted_iota(jnp.int32, x.shape, 0)
    if s > 0:
        return jnp.where(row >= s, pltpu.roll(x, s, 0), 0.0)
    return jnp.where(row < n + s, pltpu.roll(x, n + s, 0), 0.0)


def _cumsum_rows(x, rev):
    s = 1
    while s < x.shape[0]:
        x = x + _shift_rows(x, -s if rev else s)
        s *= 2
    return x


def _gather_order():
    x, y, c = _mesh_pos()
    near, far = [(1 - x, y), (x, 1 - y)], (1 - x, 1 - y)
    return ([(x, y, c), (x, y, 1 - c)] + [(*chip, c) for chip in near] + [(*chip, 1 - c) for chip in near]
            + [(*far, c), (*far, 1 - c)])


def _rms_inproj(x2, g_mix, w_shard, lb_shard):
    T = x2.shape[0]
    tm = min(BIG_TOKEN_TILE, T)
    nT = T // tm
    block_of_step = jnp.stack([4 * px + 2 * py + pc for px, py, pc in _gather_order()]).astype(jnp.int32)

    def body(order_ref, x_ref, g_ref, w_ref, lb_ref, proj_ref, u_ref, wg_ref, lbg_ref,
             u_sc, w_sc, load_sem, send_sems, recv_sems, local_sem, *lb_sems):
        k, i = pl.program_id(0), pl.program_id(1)
        order = _gather_order()
        me, sibling = order[0], order[1]

        def slot(dev):
            return wg_ref.at[4 * dev[0] + 2 * dev[1] + dev[2]]

        def copy(n, block, to, src=None):
            return pltpu.make_async_remote_copy(
                src_ref=slot(block) if src is None else src, dst_ref=slot(block),
                send_sem=send_sems.at[n], recv_sem=recv_sems.at[n], device_id=to, device_id_type=pl.DeviceIdType.MESH)

        mine = pltpu.make_async_copy(w_ref, slot(me), local_sem)
        to_sibling = copy(0, me, sibling, src=w_ref)
        to_near = [copy(1, me, order[2], src=w_ref), copy(2, me, order[3], src=w_ref)]
        to_far = copy(5, me, order[6], src=w_ref)
        passed = {2: copy(3, order[2], sibling), 3: copy(4, order[3], sibling), 6: copy(6, order[6], sibling)}
        lb_gather = _Exchange([lb_ref], [lbg_ref], lb_sems, gather=True)

        @pl.when((k == 0) & (i == 0))
        def _():
            for cp in [mine, to_sibling] + to_near:
                cp.start()
            lb_gather.start()

        for s, block in enumerate(order):
            @pl.when((k == s) & (i == 0))
            def _(s=s, block=block):
                if s == 0:
                    mine.wait()
                else:
                    copy(s - 1, block, me).wait_recv()
                if s in passed:
                    passed[s].start()
                if s == 3:
                    for cp in to_near:
                        cp.wait_send()
                    to_far.start()
                load = pltpu.make_async_copy(slot(block), w_sc, load_sem)
                load.start()
                load.wait()

        rows = pl.ds(pl.multiple_of(i * tm, tm), tm)

        @pl.when(k == 0)
        def _():
            x = x_ref[...]
            r = lax.rsqrt(jnp.mean(x * x, axis=-1, keepdims=True) + RMS_EPS)
            u = (x * r * g_ref[...]).astype(MX)
            u_sc[rows, :] = u
            u_ref[...] = u

        proj_ref[...] = jnp.dot(u_sc[rows, :], w_sc[...], preferred_element_type=F32)

        @pl.when((k == N_DEV - 1) & (i == nT - 1))
        def _():
            for cp in [to_sibling, to_far] + list(passed.values()):
                cp.wait_send()
            lb_gather.wait()

    def tile_once(k, i, order_ref):
        return (jnp.where(k == 0, i, nT - 1), 0)

    grid_spec = pltpu.PrefetchScalarGridSpec(
        num_scalar_prefetch=1, grid=(N_DEV, nT),
        in_specs=[pl.BlockSpec((tm, D_MODEL), tile_once),
                  pl.BlockSpec((1, D_MODEL), lambda k, i, order_ref: (0, 0)), ANY, ANY],
        out_specs=[pl.BlockSpec((None, tm, D_MODEL), lambda k, i, order_ref: (order_ref[k], i, 0)),
                   pl.BlockSpec((tm, D_MODEL), tile_once), ANY, ANY],
        scratch_shapes=[pltpu.VMEM((T, D_MODEL), MX), pltpu.VMEM((D_MODEL, D_MODEL), MX),
                        pltpu.SemaphoreType.DMA(()), pltpu.SemaphoreType.DMA((7,)), pltpu.SemaphoreType.DMA((7,)),
                        pltpu.SemaphoreType.DMA(())] + _Exchange.scratch(1))
    return pl.pallas_call(
        body, name="rms_inproj", grid_spec=grid_spec,
        out_shape=[jax.ShapeDtypeStruct((N_DEV, T, D_MODEL), F32), jax.ShapeDtypeStruct((T, D_MODEL), MX),
                   jax.ShapeDtypeStruct((N_DEV, D_MODEL, D_MODEL), MX),
                   jax.ShapeDtypeStruct((N_DEV,) + lb_shard.shape, lb_shard.dtype)],
        compiler_params=_params(("arbitrary", "arbitrary")),
    )(block_of_step, x2, g_mix, w_shard, lb_shard)


HBLK = 128


def _seg_cumsum(x, rev):
    n = x.shape[0]
    pos = lax.broadcasted_iota(jnp.int32, x.shape, 0) & (CHUNK - 1)
    s = 1
    while s < CHUNK:
        if rev:
            x = x + jnp.where(pos < CHUNK - s, pltpu.roll(x, n - s, 0), 0.0)
        else:
            x = x + jnp.where(pos >= s, pltpu.roll(x, s, 0), 0.0)
        s *= 2
    return x


def _block_gates(q_r, f_r, lb_row, rev):
    sq = _sig(q_r)
    q = q_r * sq
    sg = _sig(f_r)
    f = lb_row + (1.0 - lb_row) * sg
    k = 1.0 - f
    lf = jnp.log(f)
    pre = _seg_cumsum(lf, False)
    suf = _seg_cumsum(lf, True)
    tot = pre + suf - lf
    b = suf if rev else pre
    eb = jnp.exp(b)
    enb = jnp.exp(-b)
    eend = jnp.exp(tot - b)
    return dict(sq=sq, sg=sg, f=f, eb=eb, enb=enb, eend=eend, dec=jnp.exp(tot),
                P=q * eb, Kt=k * enb, Ke=k * eend)


def _block_mask(rev, transposed=False):
    ri = lax.broadcasted_iota(jnp.int32, (HBLK, HBLK), 0)
    ci = lax.broadcasted_iota(jnp.int32, (HBLK, HBLK), 1)
    same = (ri // CHUNK) == (ci // CHUNK)
    return same & ((ci >= ri) if rev != transposed else (ci <= ri))


def _chunk_rows(c):
    return pl.ds(pl.multiple_of(c * CHUNK, CHUNK), CHUNK)


def _chunk_outer_products(a, b, out_sc, kb):
    a, b = a.astype(MX), b.astype(MX)
    for u in range(HBLK // CHUNK):
        r = slice(u * CHUNK, (u + 1) * CHUNK)
        out_sc[kb * (HBLK // CHUNK) + u] = _dot_tn(a[r, :], b[r, :])


UNIT_BLOCKS = 2
UNIT_CHUNKS = UNIT_BLOCKS * (HBLK // CHUNK)


def _unit_blocks(u, fn, carry):
    for b in range(UNIT_BLOCKS):
        carry = fn(u * UNIT_BLOCKS + b, carry)
    return carry


def _unit_chunks(u, descending, fn, carry):
    for j in range(UNIT_CHUNKS):
        carry = fn(u * UNIT_CHUNKS + (UNIT_CHUNKS - 1 - j if descending else j), carry)
    return carry


def _pipelined(n_units, descending, first, c1, second, c2):
    def unit(t):
        return n_units - 1 - t if descending else t

    c1 = first(unit(0), c1)

    def both(t, cs):
        return first(unit(t + 1), cs[0]), second(unit(t), cs[1])

    c1, c2 = lax.fori_loop(0, n_units - 1, both, (c1, c2))
    return c1, second(unit(n_units - 1), c2)


def _lower_bounds(lbl_ref):
    return _sig(lbl_ref[0:1, :] - lbl_ref[1:2, :]), _sig(lbl_ref[2:3, :] - lbl_ref[3:4, :])


def _hgrn_fwd(proj, lbl, norm_g, B, L, shards):
    nC = L // CHUNK
    n = len(shards)

    def body(*refs):
        q_ref, ff_ref, fb_ref, i_ref, og_ref, lbl_ref, ng_ref = refs[:7]
        o_ref, ya_ref = refs[7 + n:9 + n]
        o_sc, p_sc, dec_sc, upd_sc = refs[9 + 2 * n:13 + 2 * n]
        step_id = pl.program_id(0) * N_HEADS + pl.program_id(1)
        gather = _Exchange(refs[7:7 + n], refs[9 + n:9 + 2 * n], refs[13 + 2 * n:], gather=True)

        @pl.when(step_id == 0)
        def _():
            gather.start()

        lb_f, lb_b = _lower_bounds(lbl_ref)
        o_sc[...] = jnp.zeros_like(o_sc)

        def run_dir(f_ref, lb_row, rev):
            def block(kb, carry):
                rows = pl.ds(pl.multiple_of(kb * HBLK, HBLK), HBLK)
                g = _block_gates(q_ref[rows, :], f_ref[rows, :], lb_row, rev)
                v = i_ref[rows, :]
                a = jnp.where(_block_mask(rev), _dot_nt(g["P"], g["Kt"]), 0.0)
                o_sc[rows, :] += _dot(a, v)
                p_sc[rows, :] = g["P"].astype(MX)
                dec_sc[rows, :] = g["dec"]
                _chunk_outer_products(v, g["Ke"], upd_sc, kb)
                return carry

            def step(c, st):
                rows = _chunk_rows(c)
                o_sc[rows, :] += _dot_nt(p_sc[rows, :], st)
                dec = dec_sc[pl.ds(pl.multiple_of(c * CHUNK, CHUNK), 1), :]
                return st * dec + upd_sc[c]

            _pipelined(nC // UNIT_CHUNKS, rev,
                       lambda u, c: _unit_blocks(u, block, c), 0,
                       lambda u, st: _unit_chunks(u, rev, step, st), jnp.zeros((HEAD_DIM, HEAD_DIM), F32))

        run_dir(ff_ref, lb_f, False)
        run_dir(fb_ref, lb_b, True)
        o = o_sc[...]
        o_ref[...] = o
        on = o * lax.rsqrt(jnp.mean(o * o, axis=-1, keepdims=True) + RMS_EPS)
        og = og_ref[...]
        ya_ref[...] = ((on * ng_ref[...]) * (og * _sig(og))).astype(MX)

        @pl.when(step_id == B * N_HEADS - 1)
        def _():
            gather.wait()

    def blk(s):
        return pl.BlockSpec((None, None, L, HEAD_DIM), lambda b, h, s=s: (s, b, 0, h))

    out_blk = pl.BlockSpec((None, L, HEAD_DIM), lambda b, h: (b, 0, h))
    outs = pl.pallas_call(
        body, name="hgrn_fwd", grid=(B, N_HEADS),
        in_specs=[blk(0), blk(1), blk(2), blk(3), blk(4),
                  pl.BlockSpec((4, HEAD_DIM), lambda b, h: (0, h)),
                  pl.BlockSpec((1, HEAD_DIM), lambda b, h: (0, h))] + [ANY] * n,
        out_specs=[out_blk, out_blk] + [ANY] * n,
        out_shape=[jax.ShapeDtypeStruct((B, L, D_MODEL), F32), jax.ShapeDtypeStruct((B, L, D_MODEL), MX)]
                  + _Exchange.out_shapes(shards, True),
        scratch_shapes=[pltpu.VMEM((L, HEAD_DIM), F32), pltpu.VMEM((L, HEAD_DIM), MX),
                        pltpu.VMEM((L, HEAD_DIM), F32), pltpu.VMEM((nC, HEAD_DIM, HEAD_DIM), F32)]
                       + _Exchange.scratch(n),
        compiler_params=_params(("arbitrary", "arbitrary")),
    )(proj, proj, proj, proj, proj, lbl, norm_g, *shards)
    return outs[0], outs[1], outs[2:]


POOL_PAD = 8


def _pool_window(p, ext_sc, half, adjoint):
    L = p.shape[0]
    n = L + 2 * POOL_PAD
    ext_sc[0:POOL_PAD, :] = jnp.zeros((POOL_PAD, p.shape[1]), F32)
    ext_sc[POOL_PAD + L:n, :] = jnp.zeros((POOL_PAD, p.shape[1]), F32)
    ext_sc[POOL_PAD:POOL_PAD + L, :] = p
    x = ext_sc[...]
    s = x + pltpu.roll(x, 1 if adjoint else n - 1, 0)
    w = 1
    while w < half:
        s = pltpu.roll(s, w, 0) + pltpu.roll(s, n - w, 0)
        w *= 2
    ext_sc[...] = s
    return ext_sc[POOL_PAD:POOL_PAD + L, :]


def _pool_count(L, half):
    t = lax.broadcasted_iota(jnp.int32, (L, 1), 0)
    lo = jnp.clip(t - half + 1, 0, L)
    hi = jnp.clip(t + half + 1, 0, L)
    return (hi - lo).astype(F32)


def _pool_fwd(proj, pool_w_full, pool_scale, B, L):
    def body(p_ref, w_ref, s_ref, yb_ref, ext_sc):
        for g, win in enumerate(POOL_WINDOWS):
            cols = slice(g * POOL_DIM, (g + 1) * POOL_DIM)
            p = p_ref[:, cols]
            y = _pool_window(p, ext_sc, win // 2, False) / _pool_count(L, win // 2) - p
            yb_ref[:, cols] = (_dot(y, w_ref[g]) * s_ref[:, cols]).astype(MX)

    return pl.pallas_call(
        body, name="pool_fwd", grid=(B,),
        in_specs=[pl.BlockSpec((None, None, L, D_MODEL), lambda b: (5, b, 0, 0)),
                  pl.BlockSpec((4, POOL_DIM, POOL_DIM), lambda b: (0, 0, 0)),
                  pl.BlockSpec((1, D_MODEL), lambda b: (0, 0))],
        out_specs=pl.BlockSpec((None, L, D_MODEL), lambda b: (b, 0, 0)),
        out_shape=jax.ShapeDtypeStruct((B, L, D_MODEL), MX),
        scratch_shapes=[pltpu.VMEM((L + 2 * POOL_PAD, POOL_DIM), F32)],
        compiler_params=_params(("parallel",)),
    )(proj, pool_w_full, pool_scale)


def _merge_out(x2, proj, ya, yb, wa, wb, wo):
    T = x2.shape[0]
    tm = min(512, T)

    def body(x_ref, ga_ref, gb_ref, ya_ref, yb_ref, wa_ref, wb_ref, wo_ref, za_ref, zb_ref, mg_ref, h_ref):
        za = jnp.dot(ya_ref[...], wa_ref[...], preferred_element_type=F32)
        zb = jnp.dot(yb_ref[...], wb_ref[...], preferred_element_type=F32)
        mg = (_sig(ga_ref[...]) * za + _sig(gb_ref[...]) * zb).astype(MX)
        za_ref[...] = za
        zb_ref[...] = zb
        mg_ref[...] = mg
        h_ref[...] = x_ref[...] + jnp.dot(mg, wo_ref[...], preferred_element_type=F32)

    tile = pl.BlockSpec((tm, D_MODEL), lambda i: (i, 0))
    wspec = pl.BlockSpec((D_MODEL, D_MODEL), lambda i: (0, 0))
    return pl.pallas_call(
        body, name="merge_out", grid=(T // tm,),
        in_specs=[tile,
                  pl.BlockSpec((None, tm, D_MODEL), lambda i: (6, i, 0)),
                  pl.BlockSpec((None, tm, D_MODEL), lambda i: (7, i, 0)),
                  tile, tile, wspec, wspec, wspec],
        out_specs=[tile, tile, tile, tile],
        out_shape=[jax.ShapeDtypeStruct((T, D_MODEL), F32), jax.ShapeDtypeStruct((T, D_MODEL), F32),
                   jax.ShapeDtypeStruct((T, D_MODEL), MX), jax.ShapeDtypeStruct((T, D_MODEL), F32)],
        compiler_params=_params(("parallel",)),
    )(x2, proj, proj, ya, yb, wa, wb, wo)


def _ffn_fwd_loss(h, tgt, g_ffn, g_final, wfi_g, wfo_g):
    T = h.shape[0]
    tm = min(512, T)
    nT = T // tm

    def body(h_ref, t_ref, gf_ref, gl_ref, wg_ref, wu_ref, wo_ref,
             gate_ref, up_ref, hid_ref, u2_ref, dh2_ref, dh2b_ref, loss_ref, dgl_ref, u2_sc, acc_sc):
        i, j = pl.program_id(0), pl.program_id(1)

        @pl.when(j == 0)
        def _():
            hh = h_ref[...]
            r = lax.rsqrt(jnp.mean(hh * hh, axis=-1, keepdims=True) + RMS_EPS)
            u2 = (hh * r * gf_ref[...]).astype(MX)
            u2_sc[...] = u2
            u2_ref[...] = u2
            acc_sc[...] = jnp.zeros_like(acc_sc)

        @pl.when((i == 0) & (j == 0))
        def _():
            dgl_ref[...] = jnp.zeros_like(dgl_ref)

        gate = jnp.dot(u2_sc[...], wg_ref[...], preferred_element_type=F32)
        up = jnp.dot(u2_sc[...], wu_ref[...], preferred_element_type=F32)
        hid = ((gate * _sig(gate)) * up).astype(MX)
        gate_ref[...] = gate
        up_ref[...] = up
        hid_ref[...] = hid
        acc_sc[...] += jnp.dot(hid, wo_ref[...], preferred_element_type=F32)

        @pl.when(j == 3)
        def _():
            h2 = h_ref[...] + acc_sc[...]
            r = lax.rsqrt(jnp.mean(h2 * h2, axis=-1, keepdims=True) + RMS_EPS)
            hn = h2 * r
            gl = gl_ref[...]
            err = hn * gl - t_ref[...]
            tok = jnp.mean(err * err, axis=-1, keepdims=True)
            loss_ref[...] = jnp.full(loss_ref.shape, 0.5 * jnp.sum(tok), F32)
            dy = err * (1.0 / D_MODEL)
            dgl_ref[...] += _fold8(dy * hn)
            a = dy * gl
            dh2 = r * a - hn * (r * jnp.mean(a * hn, axis=-1, keepdims=True))
            dh2_ref[...] = dh2
            dh2b_ref[...] = dh2.astype(MX)

    tile = pl.BlockSpec((tm, D_MODEL), lambda i, j: (i, 0))
    vec = pl.BlockSpec((1, D_MODEL), lambda i, j: (0, 0))
    ftile = pl.BlockSpec((None, tm, FF_BLOCK), lambda i, j: (j, i, 0))
    return pl.pallas_call(
        body, name="ffn_fwd_loss", grid=(nT, 4),
        in_specs=[tile, tile, vec, vec,
                  pl.BlockSpec((None, D_MODEL, FF_BLOCK), lambda i, j: (j, 0, 0)),
                  pl.BlockSpec((None, D_MODEL, FF_BLOCK), lambda i, j: (j + 4, 0, 0)),
                  pl.BlockSpec((None, FF_BLOCK, D_MODEL), lambda i, j: (j, 0, 0))],
        out_specs=[ftile, ftile, ftile, tile, tile, tile,
                   pl.BlockSpec((None, 8, 128), lambda i, j: (i, 0, 0)),
                   pl.BlockSpec((8, D_MODEL), lambda i, j: (0, 0))],
        out_shape=[jax.ShapeDtypeStruct((4, T, FF_BLOCK), F32), jax.ShapeDtypeStruct((4, T, FF_BLOCK), F32),
                   jax.ShapeDtypeStruct((4, T, FF_BLOCK), MX), jax.ShapeDtypeStruct((T, D_MODEL), MX),
                   jax.ShapeDtypeStruct((T, D_MODEL), F32), jax.ShapeDtypeStruct((T, D_MODEL), MX),
                   jax.ShapeDtypeStruct((nT, 8, 128), F32), jax.ShapeDtypeStruct((8, D_MODEL), F32)],
        scratch_shapes=[pltpu.VMEM((tm, D_MODEL), MX), pltpu.VMEM((tm, D_MODEL), F32)],
        compiler_params=_params(("arbitrary", "arbitrary")),
    )(h, tgt, g_ffn, g_final, wfi_g, wfi_g, wfo_g)


def _ffn_bwd(h, dh2, dh2b, gate, up, g_ffn, wfi_g, wfo_g):
    T = h.shape[0]
    tm = min(512, T)

    def body(h_ref, dh2_ref, dh2b_ref, gate_ref, up_ref, gf_ref, wg_ref, wu_ref, wo_ref,
             dgu_ref, dh_ref, dhb_ref, dgf_ref, acc_sc):
        i, j = pl.program_id(0), pl.program_id(1)

        @pl.when(j == 0)
        def _():
            acc_sc[...] = jnp.zeros_like(acc_sc)

        @pl.when((i == 0) & (j == 0))
        def _():
            dgf_ref[...] = jnp.zeros_like(dgf_ref)

        dhid = _dot_nt(dh2b_ref[...], wo_ref[...])
        gate, up = gate_ref[...], up_ref[...]
        sg = _sig(gate)
        dgate = (dhid * up * (sg * (1.0 + gate * (1.0 - sg)))).astype(MX)
        dup = (dhid * (gate * sg)).astype(MX)
        dgu_ref[0] = dgate
        dgu_ref[1] = dup
        acc_sc[...] += _dot_nt(dgate, wg_ref[...]) + _dot_nt(dup, wu_ref[...])

        @pl.when(j == 3)
        def _():
            hh = h_ref[...]
            r = lax.rsqrt(jnp.mean(hh * hh, axis=-1, keepdims=True) + RMS_EPS)
            hn = hh * r
            du2 = acc_sc[...]
            dgf_ref[...] += _fold8(du2 * hn)
            a = du2 * gf_ref[...]
            dh = dh2_ref[...] + r * a - hn * (r * jnp.mean(a * hn, axis=-1, keepdims=True))
            dh_ref[...] = dh
            dhb_ref[...] = dh.astype(MX)

    tile = pl.BlockSpec((tm, D_MODEL), lambda i, j: (i, 0))
    ftile = pl.BlockSpec((None, tm, FF_BLOCK), lambda i, j: (j, i, 0))
    return pl.pallas_call(
        body, name="ffn_bwd", grid=(T // tm, 4),
        in_specs=[tile, tile, tile, ftile, ftile,
                  pl.BlockSpec((1, D_MODEL), lambda i, j: (0, 0)),
                  pl.BlockSpec((None, D_MODEL, FF_BLOCK), lambda i, j: (j, 0, 0)),
                  pl.BlockSpec((None, D_MODEL, FF_BLOCK), lambda i, j: (j + 4, 0, 0)),
                  pl.BlockSpec((None, FF_BLOCK, D_MODEL), lambda i, j: (j, 0, 0))],
        out_specs=[pl.BlockSpec((2, None, tm, FF_BLOCK), lambda i, j: (0, j, i, 0)),
                   tile, tile, pl.BlockSpec((8, D_MODEL), lambda i, j: (0, 0))],
        out_shape=[jax.ShapeDtypeStruct((2, 4, T, FF_BLOCK), MX),
                   jax.ShapeDtypeStruct((T, D_MODEL), F32), jax.ShapeDtypeStruct((T, D_MODEL), MX),
                   jax.ShapeDtypeStruct((8, D_MODEL), F32)],
        scratch_shapes=[pltpu.VMEM((tm, D_MODEL), F32)],
        compiler_params=_params(("arbitrary", "arbitrary")),
    )(h, dh2, dh2b, gate, up, g_ffn, wfi_g, wfi_g, wfo_g)


def _merge_bwd(dhb, proj, za, zb, wa, wb, wo):
    T = dhb.shape[0]
    tm = min(512, T)

    def body(dh_ref, ga_ref, gb_ref, za_ref, zb_ref, wa_ref, wb_ref, wo_ref,
             dza_ref, dzb_ref, dgab_ref, dya_ref, dyb_ref):
        dm = _dot_nt(dh_ref[...], wo_ref[...])
        sa, sb = _sig(ga_ref[...]), _sig(gb_ref[...])
        dza = (dm * sa).astype(MX)
        dzb = (dm * sb).astype(MX)
        dza_ref[...] = dza
        dzb_ref[...] = dzb
        dgab_ref[0] = (dm * za_ref[...] * (sa * (1.0 - sa))).astype(MX)
        dgab_ref[1] = (dm * zb_ref[...] * (sb * (1.0 - sb))).astype(MX)
        dya_ref[...] = _dot_nt(dza, wa_ref[...])
        dyb_ref[...] = _dot_nt(dzb, wb_ref[...])

    tile = pl.BlockSpec((tm, D_MODEL), lambda i: (i, 0))
    wspec = pl.BlockSpec((D_MODEL, D_MODEL), lambda i: (0, 0))
    return pl.pallas_call(
        body, name="merge_bwd", grid=(T // tm,),
        in_specs=[tile,
                  pl.BlockSpec((None, tm, D_MODEL), lambda i: (6, i, 0)),
                  pl.BlockSpec((None, tm, D_MODEL), lambda i: (7, i, 0)),
                  tile, tile, wspec, wspec, wspec],
        out_specs=[tile, tile, pl.BlockSpec((2, tm, D_MODEL), lambda i: (0, i, 0)), tile, tile],
        out_shape=[jax.ShapeDtypeStruct((T, D_MODEL), MX), jax.ShapeDtypeStruct((T, D_MODEL), MX),
                   jax.ShapeDtypeStruct((2, T, D_MODEL), MX),
                   jax.ShapeDtypeStruct((T, D_MODEL), F32), jax.ShapeDtypeStruct((T, D_MODEL), F32)],
        compiler_params=_params(("parallel",)),
    )(dhb, proj, proj, za, zb, wa, wb, wo)


def _pool_bwd(proj, dyb, pool_w_full, pool_scale, B, L):
    def body(p_ref, dy_ref, w_ref, s_ref, dp_ref, dw_ref, ds_ref, ext_sc):
        for g, win in enumerate(POOL_WINDOWS):
            cols = slice(g * POOL_DIM, (g + 1) * POOL_DIM)
            p = p_ref[:, cols]
            cnt = _pool_count(L, win // 2)
            y = _pool_window(p, ext_sc, win // 2, False) / cnt - p
            z = _dot(y, w_ref[g])
            dyb_g = dy_ref[:, cols]
            ds_ref[:, cols] = jnp.sum(dyb_g * z, axis=0, keepdims=True)
            dz = dyb_g * s_ref[:, cols]
            dw_ref[g] = _dot_tn(y, dz)
            dy = _dot_nt(dz, w_ref[g])
            dp_ref[:, cols] = (_pool_window(dy / cnt, ext_sc, win // 2, True) - dy).astype(MX)

    seq = pl.BlockSpec((None, L, D_MODEL), lambda b: (b, 0, 0))
    return pl.pallas_call(
        body, name="pool_bwd", grid=(B,),
        in_specs=[pl.BlockSpec((None, None, L, D_MODEL), lambda b: (5, b, 0, 0)), seq,
                  pl.BlockSpec((4, POOL_DIM, POOL_DIM), lambda b: (0, 0, 0)),
                  pl.BlockSpec((1, D_MODEL), lambda b: (0, 0))],
        out_specs=[seq, pl.BlockSpec((None, 4, POOL_DIM, POOL_DIM), lambda b: (b, 0, 0, 0)),
                   pl.BlockSpec((None, 1, D_MODEL), lambda b: (b, 0, 0))],
        out_shape=[jax.ShapeDtypeStruct((B, L, D_MODEL), MX),
                   jax.ShapeDtypeStruct((B, 4, POOL_DIM, POOL_DIM), F32),
                   jax.ShapeDtypeStruct((B, 1, D_MODEL), F32)],
        scratch_shapes=[pltpu.VMEM((L + 2 * POOL_PAD, POOL_DIM), F32)],
        compiler_params=_params(("parallel",)),
    )(proj, dyb, pool_w_full, pool_scale)


def _hgrn_bwd(proj, o, dya, lbl, norm_g, B, L, grads, dests):
    nC = L // CHUNK
    n = len(grads)

    def body(*refs):
        q_ref, ff_ref, fb_ref, i_ref, og_ref, o_ref, dy_ref, lbl_ref, ng_ref = refs[:9]
        dp_ref, dlb_ref, dng_ref = refs[9 + n:12 + n]
        (do_sc, dq_sc, dv_sc, dP_sc, dKt_sc, dKe_sc, dbl_sc, dec_sc, ke_sc, ck_sc, upd_sc,
         dupd_sc) = refs[12 + 2 * n:24 + 2 * n]
        step_id = pl.program_id(0) * N_HEADS + pl.program_id(1)
        scatter = _Exchange(refs[9:9 + n], refs[12 + n:12 + 2 * n], refs[24 + 2 * n:], gather=False, dests=dests)

        @pl.when(step_id == 0)
        def _():
            scatter.start()

        lb_f, lb_b = _lower_bounds(lbl_ref)
        o_ = o_ref[...]
        r = lax.rsqrt(jnp.mean(o_ * o_, axis=-1, keepdims=True) + RMS_EPS)
        on = o_ * r
        og = og_ref[...]
        sog = _sig(og)
        dy = dy_ref[...]
        ng = ng_ref[...]
        dp_ref[4] = (dy * (on * ng) * (sog * (1.0 + og * (1.0 - sog)))).astype(MX)
        dn = dy * (og * sog)
        dng_ref[...] = jnp.sum(dn * on, axis=0, keepdims=True)
        don = dn * ng
        do_sc[...] = r * don - on * (r * jnp.mean(don * on, axis=-1, keepdims=True))

        dq_sc[...] = jnp.zeros_like(dq_sc)
        dv_sc[...] = jnp.zeros_like(dv_sc)

        def run_dir(f_ref, lb_row, rev, slot):
            def block(kb, carry):
                rows = pl.ds(pl.multiple_of(kb * HBLK, HBLK), HBLK)
                g = _block_gates(q_ref[rows, :], f_ref[rows, :], lb_row, rev)
                v = i_ref[rows, :]
                do = do_sc[rows, :]
                m, mt = _block_mask(rev), _block_mask(rev, True)
                at = jnp.where(mt, _dot_nt(g["Kt"], g["P"]), 0.0)
                da = jnp.where(m, _dot_nt(do, v), 0.0)
                dat = jnp.where(mt, _dot_nt(v, do), 0.0)
                dv_sc[rows, :] += _dot(at, do)
                dP_sc[rows, :] = _dot(da, g["Kt"])
                dKt_sc[rows, :] = _dot(dat, g["P"])
                ke_sc[rows, :] = g["Ke"].astype(MX)
                dec_sc[rows, :] = g["dec"]
                _chunk_outer_products(v, g["Ke"], upd_sc, kb)
                _chunk_outer_products(do, g["P"], dupd_sc, kb)
                return carry

            def fstep(c, st):
                rows = _chunk_rows(c)
                ck_sc[c] = st.astype(MX)
                dP_sc[rows, :] += _dot(do_sc[rows, :], st)
                dec = dec_sc[pl.ds(pl.multiple_of(c * CHUNK, CHUNK), 1), :]
                return st * dec + upd_sc[c]

            n_units = nC // UNIT_CHUNKS
            zero_state = jnp.zeros((HEAD_DIM, HEAD_DIM), F32)
            _pipelined(n_units, rev,
                       lambda u, c: _unit_blocks(u, block, c), 0,
                       lambda u, st: _unit_chunks(u, rev, fstep, st), zero_state)

            def bstep(c, dst):
                rows = _chunk_rows(c)
                dec = dec_sc[pl.ds(pl.multiple_of(c * CHUNK, CHUNK), 1), :]
                dKe_sc[rows, :] = _dot(i_ref[rows, :], dst)
                dv_sc[rows, :] += _dot_nt(ke_sc[rows, :], dst)
                dbl = dec * jnp.sum(dst * ck_sc[c].astype(F32), axis=0, keepdims=True)
                dbl_sc[rows, :] = jnp.broadcast_to(dbl, (CHUNK, HEAD_DIM))
                return dst * dec + dupd_sc[c]

            def finish(kb, acc):
                rows = pl.ds(pl.multiple_of(kb * HBLK, HBLK), HBLK)
                q_r = q_ref[rows, :]
                g = _block_gates(q_r, f_ref[rows, :], lb_row, rev)
                dP, dkt, dke = dP_sc[rows, :], dKt_sc[rows, :], dKe_sc[rows, :]
                e = dke * g["Ke"]
                dlf = (_seg_cumsum(dP * g["P"] - dkt * g["Kt"], not rev) + _seg_cumsum(e, rev) - e
                       + dbl_sc[rows, :])
                df = dlf / g["f"] - (dkt * g["enb"] + dke * g["eend"])
                dp_ref[slot, rows, :] = (df * (1.0 - lb_row) * (g["sg"] * (1.0 - g["sg"]))).astype(MX)
                dq_sc[rows, :] += (dP * g["eb"]) * (g["sq"] * (1.0 + q_r * (1.0 - g["sq"])))
                return acc + jnp.sum(df * (1.0 - g["sg"]), axis=0, keepdims=True)

            _, dlb = _pipelined(n_units, not rev,
                                lambda u, dst: _unit_chunks(u, not rev, bstep, dst), zero_state,
                                lambda u, acc: _unit_blocks(u, finish, acc), jnp.zeros((1, HEAD_DIM), F32))
            dlb_ref[slot - 1:slot, :] = dlb

        run_dir(ff_ref, lb_f, False, 1)
        run_dir(fb_ref, lb_b, True, 2)
        dp_ref[0] = dq_sc[...].astype(MX)
        dp_ref[3] = dv_sc[...].astype(MX)

        @pl.when(step_id == B * N_HEADS - 1)
        def _():
            scatter.wait()

    def blk(s):
        return pl.BlockSpec((None, None, L, HEAD_DIM), lambda b, h, s=s: (s, b, 0, h))

    seq = pl.BlockSpec((None, L, HEAD_DIM), lambda b, h: (b, 0, h))
    outs = pl.pallas_call(
        body, name="hgrn_bwd", grid=(B, N_HEADS),
        in_specs=[blk(0), blk(1), blk(2), blk(3), blk(4), seq, seq,
                  pl.BlockSpec((4, HEAD_DIM), lambda b, h: (0, h)),
                  pl.BlockSpec((1, HEAD_DIM), lambda b, h: (0, h))] + [ANY] * n,
        out_specs=[pl.BlockSpec((5, None, L, HEAD_DIM), lambda b, h: (0, b, 0, h)),
                   pl.BlockSpec((None, 2, HEAD_DIM), lambda b, h: (b, 0, h)),
                   pl.BlockSpec((None, 1, HEAD_DIM), lambda b, h: (b, 0, h))] + [ANY] * n,
        out_shape=[jax.ShapeDtypeStruct((5, B, L, D_MODEL), MX),
                   jax.ShapeDtypeStruct((B, 2, D_MODEL), F32),
                   jax.ShapeDtypeStruct((B, 1, D_MODEL), F32)] + _Exchange.out_shapes(grads, False),
        scratch_shapes=[pltpu.VMEM((L, HEAD_DIM), F32)] * 8
                       + [pltpu.VMEM((L, HEAD_DIM), MX), pltpu.VMEM((nC, HEAD_DIM, HEAD_DIM), MX),
                          pltpu.VMEM((nC, HEAD_DIM, HEAD_DIM), F32), pltpu.VMEM((nC, HEAD_DIM, HEAD_DIM), F32)]
                       + _Exchange.scratch(n),
        compiler_params=_params(("arbitrary", "arbitrary")),
    )(proj, proj, proj, proj, proj, o, dya, lbl, norm_g, *grads)
    return outs[0], outs[1], outs[2], outs[3:]


def _dproj_select(s, a5_ref, p_ref, g2_ref):
    return jnp.where(s < 5, a5_ref[...], jnp.where(s == 5, p_ref[...], g2_ref[...]))


def _dproj_specs(tm, tile_axis):
    def ix(args):
        return args[tile_axis], args[1 - tile_axis]
    a5 = pl.BlockSpec((None, tm, D_MODEL), lambda *a: (jnp.minimum(ix(a)[1], 4), ix(a)[0], 0))
    p = pl.BlockSpec((tm, D_MODEL), lambda *a: (ix(a)[0], 0))
    g2 = pl.BlockSpec((None, tm, D_MODEL), lambda *a: (jnp.clip(ix(a)[1] - 6, 0, 1), ix(a)[0], 0))
    return [a5, p, g2]


def _inproj_bwd(x2, dh, dproj5, dp, dgab, g_mix, w_in_g, d_win, dests):
    T = x2.shape[0]
    tm = min(512, T)
    nT = T // tm

    def body(a5_ref, p_ref, g2_ref, w_ref, x_ref, dh_ref, g_ref, dwin_ref, dx_ref, dg_ref, recv_ref, acc_sc, *sems):
        i, s = pl.program_id(0), pl.program_id(1)
        scatter = _Exchange([dwin_ref], [recv_ref], sems, gather=False, dests=[dests])

        @pl.when((i == 0) & (s == 0))
        def _():
            scatter.start()

        @pl.when(s == 0)
        def _():
            acc_sc[...] = jnp.zeros_like(acc_sc)

        @pl.when((i == 0) & (s == 0))
        def _():
            dg_ref[...] = jnp.zeros_like(dg_ref)

        acc_sc[...] += _dot_nt(_dproj_select(s, a5_ref, p_ref, g2_ref), w_ref[...])

        @pl.when(s == N_DEV - 1)
        def _():
            x = x_ref[...]
            r = lax.rsqrt(jnp.mean(x * x, axis=-1, keepdims=True) + RMS_EPS)
            xn = x * r
            du = acc_sc[...]
            dg_ref[...] += _fold8(du * xn)
            a = du * g_ref[...]
            dx_ref[...] = dh_ref[...] + r * a - xn * (r * jnp.mean(a * xn, axis=-1, keepdims=True))

        @pl.when((i == nT - 1) & (s == N_DEV - 1))
        def _():
            scatter.wait()

    tile = pl.BlockSpec((tm, D_MODEL), lambda i, s: (i, 0))
    return pl.pallas_call(
        body, name="inproj_bwd", grid=(nT, N_DEV),
        in_specs=_dproj_specs(tm, 0) + [pl.BlockSpec((None, D_MODEL, D_MODEL), lambda i, s: (s, 0, 0)),
                                        tile, tile, pl.BlockSpec((1, D_MODEL), lambda i, s: (0, 0)), ANY],
        out_specs=[tile, pl.BlockSpec((8, D_MODEL), lambda i, s: (0, 0)), ANY],
        out_shape=[jax.ShapeDtypeStruct((T, D_MODEL), F32), jax.ShapeDtypeStruct((8, D_MODEL), F32)]
                  + _Exchange.out_shapes([d_win], False),
        scratch_shapes=[pltpu.VMEM((tm, D_MODEL), F32)] + _Exchange.scratch(1),
        compiler_params=_params(("arbitrary", "arbitrary")),
    )(dproj5, dp, dgab, w_in_g, x2, dh, g_mix, d_win)


def _wgrad(a, g, name):
    Ba, T, K = a.shape
    Bg, _, Nn = g.shape
    nb = max(Ba, Bg)
    tm = min(BIG_TOKEN_TILE, T)
    nt = T // tm

    def body(a_ref, g_ref, out_ref, acc_sc):
        t = pl.program_id(1)

        @pl.when(t == 0)
        def _():
            acc_sc[...] = jnp.zeros_like(acc_sc)

        acc_sc[...] += _dot_tn(a_ref[...], g_ref[...])

        @pl.when(t == nt - 1)
        def _():
            out_ref[...] = acc_sc[...].astype(MX)

    return pl.pallas_call(
        body, name=name, grid=(nb, nt),
        in_specs=[pl.BlockSpec((None, tm, K), lambda s, t: (s if Ba > 1 else 0, t, 0)),
                  pl.BlockSpec((None, tm, Nn), lambda s, t: (s if Bg > 1 else 0, t, 0))],
        out_specs=pl.BlockSpec((None, K, Nn), lambda s, t: (s, 0, 0)),
        out_shape=jax.ShapeDtypeStruct((nb, K, Nn), MX),
        scratch_shapes=[pltpu.VMEM((K, Nn), F32)],
        compiler_params=_params(("parallel", "arbitrary")),
    )(a, g)


def _mesh_pos():
    return lax.axis_index("x"), lax.axis_index("y"), lax.axis_index("c")


def _device_of(p):
    return (p // 4, (p // 2) % 2, p % 2)


class _Exchange:
    def __init__(self, srcs, outs, sems, gather, dests=None):
        send_sems, recv_sems, local_sems = sems
        x, y, c = _mesh_pos()
        me = 4 * x + 2 * y + c
        self.sends, self.arrivals, self.mine = [], [], []
        for a, (src, out) in enumerate(zip(srcs, outs)):
            lo, hi = dests[a] if dests else (0, N_DEV)

            def piece(p, src=src, lo=lo, hi=hi):
                return src if gather else src.at[jnp.clip(p - lo, 0, hi - lo - 1)]

            def served(p, lo=lo, hi=hi):
                return None if (lo, hi) == (0, N_DEV) else (p >= lo) & (p < hi)

            self.mine.append((pltpu.make_async_copy(piece(me), out.at[me], local_sems.at[a]), served(me)))
            for j in range(1, N_DEV):
                to, frm = (me + j) % N_DEV, (me + N_DEV - j) % N_DEV
                pair = dict(send_sem=send_sems.at[7 * a + j - 1], recv_sem=recv_sems.at[7 * a + j - 1],
                            device_id_type=pl.DeviceIdType.MESH)
                self.sends.append((pltpu.make_async_remote_copy(
                    src_ref=piece(to), dst_ref=out.at[me], device_id=_device_of(to), **pair), served(to)))
                self.arrivals.append((pltpu.make_async_remote_copy(
                    src_ref=piece(frm), dst_ref=out.at[frm], device_id=_device_of(frm), **pair), served(me)))

    @staticmethod
    def _each(copies, act):
        for cp, takes_part in copies:
            if takes_part is None:
                act(cp)
            else:
                pl.when(takes_part)(functools.partial(act, cp))

    def start(self):
        self._each(self.mine + self.sends, lambda cp: cp.start())

    def wait(self):
        self._each(self.arrivals, lambda cp: cp.wait_recv())
        self._each(self.sends, lambda cp: cp.wait_send())
        self._each(self.mine, lambda cp: cp.wait())

    @staticmethod
    def scratch(n):
        return [pltpu.SemaphoreType.DMA((7 * n,)), pltpu.SemaphoreType.DMA((7 * n,)), pltpu.SemaphoreType.DMA((n,))]

    @staticmethod
    def out_shapes(arrays, gather):
        return [jax.ShapeDtypeStruct((N_DEV,) + (a.shape if gather else a.shape[1:]), a.dtype) for a in arrays]


def _all_reduce_small(v):
    R, C = v.shape

    def body(v_ref, out_ref, slots, send_sems, recv_sems):
        x, y, c = _mesh_pos()
        me = 4 * x + 2 * y + c

        def copy(j, to):
            return pltpu.make_async_remote_copy(
                src_ref=v_ref, dst_ref=slots.at[me],
                send_sem=send_sems.at[j - 1], recv_sem=recv_sems.at[j - 1],
                device_id=_device_of(to), device_id_type=pl.DeviceIdType.MESH)

        sends = [copy(j, (me + j) % N_DEV) for j in range(1, N_DEV)]
        for cp in sends:
            cp.start()
        slots[me] = v_ref[...]
        for j in range(1, N_DEV):
            frm = (me + N_DEV - j) % N_DEV
            pltpu.make_async_remote_copy(
                src_ref=v_ref, dst_ref=slots.at[frm], send_sem=send_sems.at[j - 1], recv_sem=recv_sems.at[j - 1],
                device_id=_device_of(frm), device_id_type=pl.DeviceIdType.MESH).wait_recv()
        for cp in sends:
            cp.wait_send()
        acc = slots[0]
        for p in range(1, N_DEV):
            acc = acc + slots[p]
        out_ref[...] = acc

    return pl.pallas_call(
        body, name="all_reduce_small",
        in_specs=[pl.BlockSpec(memory_space=pltpu.VMEM)], out_specs=pl.BlockSpec(memory_space=pltpu.VMEM),
        out_shape=jax.ShapeDtypeStruct((R, C), F32),
        scratch_shapes=[pltpu.VMEM((N_DEV, R, C), F32), pltpu.SemaphoreType.DMA((7,)), pltpu.SemaphoreType.DMA((7,))],
    )(v)


def _adamw_math(w, g, m, v):
    m = ADAM_B1 * m + (1.0 - ADAM_B1) * g
    v = ADAM_B2 * v + (1.0 - ADAM_B2) * (g * g)
    m_hat = m / (1.0 - ADAM_B1 ** ADAM_STEP)
    v_hat = v / (1.0 - ADAM_B2 ** ADAM_STEP)
    delta = -ADAM_LR * (m_hat / (jnp.sqrt(v_hat) + ADAM_EPS) + ADAM_WD * w)
    return delta, m, v


def _adamw_reduce(recvs, dests, w, m, v, name):
    R, C = w.shape
    tr = R if R <= 256 else 256
    while R % tr:
        tr //= 2
    n = len(recvs)

    def body(*refs):
        w_ref, m_ref, v_ref, g_ref, d_ref, nm_ref, nv_ref = refs[n:]

        def update(r_ref):
            g = r_ref[0].astype(F32)
            for p in range(1, N_DEV):
                g = g + r_ref[p].astype(F32)
            d, nm, nv = _adamw_math(w_ref[...], g, m_ref[...], v_ref[...])
            g_ref[...] = g
            d_ref[...] = d
            nm_ref[...] = nm
            nv_ref[...] = nv

        if n == 1:
            update(refs[0])
        else:
            x, y, c = _mesh_pos()
            me = 4 * x + 2 * y + c
            for r_ref, (lo, hi) in zip(refs[:n], dests):
                pl.when((me >= lo) & (me < hi))(functools.partial(update, r_ref))

    tile = pl.BlockSpec((tr, C), lambda i: (i, 0))
    shp = jax.ShapeDtypeStruct((R, C), F32)
    return pl.pallas_call(
        body, name=name, grid=(R // tr,),
        in_specs=[pl.BlockSpec((N_DEV, tr, C), lambda i: (0, i, 0))] * n + [tile, tile, tile],
        out_specs=[tile] * 4, out_shape=[shp] * 4,
        compiler_params=_params(("parallel",)),
    )(*recvs, w, m, v)


def _adamw_small(g, w, m, v):
    def body(g_ref, w_ref, m_ref, v_ref, go_ref, d_ref, nm_ref, nv_ref):
        go_ref[...] = g_ref[...]
        for d in range(2):
            p0 = _sig(w_ref[8 + 2 * d:9 + 2 * d, :] - w_ref[9 + 2 * d:10 + 2 * d, :])
            dl0 = g_ref[12 + d:13 + d, :] * p0 * (1.0 - p0)
            go_ref[8 + 2 * d:9 + 2 * d, :] = dl0
            go_ref[9 + 2 * d:10 + 2 * d, :] = -dl0
            go_ref[12 + d:13 + d, :] = jnp.zeros((1, D_MODEL), F32)
        d, nm, nv = _adamw_math(w_ref[...], go_ref[...], m_ref[...], v_ref[...])
        d_ref[...] = d
        nm_ref[...] = nm
        nv_ref[...] = nv

    shp = jax.ShapeDtypeStruct(g.shape, F32)
    vm = pl.BlockSpec(memory_space=pltpu.VMEM)
    return pl.pallas_call(body, name="adamw_small", in_specs=[vm] * 4, out_specs=[vm] * 4, out_shape=[shp] * 4)(g, w, m, v)


def _local_step(x, tgt, g_mix, lb_shard, norm_g, pool_scale, g_ffn, g_final, w_in_shard, late_shards):
    B, L, _ = x.shape
    T = B * L
    rows = D_MODEL // N_DEV
    x2, tgt2 = x.reshape(T, D_MODEL), tgt.reshape(T, D_MODEL)

    proj, u, w_in_g, lb_g = _rms_inproj(x2, g_mix, w_in_shard, lb_shard)
    lb = lb_g[:, :4].transpose(1, 0, 2).reshape(4, D_MODEL)
    proj4 = proj.reshape(N_DEV, B, L, D_MODEL)
    o, ya, (wa, wb, wo, wfi_g, wfo_g, pw_g) = _hgrn_fwd(proj4, lb, norm_g, B, L, late_shards)
    wa, wb, wo = (w_.reshape(D_MODEL, D_MODEL) for w_ in (wa, wb, wo))
    wfo_g = wfo_g.reshape(4, FF_BLOCK, D_MODEL)
    pool_w_full = pw_g.reshape(N_DEV, 4, 32, POOL_DIM).transpose(1, 0, 2, 3).reshape(4, POOL_DIM, POOL_DIM)
    yb = _pool_fwd(proj4, pool_w_full, pool_scale, B, L)
    ya2, yb2 = ya.reshape(T, D_MODEL), yb.reshape(T, D_MODEL)
    za, zb, mg, h = _merge_out(x2, proj, ya2, yb2, wa, wb, wo)
    gate, up, hid, u2, dh2, dh2b, loss_p, dg_final = _ffn_fwd_loss(h, tgt2, g_ffn, g_final, wfi_g, wfo_g)
    loss = jnp.sum(loss_p[:, 0, 0])

    dgu, dh, dhb, dg_ffn = _ffn_bwd(h, dh2, dh2b, gate, up, g_ffn, wfi_g, wfo_g)
    d_wfo = _wgrad(hid, dh2b[None], "wgrad_ffn_out")
    d_wfi = _wgrad(u2[None], dgu.reshape(N_DEV, T, FF_BLOCK), "wgrad_ffn_in")
    dza, dzb, dgab, dya, dyb = _merge_bwd(dhb, proj, za, zb, wa, wb, wo)
    d_wo = _wgrad(mg[None], dhb[None], "wgrad_out")
    d_wa = _wgrad(ya2[None], dza[None], "wgrad_branch_a")
    d_wb = _wgrad(yb2[None], dzb[None], "wgrad_branch_b")
    dp, dpw_p, dps_p = _pool_bwd(proj4, dyb.reshape(B, L, D_MODEL), pool_w_full, pool_scale, B, L)
    d_pw = dpw_p.sum(0).reshape(4, N_DEV, 32, POOL_DIM).transpose(1, 0, 2, 3).reshape(N_DEV, 128, POOL_DIM)
    dp2 = dp.reshape(T, D_MODEL)
    w_in_dests = [(5, 6), (6, 8), (0, 5)]
    d_win_pool = _wgrad(u[None], dp2[None], "wgrad_in_pool")
    d_win_gates = _wgrad(u[None], dgab, "wgrad_in_gates")
    slices = [d_wa.reshape(N_DEV, rows, D_MODEL), d_wb.reshape(N_DEV, rows, D_MODEL),
              d_wo.reshape(N_DEV, rows, D_MODEL), d_wfi, d_wfo.reshape(N_DEV, FF_BLOCK // 2, D_MODEL),
              d_pw.astype(MX), d_win_pool, d_win_gates]
    dests = [(0, N_DEV)] * 6 + w_in_dests[:2]
    dproj5, dlb_p, dng_p, recv = _hgrn_bwd(proj4, o, dya.reshape(B, L, D_MODEL), lb, norm_g, B, L, slices, dests)
    dproj5 = dproj5.reshape(5, T, D_MODEL)
    d_win_rec = _wgrad(u[None], dproj5, "wgrad_in_recurrence")
    grad_x, dg_mix, recv_win_rec = _inproj_bwd(x2, dh, dproj5, dp2, dgab, g_mix, w_in_g, d_win_rec, w_in_dests[2])

    small = dict(g_mix=dg_mix.sum(0), hgrn_norm_g=dng_p.sum((0, 1)), pool_scale=dps_p.sum((0, 1)),
                 g_ffn=dg_ffn.sum(0), g_final=dg_final.sum(0), lb=dlb_p.sum(0))
    recv_w_in = [recv[6], recv[7], recv_win_rec]
    return loss, grad_x.reshape(B, L, D_MODEL), (recv_w_in, w_in_dests), list(recv[:6]), small


def kernel(x, g_mix, w_in, lb_logits, hgrn_norm_g, pool_w, pool_scale, w_branch_a, w_branch_b, w_out, g_ffn, w_ffn_in, w_ffn_out, g_final, loss_target, m_g_mix, m_w_in, m_lb_logits, m_hgrn_norm_g, m_pool_w, m_pool_scale, m_w_branch_a, m_w_branch_b, m_w_out, m_g_ffn, m_w_ffn_in, m_w_ffn_out, m_g_final, v_g_mix, v_w_in, v_lb_logits, v_hgrn_norm_g, v_pool_w, v_pool_scale, v_w_branch_a, v_w_branch_b, v_w_out, v_g_ffn, v_w_ffn_in, v_w_ffn_out, v_g_final):
    me = 4 * lax.axis_index("x") + 2 * lax.axis_index("y") + lax.axis_index("c")

    late_shards = [w_branch_a[0].astype(MX), w_branch_b[0].astype(MX), w_out[0].astype(MX),
                   w_ffn_in[0].astype(MX), w_ffn_out[0].astype(MX), pool_w[0].reshape(4 * 32, POOL_DIM).astype(MX)]
    lb_shard = jnp.pad(lb_logits.reshape(4, HEAD_DIM), ((0, 4), (0, 0)))

    loss, grad_x, (recv_w_in, w_in_dests), recv_late, small = _local_step(
        x, loss_target, g_mix, lb_shard, hgrn_norm_g, pool_scale, g_ffn, g_final[None], w_in[0].astype(MX),
        late_shards)
    loss = lax.psum(loss, ("x", "y", "c"))

    packed = jnp.zeros((16, D_MODEL), F32)
    names = ["g_mix", "hgrn_norm_g", "pool_scale", "g_ffn", "g_final"]
    for i, nme in enumerate(names):
        packed = packed.at[i].set(small[nme])
    packed = packed.at[5:7].set(small["lb"])
    red = _all_reduce_small(packed)
    dlb_mine = lax.dynamic_slice_in_dim(red[5:7], me * HEAD_DIM, HEAD_DIM, axis=1)

    sw = jnp.zeros((16, D_MODEL), F32)
    sm = jnp.zeros((16, D_MODEL), F32)
    sv = jnp.ones((16, D_MODEL), F32)
    smalls = [(g_mix, m_g_mix, v_g_mix), (hgrn_norm_g, m_hgrn_norm_g, v_hgrn_norm_g),
              (pool_scale, m_pool_scale, v_pool_scale), (g_ffn, m_g_ffn, v_g_ffn),
              (g_final[None], m_g_final[None], v_g_final[None])]
    for i, (w_, m_, v_) in enumerate(smalls):
        sw, sm, sv = sw.at[i].set(w_[0]), sm.at[i].set(m_[0]), sv.at[i].set(v_[0])
    sg = red.at[5:].set(0.0)
    sg = sg.at[12:14, :HEAD_DIM].set(dlb_mine)
    sw = sw.at[8:12, :HEAD_DIM].set(lb_logits.reshape(4, HEAD_DIM))
    sm = sm.at[8:12, :HEAD_DIM].set(m_lb_logits.reshape(4, HEAD_DIM))
    sv = sv.at[8:12, :HEAD_DIM].set(v_lb_logits.reshape(4, HEAD_DIM))
    sg, sd, snm, snv = _adamw_small(sg, sw, sm, sv)

    def small_out(arr, i, like):
        return arr[i].reshape(like.shape)

    def lb_out(arr):
        return arr[8:12, :HEAD_DIM].reshape(2, 2, HEAD_DIM)

    order = ["w_in", "w_branch_a", "w_branch_b", "w_out", "w_ffn_in", "w_ffn_out", "pool_w"]
    params = dict(w_in=(w_in, m_w_in, v_w_in), w_branch_a=(w_branch_a, m_w_branch_a, v_w_branch_a),
                  w_branch_b=(w_branch_b, m_w_branch_b, v_w_branch_b), w_out=(w_out, m_w_out, v_w_out),
                  w_ffn_in=(w_ffn_in, m_w_ffn_in, v_w_ffn_in), w_ffn_out=(w_ffn_out, m_w_ffn_out, v_w_ffn_out),
                  pool_w=(pool_w, m_pool_w, v_pool_w))
    res = {}
    for nme, r, dests in zip(order, [recv_w_in] + [[r] for r in recv_late], [w_in_dests] + [None] * 6):
        w_, m_, v_ = params[nme]
        shape2 = r[0].shape[1:]
        outs = _adamw_reduce(r, dests, w_.reshape(shape2), m_.reshape(shape2), v_.reshape(shape2), "adamw_" + nme)
        res[nme] = [o_.reshape(w_.shape) for o_ in outs]

    def pick(k):
        small_src = [sg, sd, snm, snv][k]
        return [small_out(small_src, 0, g_mix), res["w_in"][k], lb_out(small_src), small_out(small_src, 1, hgrn_norm_g),
                res["pool_w"][k], small_out(small_src, 2, pool_scale), res["w_branch_a"][k], res["w_branch_b"][k],
                res["w_out"][k], small_out(small_src, 3, g_ffn), res["w_ffn_in"][k], res["w_ffn_out"][k],
                small_out(small_src, 4, g_final)]

    return (loss, grad_x, *pick(0), *pick(1), *pick(2), *pick(3))
```

```python
import functools

import jax
import jax.numpy as jnp
from jax import lax
from jax.experimental import pallas as pl
from jax.experimental.pallas import tpu as pltpu

F32 = jnp.float32
MX = jnp.bfloat16

D_MODEL = 1024
N_HEADS = 8
HEAD_DIM = 128
CHUNK = 16
POOL_WINDOWS = (2, 4, 8, 16)
POOL_DIM = 256
FF_BLOCK = 704
N_DEV = 8
RMS_EPS = 1e-6
ADAM_LR, ADAM_B1, ADAM_B2, ADAM_EPS, ADAM_WD, ADAM_STEP = 0.001, 0.9, 0.999, 1e-08, 0.01, 10
VMEM_LIMIT = 56 * 1024 * 1024
BIG_TOKEN_TILE = 1024
ANY = pl.BlockSpec(memory_space=pl.ANY)


def _params(sem=None):
    return pltpu.CompilerParams(dimension_semantics=sem, vmem_limit_bytes=VMEM_LIMIT)


def _dot(a, b):
    return lax.dot_general(a.astype(MX), b.astype(MX), (((1,), (0,)), ((), ())), preferred_element_type=F32)


def _dot_nt(a, b):
    return lax.dot_general(a.astype(MX), b.astype(MX), (((1,), (1,)), ((), ())), preferred_element_type=F32)


def _dot_tn(a, b):
    return lax.dot_general(a.astype(MX), b.astype(MX), (((0,), (0,)), ((), ())), preferred_element_type=F32)


def _sig(x):
    return 1.0 / (1.0 + jnp.exp(-x))


def _fold8(v):
    return v.reshape(v.shape[0] // 8, 8, v.shape[1]).sum(axis=0)


def _shift_rows(x, s):
    n = x.shape[0]
    row = lax.broadcasted_iota(jnp.int32, x.shape, 0)
    if s > 0:
        return jnp.where(row >= s, pltpu.roll(x, s, 0), 0.0)
    return jnp.where(row < n + s, pltpu.roll(x, n + s, 0), 0.0)


def _cumsum_rows(x, rev):
    s = 1
    while s < x.shape[0]:
        x = x + _shift_rows(x, -s if rev else s)
        s *= 2
    return x


def _gather_order():
    x, y, c = _mesh_pos()
    near, far = [(1 - x, y), (x, 1 - y)], (1 - x, 1 - y)
    return ([(x, y, c), (x, y, 1 - c)] + [(*chip, c) for chip in near] + [(*chip, 1 - c) for chip in near]
            + [(*far, c), (*far, 1 - c)])


def _rms_inproj(x2, g_mix, w_shard, lb_shard):
    T = x2.shape[0]
    tm = min(BIG_TOKEN_TILE, T)
    nT = T // tm
    block_of_step = jnp.stack([4 * px + 2 * py + pc for px, py, pc in _gather_order()]).astype(jnp.int32)

    def body(order_ref, x_ref, g_ref, w_ref, lb_ref, proj_ref, u_ref, wg_ref, lbg_ref,
             u_sc, w_sc, load_sem, send_sems, recv_sems, local_sem, *lb_sems):
        k, i = pl.program_id(0), pl.program_id(1)
        order = _gather_order()
        me, sibling = order[0], order[1]

        def slot(dev):
            return wg_ref.at[4 * dev[0] + 2 * dev[1] + dev[2]]

        def copy(n, block, to, src=None):
            return pltpu.make_async_remote_copy(
                src_ref=slot(block) if src is None else src, dst_ref=slot(block),
                send_sem=send_sems.at[n], recv_sem=recv_sems.at[n], device_id=to, device_id_type=pl.DeviceIdType.MESH)

        mine = pltpu.make_async_copy(w_ref, slot(me), local_sem)
        to_sibling = copy(0, me, sibling, src=w_ref)
        to_near = [copy(1, me, order[2], src=w_ref), copy(2, me, order[3], src=w_ref)]
        to_far = copy(5, me, order[6], src=w_ref)
        passed = {2: copy(3, order[2], sibling), 3: copy(4, order[3], sibling), 6: copy(6, order[6], sibling)}
        lb_gather = _Exchange([lb_ref], [lbg_ref], lb_sems, gather=True)

        @pl.when((k == 0) & (i == 0))
        def _():
            for cp in [mine, to_sibling] + to_near:
                cp.start()
            lb_gather.start()

        for s, block in enumerate(order):
            @pl.when((k == s) & (i == 0))
            def _(s=s, block=block):
                if s == 0:
                    mine.wait()
                else:
                    copy(s - 1, block, me).wait_recv()
                if s in passed:
                    passed[s].start()
                if s == 3:
                    for cp in to_near:
                        cp.wait_send()
                    to_far.start()
                load = pltpu.make_async_copy(slot(block), w_sc, load_sem)
                load.start()
                load.wait()

        rows = pl.ds(pl.multiple_of(i * tm, tm), tm)

        @pl.when(k == 0)
        def _():
            x = x_ref[...]
            r = lax.rsqrt(jnp.mean(x * x, axis=-1, keepdims=True) + RMS_EPS)
            u = (x * r * g_ref[...]).astype(MX)
            u_sc[rows, :] = u
            u_ref[...] = u

        proj_ref[...] = jnp.dot(u_sc[rows, :], w_sc[...], preferred_element_type=F32)

        @pl.when((k == N_DEV - 1) & (i == nT - 1))
        def _():
            for cp in [to_sibling, to_far] + list(passed.values()):
                cp.wait_send()
            lb_gather.wait()

    def tile_once(k, i, order_ref):
        return (jnp.where(k == 0, i, nT - 1), 0)

    grid_spec = pltpu.PrefetchScalarGridSpec(
        num_scalar_prefetch=1, grid=(N_DEV, nT),
        in_specs=[pl.BlockSpec((tm, D_MODEL), tile_once),
                  pl.BlockSpec((1, D_MODEL), lambda k, i, order_ref: (0, 0)), ANY, ANY],
        out_specs=[pl.BlockSpec((None, tm, D_MODEL), lambda k, i, order_ref: (order_ref[k], i, 0)),
                   pl.BlockSpec((tm, D_MODEL), tile_once), ANY, ANY],
        scratch_shapes=[pltpu.VMEM((T, D_MODEL), MX), pltpu.VMEM((D_MODEL, D_MODEL), MX),
                        pltpu.SemaphoreType.DMA(()), pltpu.SemaphoreType.DMA((7,)), pltpu.SemaphoreType.DMA((7,)),
                        pltpu.SemaphoreType.DMA(())] + _Exchange.scratch(1))
    return pl.pallas_call(
        body, name="rms_inproj", grid_spec=grid_spec,
        out_shape=[jax.ShapeDtypeStruct((N_DEV, T, D_MODEL), F32), jax.ShapeDtypeStruct((T, D_MODEL), MX),
                   jax.ShapeDtypeStruct((N_DEV, D_MODEL, D_MODEL), MX),
                   jax.ShapeDtypeStruct((N_DEV,) + lb_shard.shape, lb_shard.dtype)],
        compiler_params=_params(("arbitrary", "arbitrary")),
    )(block_of_step, x2, g_mix, w_shard, lb_shard)


HBLK = 128


def _seg_cumsum(x, rev):
    n = x.shape[0]
    pos = lax.broadcasted_iota(jnp.int32, x.shape, 0) & (CHUNK - 1)
    s = 1
    while s < CHUNK:
        if rev:
            x = x + jnp.where(pos < CHUNK - s, pltpu.roll(x, n - s, 0), 0.0)
        else:
            x = x + jnp.where(pos >= s, pltpu.roll(x, s, 0), 0.0)
        s *= 2
    return x


def _block_gates(q_r, f_r, lb_row, rev):
    sq = _sig(q_r)
    q = q_r * sq
    sg = _sig(f_r)
    f = lb_row + (1.0 - lb_row) * sg
    k = 1.0 - f
    lf = jnp.log(f)
    pre = _seg_cumsum(lf, False)
    suf = _seg_cumsum(lf, True)
    tot = pre + suf - lf
    b = suf if rev else pre
    eb = jnp.exp(b)
    enb = jnp.exp(-b)
    eend = jnp.exp(tot - b)
    return dict(sq=sq, sg=sg, f=f, eb=eb, enb=enb, eend=eend, dec=jnp.exp(tot),
                P=q * eb, Kt=k * enb, Ke=k * eend)


def _block_mask(rev, transposed=False):
    ri = lax.broadcasted_iota(jnp.int32, (HBLK, HBLK), 0)
    ci = lax.broadcasted_iota(jnp.int32, (HBLK, HBLK), 1)
    same = (ri // CHUNK) == (ci // CHUNK)
    return same & ((ci >= ri) if rev != transposed else (ci <= ri))


def _chunk_rows(c, rows=CHUNK):
    return pl.ds(c * CHUNK, rows)


def _block_rows(kb):
    return pl.ds(kb * HBLK, HBLK)


def _chunk_outer_products(a, b, out_sc, kb):
    a, b = a.astype(MX), b.astype(MX)
    for u in range(HBLK // CHUNK):
        r = slice(u * CHUNK, (u + 1) * CHUNK)
        out_sc[kb * (HBLK // CHUNK) + u] = _dot_tn(a[r, :], b[r, :])


UNIT_BLOCKS = 2
UNIT_CHUNKS = UNIT_BLOCKS * (HBLK // CHUNK)


def _unit_blocks(u, fn, carry):
    for b in range(UNIT_BLOCKS):
        carry = fn(u * UNIT_BLOCKS + b, carry)
    return carry


def _unit_chunks(u, descending, fn, carry):
    for j in range(UNIT_CHUNKS):
        carry = fn(u * UNIT_CHUNKS + (UNIT_CHUNKS - 1 - j if descending else j), carry)
    return carry


def _pipelined(n_units, descending, first, c1, second, c2):
    units = list(range(n_units))[::-1] if descending else list(range(n_units))
    c1 = first(units[0], c1)
    for t, u in enumerate(units):
        if t + 1 < n_units:
            c1 = first(units[t + 1], c1)
        c2 = second(u, c2)
    return c1, c2


def _lower_bounds(lbl_ref):
    return _sig(lbl_ref[0:1, :] - lbl_ref[1:2, :]), _sig(lbl_ref[2:3, :] - lbl_ref[3:4, :])


def _hgrn_fwd(proj, lbl, norm_g, B, L, shards):
    nC = L // CHUNK
    n = len(shards)

    def body(*refs):
        q_ref, ff_ref, fb_ref, i_ref, og_ref, lbl_ref, ng_ref = refs[:7]
        o_ref, ya_ref = refs[7 + n:9 + n]
        o_sc, p_sc, dec_sc, upd_sc = refs[9 + 2 * n:13 + 2 * n]
        step_id = pl.program_id(0) * N_HEADS + pl.program_id(1)
        gather = _Exchange(refs[7:7 + n], refs[9 + n:9 + 2 * n], refs[13 + 2 * n:], gather=True)

        @pl.when(step_id == 0)
        def _():
            gather.start()

        lb_f, lb_b = _lower_bounds(lbl_ref)
        o_sc[...] = jnp.zeros_like(o_sc)

        def run_dir(f_ref, lb_row, rev):
            def block(kb, carry):
                rows = _block_rows(kb)
                g = _block_gates(q_ref[rows, :], f_ref[rows, :], lb_row, rev)
                v = i_ref[rows, :]
                a = jnp.where(_block_mask(rev), _dot_nt(g["P"], g["Kt"]), 0.0)
                o_sc[rows, :] += _dot(a, v)
                p_sc[rows, :] = g["P"].astype(MX)
                dec_sc[rows, :] = g["dec"]
                _chunk_outer_products(v, g["Ke"], upd_sc, kb)
                return carry

            def step(c, st):
                rows = _chunk_rows(c)
                o_sc[rows, :] += _dot_nt(p_sc[rows, :], st)
                dec = dec_sc[_chunk_rows(c, 1), :]
                return st * dec + upd_sc[c]

            _pipelined(nC // UNIT_CHUNKS, rev,
                       lambda u, c: _unit_blocks(u, block, c), 0,
                       lambda u, st: _unit_chunks(u, rev, step, st), jnp.zeros((HEAD_DIM, HEAD_DIM), F32))

        run_dir(ff_ref, lb_f, False)
        run_dir(fb_ref, lb_b, True)
        o = o_sc[...]
        o_ref[...] = o
        on = o * lax.rsqrt(jnp.mean(o * o, axis=-1, keepdims=True) + RMS_EPS)
        og = og_ref[...]
        ya_ref[...] = ((on * ng_ref[...]) * (og * _sig(og))).astype(MX)

        @pl.when(step_id == B * N_HEADS - 1)
        def _():
            gather.wait()

    def blk(s):
        return pl.BlockSpec((None, None, L, HEAD_DIM), lambda b, h, s=s: (s, b, 0, h))

    out_blk = pl.BlockSpec((None, L, HEAD_DIM), lambda b, h: (b, 0, h))
    outs = pl.pallas_call(
        body, name="hgrn_fwd", grid=(B, N_HEADS),
        in_specs=[blk(0), blk(1), blk(2), blk(3), blk(4),
                  pl.BlockSpec((4, HEAD_DIM), lambda b, h: (0, h)),
                  pl.BlockSpec((1, HEAD_DIM), lambda b, h: (0, h))] + [ANY] * n,
        out_specs=[out_blk, out_blk] + [ANY] * n,
        out_shape=[jax.ShapeDtypeStruct((B, L, D_MODEL), F32), jax.ShapeDtypeStruct((B, L, D_MODEL), MX)]
                  + _Exchange.out_shapes(shards, True),
        scratch_shapes=[pltpu.VMEM((L, HEAD_DIM), F32), pltpu.VMEM((L, HEAD_DIM), MX),
                        pltpu.VMEM((L, HEAD_DIM), F32), pltpu.VMEM((nC, HEAD_DIM, HEAD_DIM), F32)]
                       + _Exchange.scratch(n),
        compiler_params=_params(("arbitrary", "arbitrary")),
    )(proj, proj, proj, proj, proj, lbl, norm_g, *shards)
    return outs[0], outs[1], outs[2:]


POOL_PAD = 8


def _pool_window(p, ext_sc, half, adjoint):
    L = p.shape[0]
    n = L + 2 * POOL_PAD
    ext_sc[0:POOL_PAD, :] = jnp.zeros((POOL_PAD, p.shape[1]), F32)
    ext_sc[POOL_PAD + L:n, :] = jnp.zeros((POOL_PAD, p.shape[1]), F32)
    ext_sc[POOL_PAD:POOL_PAD + L, :] = p
    x = ext_sc[...]
    s = x + pltpu.roll(x, 1 if adjoint else n - 1, 0)
    w = 1
    while w < half:
        s = pltpu.roll(s, w, 0) + pltpu.roll(s, n - w, 0)
        w *= 2
    ext_sc[...] = s
    return ext_sc[POOL_PAD:POOL_PAD + L, :]


def _pool_count(L, half):
    t = lax.broadcasted_iota(jnp.int32, (L, 1), 0)
    lo = jnp.clip(t - half + 1, 0, L)
    hi = jnp.clip(t + half + 1, 0, L)
    return (hi - lo).astype(F32)


def _pool_fwd(proj, pool_w_full, pool_scale, B, L):
    def body(p_ref, w_ref, s_ref, yb_ref, ext_sc):
        for g, win in enumerate(POOL_WINDOWS):
            cols = slice(g * POOL_DIM, (g + 1) * POOL_DIM)
            p = p_ref[:, cols]
            y = _pool_window(p, ext_sc, win // 2, False) / _pool_count(L, win // 2) - p
            yb_ref[:, cols] = (_dot(y, w_ref[g]) * s_ref[:, cols]).astype(MX)

    return pl.pallas_call(
        body, name="pool_fwd", grid=(B,),
        in_specs=[pl.BlockSpec((None, None, L, D_MODEL), lambda b: (5, b, 0, 0)),
                  pl.BlockSpec((4, POOL_DIM, POOL_DIM), lambda b: (0, 0, 0)),
                  pl.BlockSpec((1, D_MODEL), lambda b: (0, 0))],
        out_specs=pl.BlockSpec((None, L, D_MODEL), lambda b: (b, 0, 0)),
        out_shape=jax.ShapeDtypeStruct((B, L, D_MODEL), MX),
        scratch_shapes=[pltpu.VMEM((L + 2 * POOL_PAD, POOL_DIM), F32)],
        compiler_params=_params(("parallel",)),
    )(proj, pool_w_full, pool_scale)


def _merge_out(x2, proj, ya, yb, wa, wb, wo):
    T = x2.shape[0]
    tm = min(512, T)

    def body(x_ref, ga_ref, gb_ref, ya_ref, yb_ref, wa_ref, wb_ref, wo_ref, za_ref, zb_ref, mg_ref, h_ref):
        za = jnp.dot(ya_ref[...], wa_ref[...], preferred_element_type=F32)
        zb = jnp.dot(yb_ref[...], wb_ref[...], preferred_element_type=F32)
        mg = (_sig(ga_ref[...]) * za + _sig(gb_ref[...]) * zb).astype(MX)
        za_ref[...] = za
        zb_ref[...] = zb
        mg_ref[...] = mg
        h_ref[...] = x_ref[...] + jnp.dot(mg, wo_ref[...], preferred_element_type=F32)

    tile = pl.BlockSpec((tm, D_MODEL), lambda i: (i, 0))
    wspec = pl.BlockSpec((D_MODEL, D_MODEL), lambda i: (0, 0))
    return pl.pallas_call(
        body, name="merge_out", grid=(T // tm,),
        in_specs=[tile,
                  pl.BlockSpec((None, tm, D_MODEL), lambda i: (6, i, 0)),
                  pl.BlockSpec((None, tm, D_MODEL), lambda i: (7, i, 0)),
                  tile, tile, wspec, wspec, wspec],
        out_specs=[tile, tile, tile, tile],
        out_shape=[jax.ShapeDtypeStruct((T, D_MODEL), F32), jax.ShapeDtypeStruct((T, D_MODEL), F32),
                   jax.ShapeDtypeStruct((T, D_MODEL), MX), jax.ShapeDtypeStruct((T, D_MODEL), F32)],
        compiler_params=_params(("parallel",)),
    )(x2, proj, proj, ya, yb, wa, wb, wo)


def _ffn_fwd_loss(h, tgt, g_ffn, g_final, wfi_g, wfo_g):
    T = h.shape[0]
    tm = min(512, T)
    nT = T // tm

    def body(h_ref, t_ref, gf_ref, gl_ref, wg_ref, wu_ref, wo_ref,
             gate_ref, up_ref, hid_ref, u2_ref, dh2_ref, dh2b_ref, loss_ref, dgl_ref, u2_sc, acc_sc):
        i, j = pl.program_id(0), pl.program_id(1)

        @pl.when(j == 0)
        def _():
            hh = h_ref[...]
            r = lax.rsqrt(jnp.mean(hh * hh, axis=-1, keepdims=True) + RMS_EPS)
            u2 = (hh * r * gf_ref[...]).astype(MX)
            u2_sc[...] = u2
            u2_ref[...] = u2
            acc_sc[...] = jnp.zeros_like(acc_sc)

        @pl.when((i == 0) & (j == 0))
        def _():
            dgl_ref[...] = jnp.zeros_like(dgl_ref)

        gate = jnp.dot(u2_sc[...], wg_ref[...], preferred_element_type=F32)
        up = jnp.dot(u2_sc[...], wu_ref[...], preferred_element_type=F32)
        hid = ((gate * _sig(gate)) * up).astype(MX)
        gate_ref[...] = gate
        up_ref[...] = up
        hid_ref[...] = hid
        acc_sc[...] += jnp.dot(hid, wo_ref[...], preferred_element_type=F32)

        @pl.when(j == 3)
        def _():
            h2 = h_ref[...] + acc_sc[...]
            r = lax.rsqrt(jnp.mean(h2 * h2, axis=-1, keepdims=True) + RMS_EPS)
            hn = h2 * r
            gl = gl_ref[...]
            err = hn * gl - t_ref[...]
            tok = jnp.mean(err * err, axis=-1, keepdims=True)
            loss_ref[...] = jnp.full(loss_ref.shape, 0.5 * jnp.sum(tok), F32)
            dy = err * (1.0 / D_MODEL)
            dgl_ref[...] += _fold8(dy * hn)
            a = dy * gl
            dh2 = r * a - hn * (r * jnp.mean(a * hn, axis=-1, keepdims=True))
            dh2_ref[...] = dh2
            dh2b_ref[...] = dh2.astype(MX)

    tile = pl.BlockSpec((tm, D_MODEL), lambda i, j: (i, 0))
    vec = pl.BlockSpec((1, D_MODEL), lambda i, j: (0, 0))
    ftile = pl.BlockSpec((None, tm, FF_BLOCK), lambda i, j: (j, i, 0))
    return pl.pallas_call(
        body, name="ffn_fwd_loss", grid=(nT, 4),
        in_specs=[tile, tile, vec, vec,
                  pl.BlockSpec((None, D_MODEL, FF_BLOCK), lambda i, j: (j, 0, 0)),
                  pl.BlockSpec((None, D_MODEL, FF_BLOCK), lambda i, j: (j + 4, 0, 0)),
                  pl.BlockSpec((None, FF_BLOCK, D_MODEL), lambda i, j: (j, 0, 0))],
        out_specs=[ftile, ftile, ftile, tile, tile, tile,
                   pl.BlockSpec((None, 8, 128), lambda i, j: (i, 0, 0)),
                   pl.BlockSpec((8, D_MODEL), lambda i, j: (0, 0))],
        out_shape=[jax.ShapeDtypeStruct((4, T, FF_BLOCK), F32), jax.ShapeDtypeStruct((4, T, FF_BLOCK), F32),
                   jax.ShapeDtypeStruct((4, T, FF_BLOCK), MX), jax.ShapeDtypeStruct((T, D_MODEL), MX),
                   jax.ShapeDtypeStruct((T, D_MODEL), F32), jax.ShapeDtypeStruct((T, D_MODEL), MX),
                   jax.ShapeDtypeStruct((nT, 8, 128), F32), jax.ShapeDtypeStruct((8, D_MODEL), F32)],
        scratch_shapes=[pltpu.VMEM((tm, D_MODEL), MX), pltpu.VMEM((tm, D_MODEL), F32)],
        compiler_params=_params(("arbitrary", "arbitrary")),
    )(h, tgt, g_ffn, g_final, wfi_g, wfi_g, wfo_g)


def _ffn_bwd(h, dh2, dh2b, gate, up, g_ffn, wfi_g, wfo_g):
    T = h.shape[0]
    tm = min(512, T)

    def body(h_ref, dh2_ref, dh2b_ref, gate_ref, up_ref, gf_ref, wg_ref, wu_ref, wo_ref,
             dgu_ref, dh_ref, dhb_ref, dgf_ref, acc_sc):
        i, j = pl.program_id(0), pl.program_id(1)

        @pl.when(j == 0)
        def _():
            acc_sc[...] = jnp.zeros_like(acc_sc)

        @pl.when((i == 0) & (j == 0))
        def _():
            dgf_ref[...] = jnp.zeros_like(dgf_ref)

        dhid = _dot_nt(dh2b_ref[...], wo_ref[...])
        gate, up = gate_ref[...], up_ref[...]
        sg = _sig(gate)
        dgate = (dhid * up * (sg * (1.0 + gate * (1.0 - sg)))).astype(MX)
        dup = (dhid * (gate * sg)).astype(MX)
        dgu_ref[0] = dgate
        dgu_ref[1] = dup
        acc_sc[...] += _dot_nt(dgate, wg_ref[...]) + _dot_nt(dup, wu_ref[...])

        @pl.when(j == 3)
        def _():
            hh = h_ref[...]
            r = lax.rsqrt(jnp.mean(hh * hh, axis=-1, keepdims=True) + RMS_EPS)
            hn = hh * r
            du2 = acc_sc[...]
            dgf_ref[...] += _fold8(du2 * hn)
            a = du2 * gf_ref[...]
            dh = dh2_ref[...] + r * a - hn * (r * jnp.mean(a * hn, axis=-1, keepdims=True))
            dh_ref[...] = dh
            dhb_ref[...] = dh.astype(MX)

    tile = pl.BlockSpec((tm, D_MODEL), lambda i, j: (i, 0))
    ftile = pl.BlockSpec((None, tm, FF_BLOCK), lambda i, j: (j, i, 0))
    return pl.pallas_call(
        body, name="ffn_bwd", grid=(T // tm, 4),
        in_specs=[tile, tile, tile, ftile, ftile,
                  pl.BlockSpec((1, D_MODEL), lambda i, j: (0, 0)),
                  pl.BlockSpec((None, D_MODEL, FF_BLOCK), lambda i, j: (j, 0, 0)),
                  pl.BlockSpec((None, D_MODEL, FF_BLOCK), lambda i, j: (j + 4, 0, 0)),
                  pl.BlockSpec((None, FF_BLOCK, D_MODEL), lambda i, j: (j, 0, 0))],
        out_specs=[pl.BlockSpec((2, None, tm, FF_BLOCK), lambda i, j: (0, j, i, 0)),
                   tile, tile, pl.BlockSpec((8, D_MODEL), lambda i, j: (0, 0))],
        out_shape=[jax.ShapeDtypeStruct((2, 4, T, FF_BLOCK), MX),
                   jax.ShapeDtypeStruct((T, D_MODEL), F32), jax.ShapeDtypeStruct((T, D_MODEL), MX),
                   jax.ShapeDtypeStruct((8, D_MODEL), F32)],
        scratch_shapes=[pltpu.VMEM((tm, D_MODEL), F32)],
        compiler_params=_params(("arbitrary", "arbitrary")),
    )(h, dh2, dh2b, gate, up, g_ffn, wfi_g, wfi_g, wfo_g)


def _merge_bwd(dhb, proj, za, zb, wa, wb, wo):
    T = dhb.shape[0]
    tm = min(512, T)

    def body(dh_ref, ga_ref, gb_ref, za_ref, zb_ref, wa_ref, wb_ref, wo_ref,
             dza_ref, dzb_ref, dgab_ref, dya_ref, dyb_ref):
        dm = _dot_nt(dh_ref[...], wo_ref[...])
        sa, sb = _sig(ga_ref[...]), _sig(gb_ref[...])
        dza = (dm * sa).astype(MX)
        dzb = (dm * sb).astype(MX)
        dza_ref[...] = dza
        dzb_ref[...] = dzb
        dgab_ref[0] = (dm * za_ref[...] * (sa * (1.0 - sa))).astype(MX)
        dgab_ref[1] = (dm * zb_ref[...] * (sb * (1.0 - sb))).astype(MX)
        dya_ref[...] = _dot_nt(dza, wa_ref[...])
        dyb_ref[...] = _dot_nt(dzb, wb_ref[...])

    tile = pl.BlockSpec((tm, D_MODEL), lambda i: (i, 0))
    wspec = pl.BlockSpec((D_MODEL, D_MODEL), lambda i: (0, 0))
    return pl.pallas_call(
        body, name="merge_bwd", grid=(T // tm,),
        in_specs=[tile,
                  pl.BlockSpec((None, tm, D_MODEL), lambda i: (6, i, 0)),
                  pl.BlockSpec((None, tm, D_MODEL), lambda i: (7, i, 0)),
                  tile, tile, wspec, wspec, wspec],
        out_specs=[tile, tile, pl.BlockSpec((2, tm, D_MODEL), lambda i: (0, i, 0)), tile, tile],
        out_shape=[jax.ShapeDtypeStruct((T, D_MODEL), MX), jax.ShapeDtypeStruct((T, D_MODEL), MX),
                   jax.ShapeDtypeStruct((2, T, D_MODEL), MX),
                   jax.ShapeDtypeStruct((T, D_MODEL), F32), jax.ShapeDtypeStruct((T, D_MODEL), F32)],
        compiler_params=_params(("parallel",)),
    )(dhb, proj, proj, za, zb, wa, wb, wo)


def _pool_bwd(proj, dyb, pool_w_full, pool_scale, B, L):
    def body(p_ref, dy_ref, w_ref, s_ref, dp_ref, dw_ref, ds_ref, ext_sc):
        for g, win in enumerate(POOL_WINDOWS):
            cols = slice(g * POOL_DIM, (g + 1) * POOL_DIM)
            p = p_ref[:, cols]
            cnt = _pool_count(L, win // 2)
            y = _pool_window(p, ext_sc, win // 2, False) / cnt - p
            z = _dot(y, w_ref[g])
            dyb_g = dy_ref[:, cols]
            ds_ref[:, cols] = jnp.sum(dyb_g * z, axis=0, keepdims=True)
            dz = dyb_g * s_ref[:, cols]
            dw_ref[g] = _dot_tn(y, dz)
            dy = _dot_nt(dz, w_ref[g])
            dp_ref[:, cols] = (_pool_window(dy / cnt, ext_sc, win // 2, True) - dy).astype(MX)

    seq = pl.BlockSpec((None, L, D_MODEL), lambda b: (b, 0, 0))
    return pl.pallas_call(
        body, name="pool_bwd", grid=(B,),
        in_specs=[pl.BlockSpec((None, None, L, D_MODEL), lambda b: (5, b, 0, 0)), seq,
                  pl.BlockSpec((4, POOL_DIM, POOL_DIM), lambda b: (0, 0, 0)),
                  pl.BlockSpec((1, D_MODEL), lambda b: (0, 0))],
        out_specs=[seq, pl.BlockSpec((None, 4, POOL_DIM, POOL_DIM), lambda b: (b, 0, 0, 0)),
                   pl.BlockSpec((None, 1, D_MODEL), lambda b: (b, 0, 0))],
        out_shape=[jax.ShapeDtypeStruct((B, L, D_MODEL), MX),
                   jax.ShapeDtypeStruct((B, 4, POOL_DIM, POOL_DIM), F32),
                   jax.ShapeDtypeStruct((B, 1, D_MODEL), F32)],
        scratch_shapes=[pltpu.VMEM((L + 2 * POOL_PAD, POOL_DIM), F32)],
        compiler_params=_params(("parallel",)),
    )(proj, dyb, pool_w_full, pool_scale)


def _hgrn_bwd(proj, o, dya, lbl, norm_g, B, L, grads, dests):
    nC = L // CHUNK
    n = len(grads)

    def body(*refs):
        q_ref, ff_ref, fb_ref, i_ref, og_ref, o_ref, dy_ref, lbl_ref, ng_ref = refs[:9]
        dp_ref, dlb_ref, dng_ref = refs[9 + n:12 + n]
        (do_sc, dq_sc, dv_sc, dP_sc, dKt_sc, dKe_sc, dbl_sc, dec_sc, ke_sc, ck_sc, upd_sc,
         dupd_sc) = refs[12 + 2 * n:24 + 2 * n]
        step_id = pl.program_id(0) * N_HEADS + pl.program_id(1)
        scatter = _Exchange(refs[9:9 + n], refs[12 + n:12 + 2 * n], refs[24 + 2 * n:], gather=False, dests=dests)

        @pl.when(step_id == 0)
        def _():
            scatter.start()

        lb_f, lb_b = _lower_bounds(lbl_ref)
        o_ = o_ref[...]
        r = lax.rsqrt(jnp.mean(o_ * o_, axis=-1, keepdims=True) + RMS_EPS)
        on = o_ * r
        og = og_ref[...]
        sog = _sig(og)
        dy = dy_ref[...]
        ng = ng_ref[...]
        dp_ref[4] = (dy * (on * ng) * (sog * (1.0 + og * (1.0 - sog)))).astype(MX)
        dn = dy * (og * sog)
        dng_ref[...] = jnp.sum(dn * on, axis=0, keepdims=True)
        don = dn * ng
        do_sc[...] = r * don - on * (r * jnp.mean(don * on, axis=-1, keepdims=True))

        dq_sc[...] = jnp.zeros_like(dq_sc)
        dv_sc[...] = jnp.zeros_like(dv_sc)

        def run_dir(f_ref, lb_row, rev, slot):
            def block(kb, carry):
                rows = _block_rows(kb)
                g = _block_gates(q_ref[rows, :], f_ref[rows, :], lb_row, rev)
                v = i_ref[rows, :]
                do = do_sc[rows, :]
                m, mt = _block_mask(rev), _block_mask(rev, True)
                at = jnp.where(mt, _dot_nt(g["Kt"], g["P"]), 0.0)
                da = jnp.where(m, _dot_nt(do, v), 0.0)
                dat = jnp.where(mt, _dot_nt(v, do), 0.0)
                dv_sc[rows, :] += _dot(at, do)
                dP_sc[rows, :] = _dot(da, g["Kt"])
                dKt_sc[rows, :] = _dot(dat, g["P"])
                ke_sc[rows, :] = g["Ke"].astype(MX)
                dec_sc[rows, :] = g["dec"]
                _chunk_outer_products(v, g["Ke"], upd_sc, kb)
                _chunk_outer_products(do, g["P"], dupd_sc, kb)
                return carry

            def fstep(c, st):
                rows = _chunk_rows(c)
                ck_sc[c] = st.astype(MX)
                dP_sc[rows, :] += _dot(do_sc[rows, :], st)
                dec = dec_sc[_chunk_rows(c, 1), :]
                return st * dec + upd_sc[c]

            n_units = nC // UNIT_CHUNKS
            zero_state = jnp.zeros((HEAD_DIM, HEAD_DIM), F32)
            _pipelined(n_units, rev,
                       lambda u, c: _unit_blocks(u, block, c), 0,
                       lambda u, st: _unit_chunks(u, rev, fstep, st), zero_state)

            def bstep(c, dst):
                rows = _chunk_rows(c)
                dec = dec_sc[_chunk_rows(c, 1), :]
                dKe_sc[rows, :] = _dot(i_ref[rows, :], dst)
                dv_sc[rows, :] += _dot_nt(ke_sc[rows, :], dst)
                dbl = dec * jnp.sum(dst * ck_sc[c].astype(F32), axis=0, keepdims=True)
                dbl_sc[rows, :] = jnp.broadcast_to(dbl, (CHUNK, HEAD_DIM))
                return dst * dec + dupd_sc[c]

            def finish(kb, acc):
                rows = _block_rows(kb)
                q_r = q_ref[rows, :]
                g = _block_gates(q_r, f_ref[rows, :], lb_row, rev)
                dP, dkt, dke = dP_sc[rows, :], dKt_sc[rows, :], dKe_sc[rows, :]
                e = dke * g["Ke"]
                dlf = (_seg_cumsum(dP * g["P"] - dkt * g["Kt"], not rev) + _seg_cumsum(e, rev) - e
                       + dbl_sc[rows, :])
                df = dlf / g["f"] - (dkt * g["enb"] + dke * g["eend"])
                dp_ref[slot, rows, :] = (df * (1.0 - lb_row) * (g["sg"] * (1.0 - g["sg"]))).astype(MX)
                dq_sc[rows, :] += (dP * g["eb"]) * (g["sq"] * (1.0 + q_r * (1.0 - g["sq"])))
                return acc + jnp.sum(df * (1.0 - g["sg"]), axis=0, keepdims=True)

            _, dlb = _pipelined(n_units, not rev,
                                lambda u, dst: _unit_chunks(u, not rev, bstep, dst), zero_state,
                                lambda u, acc: _unit_blocks(u, finish, acc), jnp.zeros((1, HEAD_DIM), F32))
            dlb_ref[slot - 1:slot, :] = dlb

        run_dir(ff_ref, lb_f, False, 1)
        run_dir(fb_ref, lb_b, True, 2)
        dp_ref[0] = dq_sc[...].astype(MX)
        dp_ref[3] = dv_sc[...].astype(MX)

        @pl.when(step_id == B * N_HEADS - 1)
        def _():
            scatter.wait()

    def blk(s):
        return pl.BlockSpec((None, None, L, HEAD_DIM), lambda b, h, s=s: (s, b, 0, h))

    seq = pl.BlockSpec((None, L, HEAD_DIM), lambda b, h: (b, 0, h))
    outs = pl.pallas_call(
        body, name="hgrn_bwd", grid=(B, N_HEADS),
        in_specs=[blk(0), blk(1), blk(2), blk(3), blk(4), seq, seq,
                  pl.BlockSpec((4, HEAD_DIM), lambda b, h: (0, h)),
                  pl.BlockSpec((1, HEAD_DIM), lambda b, h: (0, h))] + [ANY] * n,
        out_specs=[pl.BlockSpec((5, None, L, HEAD_DIM), lambda b, h: (0, b, 0, h)),
                   pl.BlockSpec((None, 2, HEAD_DIM), lambda b, h: (b, 0, h)),
                   pl.BlockSpec((None, 1, HEAD_DIM), lambda b, h: (b, 0, h))] + [ANY] * n,
        out_shape=[jax.ShapeDtypeStruct((5, B, L, D_MODEL), MX),
                   jax.ShapeDtypeStruct((B, 2, D_MODEL), F32),
                   jax.ShapeDtypeStruct((B, 1, D_MODEL), F32)] + _Exchange.out_shapes(grads, False),
        scratch_shapes=[pltpu.VMEM((L, HEAD_DIM), F32)] * 8
                       + [pltpu.VMEM((L, HEAD_DIM), MX), pltpu.VMEM((nC, HEAD_DIM, HEAD_DIM), MX),
                          pltpu.VMEM((nC, HEAD_DIM, HEAD_DIM), F32), pltpu.VMEM((nC, HEAD_DIM, HEAD_DIM), F32)]
                       + _Exchange.scratch(n),
        compiler_params=_params(("arbitrary", "arbitrary")),
    )(proj, proj, proj, proj, proj, o, dya, lbl, norm_g, *grads)
    return outs[0], outs[1], outs[2], outs[3:]


def _dproj_select(s, a5_ref, p_ref, g2_ref):
    return jnp.where(s < 5, a5_ref[...], jnp.where(s == 5, p_ref[...], g2_ref[...]))


def _dproj_specs(tm, tile_axis):
    def ix(args):
        return args[tile_axis], args[1 - tile_axis]
    a5 = pl.BlockSpec((None, tm, D_MODEL), lambda *a: (jnp.minimum(ix(a)[1], 4), ix(a)[0], 0))
    p = pl.BlockSpec((tm, D_MODEL), lambda *a: (ix(a)[0], 0))
    g2 = pl.BlockSpec((None, tm, D_MODEL), lambda *a: (jnp.clip(ix(a)[1] - 6, 0, 1), ix(a)[0], 0))
    return [a5, p, g2]


def _inproj_bwd(x2, dh, dproj5, dp, dgab, g_mix, w_in_g, d_win, dests):
    T = x2.shape[0]
    tm = min(512, T)
    nT = T // tm

    def body(a5_ref, p_ref, g2_ref, w_ref, x_ref, dh_ref, g_ref, dwin_ref, dx_ref, dg_ref, recv_ref, acc_sc, *sems):
        i, s = pl.program_id(0), pl.program_id(1)
        scatter = _Exchange([dwin_ref], [recv_ref], sems, gather=False, dests=[dests])

        @pl.when((i == 0) & (s == 0))
        def _():
            scatter.start()

        @pl.when(s == 0)
        def _():
            acc_sc[...] = jnp.zeros_like(acc_sc)

        @pl.when((i == 0) & (s == 0))
        def _():
            dg_ref[...] = jnp.zeros_like(dg_ref)

        acc_sc[...] += _dot_nt(_dproj_select(s, a5_ref, p_ref, g2_ref), w_ref[...])

        @pl.when(s == N_DEV - 1)
        def _():
            x = x_ref[...]
            r = lax.rsqrt(jnp.mean(x * x, axis=-1, keepdims=True) + RMS_EPS)
            xn = x * r
            du = acc_sc[...]
            dg_ref[...] += _fold8(du * xn)
            a = du * g_ref[...]
            dx_ref[...] = dh_ref[...] + r * a - xn * (r * jnp.mean(a * xn, axis=-1, keepdims=True))

        @pl.when((i == nT - 1) & (s == N_DEV - 1))
        def _():
            scatter.wait()

    tile = pl.BlockSpec((tm, D_MODEL), lambda i, s: (i, 0))
    return pl.pallas_call(
        body, name="inproj_bwd", grid=(nT, N_DEV),
        in_specs=_dproj_specs(tm, 0) + [pl.BlockSpec((None, D_MODEL, D_MODEL), lambda i, s: (s, 0, 0)),
                                        tile, tile, pl.BlockSpec((1, D_MODEL), lambda i, s: (0, 0)), ANY],
        out_specs=[tile, pl.BlockSpec((8, D_MODEL), lambda i, s: (0, 0)), ANY],
        out_shape=[jax.ShapeDtypeStruct((T, D_MODEL), F32), jax.ShapeDtypeStruct((8, D_MODEL), F32)]
                  + _Exchange.out_shapes([d_win], False),
        scratch_shapes=[pltpu.VMEM((tm, D_MODEL), F32)] + _Exchange.scratch(1),
        compiler_params=_params(("arbitrary", "arbitrary")),
    )(dproj5, dp, dgab, w_in_g, x2, dh, g_mix, d_win)


def _wgrad(a, g, name):
    Ba, T, K = a.shape
    Bg, _, Nn = g.shape
    nb = max(Ba, Bg)
    tm = min(BIG_TOKEN_TILE, T)
    nt = T // tm

    def body(a_ref, g_ref, out_ref, acc_sc):
        t = pl.program_id(1)

        @pl.when(t == 0)
        def _():
            acc_sc[...] = jnp.zeros_like(acc_sc)

        acc_sc[...] += _dot_tn(a_ref[...], g_ref[...])

        @pl.when(t == nt - 1)
        def _():
            out_ref[...] = acc_sc[...].astype(MX)

    return pl.pallas_call(
        body, name=name, grid=(nb, nt),
        in_specs=[pl.BlockSpec((None, tm, K), lambda s, t: (s if Ba > 1 else 0, t, 0)),
                  pl.BlockSpec((None, tm, Nn), lambda s, t: (s if Bg > 1 else 0, t, 0))],
        out_specs=pl.BlockSpec((None, K, Nn), lambda s, t: (s, 0, 0)),
        out_shape=jax.ShapeDtypeStruct((nb, K, Nn), MX),
        scratch_shapes=[pltpu.VMEM((K, Nn), F32)],
        compiler_params=_params(("parallel", "arbitrary")),
    )(a, g)


def _mesh_pos():
    return lax.axis_index("x"), lax.axis_index("y"), lax.axis_index("c")


def _device_of(p):
    return (p // 4, (p // 2) % 2, p % 2)


class _Exchange:
    def __init__(self, srcs, outs, sems, gather, dests=None):
        send_sems, recv_sems, local_sems = sems
        x, y, c = _mesh_pos()
        me = 4 * x + 2 * y + c
        self.sends, self.arrivals, self.mine = [], [], []
        for a, (src, out) in enumerate(zip(srcs, outs)):
            lo, hi = dests[a] if dests else (0, N_DEV)

            def piece(p, src=src, lo=lo, hi=hi):
                return src if gather else src.at[jnp.clip(p - lo, 0, hi - lo - 1)]

            def served(p, lo=lo, hi=hi):
                return None if (lo, hi) == (0, N_DEV) else (p >= lo) & (p < hi)

            self.mine.append((pltpu.make_async_copy(piece(me), out.at[me], local_sems.at[a]), served(me)))
            for j in range(1, N_DEV):
                to, frm = (me + j) % N_DEV, (me + N_DEV - j) % N_DEV
                pair = dict(send_sem=send_sems.at[7 * a + j - 1], recv_sem=recv_sems.at[7 * a + j - 1],
                            device_id_type=pl.DeviceIdType.MESH)
                self.sends.append((pltpu.make_async_remote_copy(
                    src_ref=piece(to), dst_ref=out.at[me], device_id=_device_of(to), **pair), served(to)))
                self.arrivals.append((pltpu.make_async_remote_copy(
                    src_ref=piece(frm), dst_ref=out.at[frm], device_id=_device_of(frm), **pair), served(me)))

    @staticmethod
    def _each(copies, act):
        for cp, takes_part in copies:
            if takes_part is None:
                act(cp)
            else:
                pl.when(takes_part)(functools.partial(act, cp))

    def start(self):
        self._each(self.mine + self.sends, lambda cp: cp.start())

    def wait(self):
        self._each(self.arrivals, lambda cp: cp.wait_recv())
        self._each(self.sends, lambda cp: cp.wait_send())
        self._each(self.mine, lambda cp: cp.wait())

    @staticmethod
    def scratch(n):
        return [pltpu.SemaphoreType.DMA((7 * n,)), pltpu.SemaphoreType.DMA((7 * n,)), pltpu.SemaphoreType.DMA((n,))]

    @staticmethod
    def out_shapes(arrays, gather):
        return [jax.ShapeDtypeStruct((N_DEV,) + (a.shape if gather else a.shape[1:]), a.dtype) for a in arrays]


def _all_reduce_small(v):
    R, C = v.shape

    def body(v_ref, out_ref, slots, send_sems, recv_sems):
        x, y, c = _mesh_pos()
        me = 4 * x + 2 * y + c

        def copy(j, to):
            return pltpu.make_async_remote_copy(
                src_ref=v_ref, dst_ref=slots.at[me],
                send_sem=send_sems.at[j - 1], recv_sem=recv_sems.at[j - 1],
                device_id=_device_of(to), device_id_type=pl.DeviceIdType.MESH)

        sends = [copy(j, (me + j) % N_DEV) for j in range(1, N_DEV)]
        for cp in sends:
            cp.start()
        slots[me] = v_ref[...]
        for j in range(1, N_DEV):
            frm = (me + N_DEV - j) % N_DEV
            pltpu.make_async_remote_copy(
                src_ref=v_ref, dst_ref=slots.at[frm], send_sem=send_sems.at[j - 1], recv_sem=recv_sems.at[j - 1],
                device_id=_device_of(frm), device_id_type=pl.DeviceIdType.MESH).wait_recv()
        for cp in sends:
            cp.wait_send()
        acc = slots[0]
        for p in range(1, N_DEV):
            acc = acc + slots[p]
        out_ref[...] = acc

    return pl.pallas_call(
        body, name="all_reduce_small",
        in_specs=[pl.BlockSpec(memory_space=pltpu.VMEM)], out_specs=pl.BlockSpec(memory_space=pltpu.VMEM),
        out_shape=jax.ShapeDtypeStruct((R, C), F32),
        scratch_shapes=[pltpu.VMEM((N_DEV, R, C), F32), pltpu.SemaphoreType.DMA((7,)), pltpu.SemaphoreType.DMA((7,))],
    )(v)


def _adamw_math(w, g, m, v):
    m = ADAM_B1 * m + (1.0 - ADAM_B1) * g
    v = ADAM_B2 * v + (1.0 - ADAM_B2) * (g * g)
    m_hat = m / (1.0 - ADAM_B1 ** ADAM_STEP)
    v_hat = v / (1.0 - ADAM_B2 ** ADAM_STEP)
    delta = -ADAM_LR * (m_hat / (jnp.sqrt(v_hat) + ADAM_EPS) + ADAM_WD * w)
    return delta, m, v


def _adamw_reduce(recvs, dests, w, m, v, name):
    R, C = w.shape
    tr = R if R <= 256 else 256
    while R % tr:
        tr //= 2
    n = len(recvs)

    def body(*refs):
        w_ref, m_ref, v_ref, g_ref, d_ref, nm_ref, nv_ref = refs[n:]

        def update(r_ref):
            g = r_ref[0].astype(F32)
            for p in range(1, N_DEV):
                g = g + r_ref[p].astype(F32)
            d, nm, nv = _adamw_math(w_ref[...], g, m_ref[...], v_ref[...])
            g_ref[...] = g
            d_ref[...] = d
            nm_ref[...] = nm
            nv_ref[...] = nv

        if n == 1:
            update(refs[0])
        else:
            x, y, c = _mesh_pos()
            me = 4 * x + 2 * y + c
            for r_ref, (lo, hi) in zip(refs[:n], dests):
                pl.when((me >= lo) & (me < hi))(functools.partial(update, r_ref))

    tile = pl.BlockSpec((tr, C), lambda i: (i, 0))
    shp = jax.ShapeDtypeStruct((R, C), F32)
    return pl.pallas_call(
        body, name=name, grid=(R // tr,),
        in_specs=[pl.BlockSpec((N_DEV, tr, C), lambda i: (0, i, 0))] * n + [tile, tile, tile],
        out_specs=[tile] * 4, out_shape=[shp] * 4,
        compiler_params=_params(("parallel",)),
    )(*recvs, w, m, v)


def _adamw_small(g, w, m, v):
    def body(g_ref, w_ref, m_ref, v_ref, go_ref, d_ref, nm_ref, nv_ref):
        go_ref[...] = g_ref[...]
        for d in range(2):
            p0 = _sig(w_ref[8 + 2 * d:9 + 2 * d, :] - w_ref[9 + 2 * d:10 + 2 * d, :])
            dl0 = g_ref[12 + d:13 + d, :] * p0 * (1.0 - p0)
            go_ref[8 + 2 * d:9 + 2 * d, :] = dl0
            go_ref[9 + 2 * d:10 + 2 * d, :] = -dl0
            go_ref[12 + d:13 + d, :] = jnp.zeros((1, D_MODEL), F32)
        d, nm, nv = _adamw_math(w_ref[...], go_ref[...], m_ref[...], v_ref[...])
        d_ref[...] = d
        nm_ref[...] = nm
        nv_ref[...] = nv

    shp = jax.ShapeDtypeStruct(g.shape, F32)
    vm = pl.BlockSpec(memory_space=pltpu.VMEM)
    return pl.pallas_call(body, name="adamw_small", in_specs=[vm] * 4, out_specs=[vm] * 4, out_shape=[shp] * 4)(g, w, m, v)


def _local_step(x, tgt, g_mix, lb_shard, norm_g, pool_scale, g_ffn, g_final, w_in_shard, late_shards):
    B, L, _ = x.shape
    T = B * L
    rows = D_MODEL // N_DEV
    x2, tgt2 = x.reshape(T, D_MODEL), tgt.reshape(T, D_MODEL)

    proj, u, w_in_g, lb_g = _rms_inproj(x2, g_mix, w_in_shard, lb_shard)
    lb = lb_g[:, :4].transpose(1, 0, 2).reshape(4, D_MODEL)
    proj4 = proj.reshape(N_DEV, B, L, D_MODEL)
    o, ya, (wa, wb, wo, wfi_g, wfo_g, pw_g) = _hgrn_fwd(proj4, lb, norm_g, B, L, late_shards)
    wa, wb, wo = (w_.reshape(D_MODEL, D_MODEL) for w_ in (wa, wb, wo))
    wfo_g = wfo_g.reshape(4, FF_BLOCK, D_MODEL)
    pool_w_full = pw_g.reshape(N_DEV, 4, 32, POOL_DIM).transpose(1, 0, 2, 3).reshape(4, POOL_DIM, POOL_DIM)
    yb = _pool_fwd(proj4, pool_w_full, pool_scale, B, L)
    ya2, yb2 = ya.reshape(T, D_MODEL), yb.reshape(T, D_MODEL)
    za, zb, mg, h = _merge_out(x2, proj, ya2, yb2, wa, wb, wo)
    gate, up, hid, u2, dh2, dh2b, loss_p, dg_final = _ffn_fwd_loss(h, tgt2, g_ffn, g_final, wfi_g, wfo_g)
    loss = jnp.sum(loss_p[:, 0, 0])

    dgu, dh, dhb, dg_ffn = _ffn_bwd(h, dh2, dh2b, gate, up, g_ffn, wfi_g, wfo_g)
    d_wfo = _wgrad(hid, dh2b[None], "wgrad_ffn_out")
    d_wfi = _wgrad(u2[None], dgu.reshape(N_DEV, T, FF_BLOCK), "wgrad_ffn_in")
    dza, dzb, dgab, dya, dyb = _merge_bwd(dhb, proj, za, zb, wa, wb, wo)
    d_wo = _wgrad(mg[None], dhb[None], "wgrad_out")
    d_wa = _wgrad(ya2[None], dza[None], "wgrad_branch_a")
    d_wb = _wgrad(yb2[None], dzb[None], "wgrad_branch_b")
    dp, dpw_p, dps_p = _pool_bwd(proj4, dyb.reshape(B, L, D_MODEL), pool_w_full, pool_scale, B, L)
    d_pw = dpw_p.sum(0).reshape(4, N_DEV, 32, POOL_DIM).transpose(1, 0, 2, 3).reshape(N_DEV, 128, POOL_DIM)
    dp2 = dp.reshape(T, D_MODEL)
    w_in_dests = [(5, 6), (6, 8), (0, 5)]
    d_win_pool = _wgrad(u[None], dp2[None], "wgrad_in_pool")
    d_win_gates = _wgrad(u[None], dgab, "wgrad_in_gates")
    slices = [d_wa.reshape(N_DEV, rows, D_MODEL), d_wb.reshape(N_DEV, rows, D_MODEL),
              d_wo.reshape(N_DEV, rows, D_MODEL), d_wfi, d_wfo.reshape(N_DEV, FF_BLOCK // 2, D_MODEL),
              d_pw.astype(MX), d_win_pool, d_win_gates]
    dests = [(0, N_DEV)] * 6 + w_in_dests[:2]
    dproj5, dlb_p, dng_p, recv = _hgrn_bwd(proj4, o, dya.reshape(B, L, D_MODEL), lb, norm_g, B, L, slices, dests)
    dproj5 = dproj5.reshape(5, T, D_MODEL)
    d_win_rec = _wgrad(u[None], dproj5, "wgrad_in_recurrence")
    grad_x, dg_mix, recv_win_rec = _inproj_bwd(x2, dh, dproj5, dp2, dgab, g_mix, w_in_g, d_win_rec, w_in_dests[2])

    small = dict(g_mix=dg_mix.sum(0), hgrn_norm_g=dng_p.sum((0, 1)), pool_scale=dps_p.sum((0, 1)),
                 g_ffn=dg_ffn.sum(0), g_final=dg_final.sum(0), lb=dlb_p.sum(0))
    recv_w_in = [recv[6], recv[7], recv_win_rec]
    return loss, grad_x.reshape(B, L, D_MODEL), (recv_w_in, w_in_dests), list(recv[:6]), small


def kernel(x, g_mix, w_in, lb_logits, hgrn_norm_g, pool_w, pool_scale, w_branch_a, w_branch_b, w_out, g_ffn, w_ffn_in, w_ffn_out, g_final, loss_target, m_g_mix, m_w_in, m_lb_logits, m_hgrn_norm_g, m_pool_w, m_pool_scale, m_w_branch_a, m_w_branch_b, m_w_out, m_g_ffn, m_w_ffn_in, m_w_ffn_out, m_g_final, v_g_mix, v_w_in, v_lb_logits, v_hgrn_norm_g, v_pool_w, v_pool_scale, v_w_branch_a, v_w_branch_b, v_w_out, v_g_ffn, v_w_ffn_in, v_w_ffn_out, v_g_final):
    me = 4 * lax.axis_index("x") + 2 * lax.axis_index("y") + lax.axis_index("c")

    late_shards = [w_branch_a[0].astype(MX), w_branch_b[0].astype(MX), w_out[0].astype(MX),
                   w_ffn_in[0].astype(MX), w_ffn_out[0].astype(MX), pool_w[0].reshape(4 * 32, POOL_DIM).astype(MX)]
    lb_shard = jnp.pad(lb_logits.reshape(4, HEAD_DIM), ((0, 4), (0, 0)))

    loss, grad_x, (recv_w_in, w_in_dests), recv_late, small = _local_step(
        x, loss_target, g_mix, lb_shard, hgrn_norm_g, pool_scale, g_ffn, g_final[None], w_in[0].astype(MX),
        late_shards)
    loss = lax.psum(loss, ("x", "y", "c"))

    packed = jnp.zeros((16, D_MODEL), F32)
    names = ["g_mix", "hgrn_norm_g", "pool_scale", "g_ffn", "g_final"]
    for i, nme in enumerate(names):
        packed = packed.at[i].set(small[nme])
    packed = packed.at[5:7].set(small["lb"])
    red = _all_reduce_small(packed)
    dlb_mine = lax.dynamic_slice_in_dim(red[5:7], me * HEAD_DIM, HEAD_DIM, axis=1)

    sw = jnp.zeros((16, D_MODEL), F32)
    sm = jnp.zeros((16, D_MODEL), F32)
    sv = jnp.ones((16, D_MODEL), F32)
    smalls = [(g_mix, m_g_mix, v_g_mix), (hgrn_norm_g, m_hgrn_norm_g, v_hgrn_norm_g),
              (pool_scale, m_pool_scale, v_pool_scale), (g_ffn, m_g_ffn, v_g_ffn),
              (g_final[None], m_g_final[None], v_g_final[None])]
    for i, (w_, m_, v_) in enumerate(smalls):
        sw, sm, sv = sw.at[i].set(w_[0]), sm.at[i].set(m_[0]), sv.at[i].set(v_[0])
    sg = red.at[5:].set(0.0)
    sg = sg.at[12:14, :HEAD_DIM].set(dlb_mine)
    sw = sw.at[8:12, :HEAD_DIM].set(lb_logits.reshape(4, HEAD_DIM))
    sm = sm.at[8:12, :HEAD_DIM].set(m_lb_logits.reshape(4, HEAD_DIM))
    sv = sv.at[8:12, :HEAD_DIM].set(v_lb_logits.reshape(4, HEAD_DIM))
    sg, sd, snm, snv = _adamw_small(sg, sw, sm, sv)

    def small_out(arr, i, like):
        return arr[i].reshape(like.shape)

    def lb_out(arr):
        return arr[8:12, :HEAD_DIM].reshape(2, 2, HEAD_DIM)

    order = ["w_in", "w_branch_a", "w_branch_b", "w_out", "w_ffn_in", "w_ffn_out", "pool_w"]
    params = dict(w_in=(w_in, m_w_in, v_w_in), w_branch_a=(w_branch_a, m_w_branch_a, v_w_branch_a),
                  w_branch_b=(w_branch_b, m_w_branch_b, v_w_branch_b), w_out=(w_out, m_w_out, v_w_out),
                  w_ffn_in=(w_ffn_in, m_w_ffn_in, v_w_ffn_in), w_ffn_out=(w_ffn_out, m_w_ffn_out, v_w_ffn_out),
                  pool_w=(pool_w, m_pool_w, v_pool_w))
    res = {}
    for nme, r, dests in zip(order, [recv_w_in] + [[r] for r in recv_late], [w_in_dests] + [None] * 6):
        w_, m_, v_ = params[nme]
        shape2 = r[0].shape[1:]
        outs = _adamw_reduce(r, dests, w_.reshape(shape2), m_.reshape(shape2), v_.reshape(shape2), "adamw_" + nme)
        res[nme] = [o_.reshape(w_.shape) for o_ in outs]

    def pick(k):
        small_src = [sg, sd, snm, snv][k]
        return [small_out(small_src, 0, g_mix), res["w_in"][k], lb_out(small_src), small_out(small_src, 1, hgrn_norm_g),
                res["pool_w"][k], small_out(small_src, 2, pool_scale), res["w_branch_a"][k], res["w_branch_b"][k],
                res["w_out"][k], small_out(small_src, 3, g_ffn), res["w_ffn_in"][k], res["w_ffn_out"][k],
                small_out(small_src, 4, g_final)]

    return (loss, grad_x, *pick(0), *pick(1), *pick(2), *pick(3))
```

```python
import functools

import jax
import jax.numpy as jnp
from jax import lax
from jax.experimental import pallas as pl
from jax.experimental.pallas import tpu as pltpu

F32 = jnp.float32
MX = jnp.bfloat16

D_MODEL = 1024
N_HEADS = 8
HEAD_DIM = 128
CHUNK = 16
POOL_WINDOWS = (2, 4, 8, 16)
POOL_DIM = 256
FF_BLOCK = 704
N_DEV = 8
RMS_EPS = 1e-6
ADAM_LR, ADAM_B1, ADAM_B2, ADAM_EPS, ADAM_WD, ADAM_STEP = 0.001, 0.9, 0.999, 1e-08, 0.01, 10
VMEM_LIMIT = 56 * 1024 * 1024
BIG_TOKEN_TILE = 1024
ANY = pl.BlockSpec(memory_space=pl.ANY)


def _params(sem=None):
    return pltpu.CompilerParams(dimension_semantics=sem, vmem_limit_bytes=VMEM_LIMIT)


def _dot(a, b):
    return lax.dot_general(a.astype(MX), b.astype(MX), (((1,), (0,)), ((), ())), preferred_element_type=F32)


def _dot_nt(a, b):
    return lax.dot_general(a.astype(MX), b.astype(MX), (((1,), (1,)), ((), ())), preferred_element_type=F32)


def _dot_tn(a, b):
    return lax.dot_general(a.astype(MX), b.astype(MX), (((0,), (0,)), ((), ())), preferred_element_type=F32)


def _sig(x):
    return 1.0 / (1.0 + jnp.exp(-x))


def _fold8(v):
    return v.reshape(v.shape[0] // 8, 8, v.shape[1]).sum(axis=0)


def _shift_rows(x, s):
    n = x.shape[0]
    row = lax.broadcasted_iota(jnp.int32, x.shape, 0)
    if s > 0:
        return jnp.where(row >= s, pltpu.roll(x, s, 0), 0.0)
    return jnp.where(row < n + s, pltpu.roll(x, n + s, 0), 0.0)


def _cumsum_rows(x, rev):
    s = 1
    while s < x.shape[0]:
        x = x + _shift_rows(x, -s if rev else s)
        s *= 2
    return x


def _gather_order():
    x, y, c = _mesh_pos()
    near, far = [(1 - x, y), (x, 1 - y)], (1 - x, 1 - y)
    return ([(x, y, c), (x, y, 1 - c)] + [(*chip, c) for chip in near] + [(*chip, 1 - c) for chip in near]
            + [(*far, c), (*far, 1 - c)])


def _rms_inproj(x2, g_mix, w_shard, lb_shard):
    T = x2.shape[0]
    tm = min(BIG_TOKEN_TILE, T)
    nT = T // tm
    block_of_step = jnp.stack([4 * px + 2 * py + pc for px, py, pc in _gather_order()]).astype(jnp.int32)

    def body(order_ref, x_ref, g_ref, w_ref, lb_ref, proj_ref, u_ref, wg_ref, lbg_ref,
             u_sc, w_sc, load_sem, send_sems, recv_sems, local_sem, *lb_sems):
        k, i = pl.program_id(0), pl.program_id(1)
        order = _gather_order()
        me, sibling = order[0], order[1]

        def slot(dev):
            return wg_ref.at[4 * dev[0] + 2 * dev[1] + dev[2]]

        def copy(n, block, to, src=None):
            return pltpu.make_async_remote_copy(
                src_ref=slot(block) if src is None else src, dst_ref=slot(block),
                send_sem=send_sems.at[n], recv_sem=recv_sems.at[n], device_id=to, device_id_type=pl.DeviceIdType.MESH)

        mine = pltpu.make_async_copy(w_ref, slot(me), local_sem)
        to_sibling = copy(0, me, sibling, src=w_ref)
        to_near = [copy(1, me, order[2], src=w_ref), copy(2, me, order[3], src=w_ref)]
        to_far = copy(5, me, order[6], src=w_ref)
        passed = {2: copy(3, order[2], sibling), 3: copy(4, order[3], sibling), 6: copy(6, order[6], sibling)}
        lb_gather = _Exchange([lb_ref], [lbg_ref], lb_sems, gather=True)

        @pl.when((k == 0) & (i == 0))
        def _():
            for cp in [mine, to_sibling] + to_near:
                cp.start()
            lb_gather.start()

        for s, block in enumerate(order):
            @pl.when((k == s) & (i == 0))
            def _(s=s, block=block):
                if s == 0:
                    mine.wait()
                else:
                    copy(s - 1, block, me).wait_recv()
                if s in passed:
                    passed[s].start()
                if s == 3:
                    for cp in to_near:
                        cp.wait_send()
                    to_far.start()
                load = pltpu.make_async_copy(slot(block), w_sc, load_sem)
                load.start()
                load.wait()

        rows = pl.ds(pl.multiple_of(i * tm, tm), tm)

        @pl.when(k == 0)
        def _():
            x = x_ref[...]
            r = lax.rsqrt(jnp.mean(x * x, axis=-1, keepdims=True) + RMS_EPS)
            u = (x * r * g_ref[...]).astype(MX)
            u_sc[rows, :] = u
            u_ref[...] = u

        proj_ref[...] = jnp.dot(u_sc[rows, :], w_sc[...], preferred_element_type=F32)

        @pl.when((k == N_DEV - 1) & (i == nT - 1))
        def _():
            for cp in [to_sibling, to_far] + list(passed.values()):
                cp.wait_send()
            lb_gather.wait()

    def tile_once(k, i, order_ref):
        return (jnp.where(k == 0, i, nT - 1), 0)

    grid_spec = pltpu.PrefetchScalarGridSpec(
        num_scalar_prefetch=1, grid=(N_DEV, nT),
        in_specs=[pl.BlockSpec((tm, D_MODEL), tile_once),
                  pl.BlockSpec((1, D_MODEL), lambda k, i, order_ref: (0, 0)), ANY, ANY],
        out_specs=[pl.BlockSpec((None, tm, D_MODEL), lambda k, i, order_ref: (order_ref[k], i, 0)),
                   pl.BlockSpec((tm, D_MODEL), tile_once), ANY, ANY],
        scratch_shapes=[pltpu.VMEM((T, D_MODEL), MX), pltpu.VMEM((D_MODEL, D_MODEL), MX),
                        pltpu.SemaphoreType.DMA(()), pltpu.SemaphoreType.DMA((7,)), pltpu.SemaphoreType.DMA((7,)),
                        pltpu.SemaphoreType.DMA(())] + _Exchange.scratch(1))
    return pl.pallas_call(
        body, name="rms_inproj", grid_spec=grid_spec,
        out_shape=[jax.ShapeDtypeStruct((N_DEV, T, D_MODEL), F32), jax.ShapeDtypeStruct((T, D_MODEL), MX),
                   jax.ShapeDtypeStruct((N_DEV, D_MODEL, D_MODEL), MX),
                   jax.ShapeDtypeStruct((N_DEV,) + lb_shard.shape, lb_shard.dtype)],
        compiler_params=_params(("arbitrary", "arbitrary")),
    )(block_of_step, x2, g_mix, w_shard, lb_shard)


HBLK = 128


def _seg_cumsum(x, rev):
    n = x.shape[0]
    pos = lax.broadcasted_iota(jnp.int32, x.shape, 0) & (CHUNK - 1)
    s = 1
    while s < CHUNK:
        if rev:
            x = x + jnp.where(pos < CHUNK - s, pltpu.roll(x, n - s, 0), 0.0)
        else:
            x = x + jnp.where(pos >= s, pltpu.roll(x, s, 0), 0.0)
        s *= 2
    return x


def _block_gates(q_r, f_r, lb_row, rev):
    sq = _sig(q_r)
    q = q_r * sq
    sg = _sig(f_r)
    f = lb_row + (1.0 - lb_row) * sg
    k = 1.0 - f
    lf = jnp.log(f)
    pre = _seg_cumsum(lf, False)
    suf = _seg_cumsum(lf, True)
    tot = pre + suf - lf
    b = suf if rev else pre
    eb = jnp.exp(b)
    enb = jnp.exp(-b)
    eend = jnp.exp(tot - b)
    return dict(sq=sq, sg=sg, f=f, eb=eb, enb=enb, eend=eend, dec=jnp.exp(tot),
                P=q * eb, Kt=k * enb, Ke=k * eend)


def _block_mask(rev, transposed=False):
    ri = lax.broadcasted_iota(jnp.int32, (HBLK, HBLK), 0)
    ci = lax.broadcasted_iota(jnp.int32, (HBLK, HBLK), 1)
    same = (ri // CHUNK) == (ci // CHUNK)
    return same & ((ci >= ri) if rev != transposed else (ci <= ri))


def _chunk_rows(c, rows=CHUNK):
    return pl.ds(c * CHUNK, rows)


def _block_rows(kb):
    return pl.ds(kb * HBLK, HBLK)


def _chunk_outer_products(a, b, out_sc, kb):
    a, b = a.astype(MX), b.astype(MX)
    for u in range(HBLK // CHUNK):
        r = slice(u * CHUNK, (u + 1) * CHUNK)
        out_sc[kb * (HBLK // CHUNK) + u] = _dot_tn(a[r, :], b[r, :])


UNIT_BLOCKS = 2
UNIT_CHUNKS = UNIT_BLOCKS * (HBLK // CHUNK)


def _unit_blocks(u, fn, carry):
    for b in range(UNIT_BLOCKS):
        carry = fn(u * UNIT_BLOCKS + b, carry)
    return carry


def _unit_chunks(u, descending, fn, carry):
    for j in range(UNIT_CHUNKS):
        carry = fn(u * UNIT_CHUNKS + (UNIT_CHUNKS - 1 - j if descending else j), carry)
    return carry


def _pipelined(n_units, descending, first, c1, second, c2):
    units = list(range(n_units))[::-1] if descending else list(range(n_units))
    c1 = first(units[0], c1)
    for t, u in enumerate(units):
        if t + 1 < n_units:
            c1 = first(units[t + 1], c1)
        c2 = second(u, c2)
    return c1, c2


def _lower_bounds(lbl_ref):
    return _sig(lbl_ref[0:1, :] - lbl_ref[1:2, :]), _sig(lbl_ref[2:3, :] - lbl_ref[3:4, :])


def _hgrn_fwd(proj, lbl, norm_g, B, L, shards):
    nC = L // CHUNK
    n = len(shards)
    relay_step = (11 * B * N_HEADS) // 16

    def body(*refs):
        q_ref, ff_ref, fb_ref, i_ref, og_ref, lbl_ref, ng_ref = refs[:7]
        o_ref, ya_ref = refs[7 + n:9 + n]
        o_sc, p_sc, dec_sc, upd_sc = refs[9 + 2 * n:13 + 2 * n]
        step_id = pl.program_id(0) * N_HEADS + pl.program_id(1)
        gather = _TwoLevelGather(refs[7:7 + n], refs[9 + n:9 + 2 * n], refs[13 + 2 * n:])

        @pl.when(step_id == 0)
        def _():
            gather.start()

        @pl.when(step_id == relay_step)
        def _():
            gather.relay()

        lb_f, lb_b = _lower_bounds(lbl_ref)
        o_sc[...] = jnp.zeros_like(o_sc)

        def run_dir(f_ref, lb_row, rev):
            def block(kb, carry):
                rows = _block_rows(kb)
                g = _block_gates(q_ref[rows, :], f_ref[rows, :], lb_row, rev)
                v = i_ref[rows, :]
                a = jnp.where(_block_mask(rev), _dot_nt(g["P"], g["Kt"]), 0.0)
                o_sc[rows, :] += _dot(a, v)
                p_sc[rows, :] = g["P"].astype(MX)
                dec_sc[rows, :] = g["dec"]
                _chunk_outer_products(v, g["Ke"], upd_sc, kb)
                return carry

            def step(c, st):
                rows = _chunk_rows(c)
                o_sc[rows, :] += _dot_nt(p_sc[rows, :], st)
                dec = dec_sc[_chunk_rows(c, 1), :]
                return st * dec + upd_sc[c]

            _pipelined(nC // UNIT_CHUNKS, rev,
                       lambda u, c: _unit_blocks(u, block, c), 0,
                       lambda u, st: _unit_chunks(u, rev, step, st), jnp.zeros((HEAD_DIM, HEAD_DIM), F32))

        run_dir(ff_ref, lb_f, False)
        run_dir(fb_ref, lb_b, True)
        o = o_sc[...]
        o_ref[...] = o
        on = o * lax.rsqrt(jnp.mean(o * o, axis=-1, keepdims=True) + RMS_EPS)
        og = og_ref[...]
        ya_ref[...] = ((on * ng_ref[...]) * (og * _sig(og))).astype(MX)

        @pl.when(step_id == B * N_HEADS - 1)
        def _():
            gather.finish()

    def blk(s):
        return pl.BlockSpec((None, None, L, HEAD_DIM), lambda b, h, s=s: (s, b, 0, h))

    out_blk = pl.BlockSpec((None, L, HEAD_DIM), lambda b, h: (b, 0, h))
    outs = pl.pallas_call(
        body, name="hgrn_fwd", grid=(B, N_HEADS),
        in_specs=[blk(0), blk(1), blk(2), blk(3), blk(4),
                  pl.BlockSpec((4, HEAD_DIM), lambda b, h: (0, h)),
                  pl.BlockSpec((1, HEAD_DIM), lambda b, h: (0, h))] + [ANY] * n,
        out_specs=[out_blk, out_blk] + [ANY] * n,
        out_shape=[jax.ShapeDtypeStruct((B, L, D_MODEL), F32), jax.ShapeDtypeStruct((B, L, D_MODEL), MX)]
                  + _Exchange.out_shapes(shards, True),
        scratch_shapes=[pltpu.VMEM((L, HEAD_DIM), F32), pltpu.VMEM((L, HEAD_DIM), MX),
                        pltpu.VMEM((L, HEAD_DIM), F32), pltpu.VMEM((nC, HEAD_DIM, HEAD_DIM), F32)]
                       + _Exchange.scratch(n),
        compiler_params=_params(("arbitrary", "arbitrary")),
    )(proj, proj, proj, proj, proj, lbl, norm_g, *shards)
    return outs[0], outs[1], outs[2:]


POOL_PAD = 8


def _pool_window(p, ext_sc, half, adjoint):
    L = p.shape[0]
    n = L + 2 * POOL_PAD
    ext_sc[0:POOL_PAD, :] = jnp.zeros((POOL_PAD, p.shape[1]), F32)
    ext_sc[POOL_PAD + L:n, :] = jnp.zeros((POOL_PAD, p.shape[1]), F32)
    ext_sc[POOL_PAD:POOL_PAD + L, :] = p
    x = ext_sc[...]
    s = x + pltpu.roll(x, 1 if adjoint else n - 1, 0)
    w = 1
    while w < half:
        s = pltpu.roll(s, w, 0) + pltpu.roll(s, n - w, 0)
        w *= 2
    ext_sc[...] = s
    return ext_sc[POOL_PAD:POOL_PAD + L, :]


def _pool_count(L, half):
    t = lax.broadcasted_iota(jnp.int32, (L, 1), 0)
    lo = jnp.clip(t - half + 1, 0, L)
    hi = jnp.clip(t + half + 1, 0, L)
    return (hi - lo).astype(F32)


def _pool_fwd(proj, pool_w_full, pool_scale, B, L):
    def body(p_ref, w_ref, s_ref, yb_ref, ext_sc):
        for g, win in enumerate(POOL_WINDOWS):
            cols = slice(g * POOL_DIM, (g + 1) * POOL_DIM)
            p = p_ref[:, cols]
            y = _pool_window(p, ext_sc, win // 2, False) / _pool_count(L, win // 2) - p
            yb_ref[:, cols] = (_dot(y, w_ref[g]) * s_ref[:, cols]).astype(MX)

    return pl.pallas_call(
        body, name="pool_fwd", grid=(B,),
        in_specs=[pl.BlockSpec((None, None, L, D_MODEL), lambda b: (5, b, 0, 0)),
                  pl.BlockSpec((4, POOL_DIM, POOL_DIM), lambda b: (0, 0, 0)),
                  pl.BlockSpec((1, D_MODEL), lambda b: (0, 0))],
        out_specs=pl.BlockSpec((None, L, D_MODEL), lambda b: (b, 0, 0)),
        out_shape=jax.ShapeDtypeStruct((B, L, D_MODEL), MX),
        scratch_shapes=[pltpu.VMEM((L + 2 * POOL_PAD, POOL_DIM), F32)],
        compiler_params=_params(("parallel",)),
    )(proj, pool_w_full, pool_scale)


def _merge_out(x2, proj, ya, yb, wa, wb, wo):
    T = x2.shape[0]
    tm = min(512, T)

    def body(x_ref, ga_ref, gb_ref, ya_ref, yb_ref, wa_ref, wb_ref, wo_ref, za_ref, zb_ref, mg_ref, h_ref):
        za = jnp.dot(ya_ref[...], wa_ref[...], preferred_element_type=F32)
        zb = jnp.dot(yb_ref[...], wb_ref[...], preferred_element_type=F32)
        mg = (_sig(ga_ref[...]) * za + _sig(gb_ref[...]) * zb).astype(MX)
        za_ref[...] = za
        zb_ref[...] = zb
        mg_ref[...] = mg
        h_ref[...] = x_ref[...] + jnp.dot(mg, wo_ref[...], preferred_element_type=F32)

    tile = pl.BlockSpec((tm, D_MODEL), lambda i: (i, 0))
    wspec = pl.BlockSpec((D_MODEL, D_MODEL), lambda i: (0, 0))
    return pl.pallas_call(
        body, name="merge_out", grid=(T // tm,),
        in_specs=[tile,
                  pl.BlockSpec((None, tm, D_MODEL), lambda i: (6, i, 0)),
                  pl.BlockSpec((None, tm, D_MODEL), lambda i: (7, i, 0)),
                  tile, tile, wspec, wspec, wspec],
        out_specs=[tile, tile, tile, tile],
        out_shape=[jax.ShapeDtypeStruct((T, D_MODEL), F32), jax.ShapeDtypeStruct((T, D_MODEL), F32),
                   jax.ShapeDtypeStruct((T, D_MODEL), MX), jax.ShapeDtypeStruct((T, D_MODEL), F32)],
        compiler_params=_params(("parallel",)),
    )(x2, proj, proj, ya, yb, wa, wb, wo)


def _ffn_fwd_loss(h, tgt, g_ffn, g_final, wfi_g, wfo_g):
    T = h.shape[0]
    tm = min(512, T)
    nT = T // tm

    def body(h_ref, t_ref, gf_ref, gl_ref, wg_ref, wu_ref, wo_ref,
             gate_ref, up_ref, hid_ref, u2_ref, dh2_ref, dh2b_ref, loss_ref, dgl_ref, u2_sc, acc_sc):
        i, j = pl.program_id(0), pl.program_id(1)

        @pl.when(j == 0)
        def _():
            hh = h_ref[...]
            r = lax.rsqrt(jnp.mean(hh * hh, axis=-1, keepdims=True) + RMS_EPS)
            u2 = (hh * r * gf_ref[...]).astype(MX)
            u2_sc[...] = u2
            u2_ref[...] = u2
            acc_sc[...] = jnp.zeros_like(acc_sc)

        @pl.when((i == 0) & (j == 0))
        def _():
            dgl_ref[...] = jnp.zeros_like(dgl_ref)

        gate = jnp.dot(u2_sc[...], wg_ref[...], preferred_element_type=F32)
        up = jnp.dot(u2_sc[...], wu_ref[...], preferred_element_type=F32)
        hid = ((gate * _sig(gate)) * up).astype(MX)
        gate_ref[...] = gate
        up_ref[...] = up
        hid_ref[...] = hid
        acc_sc[...] += jnp.dot(hid, wo_ref[...], preferred_element_type=F32)

        @pl.when(j == 3)
        def _():
            h2 = h_ref[...] + acc_sc[...]
            r = lax.rsqrt(jnp.mean(h2 * h2, axis=-1, keepdims=True) + RMS_EPS)
            hn = h2 * r
            gl = gl_ref[...]
            err = hn * gl - t_ref[...]
            tok = jnp.mean(err * err, axis=-1, keepdims=True)
            loss_ref[...] = jnp.full(loss_ref.shape, 0.5 * jnp.sum(tok), F32)
            dy = err * (1.0 / D_MODEL)
            dgl_ref[...] += _fold8(dy * hn)
            a = dy * gl
            dh2 = r * a - hn * (r * jnp.mean(a * hn, axis=-1, keepdims=True))
            dh2_ref[...] = dh2
            dh2b_ref[...] = dh2.astype(MX)

    tile = pl.BlockSpec((tm, D_MODEL), lambda i, j: (i, 0))
    vec = pl.BlockSpec((1, D_MODEL), lambda i, j: (0, 0))
    ftile = pl.BlockSpec((None, tm, FF_BLOCK), lambda i, j: (j, i, 0))
    return pl.pallas_call(
        body, name="ffn_fwd_loss", grid=(nT, 4),
        in_specs=[tile, tile, vec, vec,
                  pl.BlockSpec((None, D_MODEL, FF_BLOCK), lambda i, j: (j, 0, 0)),
                  pl.BlockSpec((None, D_MODEL, FF_BLOCK), lambda i, j: (j + 4, 0, 0)),
                  pl.BlockSpec((None, FF_BLOCK, D_MODEL), lambda i, j: (j, 0, 0))],
        out_specs=[ftile, ftile, ftile, tile, tile, tile,
                   pl.BlockSpec((None, 8, 128), lambda i, j: (i, 0, 0)),
                   pl.BlockSpec((8, D_MODEL), lambda i, j: (0, 0))],
        out_shape=[jax.ShapeDtypeStruct((4, T, FF_BLOCK), F32), jax.ShapeDtypeStruct((4, T, FF_BLOCK), F32),
                   jax.ShapeDtypeStruct((4, T, FF_BLOCK), MX), jax.ShapeDtypeStruct((T, D_MODEL), MX),
                   jax.ShapeDtypeStruct((T, D_MODEL), F32), jax.ShapeDtypeStruct((T, D_MODEL), MX),
                   jax.ShapeDtypeStruct((nT, 8, 128), F32), jax.ShapeDtypeStruct((8, D_MODEL), F32)],
        scratch_shapes=[pltpu.VMEM((tm, D_MODEL), MX), pltpu.VMEM((tm, D_MODEL), F32)],
        compiler_params=_params(("arbitrary", "arbitrary")),
    )(h, tgt, g_ffn, g_final, wfi_g, wfi_g, wfo_g)


def _ffn_bwd(h, dh2, dh2b, gate, up, g_ffn, wfi_g, wfo_g):
    T = h.shape[0]
    tm = min(512, T)

    def body(h_ref, dh2_ref, dh2b_ref, gate_ref, up_ref, gf_ref, wg_ref, wu_ref, wo_ref,
             dgu_ref, dh_ref, dhb_ref, dgf_ref, acc_sc):
        i, j = pl.program_id(0), pl.program_id(1)

        @pl.when(j == 0)
        def _():
            acc_sc[...] = jnp.zeros_like(acc_sc)

        @pl.when((i == 0) & (j == 0))
        def _():
            dgf_ref[...] = jnp.zeros_like(dgf_ref)

        dhid = _dot_nt(dh2b_ref[...], wo_ref[...])
        gate, up = gate_ref[...], up_ref[...]
        sg = _sig(gate)
        dgate = (dhid * up * (sg * (1.0 + gate * (1.0 - sg)))).astype(MX)
        dup = (dhid * (gate * sg)).astype(MX)
        dgu_ref[0] = dgate
        dgu_ref[1] = dup
        acc_sc[...] += _dot_nt(dgate, wg_ref[...]) + _dot_nt(dup, wu_ref[...])

        @pl.when(j == 3)
        def _():
            hh = h_ref[...]
            r = lax.rsqrt(jnp.mean(hh * hh, axis=-1, keepdims=True) + RMS_EPS)
            hn = hh * r
            du2 = acc_sc[...]
            dgf_ref[...] += _fold8(du2 * hn)
            a = du2 * gf_ref[...]
            dh = dh2_ref[...] + r * a - hn * (r * jnp.mean(a * hn, axis=-1, keepdims=True))
            dh_ref[...] = dh
            dhb_ref[...] = dh.astype(MX)

    tile = pl.BlockSpec((tm, D_MODEL), lambda i, j: (i, 0))
    ftile = pl.BlockSpec((None, tm, FF_BLOCK), lambda i, j: (j, i, 0))
    return pl.pallas_call(
        body, name="ffn_bwd", grid=(T // tm, 4),
        in_specs=[tile, tile, tile, ftile, ftile,
                  pl.BlockSpec((1, D_MODEL), lambda i, j: (0, 0)),
                  pl.BlockSpec((None, D_MODEL, FF_BLOCK), lambda i, j: (j, 0, 0)),
                  pl.BlockSpec((None, D_MODEL, FF_BLOCK), lambda i, j: (j + 4, 0, 0)),
                  pl.BlockSpec((None, FF_BLOCK, D_MODEL), lambda i, j: (j, 0, 0))],
        out_specs=[pl.BlockSpec((2, None, tm, FF_BLOCK), lambda i, j: (0, j, i, 0)),
                   tile, tile, pl.BlockSpec((8, D_MODEL), lambda i, j: (0, 0))],
        out_shape=[jax.ShapeDtypeStruct((2, 4, T, FF_BLOCK), MX),
                   jax.ShapeDtypeStruct((T, D_MODEL), F32), jax.ShapeDtypeStruct((T, D_MODEL), MX),
                   jax.ShapeDtypeStruct((8, D_MODEL), F32)],
        scratch_shapes=[pltpu.VMEM((tm, D_MODEL), F32)],
        compiler_params=_params(("arbitrary", "arbitrary")),
    )(h, dh2, dh2b, gate, up, g_ffn, wfi_g, wfi_g, wfo_g)


def _merge_bwd(dhb, proj, za, zb, wa, wb, wo):
    T = dhb.shape[0]
    tm = min(512, T)

    def body(dh_ref, ga_ref, gb_ref, za_ref, zb_ref, wa_ref, wb_ref, wo_ref,
             dza_ref, dzb_ref, dgab_ref, dya_ref, dyb_ref):
        dm = _dot_nt(dh_ref[...], wo_ref[...])
        sa, sb = _sig(ga_ref[...]), _sig(gb_ref[...])
        dza = (dm * sa).astype(MX)
        dzb = (dm * sb).astype(MX)
        dza_ref[...] = dza
        dzb_ref[...] = dzb
        dgab_ref[0] = (dm * za_ref[...] * (sa * (1.0 - sa))).astype(MX)
        dgab_ref[1] = (dm * zb_ref[...] * (sb * (1.0 - sb))).astype(MX)
        dya_ref[...] = _dot_nt(dza, wa_ref[...])
        dyb_ref[...] = _dot_nt(dzb, wb_ref[...])

    tile = pl.BlockSpec((tm, D_MODEL), lambda i: (i, 0))
    wspec = pl.BlockSpec((D_MODEL, D_MODEL), lambda i: (0, 0))
    return pl.pallas_call(
        body, name="merge_bwd", grid=(T // tm,),
        in_specs=[tile,
                  pl.BlockSpec((None, tm, D_MODEL), lambda i: (6, i, 0)),
                  pl.BlockSpec((None, tm, D_MODEL), lambda i: (7, i, 0)),
                  tile, tile, wspec, wspec, wspec],
        out_specs=[tile, tile, pl.BlockSpec((2, tm, D_MODEL), lambda i: (0, i, 0)), tile, tile],
        out_shape=[jax.ShapeDtypeStruct((T, D_MODEL), MX), jax.ShapeDtypeStruct((T, D_MODEL), MX),
                   jax.ShapeDtypeStruct((2, T, D_MODEL), MX),
                   jax.ShapeDtypeStruct((T, D_MODEL), F32), jax.ShapeDtypeStruct((T, D_MODEL), F32)],
        compiler_params=_params(("parallel",)),
    )(dhb, proj, proj, za, zb, wa, wb, wo)


def _pool_bwd(proj, dyb, pool_w_full, pool_scale, B, L):
    def body(p_ref, dy_ref, w_ref, s_ref, dp_ref, dw_ref, ds_ref, ext_sc):
        for g, win in enumerate(POOL_WINDOWS):
            cols = slice(g * POOL_DIM, (g + 1) * POOL_DIM)
            p = p_ref[:, cols]
            cnt = _pool_count(L, win // 2)
            y = _pool_window(p, ext_sc, win // 2, False) / cnt - p
            z = _dot(y, w_ref[g])
            dyb_g = dy_ref[:, cols]
            ds_ref[:, cols] = jnp.sum(dyb_g * z, axis=0, keepdims=True)
            dz = dyb_g * s_ref[:, cols]
            dw_ref[g] = _dot_tn(y, dz)
            dy = _dot_nt(dz, w_ref[g])
            dp_ref[:, cols] = (_pool_window(dy / cnt, ext_sc, win // 2, True) - dy).astype(MX)

    seq = pl.BlockSpec((None, L, D_MODEL), lambda b: (b, 0, 0))
    return pl.pallas_call(
        body, name="pool_bwd", grid=(B,),
        in_specs=[pl.BlockSpec((None, None, L, D_MODEL), lambda b: (5, b, 0, 0)), seq,
                  pl.BlockSpec((4, POOL_DIM, POOL_DIM), lambda b: (0, 0, 0)),
                  pl.BlockSpec((1, D_MODEL), lambda b: (0, 0))],
        out_specs=[seq, pl.BlockSpec((None, 4, POOL_DIM, POOL_DIM), lambda b: (b, 0, 0, 0)),
                   pl.BlockSpec((None, 1, D_MODEL), lambda b: (b, 0, 0))],
        out_shape=[jax.ShapeDtypeStruct((B, L, D_MODEL), MX),
                   jax.ShapeDtypeStruct((B, 4, POOL_DIM, POOL_DIM), F32),
                   jax.ShapeDtypeStruct((B, 1, D_MODEL), F32)],
        scratch_shapes=[pltpu.VMEM((L + 2 * POOL_PAD, POOL_DIM), F32)],
        compiler_params=_params(("parallel",)),
    )(proj, dyb, pool_w_full, pool_scale)


def _hgrn_bwd(proj, o, dya, lbl, norm_g, B, L, grads, dests):
    nC = L // CHUNK
    n = len(grads)

    def body(*refs):
        q_ref, ff_ref, fb_ref, i_ref, og_ref, o_ref, dy_ref, lbl_ref, ng_ref = refs[:9]
        dp_ref, dlb_ref, dng_ref = refs[9 + n:12 + n]
        (do_sc, dq_sc, dv_sc, dP_sc, dKt_sc, dKe_sc, dbl_sc, dec_sc, ke_sc, ck_sc, upd_sc,
         dupd_sc) = refs[12 + 2 * n:24 + 2 * n]
        step_id = pl.program_id(0) * N_HEADS + pl.program_id(1)
        scatter = _Exchange(refs[9:9 + n], refs[12 + n:12 + 2 * n], refs[24 + 2 * n:], gather=False, dests=dests)

        @pl.when(step_id == 0)
        def _():
            scatter.start()

        lb_f, lb_b = _lower_bounds(lbl_ref)
        o_ = o_ref[...]
        r = lax.rsqrt(jnp.mean(o_ * o_, axis=-1, keepdims=True) + RMS_EPS)
        on = o_ * r
        og = og_ref[...]
        sog = _sig(og)
        dy = dy_ref[...]
        ng = ng_ref[...]
        dp_ref[4] = (dy * (on * ng) * (sog * (1.0 + og * (1.0 - sog)))).astype(MX)
        dn = dy * (og * sog)
        dng_ref[...] = jnp.sum(dn * on, axis=0, keepdims=True)
        don = dn * ng
        do_sc[...] = r * don - on * (r * jnp.mean(don * on, axis=-1, keepdims=True))

        dq_sc[...] = jnp.zeros_like(dq_sc)
        dv_sc[...] = jnp.zeros_like(dv_sc)

        def run_dir(f_ref, lb_row, rev, slot):
            def block(kb, carry):
                rows = _block_rows(kb)
                g = _block_gates(q_ref[rows, :], f_ref[rows, :], lb_row, rev)
                v = i_ref[rows, :]
                do = do_sc[rows, :]
                m, mt = _block_mask(rev), _block_mask(rev, True)
                at = jnp.where(mt, _dot_nt(g["Kt"], g["P"]), 0.0)
                da = jnp.where(m, _dot_nt(do, v), 0.0)
                dat = jnp.where(mt, _dot_nt(v, do), 0.0)
                dv_sc[rows, :] += _dot(at, do)
                dP_sc[rows, :] = _dot(da, g["Kt"])
                dKt_sc[rows, :] = _dot(dat, g["P"])
                ke_sc[rows, :] = g["Ke"].astype(MX)
                dec_sc[rows, :] = g["dec"]
                _chunk_outer_products(v, g["Ke"], upd_sc, kb)
                _chunk_outer_products(do, g["P"], dupd_sc, kb)
                return carry

            def fstep(c, st):
                rows = _chunk_rows(c)
                ck_sc[c] = st.astype(MX)
                dP_sc[rows, :] += _dot(do_sc[rows, :], st)
                dec = dec_sc[_chunk_rows(c, 1), :]
                return st * dec + upd_sc[c]

            n_units = nC // UNIT_CHUNKS
            zero_state = jnp.zeros((HEAD_DIM, HEAD_DIM), F32)
            _pipelined(n_units, rev,
                       lambda u, c: _unit_blocks(u, block, c), 0,
                       lambda u, st: _unit_chunks(u, rev, fstep, st), zero_state)

            def bstep(c, dst):
                rows = _chunk_rows(c)
                dec = dec_sc[_chunk_rows(c, 1), :]
                dKe_sc[rows, :] = _dot(i_ref[rows, :], dst)
                dv_sc[rows, :] += _dot_nt(ke_sc[rows, :], dst)
                dbl = dec * jnp.sum(dst * ck_sc[c].astype(F32), axis=0, keepdims=True)
                dbl_sc[rows, :] = jnp.broadcast_to(dbl, (CHUNK, HEAD_DIM))
                return dst * dec + dupd_sc[c]

            def finish(kb, acc):
                rows = _block_rows(kb)
                q_r = q_ref[rows, :]
                g = _block_gates(q_r, f_ref[rows, :], lb_row, rev)
                dP, dkt, dke = dP_sc[rows, :], dKt_sc[rows, :], dKe_sc[rows, :]
                e = dke * g["Ke"]
                dlf = (_seg_cumsum(dP * g["P"] - dkt * g["Kt"], not rev) + _seg_cumsum(e, rev) - e
                       + dbl_sc[rows, :])
                df = dlf / g["f"] - (dkt * g["enb"] + dke * g["eend"])
                dp_ref[slot, rows, :] = (df * (1.0 - lb_row) * (g["sg"] * (1.0 - g["sg"]))).astype(MX)
                dq_sc[rows, :] += (dP * g["eb"]) * (g["sq"] * (1.0 + q_r * (1.0 - g["sq"])))
                return acc + jnp.sum(df * (1.0 - g["sg"]), axis=0, keepdims=True)

            _, dlb = _pipelined(n_units, not rev,
                                lambda u, dst: _unit_chunks(u, not rev, bstep, dst), zero_state,
                                lambda u, acc: _unit_blocks(u, finish, acc), jnp.zeros((1, HEAD_DIM), F32))
            dlb_ref[slot - 1:slot, :] = dlb

        run_dir(ff_ref, lb_f, False, 1)
        run_dir(fb_ref, lb_b, True, 2)
        dp_ref[0] = dq_sc[...].astype(MX)
        dp_ref[3] = dv_sc[...].astype(MX)

        @pl.when(step_id == B * N_HEADS - 1)
        def _():
            scatter.wait()

    def blk(s):
        return pl.BlockSpec((None, None, L, HEAD_DIM), lambda b, h, s=s: (s, b, 0, h))

    seq = pl.BlockSpec((None, L, HEAD_DIM), lambda b, h: (b, 0, h))
    outs = pl.pallas_call(
        body, name="hgrn_bwd", grid=(B, N_HEADS),
        in_specs=[blk(0), blk(1), blk(2), blk(3), blk(4), seq, seq,
                  pl.BlockSpec((4, HEAD_DIM), lambda b, h: (0, h)),
                  pl.BlockSpec((1, HEAD_DIM), lambda b, h: (0, h))] + [ANY] * n,
        out_specs=[pl.BlockSpec((5, None, L, HEAD_DIM), lambda b, h: (0, b, 0, h)),
                   pl.BlockSpec((None, 2, HEAD_DIM), lambda b, h: (b, 0, h)),
                   pl.BlockSpec((None, 1, HEAD_DIM), lambda b, h: (b, 0, h))] + [ANY] * n,
        out_shape=[jax.ShapeDtypeStruct((5, B, L, D_MODEL), MX),
                   jax.ShapeDtypeStruct((B, 2, D_MODEL), F32),
                   jax.ShapeDtypeStruct((B, 1, D_MODEL), F32)] + _Exchange.out_shapes(grads, False),
        scratch_shapes=[pltpu.VMEM((L, HEAD_DIM), F32)] * 8
                       + [pltpu.VMEM((L, HEAD_DIM), MX), pltpu.VMEM((nC, HEAD_DIM, HEAD_DIM), MX),
                          pltpu.VMEM((nC, HEAD_DIM, HEAD_DIM), F32), pltpu.VMEM((nC, HEAD_DIM, HEAD_DIM), F32)]
                       + _Exchange.scratch(n),
        compiler_params=_params(("arbitrary", "arbitrary")),
    )(proj, proj, proj, proj, proj, o, dya, lbl, norm_g, *grads)
    return outs[0], outs[1], outs[2], outs[3:]


def _dproj_select(s, a5_ref, p_ref, g2_ref):
    return jnp.where(s < 5, a5_ref[...], jnp.where(s == 5, p_ref[...], g2_ref[...]))


def _dproj_specs(tm, tile_axis):
    def ix(args):
        return args[tile_axis], args[1 - tile_axis]
    a5 = pl.BlockSpec((None, tm, D_MODEL), lambda *a: (jnp.minimum(ix(a)[1], 4), ix(a)[0], 0))
    p = pl.BlockSpec((tm, D_MODEL), lambda *a: (ix(a)[0], 0))
    g2 = pl.BlockSpec((None, tm, D_MODEL), lambda *a: (jnp.clip(ix(a)[1] - 6, 0, 1), ix(a)[0], 0))
    return [a5, p, g2]


def _inproj_bwd(x2, dh, dproj5, dp, dgab, g_mix, w_in_g, d_win, dests):
    T = x2.shape[0]
    tm = min(512, T)
    nT = T // tm

    def body(a5_ref, p_ref, g2_ref, w_ref, x_ref, dh_ref, g_ref, dwin_ref, dx_ref, dg_ref, recv_ref, acc_sc, *sems):
        i, s = pl.program_id(0), pl.program_id(1)
        scatter = _Exchange([dwin_ref], [recv_ref], sems, gather=False, dests=[dests])

        @pl.when((i == 0) & (s == 0))
        def _():
            scatter.start()

        @pl.when(s == 0)
        def _():
            acc_sc[...] = jnp.zeros_like(acc_sc)

        @pl.when((i == 0) & (s == 0))
        def _():
            dg_ref[...] = jnp.zeros_like(dg_ref)

        acc_sc[...] += _dot_nt(_dproj_select(s, a5_ref, p_ref, g2_ref), w_ref[...])

        @pl.when(s == N_DEV - 1)
        def _():
            x = x_ref[...]
            r = lax.rsqrt(jnp.mean(x * x, axis=-1, keepdims=True) + RMS_EPS)
            xn = x * r
            du = acc_sc[...]
            dg_ref[...] += _fold8(du * xn)
            a = du * g_ref[...]
            dx_ref[...] = dh_ref[...] + r * a - xn * (r * jnp.mean(a * xn, axis=-1, keepdims=True))

        @pl.when((i == nT - 1) & (s == N_DEV - 1))
        def _():
            scatter.wait()

    tile = pl.BlockSpec((tm, D_MODEL), lambda i, s: (i, 0))
    return pl.pallas_call(
        body, name="inproj_bwd", grid=(nT, N_DEV),
        in_specs=_dproj_specs(tm, 0) + [pl.BlockSpec((None, D_MODEL, D_MODEL), lambda i, s: (s, 0, 0)),
                                        tile, tile, pl.BlockSpec((1, D_MODEL), lambda i, s: (0, 0)), ANY],
        out_specs=[tile, pl.BlockSpec((8, D_MODEL), lambda i, s: (0, 0)), ANY],
        out_shape=[jax.ShapeDtypeStruct((T, D_MODEL), F32), jax.ShapeDtypeStruct((8, D_MODEL), F32)]
                  + _Exchange.out_shapes([d_win], False),
        scratch_shapes=[pltpu.VMEM((tm, D_MODEL), F32)] + _Exchange.scratch(1),
        compiler_params=_params(("arbitrary", "arbitrary")),
    )(dproj5, dp, dgab, w_in_g, x2, dh, g_mix, d_win)


def _wgrad(a, g, name):
    Ba, T, K = a.shape
    Bg, _, Nn = g.shape
    nb = max(Ba, Bg)
    tm = min(BIG_TOKEN_TILE, T)
    nt = T // tm

    def body(a_ref, g_ref, out_ref, acc_sc):
        t = pl.program_id(1)

        @pl.when(t == 0)
        def _():
            acc_sc[...] = jnp.zeros_like(acc_sc)

        acc_sc[...] += _dot_tn(a_ref[...], g_ref[...])

        @pl.when(t == nt - 1)
        def _():
            out_ref[...] = acc_sc[...].astype(MX)

    return pl.pallas_call(
        body, name=name, grid=(nb, nt),
        in_specs=[pl.BlockSpec((None, tm, K), lambda s, t: (s if Ba > 1 else 0, t, 0)),
                  pl.BlockSpec((None, tm, Nn), lambda s, t: (s if Bg > 1 else 0, t, 0))],
        out_specs=pl.BlockSpec((None, K, Nn), lambda s, t: (s, 0, 0)),
        out_shape=jax.ShapeDtypeStruct((nb, K, Nn), MX),
        scratch_shapes=[pltpu.VMEM((K, Nn), F32)],
        compiler_params=_params(("parallel", "arbitrary")),
    )(a, g)


def _mesh_pos():
    return lax.axis_index("x"), lax.axis_index("y"), lax.axis_index("c")


def _device_of(p):
    return (p // 4, (p // 2) % 2, p % 2)


class _Exchange:
    def __init__(self, srcs, outs, sems, gather, dests=None):
        send_sems, recv_sems, local_sems = sems
        x, y, c = _mesh_pos()
        me = 4 * x + 2 * y + c
        self.sends, self.arrivals, self.mine = [], [], []
        for a, (src, out) in enumerate(zip(srcs, outs)):
            lo, hi = dests[a] if dests else (0, N_DEV)

            def piece(p, src=src, lo=lo, hi=hi):
                return src if gather else src.at[jnp.clip(p - lo, 0, hi - lo - 1)]

            def served(p, lo=lo, hi=hi):
                return None if (lo, hi) == (0, N_DEV) else (p >= lo) & (p < hi)

            self.mine.append((pltpu.make_async_copy(piece(me), out.at[me], local_sems.at[a]), served(me)))
            for j in range(1, N_DEV):
                to, frm = (me + j) % N_DEV, (me + N_DEV - j) % N_DEV
                pair = dict(send_sem=send_sems.at[7 * a + j - 1], recv_sem=recv_sems.at[7 * a + j - 1],
                            device_id_type=pl.DeviceIdType.MESH)
                self.sends.append((pltpu.make_async_remote_copy(
                    src_ref=piece(to), dst_ref=out.at[me], device_id=_device_of(to), **pair), served(to)))
                self.arrivals.append((pltpu.make_async_remote_copy(
                    src_ref=piece(frm), dst_ref=out.at[frm], device_id=_device_of(frm), **pair), served(me)))

    @staticmethod
    def _each(copies, act):
        for cp, takes_part in copies:
            if takes_part is None:
                act(cp)
            else:
                pl.when(takes_part)(functools.partial(act, cp))

    def start(self):
        self._each(self.mine + self.sends, lambda cp: cp.start())

    def wait(self):
        self._each(self.arrivals, lambda cp: cp.wait_recv())
        self._each(self.sends, lambda cp: cp.wait_send())
        self._each(self.mine, lambda cp: cp.wait())

    @staticmethod
    def scratch(n):
        return [pltpu.SemaphoreType.DMA((7 * n,)), pltpu.SemaphoreType.DMA((7 * n,)), pltpu.SemaphoreType.DMA((n,))]

    @staticmethod
    def out_shapes(arrays, gather):
        return [jax.ShapeDtypeStruct((N_DEV,) + (a.shape if gather else a.shape[1:]), a.dtype) for a in arrays]


class _TwoLevelGather:
    OVER_ICI = (2, 3, 6)

    def __init__(self, srcs, outs, sems):
        send_sems, recv_sems, local_sems = sems
        order = _gather_order()
        me, sibling = order[0], order[1]
        self.mine, self.first, self.passed, self.arrivals = [], [], [], {}
        for a, (src, out) in enumerate(zip(srcs, outs)):
            def copy(s, block, to, src_ref=None, out=out, a=a):
                slot = out.at[4 * block[0] + 2 * block[1] + block[2]]
                return pltpu.make_async_remote_copy(
                    src_ref=slot if src_ref is None else src_ref, dst_ref=slot,
                    send_sem=send_sems.at[7 * a + s - 1], recv_sem=recv_sems.at[7 * a + s - 1],
                    device_id=to, device_id_type=pl.DeviceIdType.MESH)

            self.mine.append(pltpu.make_async_copy(src, out.at[4 * me[0] + 2 * me[1] + me[2]], local_sems.at[a]))
            self.first.append(copy(1, me, sibling, src))
            self.first += [copy(s, me, order[s], src) for s in self.OVER_ICI]
            self.passed += [(s, copy(s + (1 if s == 6 else 2), order[s], sibling)) for s in self.OVER_ICI]
            for s in range(1, N_DEV):
                self.arrivals[(a, s)] = copy(s, order[s], me)
        self.n = len(srcs)

    def start(self):
        for cp in self.mine + self.first:
            cp.start()

    def relay(self):
        for k, (s, cp) in enumerate(self.passed):
            self.arrivals[(k // len(self.OVER_ICI), s)].wait_recv()
            cp.start()

    def finish(self):
        for a in range(self.n):
            for s in range(1, N_DEV):
                if s not in self.OVER_ICI:
                    self.arrivals[(a, s)].wait_recv()
        for cp in self.first + [cp for _, cp in self.passed]:
            cp.wait_send()
        for cp in self.mine:
            cp.wait()


def _all_reduce_small(v):
    R, C = v.shape

    def body(v_ref, out_ref, slots, send_sems, recv_sems):
        x, y, c = _mesh_pos()
        me = 4 * x + 2 * y + c

        def copy(j, to):
            return pltpu.make_async_remote_copy(
                src_ref=v_ref, dst_ref=slots.at[me],
                send_sem=send_sems.at[j - 1], recv_sem=recv_sems.at[j - 1],
                device_id=_device_of(to), device_id_type=pl.DeviceIdType.MESH)

        sends = [copy(j, (me + j) % N_DEV) for j in range(1, N_DEV)]
        for cp in sends:
            cp.start()
        slots[me] = v_ref[...]
        for j in range(1, N_DEV):
            frm = (me + N_DEV - j) % N_DEV
            pltpu.make_async_remote_copy(
                src_ref=v_ref, dst_ref=slots.at[frm], send_sem=send_sems.at[j - 1], recv_sem=recv_sems.at[j - 1],
                device_id=_device_of(frm), device_id_type=pl.DeviceIdType.MESH).wait_recv()
        for cp in sends:
            cp.wait_send()
        acc = slots[0]
        for p in range(1, N_DEV):
            acc = acc + slots[p]
        out_ref[...] = acc

    return pl.pallas_call(
        body, name="all_reduce_small",
        in_specs=[pl.BlockSpec(memory_space=pltpu.VMEM)], out_specs=pl.BlockSpec(memory_space=pltpu.VMEM),
        out_shape=jax.ShapeDtypeStruct((R, C), F32),
        scratch_shapes=[pltpu.VMEM((N_DEV, R, C), F32), pltpu.SemaphoreType.DMA((7,)), pltpu.SemaphoreType.DMA((7,))],
    )(v)


def _adamw_math(w, g, m, v):
    m = ADAM_B1 * m + (1.0 - ADAM_B1) * g
    v = ADAM_B2 * v + (1.0 - ADAM_B2) * (g * g)
    m_hat = m / (1.0 - ADAM_B1 ** ADAM_STEP)
    v_hat = v / (1.0 - ADAM_B2 ** ADAM_STEP)
    delta = -ADAM_LR * (m_hat / (jnp.sqrt(v_hat) + ADAM_EPS) + ADAM_WD * w)
    return delta, m, v


def _adamw_reduce(recvs, dests, w, m, v, name):
    R, C = w.shape
    tr = R if R <= 256 else 256
    while R % tr:
        tr //= 2
    n = len(recvs)

    def body(*refs):
        w_ref, m_ref, v_ref, g_ref, d_ref, nm_ref, nv_ref = refs[n:]

        def update(r_ref):
            g = r_ref[0].astype(F32)
            for p in range(1, N_DEV):
                g = g + r_ref[p].astype(F32)
            d, nm, nv = _adamw_math(w_ref[...], g, m_ref[...], v_ref[...])
            g_ref[...] = g
            d_ref[...] = d
            nm_ref[...] = nm
            nv_ref[...] = nv

        if n == 1:
            update(refs[0])
        else:
            x, y, c = _mesh_pos()
            me = 4 * x + 2 * y + c
            for r_ref, (lo, hi) in zip(refs[:n], dests):
                pl.when((me >= lo) & (me < hi))(functools.partial(update, r_ref))

    tile = pl.BlockSpec((tr, C), lambda i: (i, 0))
    shp = jax.ShapeDtypeStruct((R, C), F32)
    return pl.pallas_call(
        body, name=name, grid=(R // tr,),
        in_specs=[pl.BlockSpec((N_DEV, tr, C), lambda i: (0, i, 0))] * n + [tile, tile, tile],
        out_specs=[tile] * 4, out_shape=[shp] * 4,
        compiler_params=_params(("parallel",)),
    )(*recvs, w, m, v)


def _adamw_small(g, w, m, v):
    def body(g_ref, w_ref, m_ref, v_ref, go_ref, d_ref, nm_ref, nv_ref):
        go_ref[...] = g_ref[...]
        for d in range(2):
            p0 = _sig(w_ref[8 + 2 * d:9 + 2 * d, :] - w_ref[9 + 2 * d:10 + 2 * d, :])
            dl0 = g_ref[12 + d:13 + d, :] * p0 * (1.0 - p0)
            go_ref[8 + 2 * d:9 + 2 * d, :] = dl0
            go_ref[9 + 2 * d:10 + 2 * d, :] = -dl0
            go_ref[12 + d:13 + d, :] = jnp.zeros((1, D_MODEL), F32)
        d, nm, nv = _adamw_math(w_ref[...], go_ref[...], m_ref[...], v_ref[...])
        d_ref[...] = d
        nm_ref[...] = nm
        nv_ref[...] = nv

    shp = jax.ShapeDtypeStruct(g.shape, F32)
    vm = pl.BlockSpec(memory_space=pltpu.VMEM)
    return pl.pallas_call(body, name="adamw_small", in_specs=[vm] * 4, out_specs=[vm] * 4, out_shape=[shp] * 4)(g, w, m, v)


def _local_step(x, tgt, g_mix, lb_shard, norm_g, pool_scale, g_ffn, g_final, w_in_shard, late_shards):
    B, L, _ = x.shape
    T = B * L
    rows = D_MODEL // N_DEV
    x2, tgt2 = x.reshape(T, D_MODEL), tgt.reshape(T, D_MODEL)

    proj, u, w_in_g, lb_g = _rms_inproj(x2, g_mix, w_in_shard, lb_shard)
    lb = lb_g[:, :4].transpose(1, 0, 2).reshape(4, D_MODEL)
    proj4 = proj.reshape(N_DEV, B, L, D_MODEL)
    o, ya, (wa, wb, wo, wfi_g, wfo_g, pw_g) = _hgrn_fwd(proj4, lb, norm_g, B, L, late_shards)
    wa, wb, wo = (w_.reshape(D_MODEL, D_MODEL) for w_ in (wa, wb, wo))
    wfo_g = wfo_g.reshape(4, FF_BLOCK, D_MODEL)
    pool_w_full = pw_g.reshape(N_DEV, 4, 32, POOL_DIM).transpose(1, 0, 2, 3).reshape(4, POOL_DIM, POOL_DIM)
    yb = _pool_fwd(proj4, pool_w_full, pool_scale, B, L)
    ya2, yb2 = ya.reshape(T, D_MODEL), yb.reshape(T, D_MODEL)
    za, zb, mg, h = _merge_out(x2, proj, ya2, yb2, wa, wb, wo)
    gate, up, hid, u2, dh2, dh2b, loss_p, dg_final = _ffn_fwd_loss(h, tgt2, g_ffn, g_final, wfi_g, wfo_g)
    loss = jnp.sum(loss_p[:, 0, 0])

    dgu, dh, dhb, dg_ffn = _ffn_bwd(h, dh2, dh2b, gate, up, g_ffn, wfi_g, wfo_g)
    d_wfo = _wgrad(hid, dh2b[None], "wgrad_ffn_out")
    d_wfi = _wgrad(u2[None], dgu.reshape(N_DEV, T, FF_BLOCK), "wgrad_ffn_in")
    dza, dzb, dgab, dya, dyb = _merge_bwd(dhb, proj, za, zb, wa, wb, wo)
    d_wo = _wgrad(mg[None], dhb[None], "wgrad_out")
    d_wa = _wgrad(ya2[None], dza[None], "wgrad_branch_a")
    d_wb = _wgrad(yb2[None], dzb[None], "wgrad_branch_b")
    dp, dpw_p, dps_p = _pool_bwd(proj4, dyb.reshape(B, L, D_MODEL), pool_w_full, pool_scale, B, L)
    d_pw = dpw_p.sum(0).reshape(4, N_DEV, 32, POOL_DIM).transpose(1, 0, 2, 3).reshape(N_DEV, 128, POOL_DIM)
    dp2 = dp.reshape(T, D_MODEL)
    w_in_dests = [(5, 6), (6, 8), (0, 5)]
    d_win_pool = _wgrad(u[None], dp2[None], "wgrad_in_pool")
    d_win_gates = _wgrad(u[None], dgab, "wgrad_in_gates")
    slices = [d_wa.reshape(N_DEV, rows, D_MODEL), d_wb.reshape(N_DEV, rows, D_MODEL),
              d_wo.reshape(N_DEV, rows, D_MODEL), d_wfi, d_wfo.reshape(N_DEV, FF_BLOCK // 2, D_MODEL),
              d_pw.astype(MX), d_win_pool, d_win_gates]
    dests = [(0, N_DEV)] * 6 + w_in_dests[:2]
    dproj5, dlb_p, dng_p, recv = _hgrn_bwd(proj4, o, dya.reshape(B, L, D_MODEL), lb, norm_g, B, L, slices, dests)
    dproj5 = dproj5.reshape(5, T, D_MODEL)
    d_win_rec = _wgrad(u[None], dproj5, "wgrad_in_recurrence")
    grad_x, dg_mix, recv_win_rec = _inproj_bwd(x2, dh, dproj5, dp2, dgab, g_mix, w_in_g, d_win_rec, w_in_dests[2])

    small = dict(g_mix=dg_mix.sum(0), hgrn_norm_g=dng_p.sum((0, 1)), pool_scale=dps_p.sum((0, 1)),
                 g_ffn=dg_ffn.sum(0), g_final=dg_final.sum(0), lb=dlb_p.sum(0))
    recv_w_in = [recv[6], recv[7], recv_win_rec]
    return loss, grad_x.reshape(B, L, D_MODEL), (recv_w_in, w_in_dests), list(recv[:6]), small


def kernel(x, g_mix, w_in, lb_logits, hgrn_norm_g, pool_w, pool_scale, w_branch_a, w_branch_b, w_out, g_ffn, w_ffn_in, w_ffn_out, g_final, loss_target, m_g_mix, m_w_in, m_lb_logits, m_hgrn_norm_g, m_pool_w, m_pool_scale, m_w_branch_a, m_w_branch_b, m_w_out, m_g_ffn, m_w_ffn_in, m_w_ffn_out, m_g_final, v_g_mix, v_w_in, v_lb_logits, v_hgrn_norm_g, v_pool_w, v_pool_scale, v_w_branch_a, v_w_branch_b, v_w_out, v_g_ffn, v_w_ffn_in, v_w_ffn_out, v_g_final):
    me = 4 * lax.axis_index("x") + 2 * lax.axis_index("y") + lax.axis_index("c")

    late_shards = [w_branch_a[0].astype(MX), w_branch_b[0].astype(MX), w_out[0].astype(MX),
                   w_ffn_in[0].astype(MX), w_ffn_out[0].astype(MX), pool_w[0].reshape(4 * 32, POOL_DIM).astype(MX)]
    lb_shard = jnp.pad(lb_logits.reshape(4, HEAD_DIM), ((0, 4), (0, 0)))

    loss, grad_x, (recv_w_in, w_in_dests), recv_late, small = _local_step(
        x, loss_target, g_mix, lb_shard, hgrn_norm_g, pool_scale, g_ffn, g_final[None], w_in[0].astype(MX),
        late_shards)
    loss = lax.psum(loss, ("x", "y", "c"))

    packed = jnp.zeros((16, D_MODEL), F32)
    names = ["g_mix", "hgrn_norm_g", "pool_scale", "g_ffn", "g_final"]
    for i, nme in enumerate(names):
        packed = packed.at[i].set(small[nme])
    packed = packed.at[5:7].set(small["lb"])
    red = _all_reduce_small(packed)
    dlb_mine = lax.dynamic_slice_in_dim(red[5:7], me * HEAD_DIM, HEAD_DIM, axis=1)

    sw = jnp.zeros((16, D_MODEL), F32)
    sm = jnp.zeros((16, D_MODEL), F32)
    sv = jnp.ones((16, D_MODEL), F32)
    smalls = [(g_mix, m_g_mix, v_g_mix), (hgrn_norm_g, m_hgrn_norm_g, v_hgrn_norm_g),
              (pool_scale, m_pool_scale, v_pool_scale), (g_ffn, m_g_ffn, v_g_ffn),
              (g_final[None], m_g_final[None], v_g_final[None])]
    for i, (w_, m_, v_) in enumerate(smalls):
        sw, sm, sv = sw.at[i].set(w_[0]), sm.at[i].set(m_[0]), sv.at[i].set(v_[0])
    sg = red.at[5:].set(0.0)
    sg = sg.at[12:14, :HEAD_DIM].set(dlb_mine)
    sw = sw.at[8:12, :HEAD_DIM].set(lb_logits.reshape(4, HEAD_DIM))
    sm = sm.at[8:12, :HEAD_DIM].set(m_lb_logits.reshape(4, HEAD_DIM))
    sv = sv.at[8:12, :HEAD_DIM].set(v_lb_logits.reshape(4, HEAD_DIM))
    sg, sd, snm, snv = _adamw_small(sg, sw, sm, sv)

    def small_out(arr, i, like):
        return arr[i].reshape(like.shape)

    def lb_out(arr):
        return arr[8:12, :HEAD_DIM].reshape(2, 2, HEAD_DIM)

    order = ["w_in", "w_branch_a", "w_branch_b", "w_out", "w_ffn_in", "w_ffn_out", "pool_w"]
    params = dict(w_in=(w_in, m_w_in, v_w_in), w_branch_a=(w_branch_a, m_w_branch_a, v_w_branch_a),
                  w_branch_b=(w_branch_b, m_w_branch_b, v_w_branch_b), w_out=(w_out, m_w_out, v_w_out),
                  w_ffn_in=(w_ffn_in, m_w_ffn_in, v_w_ffn_in), w_ffn_out=(w_ffn_out, m_w_ffn_out, v_w_ffn_out),
                  pool_w=(pool_w, m_pool_w, v_pool_w))
    res = {}
    for nme, r, dests in zip(order, [recv_w_in] + [[r] for r in recv_late], [w_in_dests] + [None] * 6):
        w_, m_, v_ = params[nme]
        shape2 = r[0].shape[1:]
        outs = _adamw_reduce(r, dests, w_.reshape(shape2), m_.reshape(shape2), v_.reshape(shape2), "adamw_" + nme)
        res[nme] = [o_.reshape(w_.shape) for o_ in outs]

    def pick(k):
        small_src = [sg, sd, snm, snv][k]
        return [small_out(small_src, 0, g_mix), res["w_in"][k], lb_out(small_src), small_out(small_src, 1, hgrn_norm_g),
                res["pool_w"][k], small_out(small_src, 2, pool_scale), res["w_branch_a"][k], res["w_branch_b"][k],
                res["w_out"][k], small_out(small_src, 3, g_ffn), res["w_ffn_in"][k], res["w_ffn_out"][k],
                small_out(small_src, 4, g_final)]

    return (loss, grad_x, *pick(0), *pick(1), *pick(2), *pick(3))
```

```python
import functools

import jax
import jax.numpy as jnp
from jax import lax
from jax.experimental import pallas as pl
from jax.experimental.pallas import tpu as pltpu

F32 = jnp.float32
MX = jnp.bfloat16

D_MODEL = 1024
N_HEADS = 8
HEAD_DIM = 128
CHUNK = 16
POOL_WINDOWS = (2, 4, 8, 16)
POOL_DIM = 256
FF_BLOCK = 704
N_DEV = 8
RMS_EPS = 1e-6
ADAM_LR, ADAM_B1, ADAM_B2, ADAM_EPS, ADAM_WD, ADAM_STEP = 0.001, 0.9, 0.999, 1e-08, 0.01, 10
VMEM_LIMIT = 56 * 1024 * 1024
BIG_TOKEN_TILE = 1024
ANY = pl.BlockSpec(memory_space=pl.ANY)


def _params(sem=None):
    return pltpu.CompilerParams(dimension_semantics=sem, vmem_limit_bytes=VMEM_LIMIT)


def _dot(a, b):
    return lax.dot_general(a.astype(MX), b.astype(MX), (((1,), (0,)), ((), ())), preferred_element_type=F32)


def _dot_nt(a, b):
    return lax.dot_general(a.astype(MX), b.astype(MX), (((1,), (1,)), ((), ())), preferred_element_type=F32)


def _dot_tn(a, b):
    return lax.dot_general(a.astype(MX), b.astype(MX), (((0,), (0,)), ((), ())), preferred_element_type=F32)


def _sig(x):
    return 1.0 / (1.0 + jnp.exp(-x))


def _fold8(v):
    return v.reshape(v.shape[0] // 8, 8, v.shape[1]).sum(axis=0)


def _gather_order():
    x, y, c = _mesh_pos()
    near, far = [(1 - x, y), (x, 1 - y)], (1 - x, 1 - y)
    return ([(x, y, c), (x, y, 1 - c)] + [(*chip, c) for chip in near] + [(*chip, 1 - c) for chip in near]
            + [(*far, c), (*far, 1 - c)])


def _rms_inproj(x2, g_mix, w_shard, lb_shard):
    T = x2.shape[0]
    tm = min(BIG_TOKEN_TILE, T)
    nT = T // tm
    block_of_step = jnp.stack([4 * px + 2 * py + pc for px, py, pc in _gather_order()]).astype(jnp.int32)

    def body(order_ref, x_ref, g_ref, w_ref, lb_ref, proj_ref, u_ref, wg_ref, lbg_ref,
             u_sc, w_sc, load_sem, send_sems, recv_sems, local_sem, *lb_sems):
        k, i = pl.program_id(0), pl.program_id(1)
        order = _gather_order()
        me, sibling = order[0], order[1]

        def slot(dev):
            return wg_ref.at[4 * dev[0] + 2 * dev[1] + dev[2]]

        def copy(n, block, to, src=None):
            return pltpu.make_async_remote_copy(
                src_ref=slot(block) if src is None else src, dst_ref=slot(block),
                send_sem=send_sems.at[n], recv_sem=recv_sems.at[n], device_id=to, device_id_type=pl.DeviceIdType.MESH)

        mine = pltpu.make_async_copy(w_ref, slot(me), local_sem)
        to_sibling = copy(0, me, sibling, src=w_ref)
        to_near = [copy(1, me, order[2], src=w_ref), copy(2, me, order[3], src=w_ref)]
        to_far = copy(5, me, order[6], src=w_ref)
        passed = {2: copy(3, order[2], sibling), 3: copy(4, order[3], sibling), 6: copy(6, order[6], sibling)}
        lb_gather = _Exchange([lb_ref], [lbg_ref], lb_sems, gather=True)

        @pl.when((k == 0) & (i == 0))
        def _():
            for cp in [mine, to_sibling] + to_near:
                cp.start()
            lb_gather.start()

        for s, block in enumerate(order):
            @pl.when((k == s) & (i == 0))
            def _(s=s, block=block):
                if s == 0:
                    mine.wait()
                else:
                    copy(s - 1, block, me).wait_recv()
                if s in passed:
                    passed[s].start()
                if s == 3:
                    for cp in to_near:
                        cp.wait_send()
                    to_far.start()
                load = pltpu.make_async_copy(slot(block), w_sc, load_sem)
                load.start()
                load.wait()

        rows = pl.ds(pl.multiple_of(i * tm, tm), tm)

        @pl.when(k == 0)
        def _():
            x = x_ref[...]
            r = lax.rsqrt(jnp.mean(x * x, axis=-1, keepdims=True) + RMS_EPS)
            u = (x * r * g_ref[...]).astype(MX)
            u_sc[rows, :] = u
            u_ref[...] = u

        proj_ref[...] = jnp.dot(u_sc[rows, :], w_sc[...], preferred_element_type=F32)

        @pl.when((k == N_DEV - 1) & (i == nT - 1))
        def _():
            for cp in [to_sibling, to_far] + list(passed.values()):
                cp.wait_send()
            lb_gather.wait()

    def tile_once(k, i, order_ref):
        return (jnp.where(k == 0, i, nT - 1), 0)

    grid_spec = pltpu.PrefetchScalarGridSpec(
        num_scalar_prefetch=1, grid=(N_DEV, nT),
        in_specs=[pl.BlockSpec((tm, D_MODEL), tile_once),
                  pl.BlockSpec((1, D_MODEL), lambda k, i, order_ref: (0, 0)), ANY, ANY],
        out_specs=[pl.BlockSpec((None, tm, D_MODEL), lambda k, i, order_ref: (order_ref[k], i, 0)),
                   pl.BlockSpec((tm, D_MODEL), tile_once), ANY, ANY],
        scratch_shapes=[pltpu.VMEM((T, D_MODEL), MX), pltpu.VMEM((D_MODEL, D_MODEL), MX),
                        pltpu.SemaphoreType.DMA(()), pltpu.SemaphoreType.DMA((7,)), pltpu.SemaphoreType.DMA((7,)),
                        pltpu.SemaphoreType.DMA(())] + _Exchange.scratch(1))
    return pl.pallas_call(
        body, name="rms_inproj", grid_spec=grid_spec,
        out_shape=[jax.ShapeDtypeStruct((N_DEV, T, D_MODEL), F32), jax.ShapeDtypeStruct((T, D_MODEL), MX),
                   jax.ShapeDtypeStruct((N_DEV, D_MODEL, D_MODEL), MX),
                   jax.ShapeDtypeStruct((N_DEV,) + lb_shard.shape, lb_shard.dtype)],
        compiler_params=_params(("arbitrary", "arbitrary")),
    )(block_of_step, x2, g_mix, w_shard, lb_shard)


HBLK = 128


def _seg_cumsum(x, rev):
    n = x.shape[0]
    pos = lax.broadcasted_iota(jnp.int32, x.shape, 0) & (CHUNK - 1)
    s = 1
    while s < CHUNK:
        if rev:
            x = x + jnp.where(pos < CHUNK - s, pltpu.roll(x, n - s, 0), 0.0)
        else:
            x = x + jnp.where(pos >= s, pltpu.roll(x, s, 0), 0.0)
        s *= 2
    return x


def _block_gates(q_r, f_r, lb_row, rev):
    sq = _sig(q_r)
    q = q_r * sq
    sg = _sig(f_r)
    f = lb_row + (1.0 - lb_row) * sg
    k = 1.0 - f
    lf = jnp.log(f)
    pre = _seg_cumsum(lf, False)
    suf = _seg_cumsum(lf, True)
    tot = pre + suf - lf
    b = suf if rev else pre
    eb = jnp.exp(b)
    enb = jnp.exp(-b)
    eend = jnp.exp(tot - b)
    return dict(sq=sq, sg=sg, f=f, eb=eb, enb=enb, eend=eend, dec=jnp.exp(tot),
                P=q * eb, Kt=k * enb, Ke=k * eend)


def _block_mask(rev, transposed=False):
    ri = lax.broadcasted_iota(jnp.int32, (HBLK, HBLK), 0)
    ci = lax.broadcasted_iota(jnp.int32, (HBLK, HBLK), 1)
    same = (ri // CHUNK) == (ci // CHUNK)
    return same & ((ci >= ri) if rev != transposed else (ci <= ri))


def _chunk_rows(c, rows=CHUNK):
    return pl.ds(c * CHUNK, rows)


def _block_rows(kb):
    return pl.ds(kb * HBLK, HBLK)


def _chunk_outer_products(a, b, out_sc, kb):
    a, b = a.astype(MX), b.astype(MX)
    for u in range(HBLK // CHUNK):
        r = slice(u * CHUNK, (u + 1) * CHUNK)
        out_sc[kb * (HBLK // CHUNK) + u] = _dot_tn(a[r, :], b[r, :])


UNIT_BLOCKS = 2
UNIT_CHUNKS = UNIT_BLOCKS * (HBLK // CHUNK)


def _unit_blocks(u, fn, carry):
    for b in range(UNIT_BLOCKS):
        carry = fn(u * UNIT_BLOCKS + b, carry)
    return carry


def _unit_chunks(u, descending, fn, carry):
    for j in range(UNIT_CHUNKS):
        carry = fn(u * UNIT_CHUNKS + (UNIT_CHUNKS - 1 - j if descending else j), carry)
    return carry


def _pipelined(n_units, descending, first, c1, second, c2):
    units = list(range(n_units))[::-1] if descending else list(range(n_units))
    c1 = first(units[0], c1)
    for t, u in enumerate(units):
        if t + 1 < n_units:
            c1 = first(units[t + 1], c1)
        c2 = second(u, c2)
    return c1, c2


def _lower_bounds(lbl_ref):
    return _sig(lbl_ref[0:1, :] - lbl_ref[1:2, :]), _sig(lbl_ref[2:3, :] - lbl_ref[3:4, :])


def _hgrn_fwd(proj, lbl, norm_g, B, L, shards):
    nC = L // CHUNK
    n = len(shards)
    relay_step = (11 * B * N_HEADS) // 16

    def body(*refs):
        q_ref, ff_ref, fb_ref, i_ref, og_ref, lbl_ref, ng_ref = refs[:7]
        o_ref, ya_ref = refs[7 + n:9 + n]
        o_sc, p_sc, dec_sc, upd_sc = refs[9 + 2 * n:13 + 2 * n]
        step_id = pl.program_id(0) * N_HEADS + pl.program_id(1)
        gather = _TwoLevelGather(refs[7:7 + n], refs[9 + n:9 + 2 * n], refs[13 + 2 * n:])

        @pl.when(step_id == 0)
        def _():
            gather.start()

        @pl.when(step_id == relay_step)
        def _():
            gather.relay()

        lb_f, lb_b = _lower_bounds(lbl_ref)
        o_sc[...] = jnp.zeros_like(o_sc)

        def run_dir(f_ref, lb_row, rev):
            def block(kb, carry):
                rows = _block_rows(kb)
                g = _block_gates(q_ref[rows, :], f_ref[rows, :], lb_row, rev)
                v = i_ref[rows, :]
                a = jnp.where(_block_mask(rev), _dot_nt(g["P"], g["Kt"]), 0.0)
                o_sc[rows, :] += _dot(a, v)
                p_sc[rows, :] = g["P"].astype(MX)
                dec_sc[rows, :] = g["dec"]
                _chunk_outer_products(v, g["Ke"], upd_sc, kb)
                return carry

            def step(c, st):
                rows = _chunk_rows(c)
                o_sc[rows, :] += _dot_nt(p_sc[rows, :], st)
                dec = dec_sc[_chunk_rows(c, 1), :]
                return st * dec + upd_sc[c]

            _pipelined(nC // UNIT_CHUNKS, rev,
                       lambda u, c: _unit_blocks(u, block, c), 0,
                       lambda u, st: _unit_chunks(u, rev, step, st), jnp.zeros((HEAD_DIM, HEAD_DIM), F32))

        run_dir(ff_ref, lb_f, False)
        run_dir(fb_ref, lb_b, True)
        o = o_sc[...]
        o_ref[...] = o
        on = o * lax.rsqrt(jnp.mean(o * o, axis=-1, keepdims=True) + RMS_EPS)
        og = og_ref[...]
        ya_ref[...] = ((on * ng_ref[...]) * (og * _sig(og))).astype(MX)

        @pl.when(step_id == B * N_HEADS - 1)
        def _():
            gather.finish()

    def blk(s):
        return pl.BlockSpec((None, None, L, HEAD_DIM), lambda b, h, s=s: (s, b, 0, h))

    out_blk = pl.BlockSpec((None, L, HEAD_DIM), lambda b, h: (b, 0, h))
    outs = pl.pallas_call(
        body, name="hgrn_fwd", grid=(B, N_HEADS),
        in_specs=[blk(0), blk(1), blk(2), blk(3), blk(4),
                  pl.BlockSpec((4, HEAD_DIM), lambda b, h: (0, h)),
                  pl.BlockSpec((1, HEAD_DIM), lambda b, h: (0, h))] + [ANY] * n,
        out_specs=[out_blk, out_blk] + [ANY] * n,
        out_shape=[jax.ShapeDtypeStruct((B, L, D_MODEL), F32), jax.ShapeDtypeStruct((B, L, D_MODEL), MX)]
                  + _Exchange.out_shapes(shards, True),
        scratch_shapes=[pltpu.VMEM((L, HEAD_DIM), F32), pltpu.VMEM((L, HEAD_DIM), MX),
                        pltpu.VMEM((L, HEAD_DIM), F32), pltpu.VMEM((nC, HEAD_DIM, HEAD_DIM), F32)]
                       + _Exchange.scratch(n),
        compiler_params=_params(("arbitrary", "arbitrary")),
    )(proj, proj, proj, proj, proj, lbl, norm_g, *shards)
    return outs[0], outs[1], outs[2:]


POOL_PAD = 8


def _pool_window(p, ext_sc, half, adjoint):
    L = p.shape[0]
    n = L + 2 * POOL_PAD
    ext_sc[0:POOL_PAD, :] = jnp.zeros((POOL_PAD, p.shape[1]), F32)
    ext_sc[POOL_PAD + L:n, :] = jnp.zeros((POOL_PAD, p.shape[1]), F32)
    ext_sc[POOL_PAD:POOL_PAD + L, :] = p
    x = ext_sc[...]
    s = x + pltpu.roll(x, 1 if adjoint else n - 1, 0)
    w = 1
    while w < half:
        s = pltpu.roll(s, w, 0) + pltpu.roll(s, n - w, 0)
        w *= 2
    ext_sc[...] = s
    return ext_sc[POOL_PAD:POOL_PAD + L, :]


def _pool_count(L, half):
    t = lax.broadcasted_iota(jnp.int32, (L, 1), 0)
    lo = jnp.clip(t - half + 1, 0, L)
    hi = jnp.clip(t + half + 1, 0, L)
    return (hi - lo).astype(F32)


def _pool_fwd(proj, pool_w_full, pool_scale, B, L):
    def body(p_ref, w_ref, s_ref, yb_ref, ext_sc):
        for g, win in enumerate(POOL_WINDOWS):
            cols = slice(g * POOL_DIM, (g + 1) * POOL_DIM)
            p = p_ref[:, cols]
            y = _pool_window(p, ext_sc, win // 2, False) / _pool_count(L, win // 2) - p
            yb_ref[:, cols] = (_dot(y, w_ref[g]) * s_ref[:, cols]).astype(MX)

    return pl.pallas_call(
        body, name="pool_fwd", grid=(B,),
        in_specs=[pl.BlockSpec((None, None, L, D_MODEL), lambda b: (5, b, 0, 0)),
                  pl.BlockSpec((4, POOL_DIM, POOL_DIM), lambda b: (0, 0, 0)),
                  pl.BlockSpec((1, D_MODEL), lambda b: (0, 0))],
        out_specs=pl.BlockSpec((None, L, D_MODEL), lambda b: (b, 0, 0)),
        out_shape=jax.ShapeDtypeStruct((B, L, D_MODEL), MX),
        scratch_shapes=[pltpu.VMEM((L + 2 * POOL_PAD, POOL_DIM), F32)],
        compiler_params=_params(("parallel",)),
    )(proj, pool_w_full, pool_scale)


def _merge_out(x2, proj, ya, yb, wa, wb, wo):
    T = x2.shape[0]
    tm = min(512, T)

    def body(x_ref, ga_ref, gb_ref, ya_ref, yb_ref, wa_ref, wb_ref, wo_ref, za_ref, zb_ref, mg_ref, h_ref):
        za = jnp.dot(ya_ref[...], wa_ref[...], preferred_element_type=F32)
        zb = jnp.dot(yb_ref[...], wb_ref[...], preferred_element_type=F32)
        mg = (_sig(ga_ref[...]) * za + _sig(gb_ref[...]) * zb).astype(MX)
        za_ref[...] = za
        zb_ref[...] = zb
        mg_ref[...] = mg
        h_ref[...] = x_ref[...] + jnp.dot(mg, wo_ref[...], preferred_element_type=F32)

    tile = pl.BlockSpec((tm, D_MODEL), lambda i: (i, 0))
    wspec = pl.BlockSpec((D_MODEL, D_MODEL), lambda i: (0, 0))
    return pl.pallas_call(
        body, name="merge_out", grid=(T // tm,),
        in_specs=[tile,
                  pl.BlockSpec((None, tm, D_MODEL), lambda i: (6, i, 0)),
                  pl.BlockSpec((None, tm, D_MODEL), lambda i: (7, i, 0)),
                  tile, tile, wspec, wspec, wspec],
        out_specs=[tile, tile, tile, tile],
        out_shape=[jax.ShapeDtypeStruct((T, D_MODEL), F32), jax.ShapeDtypeStruct((T, D_MODEL), F32),
                   jax.ShapeDtypeStruct((T, D_MODEL), MX), jax.ShapeDtypeStruct((T, D_MODEL), F32)],
        compiler_params=_params(("parallel",)),
    )(x2, proj, proj, ya, yb, wa, wb, wo)


def _ffn_fwd_loss(h, tgt, g_ffn, g_final, wfi_g, wfo_g):
    T = h.shape[0]
    tm = min(512, T)
    nT = T // tm

    def body(h_ref, t_ref, gf_ref, gl_ref, wg_ref, wu_ref, wo_ref,
             gate_ref, up_ref, hid_ref, u2_ref, dh2_ref, dh2b_ref, loss_ref, dgl_ref, u2_sc, acc_sc):
        i, j = pl.program_id(0), pl.program_id(1)

        @pl.when(j == 0)
        def _():
            hh = h_ref[...]
            r = lax.rsqrt(jnp.mean(hh * hh, axis=-1, keepdims=True) + RMS_EPS)
            u2 = (hh * r * gf_ref[...]).astype(MX)
            u2_sc[...] = u2
            u2_ref[...] = u2
            acc_sc[...] = jnp.zeros_like(acc_sc)

        @pl.when((i == 0) & (j == 0))
        def _():
            dgl_ref[...] = jnp.zeros_like(dgl_ref)

        gate = jnp.dot(u2_sc[...], wg_ref[...], preferred_element_type=F32)
        up = jnp.dot(u2_sc[...], wu_ref[...], preferred_element_type=F32)
        hid = ((gate * _sig(gate)) * up).astype(MX)
        gate_ref[...] = gate
        up_ref[...] = up
        hid_ref[...] = hid
        acc_sc[...] += jnp.dot(hid, wo_ref[...], preferred_element_type=F32)

        @pl.when(j == 3)
        def _():
            h2 = h_ref[...] + acc_sc[...]
            r = lax.rsqrt(jnp.mean(h2 * h2, axis=-1, keepdims=True) + RMS_EPS)
            hn = h2 * r
            gl = gl_ref[...]
            err = hn * gl - t_ref[...]
            tok = jnp.mean(err * err, axis=-1, keepdims=True)
            loss_ref[...] = jnp.full(loss_ref.shape, 0.5 * jnp.sum(tok), F32)
            dy = err * (1.0 / D_MODEL)
            dgl_ref[...] += _fold8(dy * hn)
            a = dy * gl
            dh2 = r * a - hn * (r * jnp.mean(a * hn, axis=-1, keepdims=True))
            dh2_ref[...] = dh2
            dh2b_ref[...] = dh2.astype(MX)

    tile = pl.BlockSpec((tm, D_MODEL), lambda i, j: (i, 0))
    vec = pl.BlockSpec((1, D_MODEL), lambda i, j: (0, 0))
    ftile = pl.BlockSpec((None, tm, FF_BLOCK), lambda i, j: (j, i, 0))
    return pl.pallas_call(
        body, name="ffn_fwd_loss", grid=(nT, 4),
        in_specs=[tile, tile, vec, vec,
                  pl.BlockSpec((None, D_MODEL, FF_BLOCK), lambda i, j: (j, 0, 0)),
                  pl.BlockSpec((None, D_MODEL, FF_BLOCK), lambda i, j: (j + 4, 0, 0)),
                  pl.BlockSpec((None, FF_BLOCK, D_MODEL), lambda i, j: (j, 0, 0))],
        out_specs=[ftile, ftile, ftile, tile, tile, tile,
                   pl.BlockSpec((None, 8, 128), lambda i, j: (i, 0, 0)),
                   pl.BlockSpec((8, D_MODEL), lambda i, j: (0, 0))],
        out_shape=[jax.ShapeDtypeStruct((4, T, FF_BLOCK), F32), jax.ShapeDtypeStruct((4, T, FF_BLOCK), F32),
                   jax.ShapeDtypeStruct((4, T, FF_BLOCK), MX), jax.ShapeDtypeStruct((T, D_MODEL), MX),
                   jax.ShapeDtypeStruct((T, D_MODEL), F32), jax.ShapeDtypeStruct((T, D_MODEL), MX),
                   jax.ShapeDtypeStruct((nT, 8, 128), F32), jax.ShapeDtypeStruct((8, D_MODEL), F32)],
        scratch_shapes=[pltpu.VMEM((tm, D_MODEL), MX), pltpu.VMEM((tm, D_MODEL), F32)],
        compiler_params=_params(("arbitrary", "arbitrary")),
    )(h, tgt, g_ffn, g_final, wfi_g, wfi_g, wfo_g)


def _ffn_bwd(h, dh2, dh2b, gate, up, g_ffn, wfi_g, wfo_g):
    T = h.shape[0]
    tm = min(512, T)

    def body(h_ref, dh2_ref, dh2b_ref, gate_ref, up_ref, gf_ref, wg_ref, wu_ref, wo_ref,
             dgu_ref, dh_ref, dhb_ref, dgf_ref, acc_sc):
        i, j = pl.program_id(0), pl.program_id(1)

        @pl.when(j == 0)
        def _():
            acc_sc[...] = jnp.zeros_like(acc_sc)

        @pl.when((i == 0) & (j == 0))
        def _():
            dgf_ref[...] = jnp.zeros_like(dgf_ref)

        dhid = _dot_nt(dh2b_ref[...], wo_ref[...])
        gate, up = gate_ref[...], up_ref[...]
        sg = _sig(gate)
        dgate = (dhid * up * (sg * (1.0 + gate * (1.0 - sg)))).astype(MX)
        dup = (dhid * (gate * sg)).astype(MX)
        dgu_ref[0] = dgate
        dgu_ref[1] = dup
        acc_sc[...] += _dot_nt(dgate, wg_ref[...]) + _dot_nt(dup, wu_ref[...])

        @pl.when(j == 3)
        def _():
            hh = h_ref[...]
            r = lax.rsqrt(jnp.mean(hh * hh, axis=-1, keepdims=True) + RMS_EPS)
            hn = hh * r
            du2 = acc_sc[...]
            dgf_ref[...] += _fold8(du2 * hn)
            a = du2 * gf_ref[...]
            dh = dh2_ref[...] + r * a - hn * (r * jnp.mean(a * hn, axis=-1, keepdims=True))
            dh_ref[...] = dh
            dhb_ref[...] = dh.astype(MX)

    tile = pl.BlockSpec((tm, D_MODEL), lambda i, j: (i, 0))
    ftile = pl.BlockSpec((None, tm, FF_BLOCK), lambda i, j: (j, i, 0))
    return pl.pallas_call(
        body, name="ffn_bwd", grid=(T // tm, 4),
        in_specs=[tile, tile, tile, ftile, ftile,
                  pl.BlockSpec((1, D_MODEL), lambda i, j: (0, 0)),
                  pl.BlockSpec((None, D_MODEL, FF_BLOCK), lambda i, j: (j, 0, 0)),
                  pl.BlockSpec((None, D_MODEL, FF_BLOCK), lambda i, j: (j + 4, 0, 0)),
                  pl.BlockSpec((None, FF_BLOCK, D_MODEL), lambda i, j: (j, 0, 0))],
        out_specs=[pl.BlockSpec((2, None, tm, FF_BLOCK), lambda i, j: (0, j, i, 0)),
                   tile, tile, pl.BlockSpec((8, D_MODEL), lambda i, j: (0, 0))],
        out_shape=[jax.ShapeDtypeStruct((2, 4, T, FF_BLOCK), MX),
                   jax.ShapeDtypeStruct((T, D_MODEL), F32), jax.ShapeDtypeStruct((T, D_MODEL), MX),
                   jax.ShapeDtypeStruct((8, D_MODEL), F32)],
        scratch_shapes=[pltpu.VMEM((tm, D_MODEL), F32)],
        compiler_params=_params(("arbitrary", "arbitrary")),
    )(h, dh2, dh2b, gate, up, g_ffn, wfi_g, wfi_g, wfo_g)


def _merge_bwd(dhb, proj, za, zb, wa, wb, wo):
    T = dhb.shape[0]
    tm = min(512, T)

    def body(dh_ref, ga_ref, gb_ref, za_ref, zb_ref, wa_ref, wb_ref, wo_ref,
             dza_ref, dzb_ref, dgab_ref, dya_ref, dyb_ref):
        dm = _dot_nt(dh_ref[...], wo_ref[...])
        sa, sb = _sig(ga_ref[...]), _sig(gb_ref[...])
        dza = (dm * sa).astype(MX)
        dzb = (dm * sb).astype(MX)
        dza_ref[...] = dza
        dzb_ref[...] = dzb
        dgab_ref[0] = (dm * za_ref[...] * (sa * (1.0 - sa))).astype(MX)
        dgab_ref[1] = (dm * zb_ref[...] * (sb * (1.0 - sb))).astype(MX)
        dya_ref[...] = _dot_nt(dza, wa_ref[...])
        dyb_ref[...] = _dot_nt(dzb, wb_ref[...])

    tile = pl.BlockSpec((tm, D_MODEL), lambda i: (i, 0))
    wspec = pl.BlockSpec((D_MODEL, D_MODEL), lambda i: (0, 0))
    return pl.pallas_call(
        body, name="merge_bwd", grid=(T // tm,),
        in_specs=[tile,
                  pl.BlockSpec((None, tm, D_MODEL), lambda i: (6, i, 0)),
                  pl.BlockSpec((None, tm, D_MODEL), lambda i: (7, i, 0)),
                  tile, tile, wspec, wspec, wspec],
        out_specs=[tile, tile, pl.BlockSpec((2, tm, D_MODEL), lambda i: (0, i, 0)), tile, tile],
        out_shape=[jax.ShapeDtypeStruct((T, D_MODEL), MX), jax.ShapeDtypeStruct((T, D_MODEL), MX),
                   jax.ShapeDtypeStruct((2, T, D_MODEL), MX),
                   jax.ShapeDtypeStruct((T, D_MODEL), F32), jax.ShapeDtypeStruct((T, D_MODEL), F32)],
        compiler_params=_params(("parallel",)),
    )(dhb, proj, proj, za, zb, wa, wb, wo)


def _pool_bwd(proj, dyb, pool_w_full, pool_scale, B, L):
    def body(p_ref, dy_ref, w_ref, s_ref, dp_ref, dw_ref, ds_ref, ext_sc):
        for g, win in enumerate(POOL_WINDOWS):
            cols = slice(g * POOL_DIM, (g + 1) * POOL_DIM)
            p = p_ref[:, cols]
            cnt = _pool_count(L, win // 2)
            y = _pool_window(p, ext_sc, win // 2, False) / cnt - p
            z = _dot(y, w_ref[g])
            dyb_g = dy_ref[:, cols]
            ds_ref[:, cols] = jnp.sum(dyb_g * z, axis=0, keepdims=True)
            dz = dyb_g * s_ref[:, cols]
            dw_ref[g] = _dot_tn(y, dz)
            dy = _dot_nt(dz, w_ref[g])
            dp_ref[:, cols] = (_pool_window(dy / cnt, ext_sc, win // 2, True) - dy).astype(MX)

    seq = pl.BlockSpec((None, L, D_MODEL), lambda b: (b, 0, 0))
    return pl.pallas_call(
        body, name="pool_bwd", grid=(B,),
        in_specs=[pl.BlockSpec((None, None, L, D_MODEL), lambda b: (5, b, 0, 0)), seq,
                  pl.BlockSpec((4, POOL_DIM, POOL_DIM), lambda b: (0, 0, 0)),
                  pl.BlockSpec((1, D_MODEL), lambda b: (0, 0))],
        out_specs=[seq, pl.BlockSpec((None, 4, POOL_DIM, POOL_DIM), lambda b: (b, 0, 0, 0)),
                   pl.BlockSpec((None, 1, D_MODEL), lambda b: (b, 0, 0))],
        out_shape=[jax.ShapeDtypeStruct((B, L, D_MODEL), MX),
                   jax.ShapeDtypeStruct((B, 4, POOL_DIM, POOL_DIM), F32),
                   jax.ShapeDtypeStruct((B, 1, D_MODEL), F32)],
        scratch_shapes=[pltpu.VMEM((L + 2 * POOL_PAD, POOL_DIM), F32)],
        compiler_params=_params(("parallel",)),
    )(proj, dyb, pool_w_full, pool_scale)


def _hgrn_bwd(proj, o, dya, lbl, norm_g, B, L, grads, dests):
    nC = L // CHUNK
    n = len(grads)

    def body(*refs):
        q_ref, ff_ref, fb_ref, i_ref, og_ref, o_ref, dy_ref, lbl_ref, ng_ref = refs[:9]
        dp_ref, dlb_ref, dng_ref = refs[9 + n:12 + n]
        (do_sc, dq_sc, dv_sc, dP_sc, dKt_sc, dKe_sc, dbl_sc, dec_sc, ke_sc, ck_sc, upd_sc,
         dupd_sc) = refs[12 + 2 * n:24 + 2 * n]
        step_id = pl.program_id(0) * N_HEADS + pl.program_id(1)
        scatter = _Exchange(refs[9:9 + n], refs[12 + n:12 + 2 * n], refs[24 + 2 * n:], gather=False, dests=dests)

        @pl.when(step_id == 0)
        def _():
            scatter.start()

        lb_f, lb_b = _lower_bounds(lbl_ref)
        o_ = o_ref[...]
        r = lax.rsqrt(jnp.mean(o_ * o_, axis=-1, keepdims=True) + RMS_EPS)
        on = o_ * r
        og = og_ref[...]
        sog = _sig(og)
        dy = dy_ref[...]
        ng = ng_ref[...]
        dp_ref[4] = (dy * (on * ng) * (sog * (1.0 + og * (1.0 - sog)))).astype(MX)
        dn = dy * (og * sog)
        dng_ref[...] = jnp.sum(dn * on, axis=0, keepdims=True)
        don = dn * ng
        do_sc[...] = r * don - on * (r * jnp.mean(don * on, axis=-1, keepdims=True))

        dq_sc[...] = jnp.zeros_like(dq_sc)
        dv_sc[...] = jnp.zeros_like(dv_sc)

        def run_dir(f_ref, lb_row, rev, slot):
            def block(kb, carry):
                rows = _block_rows(kb)
                g = _block_gates(q_ref[rows, :], f_ref[rows, :], lb_row, rev)
                v = i_ref[rows, :]
                do = do_sc[rows, :]
                m, mt = _block_mask(rev), _block_mask(rev, True)
                at = jnp.where(mt, _dot_nt(g["Kt"], g["P"]), 0.0)
                da = jnp.where(m, _dot_nt(do, v), 0.0)
                dat = jnp.where(mt, _dot_nt(v, do), 0.0)
                dv_sc[rows, :] += _dot(at, do)
                dP_sc[rows, :] = _dot(da, g["Kt"])
                dKt_sc[rows, :] = _dot(dat, g["P"])
                ke_sc[rows, :] = g["Ke"].astype(MX)
                dec_sc[rows, :] = g["dec"]
                _chunk_outer_products(v, g["Ke"], upd_sc, kb)
                _chunk_outer_products(do, g["P"], dupd_sc, kb)
                return carry

            def fstep(c, st):
                rows = _chunk_rows(c)
                ck_sc[c] = st.astype(MX)
                dP_sc[rows, :] += _dot(do_sc[rows, :], st)
                dec = dec_sc[_chunk_rows(c, 1), :]
                return st * dec + upd_sc[c]

            n_units = nC // UNIT_CHUNKS
            zero_state = jnp.zeros((HEAD_DIM, HEAD_DIM), F32)
            _pipelined(n_units, rev,
                       lambda u, c: _unit_blocks(u, block, c), 0,
                       lambda u, st: _unit_chunks(u, rev, fstep, st), zero_state)

            def bstep(c, dst):
                rows = _chunk_rows(c)
                dec = dec_sc[_chunk_rows(c, 1), :]
                dKe_sc[rows, :] = _dot(i_ref[rows, :], dst)
                dv_sc[rows, :] += _dot_nt(ke_sc[rows, :], dst)
                dbl = dec * jnp.sum(dst * ck_sc[c].astype(F32), axis=0, keepdims=True)
                dbl_sc[rows, :] = jnp.broadcast_to(dbl, (CHUNK, HEAD_DIM))
                return dst * dec + dupd_sc[c]

            def finish(kb, acc):
                rows = _block_rows(kb)
                q_r = q_ref[rows, :]
                g = _block_gates(q_r, f_ref[rows, :], lb_row, rev)
                dP, dkt, dke = dP_sc[rows, :], dKt_sc[rows, :], dKe_sc[rows, :]
                e = dke * g["Ke"]
                dlf = (_seg_cumsum(dP * g["P"] - dkt * g["Kt"], not rev) + _seg_cumsum(e, rev) - e
                       + dbl_sc[rows, :])
                df = dlf / g["f"] - (dkt * g["enb"] + dke * g["eend"])
                dp_ref[slot, rows, :] = (df * (1.0 - lb_row) * (g["sg"] * (1.0 - g["sg"]))).astype(MX)
                dq_sc[rows, :] += (dP * g["eb"]) * (g["sq"] * (1.0 + q_r * (1.0 - g["sq"])))
                return acc + jnp.sum(df * (1.0 - g["sg"]), axis=0, keepdims=True)

            _, dlb = _pipelined(n_units, not rev,
                                lambda u, dst: _unit_chunks(u, not rev, bstep, dst), zero_state,
                                lambda u, acc: _unit_blocks(u, finish, acc), jnp.zeros((1, HEAD_DIM), F32))
            dlb_ref[slot - 1:slot, :] = dlb

        run_dir(ff_ref, lb_f, False, 1)
        run_dir(fb_ref, lb_b, True, 2)
        dp_ref[0] = dq_sc[...].astype(MX)
        dp_ref[3] = dv_sc[...].astype(MX)

        @pl.when(step_id == B * N_HEADS - 1)
        def _():
            scatter.wait()

    def blk(s):
        return pl.BlockSpec((None, None, L, HEAD_DIM), lambda b, h, s=s: (s, b, 0, h))

    seq = pl.BlockSpec((None, L, HEAD_DIM), lambda b, h: (b, 0, h))
    outs = pl.pallas_call(
        body, name="hgrn_bwd", grid=(B, N_HEADS),
        in_specs=[blk(0), blk(1), blk(2), blk(3), blk(4), seq, seq,
                  pl.BlockSpec((4, HEAD_DIM), lambda b, h: (0, h)),
                  pl.BlockSpec((1, HEAD_DIM), lambda b, h: (0, h))] + [ANY] * n,
        out_specs=[pl.BlockSpec((5, None, L, HEAD_DIM), lambda b, h: (0, b, 0, h)),
                   pl.BlockSpec((None, 2, HEAD_DIM), lambda b, h: (b, 0, h)),
                   pl.BlockSpec((None, 1, HEAD_DIM), lambda b, h: (b, 0, h))] + [ANY] * n,
        out_shape=[jax.ShapeDtypeStruct((5, B, L, D_MODEL), MX),
                   jax.ShapeDtypeStruct((B, 2, D_MODEL), F32),
                   jax.ShapeDtypeStruct((B, 1, D_MODEL), F32)] + _Exchange.out_shapes(grads, False),
        scratch_shapes=[pltpu.VMEM((L, HEAD_DIM), F32)] * 8
                       + [pltpu.VMEM((L, HEAD_DIM), MX), pltpu.VMEM((nC, HEAD_DIM, HEAD_DIM), MX),
                          pltpu.VMEM((nC, HEAD_DIM, HEAD_DIM), F32), pltpu.VMEM((nC, HEAD_DIM, HEAD_DIM), F32)]
                       + _Exchange.scratch(n),
        compiler_params=_params(("arbitrary", "arbitrary")),
    )(proj, proj, proj, proj, proj, o, dya, lbl, norm_g, *grads)
    return outs[0], outs[1], outs[2], outs[3:]


def _dproj_select(s, a5_ref, p_ref, g2_ref):
    return jnp.where(s < 5, a5_ref[...], jnp.where(s == 5, p_ref[...], g2_ref[...]))


def _dproj_specs(tm):
    a5 = pl.BlockSpec((None, tm, D_MODEL), lambda i, s: (jnp.minimum(s, 4), i, 0))
    p = pl.BlockSpec((tm, D_MODEL), lambda i, s: (i, 0))
    g2 = pl.BlockSpec((None, tm, D_MODEL), lambda i, s: (jnp.clip(s - 6, 0, 1), i, 0))
    return [a5, p, g2]


def _inproj_bwd(x2, dh, dproj5, dp, dgab, g_mix, w_in_g, d_win, dests):
    T = x2.shape[0]
    tm = min(512, T)
    nT = T // tm

    def body(a5_ref, p_ref, g2_ref, w_ref, x_ref, dh_ref, g_ref, dwin_ref, dx_ref, dg_ref, recv_ref, acc_sc, *sems):
        i, s = pl.program_id(0), pl.program_id(1)
        scatter = _Exchange([dwin_ref], [recv_ref], sems, gather=False, dests=[dests])

        @pl.when((i == 0) & (s == 0))
        def _():
            scatter.start()

        @pl.when(s == 0)
        def _():
            acc_sc[...] = jnp.zeros_like(acc_sc)

        @pl.when((i == 0) & (s == 0))
        def _():
            dg_ref[...] = jnp.zeros_like(dg_ref)

        acc_sc[...] += _dot_nt(_dproj_select(s, a5_ref, p_ref, g2_ref), w_ref[...])

        @pl.when(s == N_DEV - 1)
        def _():
            x = x_ref[...]
            r = lax.rsqrt(jnp.mean(x * x, axis=-1, keepdims=True) + RMS_EPS)
            xn = x * r
            du = acc_sc[...]
            dg_ref[...] += _fold8(du * xn)
            a = du * g_ref[...]
            dx_ref[...] = dh_ref[...] + r * a - xn * (r * jnp.mean(a * xn, axis=-1, keepdims=True))

        @pl.when((i == nT - 1) & (s == N_DEV - 1))
        def _():
            scatter.wait()

    tile = pl.BlockSpec((tm, D_MODEL), lambda i, s: (i, 0))
    return pl.pallas_call(
        body, name="inproj_bwd", grid=(nT, N_DEV),
        in_specs=_dproj_specs(tm) + [pl.BlockSpec((None, D_MODEL, D_MODEL), lambda i, s: (s, 0, 0)),
                                        tile, tile, pl.BlockSpec((1, D_MODEL), lambda i, s: (0, 0)), ANY],
        out_specs=[tile, pl.BlockSpec((8, D_MODEL), lambda i, s: (0, 0)), ANY],
        out_shape=[jax.ShapeDtypeStruct((T, D_MODEL), F32), jax.ShapeDtypeStruct((8, D_MODEL), F32)]
                  + _Exchange.out_shapes([d_win], False),
        scratch_shapes=[pltpu.VMEM((tm, D_MODEL), F32)] + _Exchange.scratch(1),
        compiler_params=_params(("arbitrary", "arbitrary")),
    )(dproj5, dp, dgab, w_in_g, x2, dh, g_mix, d_win)


def _wgrad(a, g, name, hosted=()):
    Ba, T, K = a.shape
    Bg, _, Nn = g.shape
    nb = max(Ba, Bg)
    tm = min(BIG_TOKEN_TILE, T)
    nt = T // tm
    n = len(hosted)

    def body(*refs):
        a_ref, g_ref = refs[:2]
        out_ref = refs[2 + n]
        acc_sc = refs[3 + 2 * n]
        s, t = pl.program_id(0), pl.program_id(1)
        if n:
            scatter = _Exchange(refs[2:2 + n], refs[3 + n:3 + 2 * n], refs[4 + 2 * n:], gather=False)
            pl.when((s == 0) & (t == 0))(scatter.start)

        @pl.when(t == 0)
        def _():
            acc_sc[...] = jnp.zeros_like(acc_sc)

        acc_sc[...] += _dot_tn(a_ref[...], g_ref[...])

        @pl.when(t == nt - 1)
        def _():
            out_ref[...] = acc_sc[...].astype(MX)

        if n:
            pl.when((s == nb - 1) & (t == nt - 1))(scatter.wait)

    outs = pl.pallas_call(
        body, name=name, grid=(nb, nt),
        in_specs=[pl.BlockSpec((None, tm, K), lambda s, t: (s if Ba > 1 else 0, t, 0)),
                  pl.BlockSpec((None, tm, Nn), lambda s, t: (s if Bg > 1 else 0, t, 0))] + [ANY] * n,
        out_specs=[pl.BlockSpec((None, K, Nn), lambda s, t: (s, 0, 0))] + [ANY] * n,
        out_shape=[jax.ShapeDtypeStruct((nb, K, Nn), MX)] + _Exchange.out_shapes(hosted, False),
        scratch_shapes=[pltpu.VMEM((K, Nn), F32)] + (_Exchange.scratch(n) if n else []),
        compiler_params=_params(("arbitrary", "arbitrary") if n else ("parallel", "arbitrary")),
    )(a, g, *hosted)
    return (outs[0], outs[1:]) if n else outs[0]


def _mesh_pos():
    return lax.axis_index("x"), lax.axis_index("y"), lax.axis_index("c")


def _device_of(p):
    return (p // 4, (p // 2) % 2, p % 2)


class _Exchange:
    def __init__(self, srcs, outs, sems, gather, dests=None):
        send_sems, recv_sems, local_sems = sems
        x, y, c = _mesh_pos()
        me = 4 * x + 2 * y + c
        self.sends, self.arrivals, self.mine = [], [], []
        for a, (src, out) in enumerate(zip(srcs, outs)):
            lo, hi = dests[a] if dests else (0, N_DEV)

            def piece(p, src=src, lo=lo, hi=hi):
                return src if gather else src.at[jnp.clip(p - lo, 0, hi - lo - 1)]

            def served(p, lo=lo, hi=hi):
                return None if (lo, hi) == (0, N_DEV) else (p >= lo) & (p < hi)

            self.mine.append((pltpu.make_async_copy(piece(me), out.at[me], local_sems.at[a]), served(me)))
            for j in range(1, N_DEV):
                to, frm = (me + j) % N_DEV, (me + N_DEV - j) % N_DEV
                pair = dict(send_sem=send_sems.at[7 * a + j - 1], recv_sem=recv_sems.at[7 * a + j - 1],
                            device_id_type=pl.DeviceIdType.MESH)
                self.sends.append((pltpu.make_async_remote_copy(
                    src_ref=piece(to), dst_ref=out.at[me], device_id=_device_of(to), **pair), served(to)))
                self.arrivals.append((pltpu.make_async_remote_copy(
                    src_ref=piece(frm), dst_ref=out.at[frm], device_id=_device_of(frm), **pair), served(me)))

    @staticmethod
    def _each(copies, act):
        for cp, takes_part in copies:
            if takes_part is None:
                act(cp)
            else:
                pl.when(takes_part)(functools.partial(act, cp))

    def start(self):
        self._each(self.mine + self.sends, lambda cp: cp.start())

    def wait(self):
        self._each(self.arrivals, lambda cp: cp.wait_recv())
        self._each(self.sends, lambda cp: cp.wait_send())
        self._each(self.mine, lambda cp: cp.wait())

    @staticmethod
    def scratch(n):
        return [pltpu.SemaphoreType.DMA((7 * n,)), pltpu.SemaphoreType.DMA((7 * n,)), pltpu.SemaphoreType.DMA((n,))]

    @staticmethod
    def out_shapes(arrays, gather):
        return [jax.ShapeDtypeStruct((N_DEV,) + (a.shape if gather else a.shape[1:]), a.dtype) for a in arrays]


class _TwoLevelGather:
    OVER_ICI = (2, 3, 6)

    def __init__(self, srcs, outs, sems):
        send_sems, recv_sems, local_sems = sems
        order = _gather_order()
        me, sibling = order[0], order[1]
        self.mine, self.first, self.passed, self.arrivals = [], [], [], {}
        for a, (src, out) in enumerate(zip(srcs, outs)):
            def copy(s, block, to, src_ref=None, out=out, a=a):
                slot = out.at[4 * block[0] + 2 * block[1] + block[2]]
                return pltpu.make_async_remote_copy(
                    src_ref=slot if src_ref is None else src_ref, dst_ref=slot,
                    send_sem=send_sems.at[7 * a + s - 1], recv_sem=recv_sems.at[7 * a + s - 1],
                    device_id=to, device_id_type=pl.DeviceIdType.MESH)

            self.mine.append(pltpu.make_async_copy(src, out.at[4 * me[0] + 2 * me[1] + me[2]], local_sems.at[a]))
            self.first.append(copy(1, me, sibling, src))
            self.first += [copy(s, me, order[s], src) for s in self.OVER_ICI]
            self.passed += [(s, copy(s + (1 if s == 6 else 2), order[s], sibling)) for s in self.OVER_ICI]
            for s in range(1, N_DEV):
                self.arrivals[(a, s)] = copy(s, order[s], me)
        self.n = len(srcs)

    def start(self):
        for cp in self.mine + self.first:
            cp.start()

    def relay(self):
        for k, (s, cp) in enumerate(self.passed):
            self.arrivals[(k // len(self.OVER_ICI), s)].wait_recv()
            cp.start()

    def finish(self):
        for a in range(self.n):
            for s in range(1, N_DEV):
                if s not in self.OVER_ICI:
                    self.arrivals[(a, s)].wait_recv()
        for cp in self.first + [cp for _, cp in self.passed]:
            cp.wait_send()
        for cp in self.mine:
            cp.wait()


def _all_reduce_small(v):
    R, C = v.shape

    def body(v_ref, out_ref, slots, send_sems, recv_sems):
        x, y, c = _mesh_pos()
        me = 4 * x + 2 * y + c

        def copy(j, to):
            return pltpu.make_async_remote_copy(
                src_ref=v_ref, dst_ref=slots.at[me],
                send_sem=send_sems.at[j - 1], recv_sem=recv_sems.at[j - 1],
                device_id=_device_of(to), device_id_type=pl.DeviceIdType.MESH)

        sends = [copy(j, (me + j) % N_DEV) for j in range(1, N_DEV)]
        for cp in sends:
            cp.start()
        slots[me] = v_ref[...]
        for j in range(1, N_DEV):
            frm = (me + N_DEV - j) % N_DEV
            pltpu.make_async_remote_copy(
                src_ref=v_ref, dst_ref=slots.at[frm], send_sem=send_sems.at[j - 1], recv_sem=recv_sems.at[j - 1],
                device_id=_device_of(frm), device_id_type=pl.DeviceIdType.MESH).wait_recv()
        for cp in sends:
            cp.wait_send()
        acc = slots[0]
        for p in range(1, N_DEV):
            acc = acc + slots[p]
        out_ref[...] = acc

    return pl.pallas_call(
        body, name="all_reduce_small",
        in_specs=[pl.BlockSpec(memory_space=pltpu.VMEM)], out_specs=pl.BlockSpec(memory_space=pltpu.VMEM),
        out_shape=jax.ShapeDtypeStruct((R, C), F32),
        scratch_shapes=[pltpu.VMEM((N_DEV, R, C), F32), pltpu.SemaphoreType.DMA((7,)), pltpu.SemaphoreType.DMA((7,))],
    )(v)


def _adamw_math(w, g, m, v):
    m = ADAM_B1 * m + (1.0 - ADAM_B1) * g
    v = ADAM_B2 * v + (1.0 - ADAM_B2) * (g * g)
    m_hat = m / (1.0 - ADAM_B1 ** ADAM_STEP)
    v_hat = v / (1.0 - ADAM_B2 ** ADAM_STEP)
    delta = -ADAM_LR * (m_hat / (jnp.sqrt(v_hat) + ADAM_EPS) + ADAM_WD * w)
    return delta, m, v


def _adamw_reduce(recvs, dests, w, m, v, name):
    R, C = w.shape
    tr = R if R <= 256 else 256
    while R % tr:
        tr //= 2
    n = len(recvs)

    def body(*refs):
        w_ref, m_ref, v_ref, g_ref, d_ref, nm_ref, nv_ref = refs[n:]

        def update(r_ref):
            g = r_ref[0].astype(F32)
            for p in range(1, N_DEV):
                g = g + r_ref[p].astype(F32)
            d, nm, nv = _adamw_math(w_ref[...], g, m_ref[...], v_ref[...])
            g_ref[...] = g
            d_ref[...] = d
            nm_ref[...] = nm
            nv_ref[...] = nv

        if n == 1:
            update(refs[0])
        else:
            x, y, c = _mesh_pos()
            me = 4 * x + 2 * y + c
            for r_ref, (lo, hi) in zip(refs[:n], dests):
                pl.when((me >= lo) & (me < hi))(functools.partial(update, r_ref))

    tile = pl.BlockSpec((tr, C), lambda i: (i, 0))
    shp = jax.ShapeDtypeStruct((R, C), F32)
    return pl.pallas_call(
        body, name=name, grid=(R // tr,),
        in_specs=[pl.BlockSpec((N_DEV, tr, C), lambda i: (0, i, 0))] * n + [tile, tile, tile],
        out_specs=[tile] * 4, out_shape=[shp] * 4,
        compiler_params=_params(("parallel",)),
    )(*recvs, w, m, v)


def _adamw_small(g, w, m, v):
    def body(g_ref, w_ref, m_ref, v_ref, go_ref, d_ref, nm_ref, nv_ref):
        go_ref[...] = g_ref[...]
        for d in range(2):
            p0 = _sig(w_ref[8 + 2 * d:9 + 2 * d, :] - w_ref[9 + 2 * d:10 + 2 * d, :])
            dl0 = g_ref[12 + d:13 + d, :] * p0 * (1.0 - p0)
            go_ref[8 + 2 * d:9 + 2 * d, :] = dl0
            go_ref[9 + 2 * d:10 + 2 * d, :] = -dl0
            go_ref[12 + d:13 + d, :] = jnp.zeros((1, D_MODEL), F32)
        d, nm, nv = _adamw_math(w_ref[...], go_ref[...], m_ref[...], v_ref[...])
        d_ref[...] = d
        nm_ref[...] = nm
        nv_ref[...] = nv

    shp = jax.ShapeDtypeStruct(g.shape, F32)
    vm = pl.BlockSpec(memory_space=pltpu.VMEM)
    return pl.pallas_call(body, name="adamw_small", in_specs=[vm] * 4, out_specs=[vm] * 4, out_shape=[shp] * 4)(g, w, m, v)


def _local_step(x, tgt, g_mix, lb_shard, norm_g, pool_scale, g_ffn, g_final, w_in_shard, late_shards):
    B, L, _ = x.shape
    T = B * L
    rows = D_MODEL // N_DEV
    x2, tgt2 = x.reshape(T, D_MODEL), tgt.reshape(T, D_MODEL)

    proj, u, w_in_g, lb_g = _rms_inproj(x2, g_mix, w_in_shard, lb_shard)
    lb = lb_g[:, :4].transpose(1, 0, 2).reshape(4, D_MODEL)
    proj4 = proj.reshape(N_DEV, B, L, D_MODEL)
    o, ya, (wa, wb, wo, wfi_g, wfo_g, pw_g) = _hgrn_fwd(proj4, lb, norm_g, B, L, late_shards)
    wa, wb, wo = (w_.reshape(D_MODEL, D_MODEL) for w_ in (wa, wb, wo))
    wfo_g = wfo_g.reshape(4, FF_BLOCK, D_MODEL)
    pool_w_full = pw_g.reshape(N_DEV, 4, 32, POOL_DIM).transpose(1, 0, 2, 3).reshape(4, POOL_DIM, POOL_DIM)
    yb = _pool_fwd(proj4, pool_w_full, pool_scale, B, L)
    ya2, yb2 = ya.reshape(T, D_MODEL), yb.reshape(T, D_MODEL)
    za, zb, mg, h = _merge_out(x2, proj, ya2, yb2, wa, wb, wo)
    gate, up, hid, u2, dh2, dh2b, loss_p, dg_final = _ffn_fwd_loss(h, tgt2, g_ffn, g_final, wfi_g, wfo_g)
    loss = jnp.sum(loss_p[:, 0, 0])

    dgu, dh, dhb, dg_ffn = _ffn_bwd(h, dh2, dh2b, gate, up, g_ffn, wfi_g, wfo_g)
    d_wfo = _wgrad(hid, dh2b[None], "wgrad_ffn_out")
    d_wfi, (recv_wfo,) = _wgrad(u2[None], dgu.reshape(N_DEV, T, FF_BLOCK), "wgrad_ffn_in",
                                hosted=[d_wfo.reshape(N_DEV, FF_BLOCK // 2, D_MODEL)])
    dza, dzb, dgab, dya, dyb = _merge_bwd(dhb, proj, za, zb, wa, wb, wo)
    d_wo = _wgrad(mg[None], dhb[None], "wgrad_out")
    d_wa = _wgrad(ya2[None], dza[None], "wgrad_branch_a")
    d_wb = _wgrad(yb2[None], dzb[None], "wgrad_branch_b")
    dp, dpw_p, dps_p = _pool_bwd(proj4, dyb.reshape(B, L, D_MODEL), pool_w_full, pool_scale, B, L)
    d_pw = dpw_p.sum(0).reshape(4, N_DEV, 32, POOL_DIM).transpose(1, 0, 2, 3).reshape(N_DEV, 128, POOL_DIM)
    dp2 = dp.reshape(T, D_MODEL)
    w_in_dests = [(5, 6), (6, 8), (0, 5)]
    d_win_pool = _wgrad(u[None], dp2[None], "wgrad_in_pool")
    d_win_gates = _wgrad(u[None], dgab, "wgrad_in_gates")
    slices = [d_wa.reshape(N_DEV, rows, D_MODEL), d_wb.reshape(N_DEV, rows, D_MODEL),
              d_wo.reshape(N_DEV, rows, D_MODEL), d_wfi, d_pw.astype(MX), d_win_pool, d_win_gates]
    dests = [(0, N_DEV)] * 5 + w_in_dests[:2]
    dproj5, dlb_p, dng_p, recv = _hgrn_bwd(proj4, o, dya.reshape(B, L, D_MODEL), lb, norm_g, B, L, slices, dests)
    dproj5 = dproj5.reshape(5, T, D_MODEL)
    d_win_rec = _wgrad(u[None], dproj5, "wgrad_in_recurrence")
    grad_x, dg_mix, recv_win_rec = _inproj_bwd(x2, dh, dproj5, dp2, dgab, g_mix, w_in_g, d_win_rec, w_in_dests[2])

    small = dict(g_mix=dg_mix.sum(0), hgrn_norm_g=dng_p.sum((0, 1)), pool_scale=dps_p.sum((0, 1)),
                 g_ffn=dg_ffn.sum(0), g_final=dg_final.sum(0), lb=dlb_p.sum(0))
    recv_w_in = [recv[5], recv[6], recv_win_rec]
    recv_late = list(recv[:4]) + [recv_wfo, recv[4]]
    return loss, grad_x.reshape(B, L, D_MODEL), (recv_w_in, w_in_dests), recv_late, small


def kernel(x, g_mix, w_in, lb_logits, hgrn_norm_g, pool_w, pool_scale, w_branch_a, w_branch_b, w_out, g_ffn, w_ffn_in, w_ffn_out, g_final, loss_target, m_g_mix, m_w_in, m_lb_logits, m_hgrn_norm_g, m_pool_w, m_pool_scale, m_w_branch_a, m_w_branch_b, m_w_out, m_g_ffn, m_w_ffn_in, m_w_ffn_out, m_g_final, v_g_mix, v_w_in, v_lb_logits, v_hgrn_norm_g, v_pool_w, v_pool_scale, v_w_branch_a, v_w_branch_b, v_w_out, v_g_ffn, v_w_ffn_in, v_w_ffn_out, v_g_final):
    me = 4 * lax.axis_index("x") + 2 * lax.axis_index("y") + lax.axis_index("c")

    late_shards = [w_branch_a[0].astype(MX), w_branch_b[0].astype(MX), w_out[0].astype(MX),
                   w_ffn_in[0].astype(MX), w_ffn_out[0].astype(MX), pool_w[0].reshape(4 * 32, POOL_DIM).astype(MX)]
    lb_shard = jnp.pad(lb_logits.reshape(4, HEAD_DIM), ((0, 4), (0, 0)))

    loss, grad_x, (recv_w_in, w_in_dests), recv_late, small = _local_step(
        x, loss_target, g_mix, lb_shard, hgrn_norm_g, pool_scale, g_ffn, g_final[None], w_in[0].astype(MX),
        late_shards)
    loss = lax.psum(loss, ("x", "y", "c"))

    packed = jnp.zeros((16, D_MODEL), F32)
    names = ["g_mix", "hgrn_norm_g", "pool_scale", "g_ffn", "g_final"]
    for i, nme in enumerate(names):
        packed = packed.at[i].set(small[nme])
    packed = packed.at[5:7].set(small["lb"])
    red = _all_reduce_small(packed)
    dlb_mine = lax.dynamic_slice_in_dim(red[5:7], me * HEAD_DIM, HEAD_DIM, axis=1)

    sw = jnp.zeros((16, D_MODEL), F32)
    sm = jnp.zeros((16, D_MODEL), F32)
    sv = jnp.ones((16, D_MODEL), F32)
    smalls = [(g_mix, m_g_mix, v_g_mix), (hgrn_norm_g, m_hgrn_norm_g, v_hgrn_norm_g),
              (pool_scale, m_pool_scale, v_pool_scale), (g_ffn, m_g_ffn, v_g_ffn),
              (g_final[None], m_g_final[None], v_g_final[None])]
    for i, (w_, m_, v_) in enumerate(smalls):
        sw, sm, sv = sw.at[i].set(w_[0]), sm.at[i].set(m_[0]), sv.at[i].set(v_[0])
    sg = red.at[5:].set(0.0)
    sg = sg.at[12:14, :HEAD_DIM].set(dlb_mine)
    sw = sw.at[8:12, :HEAD_DIM].set(lb_logits.reshape(4, HEAD_DIM))
    sm = sm.at[8:12, :HEAD_DIM].set(m_lb_logits.reshape(4, HEAD_DIM))
    sv = sv.at[8:12, :HEAD_DIM].set(v_lb_logits.reshape(4, HEAD_DIM))
    sg, sd, snm, snv = _adamw_small(sg, sw, sm, sv)

    def small_out(arr, i, like):
        return arr[i].reshape(like.shape)

    def lb_out(arr):
        return arr[8:12, :HEAD_DIM].reshape(2, 2, HEAD_DIM)

    order = ["w_in", "w_branch_a", "w_branch_b", "w_out", "w_ffn_in", "w_ffn_out", "pool_w"]
    params = dict(w_in=(w_in, m_w_in, v_w_in), w_branch_a=(w_branch_a, m_w_branch_a, v_w_branch_a),
                  w_branch_b=(w_branch_b, m_w_branch_b, v_w_branch_b), w_out=(w_out, m_w_out, v_w_out),
                  w_ffn_in=(w_ffn_in, m_w_ffn_in, v_w_ffn_in), w_ffn_out=(w_ffn_out, m_w_ffn_out, v_w_ffn_out),
                  pool_w=(pool_w, m_pool_w, v_pool_w))
    res = {}
    for nme, r, dests in zip(order, [recv_w_in] + [[r] for r in recv_late], [w_in_dests] + [None] * 6):
        w_, m_, v_ = params[nme]
        shape2 = r[0].shape[1:]
        outs = _adamw_reduce(r, dests, w_.reshape(shape2), m_.reshape(shape2), v_.reshape(shape2), "adamw_" + nme)
        res[nme] = [o_.reshape(w_.shape) for o_ in outs]

    def pick(k):
        small_src = [sg, sd, snm, snv][k]
        return [small_out(small_src, 0, g_mix), res["w_in"][k], lb_out(small_src), small_out(small_src, 1, hgrn_norm_g),
                res["pool_w"][k], small_out(small_src, 2, pool_scale), res["w_branch_a"][k], res["w_branch_b"][k],
                res["w_out"][k], small_out(small_src, 3, g_ffn), res["w_ffn_in"][k], res["w_ffn_out"][k],
                small_out(small_src, 4, g_final)]

    return (loss, grad_x, *pick(0), *pick(1), *pick(2), *pick(3))
```

```python
import functools

import jax
import jax.numpy as jnp
from jax import lax
from jax.experimental import pallas as pl
from jax.experimental.pallas import tpu as pltpu

F32 = jnp.float32
MX = jnp.bfloat16

D_MODEL = 1024
N_HEADS = 8
HEAD_DIM = 128
CHUNK = 16
POOL_WINDOWS = (2, 4, 8, 16)
POOL_DIM = 256
FF_BLOCK = 704
N_DEV = 8
RMS_EPS = 1e-6
ADAM_LR, ADAM_B1, ADAM_B2, ADAM_EPS, ADAM_WD, ADAM_STEP = 0.001, 0.9, 0.999, 1e-08, 0.01, 10
VMEM_LIMIT = 56 * 1024 * 1024
BIG_TOKEN_TILE = 1024
ANY = pl.BlockSpec(memory_space=pl.ANY)


def _params(sem=None):
    return pltpu.CompilerParams(dimension_semantics=sem, vmem_limit_bytes=VMEM_LIMIT)


def _dot(a, b):
    return lax.dot_general(a.astype(MX), b.astype(MX), (((1,), (0,)), ((), ())), preferred_element_type=F32)


def _dot_nt(a, b):
    return lax.dot_general(a.astype(MX), b.astype(MX), (((1,), (1,)), ((), ())), preferred_element_type=F32)


def _dot_tn(a, b):
    return lax.dot_general(a.astype(MX), b.astype(MX), (((0,), (0,)), ((), ())), preferred_element_type=F32)


def _sig(x):
    return 1.0 / (1.0 + jnp.exp(-x))


def _fold8(v):
    return v.reshape(v.shape[0] // 8, 8, v.shape[1]).sum(axis=0)


def _gather_order():
    x, y, c = _mesh_pos()
    near, far = [(1 - x, y), (x, 1 - y)], (1 - x, 1 - y)
    return ([(x, y, c), (x, y, 1 - c)] + [(*chip, c) for chip in near] + [(*chip, 1 - c) for chip in near]
            + [(*far, c), (*far, 1 - c)])


def _rms_inproj(x2, g_mix, w_shard, lb_shard):
    T = x2.shape[0]
    tm = min(BIG_TOKEN_TILE, T)
    nT = T // tm
    block_of_step = jnp.stack([4 * px + 2 * py + pc for px, py, pc in _gather_order()]).astype(jnp.int32)

    def body(order_ref, x_ref, g_ref, w_ref, lb_ref, proj_ref, u_ref, wg_ref, lbg_ref,
             u_sc, w_sc, load_sem, send_sems, recv_sems, local_sem, *lb_sems):
        k, i = pl.program_id(0), pl.program_id(1)
        order = _gather_order()
        me, sibling = order[0], order[1]

        def slot(dev):
            return wg_ref.at[4 * dev[0] + 2 * dev[1] + dev[2]]

        def copy(n, block, to, src=None):
            return pltpu.make_async_remote_copy(
                src_ref=slot(block) if src is None else src, dst_ref=slot(block),
                send_sem=send_sems.at[n], recv_sem=recv_sems.at[n], device_id=to, device_id_type=pl.DeviceIdType.MESH)

        mine = pltpu.make_async_copy(w_ref, slot(me), local_sem)
        to_sibling = copy(0, me, sibling, src=w_ref)
        to_near = [copy(1, me, order[2], src=w_ref), copy(2, me, order[3], src=w_ref)]
        to_far = copy(5, me, order[6], src=w_ref)
        passed = {2: copy(3, order[2], sibling), 3: copy(4, order[3], sibling), 6: copy(6, order[6], sibling)}
        lb_gather = _Exchange([lb_ref], [lbg_ref], lb_sems, gather=True)

        @pl.when((k == 0) & (i == 0))
        def _():
            for cp in [mine, to_sibling] + to_near:
                cp.start()
            lb_gather.start()

        for s, block in enumerate(order):
            @pl.when((k == s) & (i == 0))
            def _(s=s, block=block):
                if s == 0:
                    mine.wait()
                else:
                    copy(s - 1, block, me).wait_recv()
                if s in passed:
                    passed[s].start()
                if s == 3:
                    for cp in to_near:
                        cp.wait_send()
                    to_far.start()
                load = pltpu.make_async_copy(slot(block), w_sc, load_sem)
                load.start()
                load.wait()

        rows = pl.ds(pl.multiple_of(i * tm, tm), tm)

        @pl.when(k == 0)
        def _():
            x = x_ref[...]
            r = lax.rsqrt(jnp.mean(x * x, axis=-1, keepdims=True) + RMS_EPS)
            u = (x * r * g_ref[...]).astype(MX)
            u_sc[rows, :] = u
            u_ref[...] = u

        proj_ref[...] = jnp.dot(u_sc[rows, :], w_sc[...], preferred_element_type=F32)

        @pl.when((k == N_DEV - 1) & (i == nT - 1))
        def _():
            for cp in [to_sibling, to_far] + list(passed.values()):
                cp.wait_send()
            lb_gather.wait()

    def tile_once(k, i, order_ref):
        return (jnp.where(k == 0, i, nT - 1), 0)

    grid_spec = pltpu.PrefetchScalarGridSpec(
        num_scalar_prefetch=1, grid=(N_DEV, nT),
        in_specs=[pl.BlockSpec((tm, D_MODEL), tile_once),
                  pl.BlockSpec((1, D_MODEL), lambda k, i, order_ref: (0, 0)), ANY, ANY],
        out_specs=[pl.BlockSpec((None, tm, D_MODEL), lambda k, i, order_ref: (order_ref[k], i, 0)),
                   pl.BlockSpec((tm, D_MODEL), tile_once), ANY, ANY],
        scratch_shapes=[pltpu.VMEM((T, D_MODEL), MX), pltpu.VMEM((D_MODEL, D_MODEL), MX),
                        pltpu.SemaphoreType.DMA(()), pltpu.SemaphoreType.DMA((7,)), pltpu.SemaphoreType.DMA((7,)),
                        pltpu.SemaphoreType.DMA(())] + _Exchange.scratch(1))
    return pl.pallas_call(
        body, name="rms_inproj", grid_spec=grid_spec,
        out_shape=[jax.ShapeDtypeStruct((N_DEV, T, D_MODEL), F32), jax.ShapeDtypeStruct((T, D_MODEL), MX),
                   jax.ShapeDtypeStruct((N_DEV, D_MODEL, D_MODEL), MX),
                   jax.ShapeDtypeStruct((N_DEV,) + lb_shard.shape, lb_shard.dtype)],
        compiler_params=_params(("arbitrary", "arbitrary")),
    )(block_of_step, x2, g_mix, w_shard, lb_shard)


HBLK = 128


def _seg_cumsum(x, rev):
    n = x.shape[0]
    pos = lax.broadcasted_iota(jnp.int32, x.shape, 0) & (CHUNK - 1)
    s = 1
    while s < CHUNK:
        if rev:
            x = x + jnp.where(pos < CHUNK - s, pltpu.roll(x, n - s, 0), 0.0)
        else:
            x = x + jnp.where(pos >= s, pltpu.roll(x, s, 0), 0.0)
        s *= 2
    return x


def _block_gates(q_r, f_r, lb_row, rev):
    sq = _sig(q_r)
    q = q_r * sq
    sg = _sig(f_r)
    f = lb_row + (1.0 - lb_row) * sg
    k = 1.0 - f
    lf = jnp.log(f)
    pre = _seg_cumsum(lf, False)
    suf = _seg_cumsum(lf, True)
    tot = pre + suf - lf
    b = suf if rev else pre
    eb = jnp.exp(b)
    enb = jnp.exp(-b)
    eend = jnp.exp(tot - b)
    return dict(sq=sq, sg=sg, f=f, eb=eb, enb=enb, eend=eend, dec=jnp.exp(tot),
                P=q * eb, Kt=k * enb, Ke=k * eend)


def _block_mask(rev, transposed=False):
    ri = lax.broadcasted_iota(jnp.int32, (HBLK, HBLK), 0)
    ci = lax.broadcasted_iota(jnp.int32, (HBLK, HBLK), 1)
    same = (ri // CHUNK) == (ci // CHUNK)
    return same & ((ci >= ri) if rev != transposed else (ci <= ri))


def _chunk_rows(c, rows=CHUNK):
    return pl.ds(c * CHUNK, rows)


def _block_rows(kb):
    return pl.ds(kb * HBLK, HBLK)


def _chunk_outer_products(a, b, out_sc, kb):
    a, b = a.astype(MX), b.astype(MX)
    for u in range(HBLK // CHUNK):
        r = slice(u * CHUNK, (u + 1) * CHUNK)
        out_sc[kb * (HBLK // CHUNK) + u] = _dot_tn(a[r, :], b[r, :])


UNIT_BLOCKS_FWD = 2
UNIT_BLOCKS_BWD = 1


def _unit_blocks(u, fn, carry, unit_blocks):
    for b in range(unit_blocks):
        carry = fn(u * unit_blocks + b, carry)
    return carry


def _unit_chunks(u, descending, fn, carry, unit_blocks):
    unit_chunks = unit_blocks * (HBLK // CHUNK)
    for j in range(unit_chunks):
        carry = fn(u * unit_chunks + (unit_chunks - 1 - j if descending else j), carry)
    return carry


def _pipelined(n_units, descending, first, c1, second, c2):
    units = list(range(n_units))[::-1] if descending else list(range(n_units))
    c1 = first(units[0], c1)
    for t, u in enumerate(units):
        if t + 1 < n_units:
            c1 = first(units[t + 1], c1)
        c2 = second(u, c2)
    return c1, c2


def _lower_bounds(lbl_ref):
    return _sig(lbl_ref[0:1, :] - lbl_ref[1:2, :]), _sig(lbl_ref[2:3, :] - lbl_ref[3:4, :])


def _hgrn_fwd(proj, lbl, norm_g, B, L, shards):
    nC = L // CHUNK
    n = len(shards)
    relay_step = (11 * B * N_HEADS) // 16

    def body(*refs):
        q_ref, ff_ref, fb_ref, i_ref, og_ref, lbl_ref, ng_ref = refs[:7]
        o_ref, ya_ref = refs[7 + n:9 + n]
        o_sc, p_sc, dec_sc, upd_sc = refs[9 + 2 * n:13 + 2 * n]
        step_id = pl.program_id(0) * N_HEADS + pl.program_id(1)
        gather = _TwoLevelGather(refs[7:7 + n], refs[9 + n:9 + 2 * n], refs[13 + 2 * n:])

        @pl.when(step_id == 0)
        def _():
            gather.start()

        @pl.when(step_id == relay_step)
        def _():
            gather.relay()

        lb_f, lb_b = _lower_bounds(lbl_ref)
        o_sc[...] = jnp.zeros_like(o_sc)

        def run_dir(f_ref, lb_row, rev):
            def block(kb, carry):
                rows = _block_rows(kb)
                g = _block_gates(q_ref[rows, :], f_ref[rows, :], lb_row, rev)
                v = i_ref[rows, :]
                a = jnp.where(_block_mask(rev), _dot_nt(g["P"], g["Kt"]), 0.0)
                o_sc[rows, :] += _dot(a, v)
                p_sc[rows, :] = g["P"].astype(MX)
                dec_sc[rows, :] = g["dec"]
                _chunk_outer_products(v, g["Ke"], upd_sc, kb)
                return carry

            def step(c, st):
                rows = _chunk_rows(c)
                o_sc[rows, :] += _dot_nt(p_sc[rows, :], st)
                dec = dec_sc[_chunk_rows(c, 1), :]
                return st * dec + upd_sc[c]

            ub = UNIT_BLOCKS_FWD
            _pipelined(L // (HBLK * ub), rev,
                       lambda u, c: _unit_blocks(u, block, c, ub), 0,
                       lambda u, st: _unit_chunks(u, rev, step, st, ub), jnp.zeros((HEAD_DIM, HEAD_DIM), F32))

        run_dir(ff_ref, lb_f, False)
        run_dir(fb_ref, lb_b, True)
        o = o_sc[...]
        o_ref[...] = o
        on = o * lax.rsqrt(jnp.mean(o * o, axis=-1, keepdims=True) + RMS_EPS)
        og = og_ref[...]
        ya_ref[...] = ((on * ng_ref[...]) * (og * _sig(og))).astype(MX)

        @pl.when(step_id == B * N_HEADS - 1)
        def _():
            gather.finish()

    def blk(s):
        return pl.BlockSpec((None, None, L, HEAD_DIM), lambda b, h, s=s: (s, b, 0, h))

    out_blk = pl.BlockSpec((None, L, HEAD_DIM), lambda b, h: (b, 0, h))
    outs = pl.pallas_call(
        body, name="hgrn_fwd", grid=(B, N_HEADS),
        in_specs=[blk(0), blk(1), blk(2), blk(3), blk(4),
                  pl.BlockSpec((4, HEAD_DIM), lambda b, h: (0, h)),
                  pl.BlockSpec((1, HEAD_DIM), lambda b, h: (0, h))] + [ANY] * n,
        out_specs=[out_blk, out_blk] + [ANY] * n,
        out_shape=[jax.ShapeDtypeStruct((B, L, D_MODEL), F32), jax.ShapeDtypeStruct((B, L, D_MODEL), MX)]
                  + _Exchange.out_shapes(shards, True),
        scratch_shapes=[pltpu.VMEM((L, HEAD_DIM), F32), pltpu.VMEM((L, HEAD_DIM), MX),
                        pltpu.VMEM((L, HEAD_DIM), F32), pltpu.VMEM((nC, HEAD_DIM, HEAD_DIM), F32)]
                       + _Exchange.scratch(n),
        compiler_params=_params(("arbitrary", "arbitrary")),
    )(proj, proj, proj, proj, proj, lbl, norm_g, *shards)
    return outs[0], outs[1], outs[2:]


POOL_PAD = 8


def _pool_window(p, ext_sc, half, adjoint):
    L = p.shape[0]
    n = L + 2 * POOL_PAD
    ext_sc[0:POOL_PAD, :] = jnp.zeros((POOL_PAD, p.shape[1]), F32)
    ext_sc[POOL_PAD + L:n, :] = jnp.zeros((POOL_PAD, p.shape[1]), F32)
    ext_sc[POOL_PAD:POOL_PAD + L, :] = p
    x = ext_sc[...]
    s = x + pltpu.roll(x, 1 if adjoint else n - 1, 0)
    w = 1
    while w < half:
        s = pltpu.roll(s, w, 0) + pltpu.roll(s, n - w, 0)
        w *= 2
    ext_sc[...] = s
    return ext_sc[POOL_PAD:POOL_PAD + L, :]


def _pool_count(L, half):
    t = lax.broadcasted_iota(jnp.int32, (L, 1), 0)
    lo = jnp.clip(t - half + 1, 0, L)
    hi = jnp.clip(t + half + 1, 0, L)
    return (hi - lo).astype(F32)


def _pool_fwd(proj, pool_w_full, pool_scale, B, L):
    def body(p_ref, w_ref, s_ref, yb_ref, ext_sc):
        for g, win in enumerate(POOL_WINDOWS):
            cols = slice(g * POOL_DIM, (g + 1) * POOL_DIM)
            p = p_ref[:, cols]
            y = _pool_window(p, ext_sc, win // 2, False) / _pool_count(L, win // 2) - p
            yb_ref[:, cols] = (_dot(y, w_ref[g]) * s_ref[:, cols]).astype(MX)

    return pl.pallas_call(
        body, name="pool_fwd", grid=(B,),
        in_specs=[pl.BlockSpec((None, None, L, D_MODEL), lambda b: (5, b, 0, 0)),
                  pl.BlockSpec((4, POOL_DIM, POOL_DIM), lambda b: (0, 0, 0)),
                  pl.BlockSpec((1, D_MODEL), lambda b: (0, 0))],
        out_specs=pl.BlockSpec((None, L, D_MODEL), lambda b: (b, 0, 0)),
        out_shape=jax.ShapeDtypeStruct((B, L, D_MODEL), MX),
        scratch_shapes=[pltpu.VMEM((L + 2 * POOL_PAD, POOL_DIM), F32)],
        compiler_params=_params(("parallel",)),
    )(proj, pool_w_full, pool_scale)


def _merge_out(x2, proj, ya, yb, wa, wb, wo):
    T = x2.shape[0]
    tm = min(512, T)

    def body(x_ref, ga_ref, gb_ref, ya_ref, yb_ref, wa_ref, wb_ref, wo_ref, za_ref, zb_ref, mg_ref, h_ref):
        za = jnp.dot(ya_ref[...], wa_ref[...], preferred_element_type=F32)
        zb = jnp.dot(yb_ref[...], wb_ref[...], preferred_element_type=F32)
        mg = (_sig(ga_ref[...]) * za + _sig(gb_ref[...]) * zb).astype(MX)
        za_ref[...] = za
        zb_ref[...] = zb
        mg_ref[...] = mg
        h_ref[...] = x_ref[...] + jnp.dot(mg, wo_ref[...], preferred_element_type=F32)

    tile = pl.BlockSpec((tm, D_MODEL), lambda i: (i, 0))
    wspec = pl.BlockSpec((D_MODEL, D_MODEL), lambda i: (0, 0))
    return pl.pallas_call(
        body, name="merge_out", grid=(T // tm,),
        in_specs=[tile,
                  pl.BlockSpec((None, tm, D_MODEL), lambda i: (6, i, 0)),
                  pl.BlockSpec((None, tm, D_MODEL), lambda i: (7, i, 0)),
                  tile, tile, wspec, wspec, wspec],
        out_specs=[tile, tile, tile, tile],
        out_shape=[jax.ShapeDtypeStruct((T, D_MODEL), F32), jax.ShapeDtypeStruct((T, D_MODEL), F32),
                   jax.ShapeDtypeStruct((T, D_MODEL), MX), jax.ShapeDtypeStruct((T, D_MODEL), F32)],
        compiler_params=_params(("parallel",)),
    )(x2, proj, proj, ya, yb, wa, wb, wo)


def _ffn_fwd_loss(h, tgt, g_ffn, g_final, wfi_g, wfo_g):
    T = h.shape[0]
    tm = min(512, T)
    nT = T // tm

    def body(h_ref, t_ref, gf_ref, gl_ref, wg_ref, wu_ref, wo_ref,
             gate_ref, up_ref, hid_ref, u2_ref, dh2_ref, dh2b_ref, loss_ref, dgl_ref, u2_sc, acc_sc):
        i, j = pl.program_id(0), pl.program_id(1)

        @pl.when(j == 0)
        def _():
            hh = h_ref[...]
            r = lax.rsqrt(jnp.mean(hh * hh, axis=-1, keepdims=True) + RMS_EPS)
            u2 = (hh * r * gf_ref[...]).astype(MX)
            u2_sc[...] = u2
            u2_ref[...] = u2
            acc_sc[...] = jnp.zeros_like(acc_sc)

        @pl.when((i == 0) & (j == 0))
        def _():
            dgl_ref[...] = jnp.zeros_like(dgl_ref)

        gate = jnp.dot(u2_sc[...], wg_ref[...], preferred_element_type=F32)
        up = jnp.dot(u2_sc[...], wu_ref[...], preferred_element_type=F32)
        hid = ((gate * _sig(gate)) * up).astype(MX)
        gate_ref[...] = gate
        up_ref[...] = up
        hid_ref[...] = hid
        acc_sc[...] += jnp.dot(hid, wo_ref[...], preferred_element_type=F32)

        @pl.when(j == 3)
        def _():
            h2 = h_ref[...] + acc_sc[...]
            r = lax.rsqrt(jnp.mean(h2 * h2, axis=-1, keepdims=True) + RMS_EPS)
            hn = h2 * r
            gl = gl_ref[...]
            err = hn * gl - t_ref[...]
            tok = jnp.mean(err * err, axis=-1, keepdims=True)
            loss_ref[...] = jnp.full(loss_ref.shape, 0.5 * jnp.sum(tok), F32)
            dy = err * (1.0 / D_MODEL)
            dgl_ref[...] += _fold8(dy * hn)
            a = dy * gl
            dh2 = r * a - hn * (r * jnp.mean(a * hn, axis=-1, keepdims=True))
            dh2_ref[...] = dh2
            dh2b_ref[...] = dh2.astype(MX)

    tile = pl.BlockSpec((tm, D_MODEL), lambda i, j: (i, 0))
    vec = pl.BlockSpec((1, D_MODEL), lambda i, j: (0, 0))
    ftile = pl.BlockSpec((None, tm, FF_BLOCK), lambda i, j: (j, i, 0))
    return pl.pallas_call(
        body, name="ffn_fwd_loss", grid=(nT, 4),
        in_specs=[tile, tile, vec, vec,
                  pl.BlockSpec((None, D_MODEL, FF_BLOCK), lambda i, j: (j, 0, 0)),
                  pl.BlockSpec((None, D_MODEL, FF_BLOCK), lambda i, j: (j + 4, 0, 0)),
                  pl.BlockSpec((None, FF_BLOCK, D_MODEL), lambda i, j: (j, 0, 0))],
        out_specs=[ftile, ftile, ftile, tile, tile, tile,
                   pl.BlockSpec((None, 8, 128), lambda i, j: (i, 0, 0)),
                   pl.BlockSpec((8, D_MODEL), lambda i, j: (0, 0))],
        out_shape=[jax.ShapeDtypeStruct((4, T, FF_BLOCK), F32), jax.ShapeDtypeStruct((4, T, FF_BLOCK), F32),
                   jax.ShapeDtypeStruct((4, T, FF_BLOCK), MX), jax.ShapeDtypeStruct((T, D_MODEL), MX),
                   jax.ShapeDtypeStruct((T, D_MODEL), F32), jax.ShapeDtypeStruct((T, D_MODEL), MX),
                   jax.ShapeDtypeStruct((nT, 8, 128), F32), jax.ShapeDtypeStruct((8, D_MODEL), F32)],
        scratch_shapes=[pltpu.VMEM((tm, D_MODEL), MX), pltpu.VMEM((tm, D_MODEL), F32)],
        compiler_params=_params(("arbitrary", "arbitrary")),
    )(h, tgt, g_ffn, g_final, wfi_g, wfi_g, wfo_g)


def _ffn_bwd(h, dh2, dh2b, gate, up, g_ffn, wfi_g, wfo_g):
    T = h.shape[0]
    tm = min(512, T)

    def body(h_ref, dh2_ref, dh2b_ref, gate_ref, up_ref, gf_ref, wg_ref, wu_ref, wo_ref,
             dgu_ref, dh_ref, dhb_ref, dgf_ref, acc_sc):
        i, j = pl.program_id(0), pl.program_id(1)

        @pl.when(j == 0)
        def _():
            acc_sc[...] = jnp.zeros_like(acc_sc)

        @pl.when((i == 0) & (j == 0))
        def _():
            dgf_ref[...] = jnp.zeros_like(dgf_ref)

        dhid = _dot_nt(dh2b_ref[...], wo_ref[...])
        gate, up = gate_ref[...], up_ref[...]
        sg = _sig(gate)
        dgate = (dhid * up * (sg * (1.0 + gate * (1.0 - sg)))).astype(MX)
        dup = (dhid * (gate * sg)).astype(MX)
        dgu_ref[0] = dgate
        dgu_ref[1] = dup
        acc_sc[...] += _dot_nt(dgate, wg_ref[...]) + _dot_nt(dup, wu_ref[...])

        @pl.when(j == 3)
        def _():
            hh = h_ref[...]
            r = lax.rsqrt(jnp.mean(hh * hh, axis=-1, keepdims=True) + RMS_EPS)
            hn = hh * r
            du2 = acc_sc[...]
            dgf_ref[...] += _fold8(du2 * hn)
            a = du2 * gf_ref[...]
            dh = dh2_ref[...] + r * a - hn * (r * jnp.mean(a * hn, axis=-1, keepdims=True))
            dh_ref[...] = dh
            dhb_ref[...] = dh.astype(MX)

    tile = pl.BlockSpec((tm, D_MODEL), lambda i, j: (i, 0))
    ftile = pl.BlockSpec((None, tm, FF_BLOCK), lambda i, j: (j, i, 0))
    return pl.pallas_call(
        body, name="ffn_bwd", grid=(T // tm, 4),
        in_specs=[tile, tile, tile, ftile, ftile,
                  pl.BlockSpec((1, D_MODEL), lambda i, j: (0, 0)),
                  pl.BlockSpec((None, D_MODEL, FF_BLOCK), lambda i, j: (j, 0, 0)),
                  pl.BlockSpec((None, D_MODEL, FF_BLOCK), lambda i, j: (j + 4, 0, 0)),
                  pl.BlockSpec((None, FF_BLOCK, D_MODEL), lambda i, j: (j, 0, 0))],
        out_specs=[pl.BlockSpec((2, None, tm, FF_BLOCK), lambda i, j: (0, j, i, 0)),
                   tile, tile, pl.BlockSpec((8, D_MODEL), lambda i, j: (0, 0))],
        out_shape=[jax.ShapeDtypeStruct((2, 4, T, FF_BLOCK), MX),
                   jax.ShapeDtypeStruct((T, D_MODEL), F32), jax.ShapeDtypeStruct((T, D_MODEL), MX),
                   jax.ShapeDtypeStruct((8, D_MODEL), F32)],
        scratch_shapes=[pltpu.VMEM((tm, D_MODEL), F32)],
        compiler_params=_params(("arbitrary", "arbitrary")),
    )(h, dh2, dh2b, gate, up, g_ffn, wfi_g, wfi_g, wfo_g)


def _merge_bwd(dhb, proj, za, zb, wa, wb, wo):
    T = dhb.shape[0]
    tm = min(512, T)

    def body(dh_ref, ga_ref, gb_ref, za_ref, zb_ref, wa_ref, wb_ref, wo_ref,
             dza_ref, dzb_ref, dgab_ref, dya_ref, dyb_ref):
        dm = _dot_nt(dh_ref[...], wo_ref[...])
        sa, sb = _sig(ga_ref[...]), _sig(gb_ref[...])
        dza = (dm * sa).astype(MX)
        dzb = (dm * sb).astype(MX)
        dza_ref[...] = dza
        dzb_ref[...] = dzb
        dgab_ref[0] = (dm * za_ref[...] * (sa * (1.0 - sa))).astype(MX)
        dgab_ref[1] = (dm * zb_ref[...] * (sb * (1.0 - sb))).astype(MX)
        dya_ref[...] = _dot_nt(dza, wa_ref[...])
        dyb_ref[...] = _dot_nt(dzb, wb_ref[...])

    tile = pl.BlockSpec((tm, D_MODEL), lambda i: (i, 0))
    wspec = pl.BlockSpec((D_MODEL, D_MODEL), lambda i: (0, 0))
    return pl.pallas_call(
        body, name="merge_bwd", grid=(T // tm,),
        in_specs=[tile,
                  pl.BlockSpec((None, tm, D_MODEL), lambda i: (6, i, 0)),
                  pl.BlockSpec((None, tm, D_MODEL), lambda i: (7, i, 0)),
                  tile, tile, wspec, wspec, wspec],
        out_specs=[tile, tile, pl.BlockSpec((2, tm, D_MODEL), lambda i: (0, i, 0)), tile, tile],
        out_shape=[jax.ShapeDtypeStruct((T, D_MODEL), MX), jax.ShapeDtypeStruct((T, D_MODEL), MX),
                   jax.ShapeDtypeStruct((2, T, D_MODEL), MX),
                   jax.ShapeDtypeStruct((T, D_MODEL), F32), jax.ShapeDtypeStruct((T, D_MODEL), F32)],
        compiler_params=_params(("parallel",)),
    )(dhb, proj, proj, za, zb, wa, wb, wo)


def _pool_bwd(proj, dyb, pool_w_full, pool_scale, B, L):
    def body(p_ref, dy_ref, w_ref, s_ref, dp_ref, dw_ref, ds_ref, ext_sc):
        for g, win in enumerate(POOL_WINDOWS):
            cols = slice(g * POOL_DIM, (g + 1) * POOL_DIM)
            p = p_ref[:, cols]
            cnt = _pool_count(L, win // 2)
            y = _pool_window(p, ext_sc, win // 2, False) / cnt - p
            z = _dot(y, w_ref[g])
            dyb_g = dy_ref[:, cols]
            ds_ref[:, cols] = jnp.sum(dyb_g * z, axis=0, keepdims=True)
            dz = dyb_g * s_ref[:, cols]
            dw_ref[g] = _dot_tn(y, dz)
            dy = _dot_nt(dz, w_ref[g])
            dp_ref[:, cols] = (_pool_window(dy / cnt, ext_sc, win // 2, True) - dy).astype(MX)

    seq = pl.BlockSpec((None, L, D_MODEL), lambda b: (b, 0, 0))
    return pl.pallas_call(
        body, name="pool_bwd", grid=(B,),
        in_specs=[pl.BlockSpec((None, None, L, D_MODEL), lambda b: (5, b, 0, 0)), seq,
                  pl.BlockSpec((4, POOL_DIM, POOL_DIM), lambda b: (0, 0, 0)),
                  pl.BlockSpec((1, D_MODEL), lambda b: (0, 0))],
        out_specs=[seq, pl.BlockSpec((None, 4, POOL_DIM, POOL_DIM), lambda b: (b, 0, 0, 0)),
                   pl.BlockSpec((None, 1, D_MODEL), lambda b: (b, 0, 0))],
        out_shape=[jax.ShapeDtypeStruct((B, L, D_MODEL), MX),
                   jax.ShapeDtypeStruct((B, 4, POOL_DIM, POOL_DIM), F32),
                   jax.ShapeDtypeStruct((B, 1, D_MODEL), F32)],
        scratch_shapes=[pltpu.VMEM((L + 2 * POOL_PAD, POOL_DIM), F32)],
        compiler_params=_params(("parallel",)),
    )(proj, dyb, pool_w_full, pool_scale)


def _hgrn_bwd(proj, o, dya, lbl, norm_g, B, L, grads, dests):
    nC = L // CHUNK
    n = len(grads)

    def body(*refs):
        q_ref, ff_ref, fb_ref, i_ref, og_ref, o_ref, dy_ref, lbl_ref, ng_ref = refs[:9]
        dp_ref, dlb_ref, dng_ref = refs[9 + n:12 + n]
        (do_sc, dq_sc, dv_sc, dP_sc, dKt_sc, dKe_sc, dbl_sc, dec_sc, ke_sc, ck_sc, upd_sc,
         dupd_sc) = refs[12 + 2 * n:24 + 2 * n]
        step_id = pl.program_id(0) * N_HEADS + pl.program_id(1)
        scatter = _Exchange(refs[9:9 + n], refs[12 + n:12 + 2 * n], refs[24 + 2 * n:], gather=False, dests=dests)

        @pl.when(step_id == 0)
        def _():
            scatter.start()

        lb_f, lb_b = _lower_bounds(lbl_ref)
        o_ = o_ref[...]
        r = lax.rsqrt(jnp.mean(o_ * o_, axis=-1, keepdims=True) + RMS_EPS)
        on = o_ * r
        og = og_ref[...]
        sog = _sig(og)
        dy = dy_ref[...]
        ng = ng_ref[...]
        dp_ref[4] = (dy * (on * ng) * (sog * (1.0 + og * (1.0 - sog)))).astype(MX)
        dn = dy * (og * sog)
        dng_ref[...] = jnp.sum(dn * on, axis=0, keepdims=True)
        don = dn * ng
        do_sc[...] = r * don - on * (r * jnp.mean(don * on, axis=-1, keepdims=True))

        dq_sc[...] = jnp.zeros_like(dq_sc)
        dv_sc[...] = jnp.zeros_like(dv_sc)

        def run_dir(f_ref, lb_row, rev, slot):
            def block(kb, carry):
                rows = _block_rows(kb)
                g = _block_gates(q_ref[rows, :], f_ref[rows, :], lb_row, rev)
                v = i_ref[rows, :]
                do = do_sc[rows, :]
                m, mt = _block_mask(rev), _block_mask(rev, True)
                at = jnp.where(mt, _dot_nt(g["Kt"], g["P"]), 0.0)
                da = jnp.where(m, _dot_nt(do, v), 0.0)
                dat = jnp.where(mt, _dot_nt(v, do), 0.0)
                dv_sc[rows, :] += _dot(at, do)
                dP_sc[rows, :] = _dot(da, g["Kt"])
                dKt_sc[rows, :] = _dot(dat, g["P"])
                ke_sc[rows, :] = g["Ke"].astype(MX)
                dec_sc[rows, :] = g["dec"]
                _chunk_outer_products(v, g["Ke"], upd_sc, kb)
                _chunk_outer_products(do, g["P"], dupd_sc, kb)
                return carry

            def fstep(c, st):
                rows = _chunk_rows(c)
                ck_sc[c] = st.astype(MX)
                dP_sc[rows, :] += _dot(do_sc[rows, :], st)
                dec = dec_sc[_chunk_rows(c, 1), :]
                return st * dec + upd_sc[c]

            ub = UNIT_BLOCKS_BWD
            n_units = L // (HBLK * ub)
            zero_state = jnp.zeros((HEAD_DIM, HEAD_DIM), F32)
            _pipelined(n_units, rev,
                       lambda u, c: _unit_blocks(u, block, c, ub), 0,
                       lambda u, st: _unit_chunks(u, rev, fstep, st, ub), zero_state)

            def bstep(c, dst):
                rows = _chunk_rows(c)
                dec = dec_sc[_chunk_rows(c, 1), :]
                dKe_sc[rows, :] = _dot(i_ref[rows, :], dst)
                dv_sc[rows, :] += _dot_nt(ke_sc[rows, :], dst)
                dbl = dec * jnp.sum(dst * ck_sc[c].astype(F32), axis=0, keepdims=True)
                dbl_sc[rows, :] = jnp.broadcast_to(dbl, (CHUNK, HEAD_DIM))
                return dst * dec + dupd_sc[c]

            def finish(kb, acc):
                rows = _block_rows(kb)
                q_r = q_ref[rows, :]
                g = _block_gates(q_r, f_ref[rows, :], lb_row, rev)
                dP, dkt, dke = dP_sc[rows, :], dKt_sc[rows, :], dKe_sc[rows, :]
                e = dke * g["Ke"]
                dlf = (_seg_cumsum(dP * g["P"] - dkt * g["Kt"], not rev) + _seg_cumsum(e, rev) - e
                       + dbl_sc[rows, :])
                df = dlf / g["f"] - (dkt * g["enb"] + dke * g["eend"])
                dp_ref[slot, rows, :] = (df * (1.0 - lb_row) * (g["sg"] * (1.0 - g["sg"]))).astype(MX)
                dq_sc[rows, :] += (dP * g["eb"]) * (g["sq"] * (1.0 + q_r * (1.0 - g["sq"])))
                return acc + jnp.sum(df * (1.0 - g["sg"]), axis=0, keepdims=True)

            _, dlb = _pipelined(n_units, not rev,
                                lambda u, dst: _unit_chunks(u, not rev, bstep, dst, ub), zero_state,
                                lambda u, acc: _unit_blocks(u, finish, acc, ub), jnp.zeros((1, HEAD_DIM), F32))
            dlb_ref[slot - 1:slot, :] = dlb

        run_dir(ff_ref, lb_f, False, 1)
        run_dir(fb_ref, lb_b, True, 2)
        dp_ref[0] = dq_sc[...].astype(MX)
        dp_ref[3] = dv_sc[...].astype(MX)

        @pl.when(step_id == B * N_HEADS - 1)
        def _():
            scatter.wait()

    def blk(s):
        return pl.BlockSpec((None, None, L, HEAD_DIM), lambda b, h, s=s: (s, b, 0, h))

    seq = pl.BlockSpec((None, L, HEAD_DIM), lambda b, h: (b, 0, h))
    outs = pl.pallas_call(
        body, name="hgrn_bwd", grid=(B, N_HEADS),
        in_specs=[blk(0), blk(1), blk(2), blk(3), blk(4), seq, seq,
                  pl.BlockSpec((4, HEAD_DIM), lambda b, h: (0, h)),
                  pl.BlockSpec((1, HEAD_DIM), lambda b, h: (0, h))] + [ANY] * n,
        out_specs=[pl.BlockSpec((5, None, L, HEAD_DIM), lambda b, h: (0, b, 0, h)),
                   pl.BlockSpec((None, 2, HEAD_DIM), lambda b, h: (b, 0, h)),
                   pl.BlockSpec((None, 1, HEAD_DIM), lambda b, h: (b, 0, h))] + [ANY] * n,
        out_shape=[jax.ShapeDtypeStruct((5, B, L, D_MODEL), MX),
                   jax.ShapeDtypeStruct((B, 2, D_MODEL), F32),
                   jax.ShapeDtypeStruct((B, 1, D_MODEL), F32)] + _Exchange.out_shapes(grads, False),
        scratch_shapes=[pltpu.VMEM((L, HEAD_DIM), F32)] * 8
                       + [pltpu.VMEM((L, HEAD_DIM), MX), pltpu.VMEM((nC, HEAD_DIM, HEAD_DIM), MX),
                          pltpu.VMEM((nC, HEAD_DIM, HEAD_DIM), F32), pltpu.VMEM((nC, HEAD_DIM, HEAD_DIM), F32)]
                       + _Exchange.scratch(n),
        compiler_params=_params(("arbitrary", "arbitrary")),
    )(proj, proj, proj, proj, proj, o, dya, lbl, norm_g, *grads)
    return outs[0], outs[1], outs[2], outs[3:]


def _dproj_select(s, a5_ref, p_ref, g2_ref):
    return jnp.where(s < 5, a5_ref[...], jnp.where(s == 5, p_ref[...], g2_ref[...]))


def _dproj_specs(tm):
    a5 = pl.BlockSpec((None, tm, D_MODEL), lambda i, s: (jnp.minimum(s, 4), i, 0))
    p = pl.BlockSpec((tm, D_MODEL), lambda i, s: (i, 0))
    g2 = pl.BlockSpec((None, tm, D_MODEL), lambda i, s: (jnp.clip(s - 6, 0, 1), i, 0))
    return [a5, p, g2]


def _inproj_bwd(x2, dh, dproj5, dp, dgab, g_mix, w_in_g, d_win, dests):
    T = x2.shape[0]
    tm = min(512, T)
    nT = T // tm

    def body(a5_ref, p_ref, g2_ref, w_ref, x_ref, dh_ref, g_ref, dwin_ref, dx_ref, dg_ref, recv_ref, acc_sc, *sems):
        i, s = pl.program_id(0), pl.program_id(1)
        scatter = _Exchange([dwin_ref], [recv_ref], sems, gather=False, dests=[dests])

        @pl.when((i == 0) & (s == 0))
        def _():
            scatter.start()

        @pl.when(s == 0)
        def _():
            acc_sc[...] = jnp.zeros_like(acc_sc)

        @pl.when((i == 0) & (s == 0))
        def _():
            dg_ref[...] = jnp.zeros_like(dg_ref)

        acc_sc[...] += _dot_nt(_dproj_select(s, a5_ref, p_ref, g2_ref), w_ref[...])

        @pl.when(s == N_DEV - 1)
        def _():
            x = x_ref[...]
            r = lax.rsqrt(jnp.mean(x * x, axis=-1, keepdims=True) + RMS_EPS)
            xn = x * r
            du = acc_sc[...]
            dg_ref[...] += _fold8(du * xn)
            a = du * g_ref[...]
            dx_ref[...] = dh_ref[...] + r * a - xn * (r * jnp.mean(a * xn, axis=-1, keepdims=True))

        @pl.when((i == nT - 1) & (s == N_DEV - 1))
        def _():
            scatter.wait()

    tile = pl.BlockSpec((tm, D_MODEL), lambda i, s: (i, 0))
    return pl.pallas_call(
        body, name="inproj_bwd", grid=(nT, N_DEV),
        in_specs=_dproj_specs(tm) + [pl.BlockSpec((None, D_MODEL, D_MODEL), lambda i, s: (s, 0, 0)),
                                        tile, tile, pl.BlockSpec((1, D_MODEL), lambda i, s: (0, 0)), ANY],
        out_specs=[tile, pl.BlockSpec((8, D_MODEL), lambda i, s: (0, 0)), ANY],
        out_shape=[jax.ShapeDtypeStruct((T, D_MODEL), F32), jax.ShapeDtypeStruct((8, D_MODEL), F32)]
                  + _Exchange.out_shapes([d_win], False),
        scratch_shapes=[pltpu.VMEM((tm, D_MODEL), F32)] + _Exchange.scratch(1),
        compiler_params=_params(("arbitrary", "arbitrary")),
    )(dproj5, dp, dgab, w_in_g, x2, dh, g_mix, d_win)


def _wgrad(a, g, name, hosted=()):
    Ba, T, K = a.shape
    Bg, _, Nn = g.shape
    nb = max(Ba, Bg)
    tm = min(BIG_TOKEN_TILE, T)
    nt = T // tm
    n = len(hosted)

    def body(*refs):
        a_ref, g_ref = refs[:2]
        out_ref = refs[2 + n]
        acc_sc = refs[3 + 2 * n]
        s, t = pl.program_id(0), pl.program_id(1)
        if n:
            scatter = _Exchange(refs[2:2 + n], refs[3 + n:3 + 2 * n], refs[4 + 2 * n:], gather=False)
            pl.when((s == 0) & (t == 0))(scatter.start)

        @pl.when(t == 0)
        def _():
            acc_sc[...] = jnp.zeros_like(acc_sc)

        acc_sc[...] += _dot_tn(a_ref[...], g_ref[...])

        @pl.when(t == nt - 1)
        def _():
            out_ref[...] = acc_sc[...].astype(MX)

        if n:
            pl.when((s == nb - 1) & (t == nt - 1))(scatter.wait)

    outs = pl.pallas_call(
        body, name=name, grid=(nb, nt),
        in_specs=[pl.BlockSpec((None, tm, K), lambda s, t: (s if Ba > 1 else 0, t, 0)),
                  pl.BlockSpec((None, tm, Nn), lambda s, t: (s if Bg > 1 else 0, t, 0))] + [ANY] * n,
        out_specs=[pl.BlockSpec((None, K, Nn), lambda s, t: (s, 0, 0))] + [ANY] * n,
        out_shape=[jax.ShapeDtypeStruct((nb, K, Nn), MX)] + _Exchange.out_shapes(hosted, False),
        scratch_shapes=[pltpu.VMEM((K, Nn), F32)] + (_Exchange.scratch(n) if n else []),
        compiler_params=_params(("arbitrary", "arbitrary") if n else ("parallel", "arbitrary")),
    )(a, g, *hosted)
    return (outs[0], outs[1:]) if n else outs[0]


def _mesh_pos():
    return lax.axis_index("x"), lax.axis_index("y"), lax.axis_index("c")


def _device_of(p):
    return (p // 4, (p // 2) % 2, p % 2)


class _Exchange:
    def __init__(self, srcs, outs, sems, gather, dests=None):
        send_sems, recv_sems, local_sems = sems
        x, y, c = _mesh_pos()
        me = 4 * x + 2 * y + c
        self.sends, self.arrivals, self.mine = [], [], []
        for a, (src, out) in enumerate(zip(srcs, outs)):
            lo, hi = dests[a] if dests else (0, N_DEV)

            def piece(p, src=src, lo=lo, hi=hi):
                return src if gather else src.at[jnp.clip(p - lo, 0, hi - lo - 1)]

            def served(p, lo=lo, hi=hi):
                return None if (lo, hi) == (0, N_DEV) else (p >= lo) & (p < hi)

            self.mine.append((pltpu.make_async_copy(piece(me), out.at[me], local_sems.at[a]), served(me)))
            for j in range(1, N_DEV):
                to, frm = (me + j) % N_DEV, (me + N_DEV - j) % N_DEV
                pair = dict(send_sem=send_sems.at[7 * a + j - 1], recv_sem=recv_sems.at[7 * a + j - 1],
                            device_id_type=pl.DeviceIdType.MESH)
                self.sends.append((pltpu.make_async_remote_copy(
                    src_ref=piece(to), dst_ref=out.at[me], device_id=_device_of(to), **pair), served(to)))
                self.arrivals.append((pltpu.make_async_remote_copy(
                    src_ref=piece(frm), dst_ref=out.at[frm], device_id=_device_of(frm), **pair), served(me)))

    @staticmethod
    def _each(copies, act):
        for cp, takes_part in copies:
            if takes_part is None:
                act(cp)
            else:
                pl.when(takes_part)(functools.partial(act, cp))

    def start(self):
        self._each(self.mine + self.sends, lambda cp: cp.start())

    def wait(self):
        self._each(self.arrivals, lambda cp: cp.wait_recv())
        self._each(self.sends, lambda cp: cp.wait_send())
        self._each(self.mine, lambda cp: cp.wait())

    @staticmethod
    def scratch(n):
        return [pltpu.SemaphoreType.DMA((7 * n,)), pltpu.SemaphoreType.DMA((7 * n,)), pltpu.SemaphoreType.DMA((n,))]

    @staticmethod
    def out_shapes(arrays, gather):
        return [jax.ShapeDtypeStruct((N_DEV,) + (a.shape if gather else a.shape[1:]), a.dtype) for a in arrays]


class _TwoLevelGather:
    OVER_ICI = (2, 3, 6)

    def __init__(self, srcs, outs, sems):
        send_sems, recv_sems, local_sems = sems
        order = _gather_order()
        me, sibling = order[0], order[1]
        self.mine, self.first, self.passed, self.arrivals = [], [], [], {}
        for a, (src, out) in enumerate(zip(srcs, outs)):
            def copy(s, block, to, src_ref=None, out=out, a=a):
                slot = out.at[4 * block[0] + 2 * block[1] + block[2]]
                return pltpu.make_async_remote_copy(
                    src_ref=slot if src_ref is None else src_ref, dst_ref=slot,
                    send_sem=send_sems.at[7 * a + s - 1], recv_sem=recv_sems.at[7 * a + s - 1],
                    device_id=to, device_id_type=pl.DeviceIdType.MESH)

            self.mine.append(pltpu.make_async_copy(src, out.at[4 * me[0] + 2 * me[1] + me[2]], local_sems.at[a]))
            self.first.append(copy(1, me, sibling, src))
            self.first += [copy(s, me, order[s], src) for s in self.OVER_ICI]
            self.passed += [(s, copy(s + (1 if s == 6 else 2), order[s], sibling)) for s in self.OVER_ICI]
            for s in range(1, N_DEV):
                self.arrivals[(a, s)] = copy(s, order[s], me)
        self.n = len(srcs)

    def start(self):
        for cp in self.mine + self.first:
            cp.start()

    def relay(self):
        for k, (s, cp) in enumerate(self.passed):
            self.arrivals[(k // len(self.OVER_ICI), s)].wait_recv()
            cp.start()

    def finish(self):
        for a in range(self.n):
            for s in range(1, N_DEV):
                if s not in self.OVER_ICI:
                    self.arrivals[(a, s)].wait_recv()
        for cp in self.first + [cp for _, cp in self.passed]:
            cp.wait_send()
        for cp in self.mine:
            cp.wait()


def _all_reduce_small(v):
    R, C = v.shape

    def body(v_ref, out_ref, slots, send_sems, recv_sems):
        x, y, c = _mesh_pos()
        me = 4 * x + 2 * y + c

        def copy(j, to):
            return pltpu.make_async_remote_copy(
                src_ref=v_ref, dst_ref=slots.at[me],
                send_sem=send_sems.at[j - 1], recv_sem=recv_sems.at[j - 1],
                device_id=_device_of(to), device_id_type=pl.DeviceIdType.MESH)

        sends = [copy(j, (me + j) % N_DEV) for j in range(1, N_DEV)]
        for cp in sends:
            cp.start()
        slots[me] = v_ref[...]
        for j in range(1, N_DEV):
            frm = (me + N_DEV - j) % N_DEV
            pltpu.make_async_remote_copy(
                src_ref=v_ref, dst_ref=slots.at[frm], send_sem=send_sems.at[j - 1], recv_sem=recv_sems.at[j - 1],
                device_id=_device_of(frm), device_id_type=pl.DeviceIdType.MESH).wait_recv()
        for cp in sends:
            cp.wait_send()
        acc = slots[0]
        for p in range(1, N_DEV):
            acc = acc + slots[p]
        out_ref[...] = acc

    return pl.pallas_call(
        body, name="all_reduce_small",
        in_specs=[pl.BlockSpec(memory_space=pltpu.VMEM)], out_specs=pl.BlockSpec(memory_space=pltpu.VMEM),
        out_shape=jax.ShapeDtypeStruct((R, C), F32),
        scratch_shapes=[pltpu.VMEM((N_DEV, R, C), F32), pltpu.SemaphoreType.DMA((7,)), pltpu.SemaphoreType.DMA((7,))],
    )(v)


def _adamw_math(w, g, m, v):
    m = ADAM_B1 * m + (1.0 - ADAM_B1) * g
    v = ADAM_B2 * v + (1.0 - ADAM_B2) * (g * g)
    m_hat = m / (1.0 - ADAM_B1 ** ADAM_STEP)
    v_hat = v / (1.0 - ADAM_B2 ** ADAM_STEP)
    delta = -ADAM_LR * (m_hat / (jnp.sqrt(v_hat) + ADAM_EPS) + ADAM_WD * w)
    return delta, m, v


def _adamw_reduce(recvs, dests, w, m, v, name):
    R, C = w.shape
    tr = R if R <= 256 else 256
    while R % tr:
        tr //= 2
    n = len(recvs)

    def body(*refs):
        w_ref, m_ref, v_ref, g_ref, d_ref, nm_ref, nv_ref = refs[n:]

        def update(r_ref):
            g = r_ref[0].astype(F32)
            for p in range(1, N_DEV):
                g = g + r_ref[p].astype(F32)
            d, nm, nv = _adamw_math(w_ref[...], g, m_ref[...], v_ref[...])
            g_ref[...] = g
            d_ref[...] = d
            nm_ref[...] = nm
            nv_ref[...] = nv

        if n == 1:
            update(refs[0])
        else:
            x, y, c = _mesh_pos()
            me = 4 * x + 2 * y + c
            for r_ref, (lo, hi) in zip(refs[:n], dests):
                pl.when((me >= lo) & (me < hi))(functools.partial(update, r_ref))

    tile = pl.BlockSpec((tr, C), lambda i: (i, 0))
    shp = jax.ShapeDtypeStruct((R, C), F32)
    return pl.pallas_call(
        body, name=name, grid=(R // tr,),
        in_specs=[pl.BlockSpec((N_DEV, tr, C), lambda i: (0, i, 0))] * n + [tile, tile, tile],
        out_specs=[tile] * 4, out_shape=[shp] * 4,
        compiler_params=_params(("parallel",)),
    )(*recvs, w, m, v)


def _adamw_small(g, w, m, v):
    def body(g_ref, w_ref, m_ref, v_ref, go_ref, d_ref, nm_ref, nv_ref):
        go_ref[...] = g_ref[...]
        for d in range(2):
            p0 = _sig(w_ref[8 + 2 * d:9 + 2 * d, :] - w_ref[9 + 2 * d:10 + 2 * d, :])
            dl0 = g_ref[12 + d:13 + d, :] * p0 * (1.0 - p0)
            go_ref[8 + 2 * d:9 + 2 * d, :] = dl0
            go_ref[9 + 2 * d:10 + 2 * d, :] = -dl0
            go_ref[12 + d:13 + d, :] = jnp.zeros((1, D_MODEL), F32)
        d, nm, nv = _adamw_math(w_ref[...], go_ref[...], m_ref[...], v_ref[...])
        d_ref[...] = d
        nm_ref[...] = nm
        nv_ref[...] = nv

    shp = jax.ShapeDtypeStruct(g.shape, F32)
    vm = pl.BlockSpec(memory_space=pltpu.VMEM)
    return pl.pallas_call(body, name="adamw_small", in_specs=[vm] * 4, out_specs=[vm] * 4, out_shape=[shp] * 4)(g, w, m, v)


def _local_step(x, tgt, g_mix, lb_shard, norm_g, pool_scale, g_ffn, g_final, w_in_shard, late_shards):
    B, L, _ = x.shape
    T = B * L
    rows = D_MODEL // N_DEV
    x2, tgt2 = x.reshape(T, D_MODEL), tgt.reshape(T, D_MODEL)

    proj, u, w_in_g, lb_g = _rms_inproj(x2, g_mix, w_in_shard, lb_shard)
    lb = lb_g[:, :4].transpose(1, 0, 2).reshape(4, D_MODEL)
    proj4 = proj.reshape(N_DEV, B, L, D_MODEL)
    o, ya, (wa, wb, wo, wfi_g, wfo_g, pw_g) = _hgrn_fwd(proj4, lb, norm_g, B, L, late_shards)
    wa, wb, wo = (w_.reshape(D_MODEL, D_MODEL) for w_ in (wa, wb, wo))
    wfo_g = wfo_g.reshape(4, FF_BLOCK, D_MODEL)
    pool_w_full = pw_g.reshape(N_DEV, 4, 32, POOL_DIM).transpose(1, 0, 2, 3).reshape(4, POOL_DIM, POOL_DIM)
    yb = _pool_fwd(proj4, pool_w_full, pool_scale, B, L)
    ya2, yb2 = ya.reshape(T, D_MODEL), yb.reshape(T, D_MODEL)
    za, zb, mg, h = _merge_out(x2, proj, ya2, yb2, wa, wb, wo)
    gate, up, hid, u2, dh2, dh2b, loss_p, dg_final = _ffn_fwd_loss(h, tgt2, g_ffn, g_final, wfi_g, wfo_g)
    loss = jnp.sum(loss_p[:, 0, 0])

    dgu, dh, dhb, dg_ffn = _ffn_bwd(h, dh2, dh2b, gate, up, g_ffn, wfi_g, wfo_g)
    d_wfo = _wgrad(hid, dh2b[None], "wgrad_ffn_out")
    d_wfi, (recv_wfo,) = _wgrad(u2[None], dgu.reshape(N_DEV, T, FF_BLOCK), "wgrad_ffn_in",
                                hosted=[d_wfo.reshape(N_DEV, FF_BLOCK // 2, D_MODEL)])
    dza, dzb, dgab, dya, dyb = _merge_bwd(dhb, proj, za, zb, wa, wb, wo)
    d_wo = _wgrad(mg[None], dhb[None], "wgrad_out")
    d_wa = _wgrad(ya2[None], dza[None], "wgrad_branch_a")
    d_wb = _wgrad(yb2[None], dzb[None], "wgrad_branch_b")
    dp, dpw_p, dps_p = _pool_bwd(proj4, dyb.reshape(B, L, D_MODEL), pool_w_full, pool_scale, B, L)
    d_pw = dpw_p.sum(0).reshape(4, N_DEV, 32, POOL_DIM).transpose(1, 0, 2, 3).reshape(N_DEV, 128, POOL_DIM)
    dp2 = dp.reshape(T, D_MODEL)
    w_in_dests = [(5, 6), (6, 8), (0, 5)]
    d_win_pool = _wgrad(u[None], dp2[None], "wgrad_in_pool")
    d_win_gates = _wgrad(u[None], dgab, "wgrad_in_gates")
    slices = [d_wa.reshape(N_DEV, rows, D_MODEL), d_wb.reshape(N_DEV, rows, D_MODEL),
              d_wo.reshape(N_DEV, rows, D_MODEL), d_wfi, d_pw.astype(MX), d_win_pool, d_win_gates]
    dests = [(0, N_DEV)] * 5 + w_in_dests[:2]
    dproj5, dlb_p, dng_p, recv = _hgrn_bwd(proj4, o, dya.reshape(B, L, D_MODEL), lb, norm_g, B, L, slices, dests)
    dproj5 = dproj5.reshape(5, T, D_MODEL)
    d_win_rec = _wgrad(u[None], dproj5, "wgrad_in_recurrence")
    grad_x, dg_mix, recv_win_rec = _inproj_bwd(x2, dh, dproj5, dp2, dgab, g_mix, w_in_g, d_win_rec, w_in_dests[2])

    small = dict(g_mix=dg_mix.sum(0), hgrn_norm_g=dng_p.sum((0, 1)), pool_scale=dps_p.sum((0, 1)),
                 g_ffn=dg_ffn.sum(0), g_final=dg_final.sum(0), lb=dlb_p.sum(0))
    recv_w_in = [recv[5], recv[6], recv_win_rec]
    recv_late = list(recv[:4]) + [recv_wfo, recv[4]]
    return loss, grad_x.reshape(B, L, D_MODEL), (recv_w_in, w_in_dests), recv_late, small


def kernel(x, g_mix, w_in, lb_logits, hgrn_norm_g, pool_w, pool_scale, w_branch_a, w_branch_b, w_out, g_ffn, w_ffn_in, w_ffn_out, g_final, loss_target, m_g_mix, m_w_in, m_lb_logits, m_hgrn_norm_g, m_pool_w, m_pool_scale, m_w_branch_a, m_w_branch_b, m_w_out, m_g_ffn, m_w_ffn_in, m_w_ffn_out, m_g_final, v_g_mix, v_w_in, v_lb_logits, v_hgrn_norm_g, v_pool_w, v_pool_scale, v_w_branch_a, v_w_branch_b, v_w_out, v_g_ffn, v_w_ffn_in, v_w_ffn_out, v_g_final):
    me = 4 * lax.axis_index("x") + 2 * lax.axis_index("y") + lax.axis_index("c")

    late_shards = [w_branch_a[0].astype(MX), w_branch_b[0].astype(MX), w_out[0].astype(MX),
                   w_ffn_in[0].astype(MX), w_ffn_out[0].astype(MX), pool_w[0].reshape(4 * 32, POOL_DIM).astype(MX)]
    lb_shard = jnp.pad(lb_logits.reshape(4, HEAD_DIM), ((0, 4), (0, 0)))

    loss, grad_x, (recv_w_in, w_in_dests), recv_late, small = _local_step(
        x, loss_target, g_mix, lb_shard, hgrn_norm_g, pool_scale, g_ffn, g_final[None], w_in[0].astype(MX),
        late_shards)
    loss = lax.psum(loss, ("x", "y", "c"))

    packed = jnp.zeros((16, D_MODEL), F32)
    names = ["g_mix", "hgrn_norm_g", "pool_scale", "g_ffn", "g_final"]
    for i, nme in enumerate(names):
        packed = packed.at[i].set(small[nme])
    packed = packed.at[5:7].set(small["lb"])
    red = _all_reduce_small(packed)
    dlb_mine = lax.dynamic_slice_in_dim(red[5:7], me * HEAD_DIM, HEAD_DIM, axis=1)

    sw = jnp.zeros((16, D_MODEL), F32)
    sm = jnp.zeros((16, D_MODEL), F32)
    sv = jnp.ones((16, D_MODEL), F32)
    smalls = [(g_mix, m_g_mix, v_g_mix), (hgrn_norm_g, m_hgrn_norm_g, v_hgrn_norm_g),
              (pool_scale, m_pool_scale, v_pool_scale), (g_ffn, m_g_ffn, v_g_ffn),
              (g_final[None], m_g_final[None], v_g_final[None])]
    for i, (w_, m_, v_) in enumerate(smalls):
        sw, sm, sv = sw.at[i].set(w_[0]), sm.at[i].set(m_[0]), sv.at[i].set(v_[0])
    sg = red.at[5:].set(0.0)
    sg = sg.at[12:14, :HEAD_DIM].set(dlb_mine)
    sw = sw.at[8:12, :HEAD_DIM].set(lb_logits.reshape(4, HEAD_DIM))
    sm = sm.at[8:12, :HEAD_DIM].set(m_lb_logits.reshape(4, HEAD_DIM))
    sv = sv.at[8:12, :HEAD_DIM].set(v_lb_logits.reshape(4, HEAD_DIM))
    sg, sd, snm, snv = _adamw_small(sg, sw, sm, sv)

    def small_out(arr, i, like):
        return arr[i].reshape(like.shape)

    def lb_out(arr):
        return arr[8:12, :HEAD_DIM].reshape(2, 2, HEAD_DIM)

    order = ["w_in", "w_branch_a", "w_branch_b", "w_out", "w_ffn_in", "w_ffn_out", "pool_w"]
    params = dict(w_in=(w_in, m_w_in, v_w_in), w_branch_a=(w_branch_a, m_w_branch_a, v_w_branch_a),
                  w_branch_b=(w_branch_b, m_w_branch_b, v_w_branch_b), w_out=(w_out, m_w_out, v_w_out),
                  w_ffn_in=(w_ffn_in, m_w_ffn_in, v_w_ffn_in), w_ffn_out=(w_ffn_out, m_w_ffn_out, v_w_ffn_out),
                  pool_w=(pool_w, m_pool_w, v_pool_w))
    res = {}
    for nme, r, dests in zip(order, [recv_w_in] + [[r] for r in recv_late], [w_in_dests] + [None] * 6):
        w_, m_, v_ = params[nme]
        shape2 = r[0].shape[1:]
        outs = _adamw_reduce(r, dests, w_.reshape(shape2), m_.reshape(shape2), v_.reshape(shape2), "adamw_" + nme)
        res[nme] = [o_.reshape(w_.shape) for o_ in outs]

    def pick(k):
        small_src = [sg, sd, snm, snv][k]
        return [small_out(small_src, 0, g_mix), res["w_in"][k], lb_out(small_src), small_out(small_src, 1, hgrn_norm_g),
                res["pool_w"][k], small_out(small_src, 2, pool_scale), res["w_branch_a"][k], res["w_branch_b"][k],
                res["w_out"][k], small_out(small_src, 3, g_ffn), res["w_ffn_in"][k], res["w_ffn_out"][k],
                small_out(small_src, 4, g_final)]

    return (loss, grad_x, *pick(0), *pick(1), *pick(2), *pick(3))
```

```python
import functools

import jax
import jax.numpy as jnp
from jax import lax
from jax.experimental import pallas as pl
from jax.experimental.pallas import tpu as pltpu

F32 = jnp.float32
MX = jnp.bfloat16

D_MODEL = 1024
N_HEADS = 8
HEAD_DIM = 128
CHUNK = 16
POOL_WINDOWS = (2, 4, 8, 16)
POOL_DIM = 256
FF_BLOCK = 704
N_DEV = 8
RMS_EPS = 1e-6
ADAM_LR, ADAM_B1, ADAM_B2, ADAM_EPS, ADAM_WD, ADAM_STEP = 0.001, 0.9, 0.999, 1e-08, 0.01, 10
VMEM_LIMIT = 56 * 1024 * 1024
BIG_TOKEN_TILE = 1024
LOCAL_DMA_PRIORITY = 1
ANY = pl.BlockSpec(memory_space=pl.ANY)


def _params(sem=None):
    return pltpu.CompilerParams(dimension_semantics=sem, vmem_limit_bytes=VMEM_LIMIT)


def _dot(a, b):
    return lax.dot_general(a.astype(MX), b.astype(MX), (((1,), (0,)), ((), ())), preferred_element_type=F32)


def _dot_nt(a, b):
    return lax.dot_general(a.astype(MX), b.astype(MX), (((1,), (1,)), ((), ())), preferred_element_type=F32)


def _dot_tn(a, b):
    return lax.dot_general(a.astype(MX), b.astype(MX), (((0,), (0,)), ((), ())), preferred_element_type=F32)


def _sig(x):
    return 1.0 / (1.0 + jnp.exp(-x))


def _fold8(v):
    return v.reshape(v.shape[0] // 8, 8, v.shape[1]).sum(axis=0)


def _gather_order():
    x, y, c = _mesh_pos()
    near, far = [(1 - x, y), (x, 1 - y)], (1 - x, 1 - y)
    return ([(x, y, c), (x, y, 1 - c)] + [(*chip, c) for chip in near] + [(*chip, 1 - c) for chip in near]
            + [(*far, c), (*far, 1 - c)])


def _rms_inproj(x2, g_mix, w_shard, lb_shard):
    T = x2.shape[0]
    tm = min(BIG_TOKEN_TILE, T)
    nT = T // tm
    block_of_step = jnp.stack([4 * px + 2 * py + pc for px, py, pc in _gather_order()]).astype(jnp.int32)

    def body(order_ref, x_ref, g_ref, w_ref, lb_ref, proj_ref, u_ref, wg_ref, lbg_ref,
             u_sc, w_sc, load_sem, send_sems, recv_sems, local_sem, *lb_sems):
        k, i = pl.program_id(0), pl.program_id(1)
        order = _gather_order()
        me, sibling = order[0], order[1]

        def slot(dev):
            return wg_ref.at[4 * dev[0] + 2 * dev[1] + dev[2]]

        def copy(n, block, to, src=None):
            return pltpu.make_async_remote_copy(
                src_ref=slot(block) if src is None else src, dst_ref=slot(block),
                send_sem=send_sems.at[n], recv_sem=recv_sems.at[n], device_id=to, device_id_type=pl.DeviceIdType.MESH)

        mine = pltpu.make_async_copy(w_ref, slot(me), local_sem)
        to_sibling = copy(0, me, sibling, src=w_ref)
        to_near = [copy(1, me, order[2], src=w_ref), copy(2, me, order[3], src=w_ref)]
        to_far = copy(5, me, order[6], src=w_ref)
        passed = {2: copy(3, order[2], sibling), 3: copy(4, order[3], sibling), 6: copy(6, order[6], sibling)}
        lb_gather = _Exchange([lb_ref], [lbg_ref], lb_sems, gather=True)

        @pl.when((k == 0) & (i == 0))
        def _():
            mine.start(LOCAL_DMA_PRIORITY)
            for cp in [to_sibling] + to_near:
                cp.start()
            lb_gather.start()

        for s, block in enumerate(order):
            @pl.when((k == s) & (i == 0))
            def _(s=s, block=block):
                if s == 0:
                    mine.wait()
                else:
                    copy(s - 1, block, me).wait_recv()
                if s in passed:
                    passed[s].start()
                if s == 3:
                    for cp in to_near:
                        cp.wait_send()
                    to_far.start()
                load = pltpu.make_async_copy(slot(block), w_sc, load_sem)
                load.start(LOCAL_DMA_PRIORITY)
                load.wait()

        rows = pl.ds(pl.multiple_of(i * tm, tm), tm)

        @pl.when(k == 0)
        def _():
            x = x_ref[...]
            r = lax.rsqrt(jnp.mean(x * x, axis=-1, keepdims=True) + RMS_EPS)
            u = (x * r * g_ref[...]).astype(MX)
            u_sc[rows, :] = u
            u_ref[...] = u

        proj_ref[...] = jnp.dot(u_sc[rows, :], w_sc[...], preferred_element_type=F32)

        @pl.when((k == N_DEV - 1) & (i == nT - 1))
        def _():
            for cp in [to_sibling, to_far] + list(passed.values()):
                cp.wait_send()
            lb_gather.wait()

    def tile_once(k, i, order_ref):
        return (jnp.where(k == 0, i, nT - 1), 0)

    grid_spec = pltpu.PrefetchScalarGridSpec(
        num_scalar_prefetch=1, grid=(N_DEV, nT),
        in_specs=[pl.BlockSpec((tm, D_MODEL), tile_once),
                  pl.BlockSpec((1, D_MODEL), lambda k, i, order_ref: (0, 0)), ANY, ANY],
        out_specs=[pl.BlockSpec((None, tm, D_MODEL), lambda k, i, order_ref: (order_ref[k], i, 0)),
                   pl.BlockSpec((tm, D_MODEL), tile_once), ANY, ANY],
        scratch_shapes=[pltpu.VMEM((T, D_MODEL), MX), pltpu.VMEM((D_MODEL, D_MODEL), MX),
                        pltpu.SemaphoreType.DMA(()), pltpu.SemaphoreType.DMA((7,)), pltpu.SemaphoreType.DMA((7,)),
                        pltpu.SemaphoreType.DMA(())] + _Exchange.scratch(1))
    return pl.pallas_call(
        body, name="rms_inproj", grid_spec=grid_spec,
        out_shape=[jax.ShapeDtypeStruct((N_DEV, T, D_MODEL), F32), jax.ShapeDtypeStruct((T, D_MODEL), MX),
                   jax.ShapeDtypeStruct((N_DEV, D_MODEL, D_MODEL), MX),
                   jax.ShapeDtypeStruct((N_DEV,) + lb_shard.shape, lb_shard.dtype)],
        compiler_params=_params(("arbitrary", "arbitrary")),
    )(block_of_step, x2, g_mix, w_shard, lb_shard)


HBLK = 128


def _seg_cumsum(x, rev):
    n = x.shape[0]
    pos = lax.broadcasted_iota(jnp.int32, x.shape, 0) & (CHUNK - 1)
    s = 1
    while s < CHUNK:
        if rev:
            x = x + jnp.where(pos < CHUNK - s, pltpu.roll(x, n - s, 0), 0.0)
        else:
            x = x + jnp.where(pos >= s, pltpu.roll(x, s, 0), 0.0)
        s *= 2
    return x


def _block_gates(q_r, f_r, lb_row, rev):
    sq = _sig(q_r)
    q = q_r * sq
    sg = _sig(f_r)
    f = lb_row + (1.0 - lb_row) * sg
    k = 1.0 - f
    lf = jnp.log(f)
    pre = _seg_cumsum(lf, False)
    suf = _seg_cumsum(lf, True)
    tot = pre + suf - lf
    b = suf if rev else pre
    eb = jnp.exp(b)
    enb = jnp.exp(-b)
    eend = jnp.exp(tot - b)
    return dict(sq=sq, sg=sg, f=f, eb=eb, enb=enb, eend=eend, dec=jnp.exp(tot),
                P=q * eb, Kt=k * enb, Ke=k * eend)


def _block_mask(rev, transposed=False):
    ri = lax.broadcasted_iota(jnp.int32, (HBLK, HBLK), 0)
    ci = lax.broadcasted_iota(jnp.int32, (HBLK, HBLK), 1)
    same = (ri // CHUNK) == (ci // CHUNK)
    return same & ((ci >= ri) if rev != transposed else (ci <= ri))


def _chunk_rows(c, rows=CHUNK):
    return pl.ds(c * CHUNK, rows)


def _block_rows(kb):
    return pl.ds(kb * HBLK, HBLK)


def _chunk_outer_products(a, b, out_sc, kb):
    a, b = a.astype(MX), b.astype(MX)
    for u in range(HBLK // CHUNK):
        r = slice(u * CHUNK, (u + 1) * CHUNK)
        out_sc[kb * (HBLK // CHUNK) + u] = _dot_tn(a[r, :], b[r, :])


UNIT_BLOCKS_FWD = 2
UNIT_BLOCKS_BWD = 1


def _unit_blocks(u, fn, carry, unit_blocks):
    for b in range(unit_blocks):
        carry = fn(u * unit_blocks + b, carry)
    return carry


def _unit_chunks(u, descending, fn, carry, unit_blocks):
    unit_chunks = unit_blocks * (HBLK // CHUNK)
    for j in range(unit_chunks):
        carry = fn(u * unit_chunks + (unit_chunks - 1 - j if descending else j), carry)
    return carry


def _pipelined(n_units, descending, first, c1, second, c2):
    units = list(range(n_units))[::-1] if descending else list(range(n_units))
    c1 = first(units[0], c1)
    for t, u in enumerate(units):
        if t + 1 < n_units:
            c1 = first(units[t + 1], c1)
        c2 = second(u, c2)
    return c1, c2


def _lower_bounds(lbl_ref):
    return _sig(lbl_ref[0:1, :] - lbl_ref[1:2, :]), _sig(lbl_ref[2:3, :] - lbl_ref[3:4, :])


def _hgrn_fwd(proj, lbl, norm_g, B, L, shards):
    nC = L // CHUNK
    n = len(shards)
    relay_step = (11 * B * N_HEADS) // 16

    def body(*refs):
        q_ref, ff_ref, fb_ref, i_ref, og_ref, lbl_ref, ng_ref = refs[:7]
        o_ref, ya_ref = refs[7 + n:9 + n]
        o_sc, p_sc, dec_sc, upd_sc = refs[9 + 2 * n:13 + 2 * n]
        step_id = pl.program_id(0) * N_HEADS + pl.program_id(1)
        gather = _TwoLevelGather(refs[7:7 + n], refs[9 + n:9 + 2 * n], refs[13 + 2 * n:])

        @pl.when(step_id == 0)
        def _():
            gather.start()

        @pl.when(step_id == relay_step)
        def _():
            gather.relay()

        lb_f, lb_b = _lower_bounds(lbl_ref)
        o_sc[...] = jnp.zeros_like(o_sc)

        def run_dir(f_ref, lb_row, rev):
            def block(kb, carry):
                rows = _block_rows(kb)
                g = _block_gates(q_ref[rows, :], f_ref[rows, :], lb_row, rev)
                v = i_ref[rows, :]
                a = jnp.where(_block_mask(rev), _dot_nt(g["P"], g["Kt"]), 0.0)
                o_sc[rows, :] += _dot(a, v)
                p_sc[rows, :] = g["P"].astype(MX)
                dec_sc[rows, :] = g["dec"]
                _chunk_outer_products(v, g["Ke"], upd_sc, kb)
                return carry

            def step(c, st):
                rows = _chunk_rows(c)
                o_sc[rows, :] += _dot_nt(p_sc[rows, :], st)
                dec = dec_sc[_chunk_rows(c, 1), :]
                return st * dec + upd_sc[c]

            ub = UNIT_BLOCKS_FWD
            _pipelined(L // (HBLK * ub), rev,
                       lambda u, c: _unit_blocks(u, block, c, ub), 0,
                       lambda u, st: _unit_chunks(u, rev, step, st, ub), jnp.zeros((HEAD_DIM, HEAD_DIM), F32))

        run_dir(ff_ref, lb_f, False)
        run_dir(fb_ref, lb_b, True)
        o = o_sc[...]
        o_ref[...] = o
        on = o * lax.rsqrt(jnp.mean(o * o, axis=-1, keepdims=True) + RMS_EPS)
        og = og_ref[...]
        ya_ref[...] = ((on * ng_ref[...]) * (og * _sig(og))).astype(MX)

        @pl.when(step_id == B * N_HEADS - 1)
        def _():
            gather.finish()

    def blk(s):
        return pl.BlockSpec((None, None, L, HEAD_DIM), lambda b, h, s=s: (s, b, 0, h))

    out_blk = pl.BlockSpec((None, L, HEAD_DIM), lambda b, h: (b, 0, h))
    outs = pl.pallas_call(
        body, name="hgrn_fwd", grid=(B, N_HEADS),
        in_specs=[blk(0), blk(1), blk(2), blk(3), blk(4),
                  pl.BlockSpec((4, HEAD_DIM), lambda b, h: (0, h)),
                  pl.BlockSpec((1, HEAD_DIM), lambda b, h: (0, h))] + [ANY] * n,
        out_specs=[out_blk, out_blk] + [ANY] * n,
        out_shape=[jax.ShapeDtypeStruct((B, L, D_MODEL), F32), jax.ShapeDtypeStruct((B, L, D_MODEL), MX)]
                  + _Exchange.out_shapes(shards, True),
        scratch_shapes=[pltpu.VMEM((L, HEAD_DIM), F32), pltpu.VMEM((L, HEAD_DIM), MX),
                        pltpu.VMEM((L, HEAD_DIM), F32), pltpu.VMEM((nC, HEAD_DIM, HEAD_DIM), F32)]
                       + _Exchange.scratch(n),
        compiler_params=_params(("arbitrary", "arbitrary")),
    )(proj, proj, proj, proj, proj, lbl, norm_g, *shards)
    return outs[0], outs[1], outs[2:]


POOL_PAD = 8


def _pool_window(p, ext_sc, half, adjoint):
    L = p.shape[0]
    n = L + 2 * POOL_PAD
    ext_sc[0:POOL_PAD, :] = jnp.zeros((POOL_PAD, p.shape[1]), F32)
    ext_sc[POOL_PAD + L:n, :] = jnp.zeros((POOL_PAD, p.shape[1]), F32)
    ext_sc[POOL_PAD:POOL_PAD + L, :] = p
    x = ext_sc[...]
    s = x + pltpu.roll(x, 1 if adjoint else n - 1, 0)
    w = 1
    while w < half:
        s = pltpu.roll(s, w, 0) + pltpu.roll(s, n - w, 0)
        w *= 2
    ext_sc[...] = s
    return ext_sc[POOL_PAD:POOL_PAD + L, :]


def _pool_count(L, half):
    t = lax.broadcasted_iota(jnp.int32, (L, 1), 0)
    lo = jnp.clip(t - half + 1, 0, L)
    hi = jnp.clip(t + half + 1, 0, L)
    return (hi - lo).astype(F32)


def _pool_fwd(proj, pool_w_full, pool_scale, B, L):
    def body(p_ref, w_ref, s_ref, yb_ref, ext_sc):
        for g, win in enumerate(POOL_WINDOWS):
            cols = slice(g * POOL_DIM, (g + 1) * POOL_DIM)
            p = p_ref[:, cols]
            y = _pool_window(p, ext_sc, win // 2, False) / _pool_count(L, win // 2) - p
            yb_ref[:, cols] = (_dot(y, w_ref[g]) * s_ref[:, cols]).astype(MX)

    return pl.pallas_call(
        body, name="pool_fwd", grid=(B,),
        in_specs=[pl.BlockSpec((None, None, L, D_MODEL), lambda b: (5, b, 0, 0)),
                  pl.BlockSpec((4, POOL_DIM, POOL_DIM), lambda b: (0, 0, 0)),
                  pl.BlockSpec((1, D_MODEL), lambda b: (0, 0))],
        out_specs=pl.BlockSpec((None, L, D_MODEL), lambda b: (b, 0, 0)),
        out_shape=jax.ShapeDtypeStruct((B, L, D_MODEL), MX),
        scratch_shapes=[pltpu.VMEM((L + 2 * POOL_PAD, POOL_DIM), F32)],
        compiler_params=_params(("parallel",)),
    )(proj, pool_w_full, pool_scale)


def _merge_out(x2, proj, ya, yb, wa, wb, wo):
    T = x2.shape[0]
    tm = min(512, T)

    def body(x_ref, ga_ref, gb_ref, ya_ref, yb_ref, wa_ref, wb_ref, wo_ref, za_ref, zb_ref, mg_ref, h_ref):
        za = jnp.dot(ya_ref[...], wa_ref[...], preferred_element_type=F32)
        zb = jnp.dot(yb_ref[...], wb_ref[...], preferred_element_type=F32)
        mg = (_sig(ga_ref[...]) * za + _sig(gb_ref[...]) * zb).astype(MX)
        za_ref[...] = za
        zb_ref[...] = zb
        mg_ref[...] = mg
        h_ref[...] = x_ref[...] + jnp.dot(mg, wo_ref[...], preferred_element_type=F32)

    tile = pl.BlockSpec((tm, D_MODEL), lambda i: (i, 0))
    wspec = pl.BlockSpec((D_MODEL, D_MODEL), lambda i: (0, 0))
    return pl.pallas_call(
        body, name="merge_out", grid=(T // tm,),
        in_specs=[tile,
                  pl.BlockSpec((None, tm, D_MODEL), lambda i: (6, i, 0)),
                  pl.BlockSpec((None, tm, D_MODEL), lambda i: (7, i, 0)),
                  tile, tile, wspec, wspec, wspec],
        out_specs=[tile, tile, tile, tile],
        out_shape=[jax.ShapeDtypeStruct((T, D_MODEL), F32), jax.ShapeDtypeStruct((T, D_MODEL), F32),
                   jax.ShapeDtypeStruct((T, D_MODEL), MX), jax.ShapeDtypeStruct((T, D_MODEL), F32)],
        compiler_params=_params(("parallel",)),
    )(x2, proj, proj, ya, yb, wa, wb, wo)


def _ffn_fwd_loss(h, tgt, g_ffn, g_final, wfi_g, wfo_g):
    T = h.shape[0]
    tm = min(512, T)
    nT = T // tm

    def body(h_ref, t_ref, gf_ref, gl_ref, wg_ref, wu_ref, wo_ref,
             gate_ref, up_ref, hid_ref, u2_ref, dh2_ref, dh2b_ref, loss_ref, dgl_ref, u2_sc, acc_sc):
        i, j = pl.program_id(0), pl.program_id(1)

        @pl.when(j == 0)
        def _():
            hh = h_ref[...]
            r = lax.rsqrt(jnp.mean(hh * hh, axis=-1, keepdims=True) + RMS_EPS)
            u2 = (hh * r * gf_ref[...]).astype(MX)
            u2_sc[...] = u2
            u2_ref[...] = u2
            acc_sc[...] = jnp.zeros_like(acc_sc)

        @pl.when((i == 0) & (j == 0))
        def _():
            dgl_ref[...] = jnp.zeros_like(dgl_ref)

        gate = jnp.dot(u2_sc[...], wg_ref[...], preferred_element_type=F32)
        up = jnp.dot(u2_sc[...], wu_ref[...], preferred_element_type=F32)
        hid = ((gate * _sig(gate)) * up).astype(MX)
        gate_ref[...] = gate
        up_ref[...] = up
        hid_ref[...] = hid
        acc_sc[...] += jnp.dot(hid, wo_ref[...], preferred_element_type=F32)

        @pl.when(j == 3)
        def _():
            h2 = h_ref[...] + acc_sc[...]
            r = lax.rsqrt(jnp.mean(h2 * h2, axis=-1, keepdims=True) + RMS_EPS)
            hn = h2 * r
            gl = gl_ref[...]
            err = hn * gl - t_ref[...]
            tok = jnp.mean(err * err, axis=-1, keepdims=True)
            loss_ref[...] = jnp.full(loss_ref.shape, 0.5 * jnp.sum(tok), F32)
            dy = err * (1.0 / D_MODEL)
            dgl_ref[...] += _fold8(dy * hn)
            a = dy * gl
            dh2 = r * a - hn * (r * jnp.mean(a * hn, axis=-1, keepdims=True))
            dh2_ref[...] = dh2
            dh2b_ref[...] = dh2.astype(MX)

    tile = pl.BlockSpec((tm, D_MODEL), lambda i, j: (i, 0))
    vec = pl.BlockSpec((1, D_MODEL), lambda i, j: (0, 0))
    ftile = pl.BlockSpec((None, tm, FF_BLOCK), lambda i, j: (j, i, 0))
    return pl.pallas_call(
        body, name="ffn_fwd_loss", grid=(nT, 4),
        in_specs=[tile, tile, vec, vec,
                  pl.BlockSpec((None, D_MODEL, FF_BLOCK), lambda i, j: (j, 0, 0)),
                  pl.BlockSpec((None, D_MODEL, FF_BLOCK), lambda i, j: (j + 4, 0, 0)),
                  pl.BlockSpec((None, FF_BLOCK, D_MODEL), lambda i, j: (j, 0, 0))],
        out_specs=[ftile, ftile, ftile, tile, tile, tile,
                   pl.BlockSpec((None, 8, 128), lambda i, j: (i, 0, 0)),
                   pl.BlockSpec((8, D_MODEL), lambda i, j: (0, 0))],
        out_shape=[jax.ShapeDtypeStruct((4, T, FF_BLOCK), F32), jax.ShapeDtypeStruct((4, T, FF_BLOCK), F32),
                   jax.ShapeDtypeStruct((4, T, FF_BLOCK), MX), jax.ShapeDtypeStruct((T, D_MODEL), MX),
                   jax.ShapeDtypeStruct((T, D_MODEL), F32), jax.ShapeDtypeStruct((T, D_MODEL), MX),
                   jax.ShapeDtypeStruct((nT, 8, 128), F32), jax.ShapeDtypeStruct((8, D_MODEL), F32)],
        scratch_shapes=[pltpu.VMEM((tm, D_MODEL), MX), pltpu.VMEM((tm, D_MODEL), F32)],
        compiler_params=_params(("arbitrary", "arbitrary")),
    )(h, tgt, g_ffn, g_final, wfi_g, wfi_g, wfo_g)


def _ffn_bwd(h, dh2, dh2b, gate, up, g_ffn, wfi_g, wfo_g):
    T = h.shape[0]
    tm = min(512, T)

    def body(h_ref, dh2_ref, dh2b_ref, gate_ref, up_ref, gf_ref, wg_ref, wu_ref, wo_ref,
             dgu_ref, dh_ref, dhb_ref, dgf_ref, acc_sc):
        i, j = pl.program_id(0), pl.program_id(1)

        @pl.when(j == 0)
        def _():
            acc_sc[...] = jnp.zeros_like(acc_sc)

        @pl.when((i == 0) & (j == 0))
        def _():
            dgf_ref[...] = jnp.zeros_like(dgf_ref)

        dhid = _dot_nt(dh2b_ref[...], wo_ref[...])
        gate, up = gate_ref[...], up_ref[...]
        sg = _sig(gate)
        dgate = (dhid * up * (sg * (1.0 + gate * (1.0 - sg)))).astype(MX)
        dup = (dhid * (gate * sg)).astype(MX)
        dgu_ref[0] = dgate
        dgu_ref[1] = dup
        acc_sc[...] += _dot_nt(dgate, wg_ref[...]) + _dot_nt(dup, wu_ref[...])

        @pl.when(j == 3)
        def _():
            hh = h_ref[...]
            r = lax.rsqrt(jnp.mean(hh * hh, axis=-1, keepdims=True) + RMS_EPS)
            hn = hh * r
            du2 = acc_sc[...]
            dgf_ref[...] += _fold8(du2 * hn)
            a = du2 * gf_ref[...]
            dh = dh2_ref[...] + r * a - hn * (r * jnp.mean(a * hn, axis=-1, keepdims=True))
            dh_ref[...] = dh
            dhb_ref[...] = dh.astype(MX)

    tile = pl.BlockSpec((tm, D_MODEL), lambda i, j: (i, 0))
    ftile = pl.BlockSpec((None, tm, FF_BLOCK), lambda i, j: (j, i, 0))
    return pl.pallas_call(
        body, name="ffn_bwd", grid=(T // tm, 4),
        in_specs=[tile, tile, tile, ftile, ftile,
                  pl.BlockSpec((1, D_MODEL), lambda i, j: (0, 0)),
                  pl.BlockSpec((None, D_MODEL, FF_BLOCK), lambda i, j: (j, 0, 0)),
                  pl.BlockSpec((None, D_MODEL, FF_BLOCK), lambda i, j: (j + 4, 0, 0)),
                  pl.BlockSpec((None, FF_BLOCK, D_MODEL), lambda i, j: (j, 0, 0))],
        out_specs=[pl.BlockSpec((2, None, tm, FF_BLOCK), lambda i, j: (0, j, i, 0)),
                   tile, tile, pl.BlockSpec((8, D_MODEL), lambda i, j: (0, 0))],
        out_shape=[jax.ShapeDtypeStruct((2, 4, T, FF_BLOCK), MX),
                   jax.ShapeDtypeStruct((T, D_MODEL), F32), jax.ShapeDtypeStruct((T, D_MODEL), MX),
                   jax.ShapeDtypeStruct((8, D_MODEL), F32)],
        scratch_shapes=[pltpu.VMEM((tm, D_MODEL), F32)],
        compiler_params=_params(("arbitrary", "arbitrary")),
    )(h, dh2, dh2b, gate, up, g_ffn, wfi_g, wfi_g, wfo_g)


def _merge_bwd(dhb, proj, za, zb, wa, wb, wo):
    T = dhb.shape[0]
    tm = min(512, T)

    def body(dh_ref, ga_ref, gb_ref, za_ref, zb_ref, wa_ref, wb_ref, wo_ref,
             dza_ref, dzb_ref, dgab_ref, dya_ref, dyb_ref):
        dm = _dot_nt(dh_ref[...], wo_ref[...])
        sa, sb = _sig(ga_ref[...]), _sig(gb_ref[...])
        dza = (dm * sa).astype(MX)
        dzb = (dm * sb).astype(MX)
        dza_ref[...] = dza
        dzb_ref[...] = dzb
        dgab_ref[0] = (dm * za_ref[...] * (sa * (1.0 - sa))).astype(MX)
        dgab_ref[1] = (dm * zb_ref[...] * (sb * (1.0 - sb))).astype(MX)
        dya_ref[...] = _dot_nt(dza, wa_ref[...])
        dyb_ref[...] = _dot_nt(dzb, wb_ref[...])

    tile = pl.BlockSpec((tm, D_MODEL), lambda i: (i, 0))
    wspec = pl.BlockSpec((D_MODEL, D_MODEL), lambda i: (0, 0))
    return pl.pallas_call(
        body, name="merge_bwd", grid=(T // tm,),
        in_specs=[tile,
                  pl.BlockSpec((None, tm, D_MODEL), lambda i: (6, i, 0)),
                  pl.BlockSpec((None, tm, D_MODEL), lambda i: (7, i, 0)),
                  tile, tile, wspec, wspec, wspec],
        out_specs=[tile, tile, pl.BlockSpec((2, tm, D_MODEL), lambda i: (0, i, 0)), tile, tile],
        out_shape=[jax.ShapeDtypeStruct((T, D_MODEL), MX), jax.ShapeDtypeStruct((T, D_MODEL), MX),
                   jax.ShapeDtypeStruct((2, T, D_MODEL), MX),
                   jax.ShapeDtypeStruct((T, D_MODEL), F32), jax.ShapeDtypeStruct((T, D_MODEL), F32)],
        compiler_params=_params(("parallel",)),
    )(dhb, proj, proj, za, zb, wa, wb, wo)


def _pool_bwd(proj, dyb, pool_w_full, pool_scale, B, L):
    def body(p_ref, dy_ref, w_ref, s_ref, dp_ref, dw_ref, ds_ref, ext_sc):
        for g, win in enumerate(POOL_WINDOWS):
            cols = slice(g * POOL_DIM, (g + 1) * POOL_DIM)
            p = p_ref[:, cols]
            cnt = _pool_count(L, win // 2)
            y = _pool_window(p, ext_sc, win // 2, False) / cnt - p
            z = _dot(y, w_ref[g])
            dyb_g = dy_ref[:, cols]
            ds_ref[:, cols] = jnp.sum(dyb_g * z, axis=0, keepdims=True)
            dz = dyb_g * s_ref[:, cols]
            dw_ref[g] = _dot_tn(y, dz)
            dy = _dot_nt(dz, w_ref[g])
            dp_ref[:, cols] = (_pool_window(dy / cnt, ext_sc, win // 2, True) - dy).astype(MX)

    seq = pl.BlockSpec((None, L, D_MODEL), lambda b: (b, 0, 0))
    return pl.pallas_call(
        body, name="pool_bwd", grid=(B,),
        in_specs=[pl.BlockSpec((None, None, L, D_MODEL), lambda b: (5, b, 0, 0)), seq,
                  pl.BlockSpec((4, POOL_DIM, POOL_DIM), lambda b: (0, 0, 0)),
                  pl.BlockSpec((1, D_MODEL), lambda b: (0, 0))],
        out_specs=[seq, pl.BlockSpec((None, 4, POOL_DIM, POOL_DIM), lambda b: (b, 0, 0, 0)),
                   pl.BlockSpec((None, 1, D_MODEL), lambda b: (b, 0, 0))],
        out_shape=[jax.ShapeDtypeStruct((B, L, D_MODEL), MX),
                   jax.ShapeDtypeStruct((B, 4, POOL_DIM, POOL_DIM), F32),
                   jax.ShapeDtypeStruct((B, 1, D_MODEL), F32)],
        scratch_shapes=[pltpu.VMEM((L + 2 * POOL_PAD, POOL_DIM), F32)],
        compiler_params=_params(("parallel",)),
    )(proj, dyb, pool_w_full, pool_scale)


def _hgrn_bwd(proj, o, dya, lbl, norm_g, B, L, grads, dests):
    nC = L // CHUNK
    n = len(grads)

    def body(*refs):
        q_ref, ff_ref, fb_ref, i_ref, og_ref, o_ref, dy_ref, lbl_ref, ng_ref = refs[:9]
        dp_ref, dlb_ref, dng_ref = refs[9 + n:12 + n]
        (do_sc, dq_sc, dv_sc, dP_sc, dKt_sc, dKe_sc, dbl_sc, dec_sc, ke_sc, ck_sc, upd_sc,
         dupd_sc) = refs[12 + 2 * n:24 + 2 * n]
        step_id = pl.program_id(0) * N_HEADS + pl.program_id(1)
        scatter = _Exchange(refs[9:9 + n], refs[12 + n:12 + 2 * n], refs[24 + 2 * n:], gather=False, dests=dests)

        @pl.when(step_id == 0)
        def _():
            scatter.start()

        lb_f, lb_b = _lower_bounds(lbl_ref)
        o_ = o_ref[...]
        r = lax.rsqrt(jnp.mean(o_ * o_, axis=-1, keepdims=True) + RMS_EPS)
        on = o_ * r
        og = og_ref[...]
        sog = _sig(og)
        dy = dy_ref[...]
        ng = ng_ref[...]
        dp_ref[4] = (dy * (on * ng) * (sog * (1.0 + og * (1.0 - sog)))).astype(MX)
        dn = dy * (og * sog)
        dng_ref[...] = jnp.sum(dn * on, axis=0, keepdims=True)
        don = dn * ng
        do_sc[...] = r * don - on * (r * jnp.mean(don * on, axis=-1, keepdims=True))

        dq_sc[...] = jnp.zeros_like(dq_sc)
        dv_sc[...] = jnp.zeros_like(dv_sc)

        def run_dir(f_ref, lb_row, rev, slot):
            def block(kb, carry):
                rows = _block_rows(kb)
                g = _block_gates(q_ref[rows, :], f_ref[rows, :], lb_row, rev)
                v = i_ref[rows, :]
                do = do_sc[rows, :]
                m, mt = _block_mask(rev), _block_mask(rev, True)
                at = jnp.where(mt, _dot_nt(g["Kt"], g["P"]), 0.0)
                da = jnp.where(m, _dot_nt(do, v), 0.0)
                dat = jnp.where(mt, _dot_nt(v, do), 0.0)
                dv_sc[rows, :] += _dot(at, do)
                dP_sc[rows, :] = _dot(da, g["Kt"])
                dKt_sc[rows, :] = _dot(dat, g["P"])
                ke_sc[rows, :] = g["Ke"].astype(MX)
                dec_sc[rows, :] = g["dec"]
                _chunk_outer_products(v, g["Ke"], upd_sc, kb)
                _chunk_outer_products(do, g["P"], dupd_sc, kb)
                return carry

            def fstep(c, st):
                rows = _chunk_rows(c)
                ck_sc[c] = st.astype(MX)
                dP_sc[rows, :] += _dot(do_sc[rows, :], st)
                dec = dec_sc[_chunk_rows(c, 1), :]
                return st * dec + upd_sc[c]

            ub = UNIT_BLOCKS_BWD
            n_units = L // (HBLK * ub)
            zero_state = jnp.zeros((HEAD_DIM, HEAD_DIM), F32)
            _pipelined(n_units, rev,
                       lambda u, c: _unit_blocks(u, block, c, ub), 0,
                       lambda u, st: _unit_chunks(u, rev, fstep, st, ub), zero_state)

            def bstep(c, dst):
                rows = _chunk_rows(c)
                dec = dec_sc[_chunk_rows(c, 1), :]
                dKe_sc[rows, :] = _dot(i_ref[rows, :], dst)
                dv_sc[rows, :] += _dot_nt(ke_sc[rows, :], dst)
                dbl = dec * jnp.sum(dst * ck_sc[c].astype(F32), axis=0, keepdims=True)
                dbl_sc[rows, :] = jnp.broadcast_to(dbl, (CHUNK, HEAD_DIM))
                return dst * dec + dupd_sc[c]

            def finish(kb, acc):
                rows = _block_rows(kb)
                q_r = q_ref[rows, :]
                g = _block_gates(q_r, f_ref[rows, :], lb_row, rev)
                dP, dkt, dke = dP_sc[rows, :], dKt_sc[rows, :], dKe_sc[rows, :]
                e = dke * g["Ke"]
                dlf = (_seg_cumsum(dP * g["P"] - dkt * g["Kt"], not rev) + _seg_cumsum(e, rev) - e
                       + dbl_sc[rows, :])
                df = dlf / g["f"] - (dkt * g["enb"] + dke * g["eend"])
                dp_ref[slot, rows, :] = (df * (1.0 - lb_row) * (g["sg"] * (1.0 - g["sg"]))).astype(MX)
                dq_sc[rows, :] += (dP * g["eb"]) * (g["sq"] * (1.0 + q_r * (1.0 - g["sq"])))
                return acc + jnp.sum(df * (1.0 - g["sg"]), axis=0, keepdims=True)

            _, dlb = _pipelined(n_units, not rev,
                                lambda u, dst: _unit_chunks(u, not rev, bstep, dst, ub), zero_state,
                                lambda u, acc: _unit_blocks(u, finish, acc, ub), jnp.zeros((1, HEAD_DIM), F32))
            dlb_ref[slot - 1:slot, :] = dlb

        run_dir(ff_ref, lb_f, False, 1)
        run_dir(fb_ref, lb_b, True, 2)
        dp_ref[0] = dq_sc[...].astype(MX)
        dp_ref[3] = dv_sc[...].astype(MX)

        @pl.when(step_id == B * N_HEADS - 1)
        def _():
            scatter.wait()

    def blk(s):
        return pl.BlockSpec((None, None, L, HEAD_DIM), lambda b, h, s=s: (s, b, 0, h))

    seq = pl.BlockSpec((None, L, HEAD_DIM), lambda b, h: (b, 0, h))
    outs = pl.pallas_call(
        body, name="hgrn_bwd", grid=(B, N_HEADS),
        in_specs=[blk(0), blk(1), blk(2), blk(3), blk(4), seq, seq,
                  pl.BlockSpec((4, HEAD_DIM), lambda b, h: (0, h)),
                  pl.BlockSpec((1, HEAD_DIM), lambda b, h: (0, h))] + [ANY] * n,
        out_specs=[pl.BlockSpec((5, None, L, HEAD_DIM), lambda b, h: (0, b, 0, h)),
                   pl.BlockSpec((None, 2, HEAD_DIM), lambda b, h: (b, 0, h)),
                   pl.BlockSpec((None, 1, HEAD_DIM), lambda b, h: (b, 0, h))] + [ANY] * n,
        out_shape=[jax.ShapeDtypeStruct((5, B, L, D_MODEL), MX),
                   jax.ShapeDtypeStruct((B, 2, D_MODEL), F32),
                   jax.ShapeDtypeStruct((B, 1, D_MODEL), F32)] + _Exchange.out_shapes(grads, False),
        scratch_shapes=[pltpu.VMEM((L, HEAD_DIM), F32)] * 8
                       + [pltpu.VMEM((L, HEAD_DIM), MX), pltpu.VMEM((nC, HEAD_DIM, HEAD_DIM), MX),
                          pltpu.VMEM((nC, HEAD_DIM, HEAD_DIM), F32), pltpu.VMEM((nC, HEAD_DIM, HEAD_DIM), F32)]
                       + _Exchange.scratch(n),
        compiler_params=_params(("arbitrary", "arbitrary")),
    )(proj, proj, proj, proj, proj, o, dya, lbl, norm_g, *grads)
    return outs[0], outs[1], outs[2], outs[3:]


def _dproj_select(s, a5_ref, p_ref, g2_ref):
    return jnp.where(s < 5, a5_ref[...], jnp.where(s == 5, p_ref[...], g2_ref[...]))


def _dproj_specs(tm):
    a5 = pl.BlockSpec((None, tm, D_MODEL), lambda i, s: (jnp.minimum(s, 4), i, 0))
    p = pl.BlockSpec((tm, D_MODEL), lambda i, s: (i, 0))
    g2 = pl.BlockSpec((None, tm, D_MODEL), lambda i, s: (jnp.clip(s - 6, 0, 1), i, 0))
    return [a5, p, g2]


def _inproj_bwd(x2, dh, dproj5, dp, dgab, g_mix, w_in_g, d_win, dests):
    T = x2.shape[0]
    tm = min(512, T)
    nT = T // tm

    def body(a5_ref, p_ref, g2_ref, w_ref, x_ref, dh_ref, g_ref, dwin_ref, dx_ref, dg_ref, recv_ref, acc_sc, *sems):
        i, s = pl.program_id(0), pl.program_id(1)
        scatter = _Exchange([dwin_ref], [recv_ref], sems, gather=False, dests=[dests])

        @pl.when((i == 0) & (s == 0))
        def _():
            scatter.start()

        @pl.when(s == 0)
        def _():
            acc_sc[...] = jnp.zeros_like(acc_sc)

        @pl.when((i == 0) & (s == 0))
        def _():
            dg_ref[...] = jnp.zeros_like(dg_ref)

        acc_sc[...] += _dot_nt(_dproj_select(s, a5_ref, p_ref, g2_ref), w_ref[...])

        @pl.when(s == N_DEV - 1)
        def _():
            x = x_ref[...]
            r = lax.rsqrt(jnp.mean(x * x, axis=-1, keepdims=True) + RMS_EPS)
            xn = x * r
            du = acc_sc[...]
            dg_ref[...] += _fold8(du * xn)
            a = du * g_ref[...]
            dx_ref[...] = dh_ref[...] + r * a - xn * (r * jnp.mean(a * xn, axis=-1, keepdims=True))

        @pl.when((i == nT - 1) & (s == N_DEV - 1))
        def _():
            scatter.wait()

    tile = pl.BlockSpec((tm, D_MODEL), lambda i, s: (i, 0))
    return pl.pallas_call(
        body, name="inproj_bwd", grid=(nT, N_DEV),
        in_specs=_dproj_specs(tm) + [pl.BlockSpec((None, D_MODEL, D_MODEL), lambda i, s: (s, 0, 0)),
                                        tile, tile, pl.BlockSpec((1, D_MODEL), lambda i, s: (0, 0)), ANY],
        out_specs=[tile, pl.BlockSpec((8, D_MODEL), lambda i, s: (0, 0)), ANY],
        out_shape=[jax.ShapeDtypeStruct((T, D_MODEL), F32), jax.ShapeDtypeStruct((8, D_MODEL), F32)]
                  + _Exchange.out_shapes([d_win], False),
        scratch_shapes=[pltpu.VMEM((tm, D_MODEL), F32)] + _Exchange.scratch(1),
        compiler_params=_params(("arbitrary", "arbitrary")),
    )(dproj5, dp, dgab, w_in_g, x2, dh, g_mix, d_win)


def _wgrad(a, g, name, hosted=()):
    Ba, T, K = a.shape
    Bg, _, Nn = g.shape
    nb = max(Ba, Bg)
    tm = min(BIG_TOKEN_TILE, T)
    nt = T // tm
    n = len(hosted)

    def body(*refs):
        a_ref, g_ref = refs[:2]
        out_ref = refs[2 + n]
        acc_sc = refs[3 + 2 * n]
        s, t = pl.program_id(0), pl.program_id(1)
        if n:
            scatter = _Exchange(refs[2:2 + n], refs[3 + n:3 + 2 * n], refs[4 + 2 * n:], gather=False)
            pl.when((s == 0) & (t == 0))(scatter.start)

        @pl.when(t == 0)
        def _():
            acc_sc[...] = jnp.zeros_like(acc_sc)

        acc_sc[...] += _dot_tn(a_ref[...], g_ref[...])

        @pl.when(t == nt - 1)
        def _():
            out_ref[...] = acc_sc[...].astype(MX)

        if n:
            pl.when((s == nb - 1) & (t == nt - 1))(scatter.wait)

    outs = pl.pallas_call(
        body, name=name, grid=(nb, nt),
        in_specs=[pl.BlockSpec((None, tm, K), lambda s, t: (s if Ba > 1 else 0, t, 0)),
                  pl.BlockSpec((None, tm, Nn), lambda s, t: (s if Bg > 1 else 0, t, 0))] + [ANY] * n,
        out_specs=[pl.BlockSpec((None, K, Nn), lambda s, t: (s, 0, 0))] + [ANY] * n,
        out_shape=[jax.ShapeDtypeStruct((nb, K, Nn), MX)] + _Exchange.out_shapes(hosted, False),
        scratch_shapes=[pltpu.VMEM((K, Nn), F32)] + (_Exchange.scratch(n) if n else []),
        compiler_params=_params(("arbitrary", "arbitrary") if n else ("parallel", "arbitrary")),
    )(a, g, *hosted)
    return (outs[0], outs[1:]) if n else outs[0]


def _mesh_pos():
    return lax.axis_index("x"), lax.axis_index("y"), lax.axis_index("c")


def _device_of(p):
    return (p // 4, (p // 2) % 2, p % 2)


class _Exchange:
    def __init__(self, srcs, outs, sems, gather, dests=None):
        send_sems, recv_sems, local_sems = sems
        x, y, c = _mesh_pos()
        me = 4 * x + 2 * y + c
        self.sends, self.arrivals, self.mine = [], [], []
        for a, (src, out) in enumerate(zip(srcs, outs)):
            lo, hi = dests[a] if dests else (0, N_DEV)

            def piece(p, src=src, lo=lo, hi=hi):
                return src if gather else src.at[jnp.clip(p - lo, 0, hi - lo - 1)]

            def served(p, lo=lo, hi=hi):
                return None if (lo, hi) == (0, N_DEV) else (p >= lo) & (p < hi)

            self.mine.append((pltpu.make_async_copy(piece(me), out.at[me], local_sems.at[a]), served(me)))
            for j in range(1, N_DEV):
                to, frm = (me + j) % N_DEV, (me + N_DEV - j) % N_DEV
                pair = dict(send_sem=send_sems.at[7 * a + j - 1], recv_sem=recv_sems.at[7 * a + j - 1],
                            device_id_type=pl.DeviceIdType.MESH)
                self.sends.append((pltpu.make_async_remote_copy(
                    src_ref=piece(to), dst_ref=out.at[me], device_id=_device_of(to), **pair), served(to)))
                self.arrivals.append((pltpu.make_async_remote_copy(
                    src_ref=piece(frm), dst_ref=out.at[frm], device_id=_device_of(frm), **pair), served(me)))

    @staticmethod
    def _each(copies, act):
        for cp, takes_part in copies:
            if takes_part is None:
                act(cp)
            else:
                pl.when(takes_part)(functools.partial(act, cp))

    def start(self):
        self._each(self.mine, lambda cp: cp.start(LOCAL_DMA_PRIORITY))
        self._each(self.sends, lambda cp: cp.start())

    def wait(self):
        self._each(self.arrivals, lambda cp: cp.wait_recv())
        self._each(self.sends, lambda cp: cp.wait_send())
        self._each(self.mine, lambda cp: cp.wait())

    @staticmethod
    def scratch(n):
        return [pltpu.SemaphoreType.DMA((7 * n,)), pltpu.SemaphoreType.DMA((7 * n,)), pltpu.SemaphoreType.DMA((n,))]

    @staticmethod
    def out_shapes(arrays, gather):
        return [jax.ShapeDtypeStruct((N_DEV,) + (a.shape if gather else a.shape[1:]), a.dtype) for a in arrays]


class _TwoLevelGather:
    OVER_ICI = (2, 3, 6)

    def __init__(self, srcs, outs, sems):
        send_sems, recv_sems, local_sems = sems
        order = _gather_order()
        me, sibling = order[0], order[1]
        self.mine, self.first, self.passed, self.arrivals = [], [], [], {}
        for a, (src, out) in enumerate(zip(srcs, outs)):
            def copy(s, block, to, src_ref=None, out=out, a=a):
                slot = out.at[4 * block[0] + 2 * block[1] + block[2]]
                return pltpu.make_async_remote_copy(
                    src_ref=slot if src_ref is None else src_ref, dst_ref=slot,
                    send_sem=send_sems.at[7 * a + s - 1], recv_sem=recv_sems.at[7 * a + s - 1],
                    device_id=to, device_id_type=pl.DeviceIdType.MESH)

            self.mine.append(pltpu.make_async_copy(src, out.at[4 * me[0] + 2 * me[1] + me[2]], local_sems.at[a]))
            self.first.append(copy(1, me, sibling, src))
            self.first += [copy(s, me, order[s], src) for s in self.OVER_ICI]
            self.passed += [(s, copy(s + (1 if s == 6 else 2), order[s], sibling)) for s in self.OVER_ICI]
            for s in range(1, N_DEV):
                self.arrivals[(a, s)] = copy(s, order[s], me)
        self.n = len(srcs)

    def start(self):
        for cp in self.mine:
            cp.start()
        for cp in self.first:
            cp.start()

    def relay(self):
        for k, (s, cp) in enumerate(self.passed):
            self.arrivals[(k // len(self.OVER_ICI), s)].wait_recv()
            cp.start()

    def finish(self):
        for a in range(self.n):
            for s in range(1, N_DEV):
                if s not in self.OVER_ICI:
                    self.arrivals[(a, s)].wait_recv()
        for cp in self.first + [cp for _, cp in self.passed]:
            cp.wait_send()
        for cp in self.mine:
            cp.wait()


def _all_reduce_small(v):
    R, C = v.shape

    def body(v_ref, out_ref, slots, send_sems, recv_sems):
        x, y, c = _mesh_pos()
        me = 4 * x + 2 * y + c

        def copy(j, to):
            return pltpu.make_async_remote_copy(
                src_ref=v_ref, dst_ref=slots.at[me],
                send_sem=send_sems.at[j - 1], recv_sem=recv_sems.at[j - 1],
                device_id=_device_of(to), device_id_type=pl.DeviceIdType.MESH)

        sends = [copy(j, (me + j) % N_DEV) for j in range(1, N_DEV)]
        for cp in sends:
            cp.start()
        slots[me] = v_ref[...]
        for j in range(1, N_DEV):
            frm = (me + N_DEV - j) % N_DEV
            pltpu.make_async_remote_copy(
                src_ref=v_ref, dst_ref=slots.at[frm], send_sem=send_sems.at[j - 1], recv_sem=recv_sems.at[j - 1],
                device_id=_device_of(frm), device_id_type=pl.DeviceIdType.MESH).wait_recv()
        for cp in sends:
            cp.wait_send()
        acc = slots[0]
        for p in range(1, N_DEV):
            acc = acc + slots[p]
        out_ref[...] = acc

    return pl.pallas_call(
        body, name="all_reduce_small",
        in_specs=[pl.BlockSpec(memory_space=pltpu.VMEM)], out_specs=pl.BlockSpec(memory_space=pltpu.VMEM),
        out_shape=jax.ShapeDtypeStruct((R, C), F32),
        scratch_shapes=[pltpu.VMEM((N_DEV, R, C), F32), pltpu.SemaphoreType.DMA((7,)), pltpu.SemaphoreType.DMA((7,))],
    )(v)


def _adamw_math(w, g, m, v):
    m = ADAM_B1 * m + (1.0 - ADAM_B1) * g
    v = ADAM_B2 * v + (1.0 - ADAM_B2) * (g * g)
    m_hat = m / (1.0 - ADAM_B1 ** ADAM_STEP)
    v_hat = v / (1.0 - ADAM_B2 ** ADAM_STEP)
    delta = -ADAM_LR * (m_hat / (jnp.sqrt(v_hat) + ADAM_EPS) + ADAM_WD * w)
    return delta, m, v


def _adamw_reduce(recvs, dests, w, m, v, name):
    R, C = w.shape
    tr = R if R <= 256 else 256
    while R % tr:
        tr //= 2
    n = len(recvs)

    def body(*refs):
        w_ref, m_ref, v_ref, g_ref, d_ref, nm_ref, nv_ref = refs[n:]

        def update(r_ref):
            g = r_ref[0].astype(F32)
            for p in range(1, N_DEV):
                g = g + r_ref[p].astype(F32)
            d, nm, nv = _adamw_math(w_ref[...], g, m_ref[...], v_ref[...])
            g_ref[...] = g
            d_ref[...] = d
            nm_ref[...] = nm
            nv_ref[...] = nv

        if n == 1:
            update(refs[0])
        else:
            x, y, c = _mesh_pos()
            me = 4 * x + 2 * y + c
            for r_ref, (lo, hi) in zip(refs[:n], dests):
                pl.when((me >= lo) & (me < hi))(functools.partial(update, r_ref))

    tile = pl.BlockSpec((tr, C), lambda i: (i, 0))
    shp = jax.ShapeDtypeStruct((R, C), F32)
    return pl.pallas_call(
        body, name=name, grid=(R // tr,),
        in_specs=[pl.BlockSpec((N_DEV, tr, C), lambda i: (0, i, 0))] * n + [tile, tile, tile],
        out_specs=[tile] * 4, out_shape=[shp] * 4,
        compiler_params=_params(("parallel",)),
    )(*recvs, w, m, v)


def _adamw_small(g, w, m, v):
    def body(g_ref, w_ref, m_ref, v_ref, go_ref, d_ref, nm_ref, nv_ref):
        go_ref[...] = g_ref[...]
        for d in range(2):
            p0 = _sig(w_ref[8 + 2 * d:9 + 2 * d, :] - w_ref[9 + 2 * d:10 + 2 * d, :])
            dl0 = g_ref[12 + d:13 + d, :] * p0 * (1.0 - p0)
            go_ref[8 + 2 * d:9 + 2 * d, :] = dl0
            go_ref[9 + 2 * d:10 + 2 * d, :] = -dl0
            go_ref[12 + d:13 + d, :] = jnp.zeros((1, D_MODEL), F32)
        d, nm, nv = _adamw_math(w_ref[...], go_ref[...], m_ref[...], v_ref[...])
        d_ref[...] = d
        nm_ref[...] = nm
        nv_ref[...] = nv

    shp = jax.ShapeDtypeStruct(g.shape, F32)
    vm = pl.BlockSpec(memory_space=pltpu.VMEM)
    return pl.pallas_call(body, name="adamw_small", in_specs=[vm] * 4, out_specs=[vm] * 4, out_shape=[shp] * 4)(g, w, m, v)


def _local_step(x, tgt, g_mix, lb_shard, norm_g, pool_scale, g_ffn, g_final, w_in_shard, late_shards):
    B, L, _ = x.shape
    T = B * L
    rows = D_MODEL // N_DEV
    x2, tgt2 = x.reshape(T, D_MODEL), tgt.reshape(T, D_MODEL)

    proj, u, w_in_g, lb_g = _rms_inproj(x2, g_mix, w_in_shard, lb_shard)
    lb = lb_g[:, :4].transpose(1, 0, 2).reshape(4, D_MODEL)
    proj4 = proj.reshape(N_DEV, B, L, D_MODEL)
    o, ya, (wa, wb, wo, wfi_g, wfo_g, pw_g) = _hgrn_fwd(proj4, lb, norm_g, B, L, late_shards)
    wa, wb, wo = (w_.reshape(D_MODEL, D_MODEL) for w_ in (wa, wb, wo))
    wfo_g = wfo_g.reshape(4, FF_BLOCK, D_MODEL)
    pool_w_full = pw_g.reshape(N_DEV, 4, 32, POOL_DIM).transpose(1, 0, 2, 3).reshape(4, POOL_DIM, POOL_DIM)
    yb = _pool_fwd(proj4, pool_w_full, pool_scale, B, L)
    ya2, yb2 = ya.reshape(T, D_MODEL), yb.reshape(T, D_MODEL)
    za, zb, mg, h = _merge_out(x2, proj, ya2, yb2, wa, wb, wo)
    gate, up, hid, u2, dh2, dh2b, loss_p, dg_final = _ffn_fwd_loss(h, tgt2, g_ffn, g_final, wfi_g, wfo_g)
    loss = jnp.sum(loss_p[:, 0, 0])

    dgu, dh, dhb, dg_ffn = _ffn_bwd(h, dh2, dh2b, gate, up, g_ffn, wfi_g, wfo_g)
    d_wfo = _wgrad(hid, dh2b[None], "wgrad_ffn_out")
    d_wfi, (recv_wfo,) = _wgrad(u2[None], dgu.reshape(N_DEV, T, FF_BLOCK), "wgrad_ffn_in",
                                hosted=[d_wfo.reshape(N_DEV, FF_BLOCK // 2, D_MODEL)])
    dza, dzb, dgab, dya, dyb = _merge_bwd(dhb, proj, za, zb, wa, wb, wo)
    d_wo = _wgrad(mg[None], dhb[None], "wgrad_out")
    d_wa = _wgrad(ya2[None], dza[None], "wgrad_branch_a")
    d_wb = _wgrad(yb2[None], dzb[None], "wgrad_branch_b")
    dp, dpw_p, dps_p = _pool_bwd(proj4, dyb.reshape(B, L, D_MODEL), pool_w_full, pool_scale, B, L)
    d_pw = dpw_p.sum(0).reshape(4, N_DEV, 32, POOL_DIM).transpose(1, 0, 2, 3).reshape(N_DEV, 128, POOL_DIM)
    dp2 = dp.reshape(T, D_MODEL)
    w_in_dests = [(5, 6), (6, 8), (0, 5)]
    d_win_pool = _wgrad(u[None], dp2[None], "wgrad_in_pool")
    d_win_gates = _wgrad(u[None], dgab, "wgrad_in_gates")
    slices = [d_wa.reshape(N_DEV, rows, D_MODEL), d_wb.reshape(N_DEV, rows, D_MODEL),
              d_wo.reshape(N_DEV, rows, D_MODEL), d_wfi, d_pw.astype(MX), d_win_pool, d_win_gates]
    dests = [(0, N_DEV)] * 5 + w_in_dests[:2]
    dproj5, dlb_p, dng_p, recv = _hgrn_bwd(proj4, o, dya.reshape(B, L, D_MODEL), lb, norm_g, B, L, slices, dests)
    dproj5 = dproj5.reshape(5, T, D_MODEL)
    d_win_rec = _wgrad(u[None], dproj5, "wgrad_in_recurrence")
    grad_x, dg_mix, recv_win_rec = _inproj_bwd(x2, dh, dproj5, dp2, dgab, g_mix, w_in_g, d_win_rec, w_in_dests[2])

    small = dict(g_mix=dg_mix.sum(0), hgrn_norm_g=dng_p.sum((0, 1)), pool_scale=dps_p.sum((0, 1)),
                 g_ffn=dg_ffn.sum(0), g_final=dg_final.sum(0), lb=dlb_p.sum(0))
    recv_w_in = [recv[5], recv[6], recv_win_rec]
    recv_late = list(recv[:4]) + [recv_wfo, recv[4]]
    return loss, grad_x.reshape(B, L, D_MODEL), (recv_w_in, w_in_dests), recv_late, small


def kernel(x, g_mix, w_in, lb_logits, hgrn_norm_g, pool_w, pool_scale, w_branch_a, w_branch_b, w_out, g_ffn, w_ffn_in, w_ffn_out, g_final, loss_target, m_g_mix, m_w_in, m_lb_logits, m_hgrn_norm_g, m_pool_w, m_pool_scale, m_w_branch_a, m_w_branch_b, m_w_out, m_g_ffn, m_w_ffn_in, m_w_ffn_out, m_g_final, v_g_mix, v_w_in, v_lb_logits, v_hgrn_norm_g, v_pool_w, v_pool_scale, v_w_branch_a, v_w_branch_b, v_w_out, v_g_ffn, v_w_ffn_in, v_w_ffn_out, v_g_final):
    me = 4 * lax.axis_index("x") + 2 * lax.axis_index("y") + lax.axis_index("c")

    late_shards = [w_branch_a[0].astype(MX), w_branch_b[0].astype(MX), w_out[0].astype(MX),
                   w_ffn_in[0].astype(MX), w_ffn_out[0].astype(MX), pool_w[0].reshape(4 * 32, POOL_DIM).astype(MX)]
    lb_shard = jnp.pad(lb_logits.reshape(4, HEAD_DIM), ((0, 4), (0, 0)))

    loss, grad_x, (recv_w_in, w_in_dests), recv_late, small = _local_step(
        x, loss_target, g_mix, lb_shard, hgrn_norm_g, pool_scale, g_ffn, g_final[None], w_in[0].astype(MX),
        late_shards)
    loss = lax.psum(loss, ("x", "y", "c"))

    packed = jnp.zeros((16, D_MODEL), F32)
    names = ["g_mix", "hgrn_norm_g", "pool_scale", "g_ffn", "g_final"]
    for i, nme in enumerate(names):
        packed = packed.at[i].set(small[nme])
    packed = packed.at[5:7].set(small["lb"])
    red = _all_reduce_small(packed)
    dlb_mine = lax.dynamic_slice_in_dim(red[5:7], me * HEAD_DIM, HEAD_DIM, axis=1)

    sw = jnp.zeros((16, D_MODEL), F32)
    sm = jnp.zeros((16, D_MODEL), F32)
    sv = jnp.ones((16, D_MODEL), F32)
    smalls = [(g_mix, m_g_mix, v_g_mix), (hgrn_norm_g, m_hgrn_norm_g, v_hgrn_norm_g),
              (pool_scale, m_pool_scale, v_pool_scale), (g_ffn, m_g_ffn, v_g_ffn),
              (g_final[None], m_g_final[None], v_g_final[None])]
    for i, (w_, m_, v_) in enumerate(smalls):
        sw, sm, sv = sw.at[i].set(w_[0]), sm.at[i].set(m_[0]), sv.at[i].set(v_[0])
    sg = red.at[5:].set(0.0)
    sg = sg.at[12:14, :HEAD_DIM].set(dlb_mine)
    sw = sw.at[8:12, :HEAD_DIM].set(lb_logits.reshape(4, HEAD_DIM))
    sm = sm.at[8:12, :HEAD_DIM].set(m_lb_logits.reshape(4, HEAD_DIM))
    sv = sv.at[8:12, :HEAD_DIM].set(v_lb_logits.reshape(4, HEAD_DIM))
    sg, sd, snm, snv = _adamw_small(sg, sw, sm, sv)

    def small_out(arr, i, like):
        return arr[i].reshape(like.shape)

    def lb_out(arr):
        return arr[8:12, :HEAD_DIM].reshape(2, 2, HEAD_DIM)

    order = ["w_in", "w_branch_a", "w_branch_b", "w_out", "w_ffn_in", "w_ffn_out", "pool_w"]
    params = dict(w_in=(w_in, m_w_in, v_w_in), w_branch_a=(w_branch_a, m_w_branch_a, v_w_branch_a),
                  w_branch_b=(w_branch_b, m_w_branch_b, v_w_branch_b), w_out=(w_out, m_w_out, v_w_out),
                  w_ffn_in=(w_ffn_in, m_w_ffn_in, v_w_ffn_in), w_ffn_out=(w_ffn_out, m_w_ffn_out, v_w_ffn_out),
                  pool_w=(pool_w, m_pool_w, v_pool_w))
    res = {}
    for nme, r, dests in zip(order, [recv_w_in] + [[r] for r in recv_late], [w_in_dests] + [None] * 6):
        w_, m_, v_ = params[nme]
        shape2 = r[0].shape[1:]
        outs = _adamw_reduce(r, dests, w_.reshape(shape2), m_.reshape(shape2), v_.reshape(shape2), "adamw_" + nme)
        res[nme] = [o_.reshape(w_.shape) for o_ in outs]

    def pick(k):
        small_src = [sg, sd, snm, snv][k]
        return [small_out(small_src, 0, g_mix), res["w_in"][k], lb_out(small_src), small_out(small_src, 1, hgrn_norm_g),
                res["pool_w"][k], small_out(small_src, 2, pool_scale), res["w_branch_a"][k], res["w_branch_b"][k],
                res["w_out"][k], small_out(small_src, 3, g_ffn), res["w_ffn_in"][k], res["w_ffn_out"][k],
                small_out(small_src, 4, g_final)]

    return (loss, grad_x, *pick(0), *pick(1), *pick(2), *pick(3))
```
